```python
import math
import jax, jax.numpy as jnp
from jax import lax
import numpy as np

D_MODEL = 1024
BATCH = 16
SEQ = 2048
DEPTH = 2
DEC_BATCH = 32
DEC_SEQ = 1
PAST_LEN = 16384
PAGE_SIZE = 128

GLA_HEADS = 4
GLA_DK = D_MODEL // 2 // GLA_HEADS
GLA_DV = D_MODEL // GLA_HEADS
GLA_RANK = 16
GLA_NORMALIZER = 16.0
GLA_CHUNK = 64
ATT_HEADS = 16
ATT_KV_HEADS = 4
ATT_DH = D_MODEL // ATT_HEADS
IDX_HEADS = 8
IDX_DH = 64
TOPK_MAX = 256
ATT_QBLK = 32
N_BUCKETS = 32
MAX_DISTANCE = 128
N_EXPERTS = 64
TOP_K = 8
N_GROUPS = 8
TOPK_GROUPS = 4
EXPERT_FF = 256
SHARED_FF = 256
ROUTE_SCALE = 2.5
MOE_BLK = 128
EPS = 1e-6

N_GLA_LAYERS = (DEPTH + 1) // 2
N_DSA_LAYERS = DEPTH // 2

kernel_name = 'gla_dsa_moe_hybrid_step'


def rms_norm(x, g):
    xf = x.astype(jnp.float32)
    y = xf * lax.rsqrt(jnp.mean(xf * xf, axis=-1, keepdims=True) + EPS)
    return (y * g.astype(jnp.float32)).astype(x.dtype)


def layer_norm(x, g, b):
    xf = x.astype(jnp.float32)
    mu = jnp.mean(xf, axis=-1, keepdims=True)
    var = jnp.mean(jnp.square(xf - mu), axis=-1, keepdims=True)
    return ((xf - mu) * lax.rsqrt(var + EPS) * g + b).astype(x.dtype)


def ada_mod(c, w, b):
    return jnp.split(jax.nn.silu(c) @ w + b, 6, axis=-1)


def modulate(x, g, shift, scale):
    return rms_norm(x, g) * (1 + scale[:, None, :]) + shift[:, None, :]


def t5_bucket(dist):
    dist = jnp.maximum(dist, 0)
    max_exact = N_BUCKETS // 2
    ratio = jnp.log(jnp.maximum(dist, max_exact).astype(jnp.float32) / max_exact) / math.log(MAX_DISTANCE / max_exact)
    large = jnp.minimum(max_exact + (ratio * (N_BUCKETS - max_exact)).astype(jnp.int32), N_BUCKETS - 1)
    return jnp.where(dist < max_exact, dist, large)


def gla_project(h, w_in, w_g2, b_g2):
    B, T = h.shape[:2]
    hk, hv = GLA_HEADS * GLA_DK, GLA_HEADS * GLA_DV
    q, k, v, r, g1 = jnp.split(h @ w_in, [hk, 2 * hk, 2 * hk + hv, 2 * hk + 2 * hv], axis=-1)
    log_a = jax.nn.log_sigmoid((g1 @ w_g2 + b_g2).astype(jnp.float32)) / GLA_NORMALIZER
    heads = lambda a, d: a.reshape(B, T, GLA_HEADS, d)
    return (heads(q, GLA_DK) * GLA_DK ** -0.5, heads(k, GLA_DK), heads(v, GLA_DV), r,
            heads(log_a, GLA_DK))


def gla_chunked(q, k, v, log_a):
    B, T = q.shape[:2]
    C = GLA_CHUNK
    nc = T // C

    def chunks(a):
        return a.astype(jnp.float32).reshape(B, nc, C, GLA_HEADS, -1).transpose(1, 0, 3, 2, 4)

    tri = jnp.tril(jnp.ones((C, C), dtype=bool))[:, :, None]

    def step(S, inp):
        qc, kc, vc, gc = inp
        b = jnp.cumsum(gc, axis=2)
        decay = jnp.exp(jnp.where(tri, b[:, :, :, None, :] - b[:, :, None, :, :], -jnp.inf))
        att = jnp.einsum('bhtd,bhsd,bhtsd->bhts', qc, kc, decay)
        o = jnp.einsum('bhts,bhsv->bhtv', att, vc) + jnp.einsum('bhtd,bhdv->bhtv', qc * jnp.exp(b), S)
        b_end = b[:, :, -1:, :]
        S = jnp.exp(b_end[:, :, 0, :, None]) * S + jnp.einsum('bhsd,bhsv->bhdv', kc * jnp.exp(b_end - b), vc)
        return S, o

    S0 = jnp.zeros((B, GLA_HEADS, GLA_DK, GLA_DV), jnp.float32)
    S, o = lax.scan(step, S0, (chunks(q), chunks(k), chunks(v), chunks(log_a)))
    return o.transpose(1, 0, 3, 2, 4).reshape(B, T, GLA_HEADS, GLA_DV), S


def gla_recurrent(q, k, v, log_a, S0):
    def step(S, inp):
        qt, kt, vt, gt = inp
        S = jnp.exp(gt)[..., None] * S + kt[..., :, None] * vt[..., None, :]
        return S, jnp.einsum('bhd,bhdv->bhv', qt, S)

    xs = tuple(a.astype(jnp.float32).swapaxes(0, 1) for a in (q, k, v, log_a))
    S, o = lax.scan(step, S0.astype(jnp.float32), xs)
    return o.swapaxes(0, 1), S


def gla_mixer(h, state, w_in, w_g2, b_g2, gnorm, w_out):
    B, T = h.shape[:2]
    q, k, v, r, log_a = gla_project(h, w_in, w_g2, b_g2)
    if state is None:
        o, S = gla_chunked(q, k, v, log_a)
    else:
        o, S = gla_recurrent(q, k, v, log_a, state)
    o = rms_norm(o.astype(h.dtype), gnorm).reshape(B, T, GLA_HEADS * GLA_DV) * jax.nn.silu(r)
    return o @ w_out, S


def dsa_project(h, w_in, ln_g, ln_b):
    B, T = h.shape[:2]
    nq, nkv, nqi = ATT_HEADS * ATT_DH, ATT_KV_HEADS * ATT_DH, IDX_HEADS * IDX_DH
    cuts = [nq, nq + nkv, nq + 2 * nkv, nq + 2 * nkv + nqi, nq + 2 * nkv + nqi + IDX_DH]
    q, k, v, qi, ki, wi = jnp.split(h @ w_in, cuts, axis=-1)
    return (q.reshape(B, T, ATT_HEADS, ATT_DH), k.reshape(B, T, ATT_KV_HEADS, ATT_DH),
            v.reshape(B, T, ATT_KV_HEADS, ATT_DH), qi.reshape(B, T, IDX_HEADS, IDX_DH),
            layer_norm(ki, ln_g, ln_b), wi)


def indexer_scores(qi, wi, ki):
    logits = jnp.einsum('bthd,bsd->bths', qi, ki).astype(jnp.float32) * IDX_DH ** -0.5
    return jnp.einsum('bths,bth->bts', jax.nn.relu(logits), wi.astype(jnp.float32) * IDX_HEADS ** -0.5)


def sparse_attend(q, k_sel, v_sel, valid, dist, rel_bias):
    B, Tq = q.shape[:2]
    G = ATT_HEADS // ATT_KV_HEADS
    qg = q.reshape(B, Tq, ATT_KV_HEADS, G, ATT_DH)
    logits = jnp.einsum('btkgd,btskd->btkgs', qg, k_sel).astype(jnp.float32) * ATT_DH ** -0.5
    bias = rel_bias[t5_bucket(dist)].astype(jnp.float32)
    bias = bias.reshape(B, Tq, -1, ATT_KV_HEADS, G).transpose(0, 1, 3, 4, 2)
    logits = jnp.where(valid[:, :, None, None, :], logits + bias, -jnp.inf)
    p = jax.nn.softmax(logits, axis=-1).astype(v_sel.dtype)
    o = jnp.einsum('btkgs,btskd->btkgd', p, v_sel)
    return o.reshape(B, Tq, ATT_HEADS * ATT_DH)


def dsa_mixer_prompt(h, rel_bias, w_in, ln_g, ln_b, w_out):
    q, k, v, qi, ki, wi = dsa_project(h, w_in, ln_g, ln_b)
    B, T = h.shape[:2]
    topk = min(TOPK_MAX, T // 4)
    n_blk = T // ATT_QBLK
    pos_k = jnp.arange(T)
    bidx = jnp.arange(B)[:, None, None]

    def block(i):
        t0 = i * ATT_QBLK
        sl = lambda a: lax.dynamic_slice_in_dim(a, t0, ATT_QBLK, axis=1)
        pos_q = t0 + jnp.arange(ATT_QBLK)
        causal = (pos_k[None, :] <= pos_q[:, None])[None]
        sc = jnp.where(causal, indexer_scores(sl(qi), sl(wi), ki), -jnp.inf)
        _, idx = lax.top_k(sc, topk)
        return sparse_attend(sl(q), k[bidx, idx], v[bidx, idx], idx <= pos_q[None, :, None],
                             pos_q[None, :, None] - idx, rel_bias)

    o = lax.map(block, jnp.arange(n_blk))
    o = jnp.swapaxes(o, 0, 1).reshape(B, T, ATT_HEADS * ATT_DH)
    return o @ w_out, (k, v, ki)


def dsa_mixer_sample(h, cache_k, cache_v, cache_ki, page_table, rel_bias, w_in, ln_g, ln_b, w_out):
    q, k, v, qi, ki, wi = dsa_project(h, w_in, ln_g, ln_b)
    B, T = h.shape[:2]
    past = page_table.shape[1] * PAGE_SIZE
    L = past + T
    topk = min(TOPK_MAX, L // 4)
    ki_past = cache_ki[page_table].reshape(B, past, IDX_DH)
    ki_all = jnp.concatenate([ki_past, ki.astype(ki_past.dtype)], axis=1)
    pos_q = past + jnp.arange(T)
    pos_k = jnp.arange(L)
    causal = (pos_k[None, :] <= pos_q[:, None])[None]
    sc = jnp.where(causal, indexer_scores(qi, wi, ki_all), -jnp.inf)
    _, idx = lax.top_k(sc, topk)
    bidx = jnp.arange(B)[:, None, None]
    pidx = jnp.minimum(idx, past - 1)
    phys = page_table[bidx, pidx // PAGE_SIZE] * PAGE_SIZE + pidx % PAGE_SIZE
    nidx = jnp.clip(idx - past, 0, T - 1)
    in_past = (idx < past)[..., None, None]
    k_sel = jnp.where(in_past, cache_k.reshape(-1, ATT_KV_HEADS, ATT_DH)[phys], k[bidx, nidx])
    v_sel = jnp.where(in_past, cache_v.reshape(-1, ATT_KV_HEADS, ATT_DH)[phys], v[bidx, nidx])
    o = sparse_attend(q, k_sel, v_sel, idx <= pos_q[None, :, None], pos_q[None, :, None] - idx, rel_bias)
    return o @ w_out, (k, v, ki)


def swiglu(x, w_g, w_u, w_d):
    return (jax.nn.silu(x @ w_g) * (x @ w_u)) @ w_d


def routed_experts(x, eidx, wsel, w_gate, w_up, w_down):
    N = x.shape[0]
    A = N * TOP_K
    e_flat = eidx.reshape(A)
    tok_flat = jnp.arange(A, dtype=jnp.int32) // TOP_K
    w_flat = wsel.reshape(A)
    order = jnp.argsort(e_flat)
    e_s, tok_s, w_s = e_flat[order], tok_flat[order], w_flat[order]
    counts = jnp.zeros((N_EXPERTS,), jnp.int32).at[e_flat].add(1)
    start = jnp.cumsum(counts) - counts
    padded = (counts + MOE_BLK - 1) // MOE_BLK * MOE_BLK
    pad_end = jnp.cumsum(padded)
    pad_start = pad_end - padded
    dest = pad_start[e_s] + (jnp.arange(A, dtype=jnp.int32) - start[e_s])
    n_blocks = (A + N_EXPERTS * (MOE_BLK - 1) + MOE_BLK - 1) // MOE_BLK
    P = n_blocks * MOE_BLK
    row_tok = jnp.zeros((P,), jnp.int32).at[dest].set(tok_s)
    row_w = jnp.zeros((P,), x.dtype).at[dest].set(w_s)
    blk_expert = jnp.minimum(jnp.searchsorted(pad_end, jnp.arange(n_blocks) * MOE_BLK, side='right'),
                             N_EXPERTS - 1)

    def body(y, inp):
        e, toks, wr = inp
        out = swiglu(x[toks], w_gate[e], w_up[e], w_down[e]) * wr[:, None]
        return y.at[toks].add(out), None

    y, _ = lax.scan(body, jnp.zeros_like(x),
                    (blk_expert, row_tok.reshape(n_blocks, MOE_BLK), row_w.reshape(n_blocks, MOE_BLK)))
    return y


def moe_ffn(h, w_router, router_bias, w_gate, w_up, w_down, ws_gate, ws_up, ws_down):
    shp = h.shape
    x = h.reshape(-1, shp[-1])
    N = x.shape[0]
    s = jax.nn.sigmoid((x @ w_router).astype(jnp.float32))
    sel = s + router_bias.astype(jnp.float32)
    grp_score = lax.top_k(sel.reshape(N, N_GROUPS, -1), 2)[0].sum(-1)
    _, gidx = lax.top_k(grp_score, TOPK_GROUPS)
    gmask = jnp.any(gidx[..., None] == jnp.arange(N_GROUPS), axis=1)
    emask = jnp.repeat(gmask, N_EXPERTS // N_GROUPS, axis=-1)
    _, eidx = lax.top_k(jnp.where(emask, sel, -jnp.inf), TOP_K)
    wsel = jnp.take_along_axis(s, eidx, axis=-1)
    wsel = wsel / jnp.sum(wsel, axis=-1, keepdims=True) * ROUTE_SCALE
    y = routed_experts(x, eidx, wsel.astype(x.dtype), w_gate, w_up, w_down) + swiglu(x, ws_gate, ws_up, ws_down)
    return y.reshape(shp)


def setup_inputs(seed: int = 0) -> dict:
    key = jax.random.key(seed)
    ks = iter(jax.random.split(key, 48))
    nrm = lambda shape, scale: jax.random.normal(next(ks), shape, jnp.float32) * scale
    n_pages = PAST_LEN // PAGE_SIZE
    n_phys = (DEC_BATCH * n_pages * 5 + 3) // 4
    D = D_MODEL
    hk, hv = GLA_HEADS * GLA_DK, GLA_HEADS * GLA_DV
    gla_in = 2 * hk + 2 * hv + GLA_RANK
    dsa_in = ATT_HEADS * ATT_DH + 2 * ATT_KV_HEADS * ATT_DH + IDX_HEADS * IDX_DH + IDX_DH + IDX_HEADS
    page_table = jax.random.permutation(next(ks), n_phys)[:DEC_BATCH * n_pages]
    page_table = page_table.reshape(DEC_BATCH, n_pages).astype(jnp.int32)
    return {
        'x_prompt': nrm((BATCH, SEQ, D), 1.0),
        'x_sample': nrm((DEC_BATCH, DEC_SEQ, D), 1.0),
        'c_prompt': nrm((BATCH, D), 1.0),
        'c_sample': nrm((DEC_BATCH, D), 1.0),
        'state_gla': nrm((N_GLA_LAYERS, DEC_BATCH, GLA_HEADS, GLA_DK, GLA_DV), 0.5),
        'cache_k': nrm((N_DSA_LAYERS, n_phys, PAGE_SIZE, ATT_KV_HEADS, ATT_DH), 1.0),
        'cache_v': nrm((N_DSA_LAYERS, n_phys, PAGE_SIZE, ATT_KV_HEADS, ATT_DH), 1.0),
        'cache_idx_k': nrm((N_DSA_LAYERS, n_phys, PAGE_SIZE, IDX_DH), 1.0),
        'page_table': page_table,
        'rel_bias': nrm((N_BUCKETS, ATT_HEADS), 0.5),
        'w_ada': nrm((DEPTH, D, 6 * D), 0.5 * D ** -0.5),
        'b_ada': nrm((DEPTH, 6 * D), 0.02),
        'norm_mix': 1.0 + nrm((DEPTH, D), 0.02),
        'norm_ffn': 1.0 + nrm((DEPTH, D), 0.02),
        'norm_final': 1.0 + nrm((D,), 0.02),
        'gla_w_in': nrm((N_GLA_LAYERS, D, gla_in), D ** -0.5),
        'gla_w_g2': nrm((N_GLA_LAYERS, GLA_RANK, hk), GLA_RANK ** -0.5),
        'gla_b_g2': nrm((N_GLA_LAYERS, hk), 0.1),
        'gla_gnorm': 1.0 + nrm((N_GLA_LAYERS, GLA_DV), 0.02),
        'gla_w_out': nrm((N_GLA_LAYERS, hv, D), hv ** -0.5),
        'dsa_w_in': nrm((N_DSA_LAYERS, D, dsa_in), D ** -0.5),
        'dsa_idx_ln_g': 1.0 + nrm((N_DSA_LAYERS, IDX_DH), 0.02),
        'dsa_idx_ln_b': nrm((N_DSA_LAYERS, IDX_DH), 0.02),
        'dsa_w_out': nrm((N_DSA_LAYERS, ATT_HEADS * ATT_DH, D), (ATT_HEADS * ATT_DH) ** -0.5),
        'moe_w_router': nrm((DEPTH, D, N_EXPERTS), D ** -0.5),
        'moe_router_bias': nrm((DEPTH, N_EXPERTS), 0.01),
        'moe_w_gate': nrm((DEPTH, N_EXPERTS, D, EXPERT_FF), D ** -0.5),
        'moe_w_up': nrm((DEPTH, N_EXPERTS, D, EXPERT_FF), D ** -0.5),
        'moe_w_down': nrm((DEPTH, N_EXPERTS, EXPERT_FF, D), EXPERT_FF ** -0.5),
        'shared_w_gate': nrm((DEPTH, D, SHARED_FF), D ** -0.5),
        'shared_w_up': nrm((DEPTH, D, SHARED_FF), D ** -0.5),
        'shared_w_down': nrm((DEPTH, SHARED_FF, D), SHARED_FF ** -0.5),
    }


def reference(x_prompt, x_sample, c_prompt, c_sample, state_gla, cache_k, cache_v, cache_idx_k, page_table,
              rel_bias, w_ada, b_ada, norm_mix, norm_ffn, norm_final,
              gla_w_in, gla_w_g2, gla_b_g2, gla_gnorm, gla_w_out,
              dsa_w_in, dsa_idx_ln_g, dsa_idx_ln_b, dsa_w_out,
              moe_w_router, moe_router_bias, moe_w_gate, moe_w_up, moe_w_down,
              shared_w_gate, shared_w_up, shared_w_down):
    xp, xs = x_prompt, x_sample
    gla_p, gla_s = [], []
    kp_l, vp_l, kip_l, ks_l, vs_l, kis_l = [], [], [], [], [], []
    for i in range(DEPTH):
        sh1p, sc1p, gt1p, sh2p, sc2p, gt2p = ada_mod(c_prompt, w_ada[i], b_ada[i])
        sh1s, sc1s, gt1s, sh2s, sc2s, gt2s = ada_mod(c_sample, w_ada[i], b_ada[i])
        hp = modulate(xp, norm_mix[i], sh1p, sc1p)
        hs = modulate(xs, norm_mix[i], sh1s, sc1s)
        j = i // 2
        if i % 2 == 0:
            gp = (gla_w_in[j], gla_w_g2[j], gla_b_g2[j], gla_gnorm[j], gla_w_out[j])
            mp, Sp = gla_mixer(hp, None, *gp)
            ms, Ss = gla_mixer(hs, state_gla[j], *gp)
            gla_p.append(Sp)
            gla_s.append(Ss)
        else:
            dp = (dsa_w_in[j], dsa_idx_ln_g[j], dsa_idx_ln_b[j], dsa_w_out[j])
            mp, (kp_, vp_, kip_) = dsa_mixer_prompt(hp, rel_bias, *dp)
            ms, (ks_, vs_, kis_) = dsa_mixer_sample(hs, cache_k[j], cache_v[j], cache_idx_k[j], page_table,
                                                   rel_bias, *dp)
            kp_l.append(kp_); vp_l.append(vp_); kip_l.append(kip_)
            ks_l.append(ks_); vs_l.append(vs_); kis_l.append(kis_)
        xp = xp + gt1p[:, None, :] * mp
        xs = xs + gt1s[:, None, :] * ms
        mo = (moe_w_router[i], moe_router_bias[i], moe_w_gate[i], moe_w_up[i], moe_w_down[i],
              shared_w_gate[i], shared_w_up[i], shared_w_down[i])
        xp = xp + gt2p[:, None, :] * moe_ffn(modulate(xp, norm_ffn[i], sh2p, sc2p), *mo)
        xs = xs + gt2s[:, None, :] * moe_ffn(modulate(xs, norm_ffn[i], sh2s, sc2s), *mo)
    y_prompt = rms_norm(xp, norm_final)
    y_sample = rms_norm(xs, norm_final)
    return (y_prompt, y_sample, jnp.stack(gla_p), jnp.stack(gla_s),
            jnp.stack(kp_l), jnp.stack(vp_l), jnp.stack(kip_l),
            jnp.stack(ks_l), jnp.stack(vs_l), jnp.stack(kis_l))
```

```python
import functools
import math

import numpy as np
import jax
import jax.numpy as jnp
from jax import lax
from jax.experimental import pallas as pl
from jax.experimental.pallas import tpu as pltpu

F32 = jnp.float32
BF16 = jnp.bfloat16
I32 = jnp.int32

GLA_HEADS = 4
GLA_RANK = 16
GLA_NORMALIZER = 16.0
ATT_HEADS = 16
ATT_KV_HEADS = 4
IDX_HEADS = 8
IDX_DH = 64
TOPK_MAX = 256
N_BUCKETS = 32
MAX_DISTANCE = 128
N_GROUPS = 8
TOPK_GROUPS = 4
TOP_K = 8
ROUTE_SCALE = 2.5
EPS = 1e-6

LANES = 128
VMEM_LIMIT = 56 * 1024 * 1024
NEG_INF = float("-inf")
INT_MIN = -2 ** 31


def _cparams(*sem):
    return pltpu.CompilerParams(dimension_semantics=sem, vmem_limit_bytes=VMEM_LIMIT)


def _bdot(a, b):
    return jnp.dot(a.astype(BF16), b.astype(BF16), preferred_element_type=F32)


def _bdot_nt(a, b):
    return lax.dot_general(a.astype(BF16), b.astype(BF16), (((1,), (1,)), ((), ())),
                           preferred_element_type=F32)


def _bdot_tn(a, b):
    return lax.dot_general(a.astype(BF16), b.astype(BF16), (((0,), (0,)), ((), ())),
                           preferred_element_type=F32)


def _split3(a):
    hi = a.astype(BF16)
    r1 = a - hi.astype(F32)
    mid = r1.astype(BF16)
    lo = (r1 - mid.astype(F32)).astype(BF16)
    return hi, mid, lo


def _silu(x):
    return x * (1.0 / (1.0 + jnp.exp(-x)))


def _sigmoid(x):
    return 1.0 / (1.0 + jnp.exp(-x))


def _rms(x, g):
    return x * lax.rsqrt(jnp.mean(x * x, axis=-1, keepdims=True) + EPS) * g


def _ada_kernel(c_ref, w_ref, b_ref, o_ref):
    c_hi, c_mid, _ = _split3(_silu(c_ref[...]))
    w_hi, w_mid, _ = _split3(w_ref[...])
    dot = lambda a, b: jnp.dot(a, b, preferred_element_type=F32)
    o_ref[...] = (dot(c_hi, w_mid) + dot(c_mid, w_hi)) + dot(c_hi, w_hi) + b_ref[...]


def ada_mod_all(c, w_ada, b_ada, tn=512):
    depth, d, n6 = w_ada.shape
    rows = c.shape[0]
    return pl.pallas_call(
        _ada_kernel,
        grid=(depth, n6 // tn),
        in_specs=[pl.BlockSpec((rows, d), lambda l, j: (0, 0)),
                  pl.BlockSpec((None, d, tn), lambda l, j: (l, 0, j)),
                  pl.BlockSpec((None, 1, tn), lambda l, j: (l, 0, j))],
        out_specs=pl.BlockSpec((None, rows, tn), lambda l, j: (l, 0, j)),
        out_shape=jax.ShapeDtypeStruct((depth, rows, n6), F32),
        compiler_params=_cparams("parallel", "parallel"),
        name="ada_mod",
    )(c, w_ada, b_ada.reshape(depth, 1, n6))


def _mod_spec(mod, which, tm, seq_len, d):
    if mod.ndim == 3:
        per_seq = seq_len // tm
        return pl.BlockSpec((None, 1, d), lambda i: (i // per_seq, 0, which))
    return pl.BlockSpec((tm, d), lambda i: (i, which))


def _gla_proj_kernel(x_ref, g_ref, sh_ref, sc_ref, w_ref, wg2_ref, bg2_ref, qkvr_ref, la_ref, *, n_main):
    h = _rms(x_ref[...], g_ref[...]) * (1.0 + sc_ref[...]) + sh_ref[...]
    y = _bdot(h, w_ref[...])
    qkvr_ref[...] = y[:, :n_main]
    g1 = y[:, n_main:n_main + GLA_RANK]
    z = _bdot(g1, wg2_ref[...]) + bg2_ref[...]
    la_ref[...] = (jnp.minimum(z, 0.0) - jnp.log(1.0 + jnp.exp(-jnp.abs(z)))) * (1.0 / GLA_NORMALIZER)


def gla_project(x, mod, norm_g, w_in, w_g2, b_g2, tm, seq_len):
    n, d = x.shape
    hk = w_g2.shape[1]
    n_main = w_in.shape[1] - GLA_RANK
    n_pad = (-w_in.shape[1]) % LANES
    w = jnp.pad(w_in, ((0, 0), (0, n_pad))).astype(BF16)
    return pl.pallas_call(
        functools.partial(_gla_proj_kernel, n_main=n_main),
        grid=(n // tm,),
        in_specs=[pl.BlockSpec((tm, d), lambda i: (i, 0)),
                  pl.BlockSpec((1, d), lambda i: (0, 0)),
                  _mod_spec(mod, 0, tm, seq_len, d),
                  _mod_spec(mod, 1, tm, seq_len, d),
                  pl.BlockSpec(w.shape, lambda i: (0, 0)),
                  pl.BlockSpec(w_g2.shape, lambda i: (0, 0)),
                  pl.BlockSpec((1, hk), lambda i: (0, 0))],
        out_specs=[pl.BlockSpec((tm, n_main), lambda i: (i, 0)),
                   pl.BlockSpec((tm, hk), lambda i: (i, 0))],
        out_shape=[jax.ShapeDtypeStruct((n, n_main), F32),
                   jax.ShapeDtypeStruct((n, hk), F32)],
        compiler_params=_cparams("parallel"),
        name="gla_proj",
    )(x, norm_g.reshape(1, d), mod, mod, w, w_g2.astype(BF16), b_g2.reshape(1, hk))


def _gla_level_matrices(c):
    levels = int(math.log2(c))
    t = np.arange(c)[:, None]
    u = np.arange(c)[None, :]
    mats, masks = [], []
    for l in range(levels):
        m = 1 << l
        ref = (t // (2 * m)) * (2 * m) + m - 1
        right = (t % (2 * m)) >= m
        mat = np.where(right, (u > ref) & (u <= t), (u > t) & (u <= ref))
        mats.append(mat)
        masks.append((t // (2 * m) == u // (2 * m)) & right & ((u % (2 * m)) < m))
    mats.append(u <= t)
    masks.append(t == u)
    return (np.stack(mats).astype(np.float32).reshape(-1, c), np.stack(masks).astype(np.float32))


def _gla_chunk_kernel(q_ref, k_ref, v_ref, r_ref, la_ref, mat_ref, mask_ref, gn_ref,
                      o_ref, s_out_ref, st_ref, *, c, dk):
    ci = pl.program_id(2)
    nlev = mask_ref.shape[0] - 1

    @pl.when(ci == 0)
    def _():
        st_ref[...] = jnp.zeros_like(st_ref)

    la = la_ref[...]
    hi = la.astype(BF16)
    lo = (la - hi.astype(F32)).astype(BF16)
    e_all = jnp.dot(mat_ref[...], jnp.concatenate([hi, lo], axis=1), preferred_element_type=F32)
    e_all = e_all[:, :dk] + e_all[:, dk:]

    q = q_ref[...] * (dk ** -0.5)
    k = k_ref[...]
    v = v_ref[...]
    att = jnp.where(mask_ref[nlev] > 0.0, _bdot_nt(q, k), 0.0)
    for l in range(nlev):
        ex = jnp.exp(e_all[l * c:(l + 1) * c])
        att = att + jnp.where(mask_ref[l] > 0.0, _bdot_nt(q * ex, k * ex), 0.0)

    b = e_all[nlev * c:(nlev + 1) * c]
    b_end = b[c - 1:c, :]
    st = st_ref[...]
    o = _bdot(att, v) + _bdot_nt(q * jnp.exp(b), st)
    st_new = st * jnp.exp(b_end) + _bdot_tn(v, k * jnp.exp(b_end - b))
    st_ref[...] = st_new

    on = o * lax.rsqrt(jnp.mean(o * o, axis=-1, keepdims=True) + EPS) * gn_ref[...]
    o_ref[...] = (on * _silu(r_ref[...])).astype(o_ref.dtype)

    @pl.when(ci == pl.num_programs(2) - 1)
    def _():
        s_out_ref[...] = st_new.T


def gla_chunked(qkvr, log_a, gnorm, batch, seq_len, c=128):
    n = qkvr.shape[0]
    hk = log_a.shape[1]
    dk = hk // GLA_HEADS
    hv = (qkvr.shape[1] - 2 * hk) // 2
    dv = hv // GLA_HEADS
    nc = seq_len // c
    mats, masks = _gla_level_matrices(c)
    k_off, v_off, r_off = hk // dk, 2 * hk // dv, (2 * hk + hv) // dv
    row = lambda b, h, ci: b * nc + ci
    return pl.pallas_call(
        functools.partial(_gla_chunk_kernel, c=c, dk=dk),
        grid=(batch, GLA_HEADS, nc),
        in_specs=[pl.BlockSpec((c, dk), lambda b, h, ci: (row(b, h, ci), h)),
                  pl.BlockSpec((c, dk), lambda b, h, ci: (row(b, h, ci), k_off + h)),
                  pl.BlockSpec((c, dv), lambda b, h, ci: (row(b, h, ci), v_off + h)),
                  pl.BlockSpec((c, dv), lambda b, h, ci: (row(b, h, ci), r_off + h)),
                  pl.BlockSpec((c, dk), lambda b, h, ci: (row(b, h, ci), h)),
                  pl.BlockSpec(mats.shape, lambda b, h, ci: (0, 0)),
                  pl.BlockSpec(masks.shape, lambda b, h, ci: (0, 0, 0)),
                  pl.BlockSpec((1, dv), lambda b, h, ci: (0, 0))],
        out_specs=[pl.BlockSpec((c, dv), lambda b, h, ci: (row(b, h, ci), h)),
                   pl.BlockSpec((None, None, dk, dv), lambda b, h, ci: (b, h, 0, 0))],
        out_shape=[jax.ShapeDtypeStruct((n, hv), BF16),
                   jax.ShapeDtypeStruct((batch, GLA_HEADS, dk, dv), F32)],
        scratch_shapes=[pltpu.VMEM((dv, dk), F32)],
        compiler_params=_cparams("parallel", "parallel", "arbitrary"),
        name="gla_chunked",
    )(qkvr, qkvr, qkvr, qkvr, log_a, jnp.asarray(mats, BF16), jnp.asarray(masks), gnorm.reshape(1, dv))


def _gla_step_kernel(q_ref, k_ref, g_ref, v_ref, r_ref, s_ref, gn_ref, o_ref, s_out_ref, *, dk):
    s_new = jnp.exp(g_ref[...]) * s_ref[...] + k_ref[...] * v_ref[...]
    s_out_ref[...] = s_new
    o = jnp.sum(q_ref[...] * (dk ** -0.5) * s_new, axis=0, keepdims=True)
    on = o * lax.rsqrt(jnp.mean(o * o, axis=-1, keepdims=True) + EPS) * gn_ref[...]
    o_ref[...] = (on * _silu(r_ref[...])).astype(o_ref.dtype)


def gla_step(qkvr, log_a, state, gnorm):
    batch = qkvr.shape[0]
    hk = log_a.shape[1]
    dk = hk // GLA_HEADS
    hv = (qkvr.shape[1] - 2 * hk) // 2
    dv = hv // GLA_HEADS
    col = lambda a: a.reshape(batch, GLA_HEADS, dk, 1)
    rowv = lambda a: a.reshape(batch, GLA_HEADS, 1, dv)
    q, k, v, r = (qkvr[:, :hk], qkvr[:, hk:2 * hk], qkvr[:, 2 * hk:2 * hk + hv], qkvr[:, 2 * hk + hv:])
    cspec = pl.BlockSpec((None, None, dk, 1), lambda b, h: (b, h, 0, 0))
    rspec = pl.BlockSpec((None, None, 1, dv), lambda b, h: (b, h, 0, 0))
    sspec = pl.BlockSpec((None, None, dk, dv), lambda b, h: (b, h, 0, 0))
    o, s_new = pl.pallas_call(
        functools.partial(_gla_step_kernel, dk=dk),
        grid=(batch, GLA_HEADS),
        in_specs=[cspec, cspec, cspec, rspec, rspec, sspec, pl.BlockSpec((1, dv), lambda b, h: (0, 0))],
        out_specs=[rspec, sspec],
        out_shape=[jax.ShapeDtypeStruct((batch, GLA_HEADS, 1, dv), BF16),
                   jax.ShapeDtypeStruct(state.shape, F32)],
        compiler_params=_cparams("parallel", "parallel"),
        name="gla_step",
    )(col(q), col(k), col(log_a), rowv(v), rowv(r), state, gnorm.reshape(1, dv))
    return o.reshape(batch, hv), s_new


def _out_proj_kernel(o_ref, w_ref, x_ref, gt_ref, xo_ref):
    xo_ref[...] = x_ref[...] + gt_ref[...] * _bdot(o_ref[...], w_ref[...])


def out_project(o, w_out, x, mod, tm, seq_len):
    n, d = x.shape
    kdim = o.shape[1]
    return pl.pallas_call(
        _out_proj_kernel,
        grid=(n // tm,),
        in_specs=[pl.BlockSpec((tm, kdim), lambda i: (i, 0)),
                  pl.BlockSpec((kdim, d), lambda i: (0, 0)),
                  pl.BlockSpec((tm, d), lambda i: (i, 0)),
                  _mod_spec(mod, 2, tm, seq_len, d)],
        out_specs=pl.BlockSpec((tm, d), lambda i: (i, 0)),
        out_shape=jax.ShapeDtypeStruct((n, d), F32),
        compiler_params=_cparams("parallel"),
        name="out_proj",
    )(o, w_out.astype(BF16), x, mod)


def _modulate_kernel(x_ref, g_ref, sh_ref, sc_ref, h_ref):
    h_ref[...] = _rms(x_ref[...], g_ref[...]) * (1.0 + sc_ref[...]) + sh_ref[...]


def modulate_rows(x, mod, norm_g, tm, seq_len):
    n, d = x.shape
    return pl.pallas_call(
        _modulate_kernel,
        grid=(n // tm,),
        in_specs=[pl.BlockSpec((tm, d), lambda i: (i, 0)),
                  pl.BlockSpec((1, d), lambda i: (0, 0)),
                  _mod_spec(mod, 3, tm, seq_len, d),
                  _mod_spec(mod, 4, tm, seq_len, d)],
        out_specs=pl.BlockSpec((tm, d), lambda i: (i, 0)),
        out_shape=jax.ShapeDtypeStruct((n, d), F32),
        compiler_params=_cparams("parallel"),
        name="modulate_ffn",
    )(x, norm_g.reshape(1, d), mod, mod)


def _dsa_proj_kernel(x_ref, g_ref, sh_ref, sc_ref, w_ref, lg_ref, lb_ref,
                     q_ref, k_ref, v_ref, qi_ref, ki_ref, wi_ref, *, cuts):
    h = _rms(x_ref[...], g_ref[...]) * (1.0 + sc_ref[...]) + sh_ref[...]
    y = _bdot(h, w_ref[...])
    c0, c1, c2, c3, c4, c5 = cuts
    q_ref[...] = y[:, :c0]
    k_ref[...] = y[:, c0:c1]
    v_ref[...] = y[:, c1:c2]
    qi_ref[...] = y[:, c2:c3]
    ki = y[:, c3:c4]
    mu = jnp.mean(ki, axis=-1, keepdims=True)
    var = jnp.mean((ki - mu) * (ki - mu), axis=-1, keepdims=True)
    ki_ref[...] = (ki - mu) * lax.rsqrt(var + EPS) * lg_ref[...] + lb_ref[...]
    wi_ref[...] = y[:, c4:c5]


def dsa_project(x, mod, norm_g, w_in, ln_g, ln_b, tm, seq_len):
    n, d = x.shape
    nq = d
    nkv = ATT_KV_HEADS * (d // ATT_HEADS)
    nqi = IDX_HEADS * IDX_DH
    cuts = (nq, nq + nkv, nq + 2 * nkv, nq + 2 * nkv + nqi, nq + 2 * nkv + nqi + IDX_DH,
            nq + 2 * nkv + nqi + IDX_DH + IDX_HEADS)
    n_pad = (-w_in.shape[1]) % LANES
    w = jnp.pad(w_in, ((0, 0), (0, n_pad))).astype(BF16)
    widths = (nq, nkv, nkv, nqi, IDX_DH, IDX_HEADS)
    return pl.pallas_call(
        functools.partial(_dsa_proj_kernel, cuts=cuts),
        grid=(n // tm,),
        in_specs=[pl.BlockSpec((tm, d), lambda i: (i, 0)),
                  pl.BlockSpec((1, d), lambda i: (0, 0)),
                  _mod_spec(mod, 0, tm, seq_len, d),
                  _mod_spec(mod, 1, tm, seq_len, d),
                  pl.BlockSpec(w.shape, lambda i: (0, 0)),
                  pl.BlockSpec((1, IDX_DH), lambda i: (0, 0)),
                  pl.BlockSpec((1, IDX_DH), lambda i: (0, 0))],
        out_specs=[pl.BlockSpec((tm, wd), lambda i: (i, 0)) for wd in widths],
        out_shape=[jax.ShapeDtypeStruct((n, wd), F32) for wd in widths],
        compiler_params=_cparams("parallel"),
        name="dsa_proj",
    )(x, norm_g.reshape(1, d), mod, mod, w, ln_g.reshape(1, IDX_DH), ln_b.reshape(1, IDX_DH))


def _t5_bucket_np(dist):
    dist = np.maximum(dist, 0)
    max_exact = N_BUCKETS // 2
    ratio = np.log(np.maximum(dist, max_exact).astype(np.float32) / max_exact) / math.log(MAX_DISTANCE / max_exact)
    large = np.minimum(max_exact + (ratio * (N_BUCKETS - max_exact)).astype(np.int32), N_BUCKETS - 1)
    return np.where(dist < max_exact, dist, large).astype(np.int32)


def _bias_table_kernel(rb_ref, bk_ref, o_ref):
    h = pl.program_id(0)
    for kind in range(3):
        bk = bk_ref[kind]
        acc = jnp.zeros(bk.shape, F32)
        for b in range(N_BUCKETS):
            acc = jnp.where(bk == b, rb_ref[b, h], acc)
        o_ref[kind] = acc


def bias_tables(rel_bias, tq):
    t = np.arange(tq)[:, None]
    s = np.arange(tq)[None, :]
    buckets = np.stack([_t5_bucket_np(t - s), _t5_bucket_np(tq + t - s), _t5_bucket_np(2 * tq + t - s)])
    assert tq >= MAX_DISTANCE and (buckets[2] == N_BUCKETS - 1).all()
    return pl.pallas_call(
        _bias_table_kernel,
        grid=(ATT_HEADS,),
        in_specs=[pl.BlockSpec(memory_space=pltpu.SMEM),
                  pl.BlockSpec(buckets.shape, lambda h: (0, 0, 0))],
        out_specs=pl.BlockSpec((None, 3, tq, tq), lambda h: (h, 0, 0, 0)),
        out_shape=jax.ShapeDtypeStruct((ATT_HEADS, 3, tq, tq), F32),
        compiler_params=_cparams("parallel"),
        name="t5_bias_tiles",
    )(rel_bias, jnp.asarray(buckets))


def _order_key(score):
    score = jnp.where(score == 0.0, 0.0, score)
    bits = pltpu.bitcast(score, I32)
    return bits ^ ((bits >> 31) & 0x7FFFFFFF)


def _dsa_prompt_kernel(q_ref, qi_ref, wi_ref, k_ref, v_ref, ki_ref, bias_ref, tri_ref, o_ref,
                       key_ref, selb_ref, m_ref, acc_ref, *, tq, dh, topk):
    i = pl.program_id(1)
    nk = i + 1
    g_heads = ATT_HEADS // ATT_KV_HEADS
    row_id = lax.broadcasted_iota(I32, (tq, tq), 0)
    col_id = lax.broadcasted_iota(I32, (tq, tq), 1)

    qi = (qi_ref[...] * (IDX_DH ** -0.5)).astype(BF16)
    w8 = wi_ref[...] * (IDX_HEADS ** -0.5)

    def score_body(j, carry):
        kij = ki_ref[pl.ds(pl.multiple_of(j * tq, tq), tq), :].astype(BF16)
        acc = jnp.zeros((tq, tq), F32)
        for h in range(IDX_HEADS):
            lg = _bdot_nt(qi[:, h * IDX_DH:(h + 1) * IDX_DH], kij)
            acc = acc + jnp.maximum(lg, 0.0) * w8[:, h:h + 1]
        valid = (col_id + j * tq) <= (row_id + i * tq)
        key_ref[j] = _order_key(jnp.where(valid, acc, NEG_INF))
        return carry

    lax.fori_loop(0, nk, score_body, 0)

    def count_ge(thr_signed):
        def body(j, acc):
            return acc + jnp.where(key_ref[j] >= thr_signed, 1.0, 0.0)
        tile = lax.fori_loop(0, nk, body, jnp.zeros((tq, tq), F32))
        return jnp.sum(tile, axis=1, keepdims=True)

    def bit_body(it, ans):
        cand = ans | (jnp.int32(1) << (31 - it))
        cnt = count_ge(cand ^ INT_MIN)
        return jnp.where(cnt >= topk, cand, ans)

    ans = lax.fori_loop(0, 32, bit_body, jnp.zeros((tq, 1), I32))
    thr = ans ^ INT_MIN

    def gt_body(j, acc):
        return acc + jnp.where(key_ref[j] > thr, 1.0, 0.0)
    n_gt = jnp.sum(lax.fori_loop(0, nk, gt_body, jnp.zeros((tq, tq), F32)), axis=1, keepdims=True)
    need = topk - n_gt

    def sel_body(j, run):
        kj = key_ref[j]
        tie = kj == thr
        tie_f = jnp.where(tie, 1.0, 0.0)
        before = run + jnp.dot(tie_f.astype(BF16), tri_ref[...], preferred_element_type=F32)
        take = jnp.where(kj > thr, 0.0, jnp.where(tie, jnp.where(before < need, 0.0, NEG_INF), NEG_INF))
        valid = (col_id + j * tq) <= (row_id + i * tq)
        selb_ref[j] = jnp.where(valid, take, NEG_INF)
        return run + jnp.sum(tie_f, axis=1, keepdims=True)

    lax.fori_loop(0, nk, sel_body, jnp.zeros((tq, 1), F32))

    @pl.when(nk % 2 == 1)
    def _():
        selb_ref[jnp.minimum(nk, selb_ref.shape[0] - 1)] = jnp.full((tq, tq), NEG_INF, F32)

    q = q_ref[...] * (dh ** -0.5)
    n_pairs = (nk + 1) // 2
    ones_cols = jnp.ones((2 * tq, LANES - dh), BF16)
    for g in range(ATT_KV_HEADS):
        qg = jnp.concatenate([q[:, (g * g_heads + a) * dh:(g * g_heads + a + 1) * dh]
                              for a in range(g_heads)], axis=0).astype(BF16)
        m_ref[...] = jnp.full(m_ref.shape, NEG_INF, F32)
        acc_ref[...] = jnp.zeros(acc_ref.shape, F32)

        def att_body(jj, carry):
            base = pl.multiple_of(jj * 2 * tq, 2 * tq)
            kt = k_ref[pl.ds(base, 2 * tq), g * dh:(g + 1) * dh]
            vt = v_ref[pl.ds(base, 2 * tq), g * dh:(g + 1) * dh].astype(BF16)
            s = _bdot_nt(qg, kt)
            j0 = 2 * jj
            d0 = jnp.clip(i - j0, 0, 2)
            d1 = jnp.clip(i - j0 - 1, 0, 2)
            add = jnp.concatenate(
                [jnp.concatenate([bias_ref[g * g_heads + a, d0] + selb_ref[j0],
                                  bias_ref[g * g_heads + a, d1] + selb_ref[j0 + 1]], axis=1)
                 for a in range(g_heads)], axis=0)
            s = s + add
            m_old = m_ref[...]
            m_new = jnp.maximum(m_old, jnp.max(s, axis=1, keepdims=True))
            m_safe = jnp.where(m_new == NEG_INF, 0.0, m_new)
            alpha = jnp.exp(m_old - m_safe)
            p = jnp.exp(s - m_safe).astype(BF16)
            v_aug = jnp.concatenate([vt, ones_cols], axis=1)
            acc_ref[...] = alpha * acc_ref[...] + jnp.dot(p, v_aug, preferred_element_type=F32)
            m_ref[...] = m_new
            return carry

        lax.fori_loop(0, n_pairs, att_body, 0)
        acc = acc_ref[...]
        og = acc[:, :dh] / acc[:, dh:dh + 1]
        for a in range(g_heads):
            hh = g * g_heads + a
            o_ref[:, hh * dh:(hh + 1) * dh] = og[a * tq:(a + 1) * tq, :].astype(o_ref.dtype)


def dsa_prompt(q, k, v, qi, ki, wi, bias_tab, batch, seq_len, tq=128):
    n, d = q.shape
    dh = d // ATT_HEADS
    nq = seq_len // tq
    assert nq % 2 == 0
    topk = min(TOPK_MAX, seq_len // 4)
    tri = np.triu(np.ones((tq, tq), np.float32), 1)
    g_heads = ATT_HEADS // ATT_KV_HEADS
    seq_spec = lambda width: pl.BlockSpec((seq_len, width), lambda b, i: (b, 0))
    blk_spec = lambda width: pl.BlockSpec((tq, width), lambda b, i: (b * nq + i, 0))
    return pl.pallas_call(
        functools.partial(_dsa_prompt_kernel, tq=tq, dh=dh, topk=float(topk)),
        grid=(batch, nq),
        in_specs=[blk_spec(d), blk_spec(qi.shape[1]), blk_spec(wi.shape[1]),
                  seq_spec(k.shape[1]), seq_spec(v.shape[1]), seq_spec(ki.shape[1]),
                  pl.BlockSpec(bias_tab.shape, lambda b, i: (0, 0, 0, 0)),
                  pl.BlockSpec((tq, tq), lambda b, i: (0, 0))],
        out_specs=blk_spec(d),
        out_shape=jax.ShapeDtypeStruct((n, d), BF16),
        scratch_shapes=[pltpu.VMEM((nq, tq, tq), I32),
                        pltpu.VMEM((nq, tq, tq), F32),
                        pltpu.VMEM((g_heads * tq, 1), F32),
                        pltpu.VMEM((g_heads * tq, LANES), F32)],
        compiler_params=_cparams("parallel", "arbitrary"),
        name="dsa_prompt",
    )(q, qi, wi, k, v, ki, bias_tab, jnp.asarray(tri, BF16))


def _router_kernel(h_ref, wr_ref, rb_ref, tri_ref, eidx_ref, wsel_ref, pos_ref, cnt_ref, carry_ref, *, n_exp):
    @pl.when(pl.program_id(0) == 0)
    def _():
        carry_ref[...] = jnp.zeros_like(carry_ref)

    h_hi, h_mid, h_lo = _split3(h_ref[...])
    w_hi, w_mid, w_lo = _split3(wr_ref[...])
    nt = lambda a, b: lax.dot_general(a, b, (((1,), (1,)), ((), ())), preferred_element_type=F32)
    logits = ((nt(w_lo, h_hi) + nt(w_hi, h_lo) + nt(w_mid, h_mid))
              + (nt(w_hi, h_mid) + nt(w_mid, h_hi))) + nt(w_hi, h_hi)
    s = _sigmoid(logits)
    sel = s + rb_ref[...]
    tm = sel.shape[1]
    gsz = n_exp // N_GROUPS
    io_g = lax.broadcasted_iota(I32, (gsz, tm), 0)

    gs = []
    for g in range(N_GROUPS):
        grp = sel[g * gsz:(g + 1) * gsz, :]
        m1 = jnp.max(grp, axis=0, keepdims=True)
        i1 = jnp.min(jnp.where(grp == m1, io_g, gsz), axis=0, keepdims=True)
        m2 = jnp.max(jnp.where(io_g == i1, NEG_INF, grp), axis=0, keepdims=True)
        gs.append(m1 + m2)
    masked = []
    for g in range(N_GROUPS):
        rank = jnp.zeros((1, tm), F32)
        for o in range(N_GROUPS):
            if o == g:
                continue
            ahead = (gs[o] >= gs[g]) if o < g else (gs[o] > gs[g])
            rank = rank + jnp.where(ahead, 1.0, 0.0)
        keep = jnp.where(rank < TOPK_GROUPS, 0.0, NEG_INF)
        masked.append(sel[g * gsz:(g + 1) * gsz, :] + keep)
    msel = jnp.concatenate(masked, axis=0)

    io_e = lax.broadcasted_iota(I32, (n_exp, tm), 0)
    chosen = jnp.zeros((n_exp, tm), F32)
    picks, weights = [], []
    for _ in range(TOP_K):
        m = jnp.max(msel, axis=0, keepdims=True)
        ei = jnp.min(jnp.where(msel == m, io_e, n_exp), axis=0, keepdims=True)
        pick = io_e == ei
        weights.append(jnp.sum(jnp.where(pick, s, 0.0), axis=0, keepdims=True))
        picks.append(ei)
        chosen = jnp.where(pick, 1.0, chosen)
        msel = jnp.where(pick, NEG_INF, msel)
    wsum = weights[0]
    for w in weights[1:]:
        wsum = wsum + w

    rank_in_expert = carry_ref[...] + jnp.dot(chosen.astype(BF16), tri_ref[...], preferred_element_type=F32)
    carry_new = carry_ref[...] + jnp.sum(chosen, axis=1, keepdims=True)
    carry_ref[...] = carry_new
    cnt_ref[...] = carry_new
    for kk in range(TOP_K):
        eidx_ref[kk:kk + 1, :] = picks[kk]
        wsel_ref[kk:kk + 1, :] = weights[kk] / wsum * ROUTE_SCALE
        pk = jnp.sum(jnp.where(io_e == picks[kk], rank_in_expert, 0.0), axis=0, keepdims=True)
        pos_ref[kk:kk + 1, :] = pk.astype(I32)


def moe_route(h, w_router, router_bias, tm):
    n, d = h.shape
    n_exp = w_router.shape[1]
    tri = np.triu(np.ones((tm, tm), np.float32), 1)
    row8 = lambda dt: jax.ShapeDtypeStruct((TOP_K, n), dt)
    return pl.pallas_call(
        functools.partial(_router_kernel, n_exp=n_exp),
        grid=(n // tm,),
        in_specs=[pl.BlockSpec((tm, d), lambda i: (i, 0)),
                  pl.BlockSpec((n_exp, d), lambda i: (0, 0)),
                  pl.BlockSpec((n_exp, 1), lambda i: (0, 0)),
                  pl.BlockSpec((tm, tm), lambda i: (0, 0))],
        out_specs=[pl.BlockSpec((TOP_K, tm), lambda i: (0, i)),
                   pl.BlockSpec((TOP_K, tm), lambda i: (0, i)),
                   pl.BlockSpec((TOP_K, tm), lambda i: (0, i)),
                   pl.BlockSpec((n_exp, 1), lambda i: (0, 0))],
        out_shape=[row8(I32), row8(F32), row8(I32), jax.ShapeDtypeStruct((n_exp, 1), F32)],
        scratch_shapes=[pltpu.VMEM((n_exp, 1), F32)],
        compiler_params=_cparams("arbitrary"),
        name="moe_router",
    )(h, w_router.T, router_bias.reshape(n_exp, 1), jnp.asarray(tri, BF16))


def _dispatch_kernel(dest_ref, h_ref, xs_ref, sem):
    tm = h_ref.shape[0]

    def row_copy(t, dst_row):
        return pltpu.make_async_copy(h_ref.at[pl.ds(t, 1), :], xs_ref.at[pl.ds(dst_row, 1), :], sem)

    def issue(t, c):
        for kk in range(TOP_K):
            row_copy(t, dest_ref[kk, t]).start()
        return c

    def drain(t, c):
        for kk in range(TOP_K):
            row_copy(t, dest_ref[kk, t]).wait()
        return c

    lax.fori_loop(0, tm, issue, 0)
    lax.fori_loop(0, tm, drain, 0)


def moe_dispatch(h, dest, tm):
    n, d = h.shape
    return pl.pallas_call(
        _dispatch_kernel,
        grid=(n // tm,),
        in_specs=[pl.BlockSpec((TOP_K, tm), lambda i: (0, i), memory_space=pltpu.SMEM),
                  pl.BlockSpec((tm, d), lambda i: (i, 0))],
        out_specs=pl.BlockSpec(memory_space=pl.ANY),
        out_shape=jax.ShapeDtypeStruct((n * TOP_K, d), F32),
        scratch_shapes=[pltpu.SemaphoreType.DMA(())],
        compiler_params=_cparams("arbitrary"),
        name="moe_dispatch",
    )(dest, h)


def _expert_kernel(blk_ref, exp_ref, lo_ref, hi_ref, x_ref, wg_ref, wu_ref, wd_ref, y_ref):
    j = pl.program_id(0)
    blk = x_ref.shape[0]
    lo = lo_ref[j] - blk_ref[j] * blk
    hi = hi_ref[j] - blk_ref[j] * blk

    @pl.when(hi > lo)
    def _():
        x = x_ref[...].astype(BF16)
        g = jnp.dot(x, wg_ref[...].astype(BF16), preferred_element_type=F32)
        u = jnp.dot(x, wu_ref[...].astype(BF16), preferred_element_type=F32)
        y = _bdot(_silu(g) * u, wd_ref[...])
        row = lax.broadcasted_iota(I32, y.shape, 0)
        y = jnp.where(jnp.logical_and(row >= lo, row < hi), y, 0.0)

        @pl.when(lo == 0)
        def _():
            y_ref[...] = y

        @pl.when(lo != 0)
        def _():
            y_ref[...] += y


def moe_experts(xs, seg_blk, seg_exp, seg_lo, seg_hi, w_gate, w_up, w_down, blk):
    n_rows, d = xs.shape
    ff = w_gate.shape[2]
    wspec = lambda shape: pl.BlockSpec((None,) + shape, lambda j, sb, se, lo, hi: (se[j], 0, 0))
    return pl.pallas_call(
        _expert_kernel,
        grid_spec=pltpu.PrefetchScalarGridSpec(
            num_scalar_prefetch=4, grid=(seg_blk.shape[0],),
            in_specs=[pl.BlockSpec((blk, d), lambda j, sb, se, lo, hi: (sb[j], 0)),
                      wspec((d, ff)), wspec((d, ff)), wspec((ff, d))],
            out_specs=pl.BlockSpec((blk, d), lambda j, sb, se, lo, hi: (sb[j], 0))),
        out_shape=jax.ShapeDtypeStruct((n_rows, d), F32),
        compiler_params=_cparams("arbitrary"),
        name="moe_experts",
    )(seg_blk, seg_exp, seg_lo, seg_hi, xs, w_gate, w_up, w_down)


def _combine_kernel(dest_ref, ys_ref, wsel_ref, h_ref, x_ref, gt_ref, sg_ref, su_ref, sd_ref, nf_ref,
                    o_ref, buf_ref, sem, *, final_norm):
    tm = h_ref.shape[0]

    def row_copy(t, kk):
        return pltpu.make_async_copy(ys_ref.at[pl.ds(dest_ref[kk, t], 1), :],
                                     buf_ref.at[pl.ds(kk * tm + t, 1), :], sem)

    def issue(t, c):
        for kk in range(TOP_K):
            row_copy(t, kk).start()
        return c

    def drain(t, c):
        for kk in range(TOP_K):
            row_copy(t, kk).wait()
        return c

    lax.fori_loop(0, tm, issue, 0)
    h = h_ref[...].astype(BF16)
    g = jnp.dot(h, sg_ref[...], preferred_element_type=F32)
    u = jnp.dot(h, su_ref[...], preferred_element_type=F32)
    y = _bdot(_silu(g) * u, sd_ref[...])
    lax.fori_loop(0, tm, drain, 0)
    w = wsel_ref[...]
    for kk in range(TOP_K):
        y = y + w[:, kk:kk + 1] * buf_ref[kk * tm:(kk + 1) * tm, :]
    x_new = x_ref[...] + gt_ref[...] * y
    o_ref[...] = _rms(x_new, nf_ref[...]) if final_norm else x_new


def moe_combine(ys, dest, wsel_t, h, x, mod, ws_gate, ws_up, ws_down, norm_final, final_norm, tm, seq_len):
    n, d = x.shape
    ff = ws_gate.shape[1]
    return pl.pallas_call(
        functools.partial(_combine_kernel, final_norm=final_norm),
        grid=(n // tm,),
        in_specs=[pl.BlockSpec((TOP_K, tm), lambda i: (0, i), memory_space=pltpu.SMEM),
                  pl.BlockSpec(memory_space=pl.ANY),
                  pl.BlockSpec((tm, TOP_K), lambda i: (i, 0)),
                  pl.BlockSpec((tm, d), lambda i: (i, 0)),
                  pl.BlockSpec((tm, d), lambda i: (i, 0)),
                  _mod_spec(mod, 5, tm, seq_len, d),
                  pl.BlockSpec((d, ff), lambda i: (0, 0)),
                  pl.BlockSpec((d, ff), lambda i: (0, 0)),
                  pl.BlockSpec((ff, d), lambda i: (0, 0)),
                  pl.BlockSpec((1, d), lambda i: (0, 0))],
        out_specs=pl.BlockSpec((tm, d), lambda i: (i, 0)),
        out_shape=jax.ShapeDtypeStruct((n, d), F32),
        scratch_shapes=[pltpu.VMEM((TOP_K * tm, d), F32), pltpu.SemaphoreType.DMA(())],
        compiler_params=_cparams("arbitrary"),
        name="moe_combine",
    )(dest, ys, wsel_t, h, x, mod, ws_gate.astype(BF16), ws_up.astype(BF16), ws_down.astype(BF16),
      norm_final.reshape(1, d))


def moe_ffn(x, mod, norm_g, w_router, router_bias, w_gate, w_up, w_down, ws_gate, ws_up, ws_down,
            norm_final, final_norm, tm, seq_len, blk):
    n, d = x.shape
    n_exp = w_router.shape[1]
    h = modulate_rows(x, mod, norm_g, tm, seq_len)
    eidx, wsel, pos, counts = moe_route(h, w_router, router_bias, tm)
    n_rows = n * TOP_K
    n_blocks = n_rows // blk
    counts = counts.reshape(n_exp).astype(I32)
    end = jnp.cumsum(counts)
    start = end - counts
    dest = start[eidx] + pos
    seg_lo = jnp.sort(jnp.concatenate([jnp.arange(n_blocks, dtype=I32) * blk, start[1:]]))
    seg_hi = jnp.concatenate([seg_lo[1:], jnp.full((1,), n_rows, I32)])
    seg_blk = jnp.minimum(seg_lo // blk, n_blocks - 1)
    seg_exp = jnp.minimum(jnp.searchsorted(end, seg_lo, side='right'), n_exp - 1).astype(I32)
    xs = moe_dispatch(h, dest, tm)
    ys = moe_experts(xs, seg_blk, seg_exp, seg_lo, seg_hi, w_gate, w_up, w_down, blk)
    return moe_combine(ys, dest, wsel.T, h, x, mod, ws_gate, ws_up, ws_down, norm_final, final_norm,
                       min(tm, 128), seq_len)


def _t5_bucket(dist):
    dist = jnp.maximum(dist, 0)
    max_exact = N_BUCKETS // 2
    ratio = jnp.log(jnp.maximum(dist, max_exact).astype(F32) / max_exact) / math.log(MAX_DISTANCE / max_exact)
    large = jnp.minimum(max_exact + (ratio * (N_BUCKETS - max_exact)).astype(I32), N_BUCKETS - 1)
    return jnp.where(dist < max_exact, dist, large)


def dsa_sample_attend(q, k, v, qi, ki, wi, cache_k, cache_v, cache_ki, page_table, rel_bias):
    b = q.shape[0]
    page = cache_k.shape[1]
    dh = q.shape[1] // ATT_HEADS
    past = page_table.shape[1] * page
    topk = min(TOPK_MAX, (past + 1) // 4)
    ki_all = jnp.concatenate([cache_ki[page_table].reshape(b, past, IDX_DH), ki[:, None, :]], axis=1)
    logits = jnp.einsum('bhd,bsd->bhs', qi.reshape(b, IDX_HEADS, IDX_DH), ki_all) * IDX_DH ** -0.5
    sc = jnp.einsum('bhs,bh->bs', jax.nn.relu(logits), wi * IDX_HEADS ** -0.5)
    _, idx = lax.top_k(sc, topk)
    bidx = jnp.arange(b)[:, None]
    pidx = jnp.minimum(idx, past - 1)
    phys = page_table[bidx, pidx // page] * page + pidx % page
    in_past = (idx < past)[..., None, None]
    kn = k.reshape(b, 1, ATT_KV_HEADS, dh)
    vn = v.reshape(b, 1, ATT_KV_HEADS, dh)
    k_sel = jnp.where(in_past, cache_k.reshape(-1, ATT_KV_HEADS, dh)[phys], kn)
    v_sel = jnp.where(in_past, cache_v.reshape(-1, ATT_KV_HEADS, dh)[phys], vn)
    g = ATT_HEADS // ATT_KV_HEADS
    qg = q.reshape(b, ATT_KV_HEADS, g, dh)
    lg = jnp.einsum('bkgd,bskd->bkgs', qg, k_sel) * dh ** -0.5
    bias = rel_bias[_t5_bucket(past - idx)].reshape(b, topk, ATT_KV_HEADS, g).transpose(0, 2, 3, 1)
    p = jax.nn.softmax(lg + bias, axis=-1)
    return jnp.einsum('bkgs,bskd->bkgd', p, v_sel).reshape(b, ATT_HEADS * dh)


def kernel(x_prompt, x_sample, c_prompt, c_sample, state_gla, cache_k, cache_v, cache_idx_k, page_table,
           rel_bias, w_ada, b_ada, norm_mix, norm_ffn, norm_final,
           gla_w_in, gla_w_g2, gla_b_g2, gla_gnorm, gla_w_out,
           dsa_w_in, dsa_idx_ln_g, dsa_idx_ln_b, dsa_w_out,
           moe_w_router, moe_router_bias, moe_w_gate, moe_w_up, moe_w_down,
           shared_w_gate, shared_w_up, shared_w_down):
    bp, t, d = x_prompt.shape
    bs = x_sample.shape[0]
    depth = w_ada.shape[0]
    dh = d // ATT_HEADS
    tm_p = min(256, t)
    tq = 128
    mods = ada_mod_all(jnp.concatenate([c_prompt, c_sample], axis=0), w_ada, b_ada)
    xp = x_prompt.reshape(bp * t, d)
    xs = x_sample.reshape(bs, d)
    bias_tab = bias_tables(rel_bias, tq)
    gla_p, gla_s, kp_l, vp_l, kip_l, ks_l, vs_l, kis_l = [], [], [], [], [], [], [], []
    for i in range(depth):
        mod_p = mods[i, :bp].reshape(bp, 1, 6 * d)
        mod_s = mods[i, bp:]
        j = i // 2
        if i % 2 == 0:
            qkvr_p, la_p = gla_project(xp, mod_p, norm_mix[i], gla_w_in[j], gla_w_g2[j], gla_b_g2[j], tm_p, t)
            o_p, s_p = gla_chunked(qkvr_p, la_p, gla_gnorm[j], bp, t)
            qkvr_s, la_s = gla_project(xs, mod_s, norm_mix[i], gla_w_in[j], gla_w_g2[j], gla_b_g2[j], bs, 1)
            o_s, s_s = gla_step(qkvr_s, la_s, state_gla[j], gla_gnorm[j])
            gla_p.append(s_p)
            gla_s.append(s_s)
            w_out = gla_w_out[j]
        else:
            dp = (dsa_w_in[j], dsa_idx_ln_g[j], dsa_idx_ln_b[j])
            q_p, k_p, v_p, qi_p, ki_p, wi_p = dsa_project(xp, mod_p, norm_mix[i], *dp, tm_p, t)
            o_p = dsa_prompt(q_p, k_p, v_p, qi_p, ki_p, wi_p, bias_tab, bp, t, tq)
            q_s, k_s, v_s, qi_s, ki_s, wi_s = dsa_project(xs, mod_s, norm_mix[i], *dp, bs, 1)
            o_s = dsa_sample_attend(q_s, k_s, v_s, qi_s, ki_s, wi_s, cache_k[j], cache_v[j], cache_idx_k[j],
                                    page_table, rel_bias).astype(BF16)
            kp_l.append(k_p.reshape(bp, t, ATT_KV_HEADS, dh))
            vp_l.append(v_p.reshape(bp, t, ATT_KV_HEADS, dh))
            kip_l.append(ki_p.reshape(bp, t, IDX_DH))
            ks_l.append(k_s.reshape(bs, 1, ATT_KV_HEADS, dh))
            vs_l.append(v_s.reshape(bs, 1, ATT_KV_HEADS, dh))
            kis_l.append(ki_s.reshape(bs, 1, IDX_DH))
            w_out = dsa_w_out[j]
        xp = out_project(o_p, w_out, xp, mod_p, tm_p, t)
        xs = out_project(o_s, w_out, xs, mod_s, bs, 1)
        last = i == depth - 1
        mo = (moe_w_router[i], moe_router_bias[i], moe_w_gate[i], moe_w_up[i], moe_w_down[i],
              shared_w_gate[i], shared_w_up[i], shared_w_down[i], norm_final, last)
        xp = moe_ffn(xp, mod_p, norm_ffn[i], *mo, tm_p, t, 256)
        xs = moe_ffn(xs, mod_s, norm_ffn[i], *mo, bs, 1, 32)
    return (xp.reshape(bp, t, d), xs.reshape(bs, 1, d), jnp.stack(gla_p), jnp.stack(gla_s),
            jnp.stack(kp_l), jnp.stack(vp_l), jnp.stack(kip_l),
            jnp.stack(ks_l), jnp.stack(vs_l), jnp.stack(kis_l))
```

```python
import functools
import math

import numpy as np
import jax
import jax.numpy as jnp
from jax import lax
from jax.experimental import pallas as pl
from jax.experimental.pallas import tpu as pltpu

F32 = jnp.float32
BF16 = jnp.bfloat16
I32 = jnp.int32

GLA_HEADS = 4
GLA_RANK = 16
GLA_NORMALIZER = 16.0
ATT_HEADS = 16
ATT_KV_HEADS = 4
IDX_HEADS = 8
IDX_DH = 64
TOPK_MAX = 256
N_BUCKETS = 32
MAX_DISTANCE = 128
N_GROUPS = 8
TOPK_GROUPS = 4
TOP_K = 8
ROUTE_SCALE = 2.5
EPS = 1e-6

LANES = 128
VMEM_LIMIT = 56 * 1024 * 1024
NEG_INF = float("-inf")
INT_MIN = -2 ** 31


def _cparams(*sem):
    return pltpu.CompilerParams(dimension_semantics=sem, vmem_limit_bytes=VMEM_LIMIT)


def _bdot(a, b):
    return jnp.dot(a.astype(BF16), b.astype(BF16), preferred_element_type=F32)


def _bdot_nt(a, b):
    return lax.dot_general(a.astype(BF16), b.astype(BF16), (((1,), (1,)), ((), ())),
                           preferred_element_type=F32)


def _bdot_tn(a, b):
    return lax.dot_general(a.astype(BF16), b.astype(BF16), (((0,), (0,)), ((), ())),
                           preferred_element_type=F32)


def _split3(a):
    hi = a.astype(BF16)
    r1 = a - hi.astype(F32)
    mid = r1.astype(BF16)
    lo = (r1 - mid.astype(F32)).astype(BF16)
    return hi, mid, lo


def _silu(x):
    return x * (1.0 / (1.0 + jnp.exp(-x)))


def _sigmoid(x):
    return 1.0 / (1.0 + jnp.exp(-x))


def _rms(x, g):
    return x * lax.rsqrt(jnp.mean(x * x, axis=-1, keepdims=True) + EPS) * g


def _ada_kernel(c_ref, w_ref, b_ref, o_ref):
    c_hi, c_mid, _ = _split3(_silu(c_ref[...]))
    w_hi, w_mid, _ = _split3(w_ref[...])
    dot = lambda a, b: jnp.dot(a, b, preferred_element_type=F32)
    o_ref[...] = (dot(c_hi, w_mid) + dot(c_mid, w_hi)) + dot(c_hi, w_hi) + b_ref[...]


def ada_mod_all(c, w_ada, b_ada, tn=512):
    depth, d, n6 = w_ada.shape
    rows = c.shape[0]
    return pl.pallas_call(
        _ada_kernel,
        grid=(depth, n6 // tn),
        in_specs=[pl.BlockSpec((rows, d), lambda l, j: (0, 0)),
                  pl.BlockSpec((None, d, tn), lambda l, j: (l, 0, j)),
                  pl.BlockSpec((None, 1, tn), lambda l, j: (l, 0, j))],
        out_specs=pl.BlockSpec((None, rows, tn), lambda l, j: (l, 0, j)),
        out_shape=jax.ShapeDtypeStruct((depth, rows, n6), F32),
        compiler_params=_cparams("parallel", "parallel"),
        name="ada_mod",
    )(c, w_ada, b_ada.reshape(depth, 1, n6))


def _mod_spec(mod, which, tm, seq_len, d):
    if mod.ndim == 3:
        per_seq = seq_len // tm
        return pl.BlockSpec((None, 1, d), lambda i: (i // per_seq, 0, which))
    return pl.BlockSpec((tm, d), lambda i: (i, which))


def _gla_proj_kernel(x_ref, g_ref, sh_ref, sc_ref, w_ref, wg2_ref, bg2_ref, qkvr_ref, la_ref, *, n_main):
    h = _rms(x_ref[...], g_ref[...]) * (1.0 + sc_ref[...]) + sh_ref[...]
    y = _bdot(h, w_ref[...])
    qkvr_ref[...] = y[:, :n_main]
    g1 = y[:, n_main:n_main + GLA_RANK]
    z = _bdot(g1, wg2_ref[...]) + bg2_ref[...]
    la_ref[...] = (jnp.minimum(z, 0.0) - jnp.log(1.0 + jnp.exp(-jnp.abs(z)))) * (1.0 / GLA_NORMALIZER)


def gla_project(x, mod, norm_g, w_in, w_g2, b_g2, tm, seq_len):
    n, d = x.shape
    hk = w_g2.shape[1]
    n_main = w_in.shape[1] - GLA_RANK
    n_pad = (-w_in.shape[1]) % LANES
    w = jnp.pad(w_in, ((0, 0), (0, n_pad))).astype(BF16)
    return pl.pallas_call(
        functools.partial(_gla_proj_kernel, n_main=n_main),
        grid=(n // tm,),
        in_specs=[pl.BlockSpec((tm, d), lambda i: (i, 0)),
                  pl.BlockSpec((1, d), lambda i: (0, 0)),
                  _mod_spec(mod, 0, tm, seq_len, d),
                  _mod_spec(mod, 1, tm, seq_len, d),
                  pl.BlockSpec(w.shape, lambda i: (0, 0)),
                  pl.BlockSpec(w_g2.shape, lambda i: (0, 0)),
                  pl.BlockSpec((1, hk), lambda i: (0, 0))],
        out_specs=[pl.BlockSpec((tm, n_main), lambda i: (i, 0)),
                   pl.BlockSpec((tm, hk), lambda i: (i, 0))],
        out_shape=[jax.ShapeDtypeStruct((n, n_main), F32),
                   jax.ShapeDtypeStruct((n, hk), F32)],
        compiler_params=_cparams("parallel"),
        name="gla_proj",
    )(x, norm_g.reshape(1, d), mod, mod, w, w_g2.astype(BF16), b_g2.reshape(1, hk))


def _gla_level_matrices(c):
    levels = int(math.log2(c))
    t = np.arange(c)[:, None]
    u = np.arange(c)[None, :]
    mats, masks = [], []
    for l in range(levels):
        m = 1 << l
        ref = (t // (2 * m)) * (2 * m) + m - 1
        right = (t % (2 * m)) >= m
        mat = np.where(right, (u > ref) & (u <= t), (u > t) & (u <= ref))
        mats.append(mat)
        masks.append((t // (2 * m) == u // (2 * m)) & right & ((u % (2 * m)) < m))
    mats.append(u <= t)
    masks.append(t == u)
    return (np.stack(mats).astype(np.float32).reshape(-1, c), np.stack(masks).astype(np.float32))


def _gla_chunk_kernel(q_ref, k_ref, v_ref, r_ref, la_ref, mat_ref, mask_ref, gn_ref,
                      o_ref, s_out_ref, st_ref, *, c, dk):
    ci = pl.program_id(2)
    nlev = mask_ref.shape[0] - 1

    @pl.when(ci == 0)
    def _():
        st_ref[...] = jnp.zeros_like(st_ref)

    la = la_ref[...]
    hi = la.astype(BF16)
    lo = (la - hi.astype(F32)).astype(BF16)
    e_all = jnp.dot(mat_ref[...], jnp.concatenate([hi, lo], axis=1), preferred_element_type=F32)
    e_all = e_all[:, :dk] + e_all[:, dk:]

    q = q_ref[...] * (dk ** -0.5)
    k = k_ref[...]
    v = v_ref[...]
    att = jnp.where(mask_ref[nlev] > 0.0, _bdot_nt(q, k), 0.0)
    for l in range(nlev):
        ex = jnp.exp(e_all[l * c:(l + 1) * c])
        att = att + jnp.where(mask_ref[l] > 0.0, _bdot_nt(q * ex, k * ex), 0.0)

    b = e_all[nlev * c:(nlev + 1) * c]
    b_end = b[c - 1:c, :]
    st = st_ref[...]
    o = _bdot(att, v) + _bdot_nt(q * jnp.exp(b), st)
    st_new = st * jnp.exp(b_end) + _bdot_tn(v, k * jnp.exp(b_end - b))
    st_ref[...] = st_new

    on = o * lax.rsqrt(jnp.mean(o * o, axis=-1, keepdims=True) + EPS) * gn_ref[...]
    o_ref[...] = (on * _silu(r_ref[...])).astype(o_ref.dtype)

    @pl.when(ci == pl.num_programs(2) - 1)
    def _():
        s_out_ref[...] = st_new.T


def gla_chunked(qkvr, log_a, gnorm, batch, seq_len, c=128):
    n = qkvr.shape[0]
    hk = log_a.shape[1]
    dk = hk // GLA_HEADS
    hv = (qkvr.shape[1] - 2 * hk) // 2
    dv = hv // GLA_HEADS
    nc = seq_len // c
    mats, masks = _gla_level_matrices(c)
    k_off, v_off, r_off = hk // dk, 2 * hk // dv, (2 * hk + hv) // dv
    row = lambda b, h, ci: b * nc + ci
    return pl.pallas_call(
        functools.partial(_gla_chunk_kernel, c=c, dk=dk),
        grid=(batch, GLA_HEADS, nc),
        in_specs=[pl.BlockSpec((c, dk), lambda b, h, ci: (row(b, h, ci), h)),
                  pl.BlockSpec((c, dk), lambda b, h, ci: (row(b, h, ci), k_off + h)),
                  pl.BlockSpec((c, dv), lambda b, h, ci: (row(b, h, ci), v_off + h)),
                  pl.BlockSpec((c, dv), lambda b, h, ci: (row(b, h, ci), r_off + h)),
                  pl.BlockSpec((c, dk), lambda b, h, ci: (row(b, h, ci), h)),
                  pl.BlockSpec(mats.shape, lambda b, h, ci: (0, 0)),
                  pl.BlockSpec(masks.shape, lambda b, h, ci: (0, 0, 0)),
                  pl.BlockSpec((1, dv), lambda b, h, ci: (0, 0))],
        out_specs=[pl.BlockSpec((c, dv), lambda b, h, ci: (row(b, h, ci), h)),
                   pl.BlockSpec((None, None, dk, dv), lambda b, h, ci: (b, h, 0, 0))],
        out_shape=[jax.ShapeDtypeStruct((n, hv), BF16),
                   jax.ShapeDtypeStruct((batch, GLA_HEADS, dk, dv), F32)],
        scratch_shapes=[pltpu.VMEM((dv, dk), F32)],
        compiler_params=_cparams("parallel", "parallel", "arbitrary"),
        name="gla_chunked",
    )(qkvr, qkvr, qkvr, qkvr, log_a, jnp.asarray(mats, BF16), jnp.asarray(masks), gnorm.reshape(1, dv))


def _gla_step_kernel(q_ref, k_ref, g_ref, v_ref, r_ref, s_ref, gn_ref, o_ref, s_out_ref, *, dk):
    s_new = jnp.exp(g_ref[...]) * s_ref[...] + k_ref[...] * v_ref[...]
    s_out_ref[...] = s_new
    o = jnp.sum(q_ref[...] * (dk ** -0.5) * s_new, axis=0, keepdims=True)
    on = o * lax.rsqrt(jnp.mean(o * o, axis=-1, keepdims=True) + EPS) * gn_ref[...]
    o_ref[...] = (on * _silu(r_ref[...])).astype(o_ref.dtype)


def gla_step(qkvr, log_a, state, gnorm):
    batch = qkvr.shape[0]
    hk = log_a.shape[1]
    dk = hk // GLA_HEADS
    hv = (qkvr.shape[1] - 2 * hk) // 2
    dv = hv // GLA_HEADS
    col = lambda a: a.reshape(batch, GLA_HEADS, dk, 1)
    rowv = lambda a: a.reshape(batch, GLA_HEADS, 1, dv)
    q, k, v, r = (qkvr[:, :hk], qkvr[:, hk:2 * hk], qkvr[:, 2 * hk:2 * hk + hv], qkvr[:, 2 * hk + hv:])
    cspec = pl.BlockSpec((None, None, dk, 1), lambda b, h: (b, h, 0, 0))
    rspec = pl.BlockSpec((None, None, 1, dv), lambda b, h: (b, h, 0, 0))
    sspec = pl.BlockSpec((None, None, dk, dv), lambda b, h: (b, h, 0, 0))
    o, s_new = pl.pallas_call(
        functools.partial(_gla_step_kernel, dk=dk),
        grid=(batch, GLA_HEADS),
        in_specs=[cspec, cspec, cspec, rspec, rspec, sspec, pl.BlockSpec((1, dv), lambda b, h: (0, 0))],
        out_specs=[rspec, sspec],
        out_shape=[jax.ShapeDtypeStruct((batch, GLA_HEADS, 1, dv), BF16),
                   jax.ShapeDtypeStruct(state.shape, F32)],
        compiler_params=_cparams("parallel", "parallel"),
        name="gla_step",
    )(col(q), col(k), col(log_a), rowv(v), rowv(r), state, gnorm.reshape(1, dv))
    return o.reshape(batch, hv), s_new


def _out_proj_kernel(o_ref, w_ref, x_ref, gt_ref, xo_ref):
    xo_ref[...] = x_ref[...] + gt_ref[...] * _bdot(o_ref[...], w_ref[...])


def out_project(o, w_out, x, mod, tm, seq_len):
    n, d = x.shape
    kdim = o.shape[1]
    return pl.pallas_call(
        _out_proj_kernel,
        grid=(n // tm,),
        in_specs=[pl.BlockSpec((tm, kdim), lambda i: (i, 0)),
                  pl.BlockSpec((kdim, d), lambda i: (0, 0)),
                  pl.BlockSpec((tm, d), lambda i: (i, 0)),
                  _mod_spec(mod, 2, tm, seq_len, d)],
        out_specs=pl.BlockSpec((tm, d), lambda i: (i, 0)),
        out_shape=jax.ShapeDtypeStruct((n, d), F32),
        compiler_params=_cparams("parallel"),
        name="out_proj",
    )(o, w_out.astype(BF16), x, mod)


def _modulate_kernel(x_ref, g_ref, sh_ref, sc_ref, h_ref):
    h_ref[...] = _rms(x_ref[...], g_ref[...]) * (1.0 + sc_ref[...]) + sh_ref[...]


def modulate_rows(x, mod, norm_g, tm, seq_len):
    n, d = x.shape
    return pl.pallas_call(
        _modulate_kernel,
        grid=(n // tm,),
        in_specs=[pl.BlockSpec((tm, d), lambda i: (i, 0)),
                  pl.BlockSpec((1, d), lambda i: (0, 0)),
                  _mod_spec(mod, 3, tm, seq_len, d),
                  _mod_spec(mod, 4, tm, seq_len, d)],
        out_specs=pl.BlockSpec((tm, d), lambda i: (i, 0)),
        out_shape=jax.ShapeDtypeStruct((n, d), F32),
        compiler_params=_cparams("parallel"),
        name="modulate_ffn",
    )(x, norm_g.reshape(1, d), mod, mod)


def _dsa_proj_kernel(x_ref, g_ref, sh_ref, sc_ref, w_ref, wwi_ref, lg_ref, lb_ref,
                     q_ref, qi_ref, wit_ref, k_ref, v_ref, ki_ref, *seq_refs, cuts, dh):
    h = (_rms(x_ref[...], g_ref[...]) * (1.0 + sc_ref[...]) + sh_ref[...]).astype(BF16)
    y = jnp.dot(h, w_ref[...], preferred_element_type=F32)
    c0, c1, c2, c3, c4 = cuts
    q_ref[...] = (y[:, :c0] * (dh ** -0.5)).astype(BF16)
    qi_ref[...] = (y[:, c2:c3] * (IDX_DH ** -0.5)).astype(BF16)
    wit_ref[...] = _bdot_nt(wwi_ref[...], h) * (IDX_HEADS ** -0.5)
    k = y[:, c0:c1]
    v = y[:, c1:c2]
    ki = y[:, c3:c4]
    mu = jnp.mean(ki, axis=-1, keepdims=True)
    var = jnp.mean((ki - mu) * (ki - mu), axis=-1, keepdims=True)
    kin = (ki - mu) * lax.rsqrt(var + EPS) * lg_ref[...] + lb_ref[...]
    if not seq_refs:
        k_ref[...] = k
        v_ref[...] = v
        ki_ref[...] = kin
    else:
        kib_ref, kg_ref, vt_ref = seq_refs
        vt = v.T
        k_ref[...] = k.T
        v_ref[...] = vt
        ki_ref[...] = jnp.concatenate([kin, jnp.zeros_like(kin)], axis=1).T[:kin.shape[1]]
        kib_ref[...] = kin.astype(BF16)
        for g in range(ATT_KV_HEADS):
            kg_ref[g] = k[:, g * dh:(g + 1) * dh].astype(BF16)
        vt_ref[...] = vt.astype(BF16)


def dsa_project(x, mod, norm_g, w_in, ln_g, ln_b, tm, seq_len, seq_layouts):
    n, d = x.shape
    dh = d // ATT_HEADS
    nkv = ATT_KV_HEADS * dh
    nqi = IDX_HEADS * IDX_DH
    cuts = (d, d + nkv, d + 2 * nkv, d + 2 * nkv + nqi, d + 2 * nkv + nqi + IDX_DH)
    w = jnp.pad(w_in[:, :cuts[4]], ((0, 0), (0, (-cuts[4]) % LANES))).astype(BF16)
    w_wi_t = w_in[:, cuts[4]:cuts[4] + IDX_HEADS].T.astype(BF16)
    row = lambda wd: pl.BlockSpec((tm, wd), lambda i: (i, 0))
    out_specs = [row(d), row(nqi), pl.BlockSpec((IDX_HEADS, tm), lambda i: (0, i))]
    out_shape = [jax.ShapeDtypeStruct((n, d), BF16), jax.ShapeDtypeStruct((n, nqi), BF16),
                 jax.ShapeDtypeStruct((IDX_HEADS, n), F32)]
    if not seq_layouts:
        out_specs += [row(nkv), row(nkv), row(IDX_DH)]
        out_shape += [jax.ShapeDtypeStruct((n, wd), F32) for wd in (nkv, nkv, IDX_DH)]
    else:
        per_seq = seq_len // tm
        tmin = lambda wd: pl.BlockSpec((None, wd, tm), lambda i: (i // per_seq, 0, i % per_seq))
        out_specs += [tmin(nkv), tmin(nkv), tmin(IDX_DH),
                      row(IDX_DH),
                      pl.BlockSpec((ATT_KV_HEADS, tm, dh), lambda i: (0, i, 0)),
                      pl.BlockSpec((None, nkv, tm), lambda i: (i, 0, 0))]
        out_shape += [jax.ShapeDtypeStruct((n // seq_len, wd, seq_len), F32) for wd in (nkv, nkv, IDX_DH)]
        out_shape += [jax.ShapeDtypeStruct((n, IDX_DH), BF16),
                      jax.ShapeDtypeStruct((ATT_KV_HEADS, n, dh), BF16),
                      jax.ShapeDtypeStruct((n // tm, nkv, tm), BF16)]
    return pl.pallas_call(
        functools.partial(_dsa_proj_kernel, cuts=cuts, dh=dh),
        grid=(n // tm,),
        in_specs=[pl.BlockSpec((tm, d), lambda i: (i, 0)),
                  pl.BlockSpec((1, d), lambda i: (0, 0)),
                  _mod_spec(mod, 0, tm, seq_len, d),
                  _mod_spec(mod, 1, tm, seq_len, d),
                  pl.BlockSpec(w.shape, lambda i: (0, 0)),
                  pl.BlockSpec(w_wi_t.shape, lambda i: (0, 0)),
                  pl.BlockSpec((1, IDX_DH), lambda i: (0, 0)),
                  pl.BlockSpec((1, IDX_DH), lambda i: (0, 0))],
        out_specs=out_specs,
        out_shape=out_shape,
        compiler_params=_cparams("parallel"),
        name="dsa_proj",
    )(x, norm_g.reshape(1, d), mod, mod, w, w_wi_t, ln_g.reshape(1, IDX_DH), ln_b.reshape(1, IDX_DH))


def _t5_bucket_np(dist):
    dist = np.maximum(dist, 0)
    max_exact = N_BUCKETS // 2
    ratio = np.log(np.maximum(dist, max_exact).astype(np.float32) / max_exact) / math.log(MAX_DISTANCE / max_exact)
    large = np.minimum(max_exact + (ratio * (N_BUCKETS - max_exact)).astype(np.int32), N_BUCKETS - 1)
    return np.where(dist < max_exact, dist, large).astype(np.int32)


def _bias_table_kernel(rb_ref, bk_ref, o_ref):
    h = pl.program_id(0)
    for kind in range(3):
        bk = bk_ref[kind]
        acc = jnp.zeros(bk.shape, F32)
        for b in range(N_BUCKETS):
            acc = jnp.where(bk == b, rb_ref[b, h], acc)
        o_ref[kind] = acc


def bias_tables(rel_bias, tq):
    s = np.arange(tq)[:, None]
    t = np.arange(tq)[None, :]
    buckets = np.stack([_t5_bucket_np(t - s), _t5_bucket_np(tq + t - s), _t5_bucket_np(2 * tq + t - s)])
    assert tq >= MAX_DISTANCE and (buckets[2] == N_BUCKETS - 1).all()
    return pl.pallas_call(
        _bias_table_kernel,
        grid=(ATT_HEADS,),
        in_specs=[pl.BlockSpec(memory_space=pltpu.SMEM),
                  pl.BlockSpec(buckets.shape, lambda h: (0, 0, 0))],
        out_specs=pl.BlockSpec((None, 3, tq, tq), lambda h: (h, 0, 0, 0)),
        out_shape=jax.ShapeDtypeStruct((ATT_HEADS, 3, tq, tq), F32),
        compiler_params=_cparams("parallel"),
        name="t5_bias_tiles",
    )(rel_bias, jnp.asarray(buckets))


def _order_key(score):
    score = jnp.where(score == 0.0, 0.0, score)
    bits = pltpu.bitcast(score, I32)
    return bits ^ ((bits >> 31) & 0x7FFFFFFF)


def _dsa_prompt_kernel(q_ref, qi_ref, wit_ref, kib_ref, kg_ref, vt_ref, bias_ref, tril_ref, o_ref,
                       key_ref, selb_ref, qis_ref, qg_ref, sc_ref, ot_ref, *, tq, dh, topk):
    i = pl.program_id(1)
    nk = i + 1
    g_heads = ATT_HEADS // ATT_KV_HEADS
    s_id = lax.broadcasted_iota(I32, (tq, tq), 0)
    t_id = lax.broadcasted_iota(I32, (tq, tq), 1)
    fold8 = lambda a, op: op(a.reshape(a.shape[0] // 8, 8, a.shape[1]), axis=0)

    for h in range(IDX_HEADS):
        qis_ref[h * tq:(h + 1) * tq, :] = qi_ref[:, h * IDX_DH:(h + 1) * IDX_DH]
    for hh in range(ATT_HEADS):
        g, a = divmod(hh, g_heads)
        qg_ref[g, a * tq:(a + 1) * tq, :] = q_ref[:, hh * dh:(hh + 1) * dh]
    w8 = wit_ref[...]

    def score_body(j, carry):
        kij = kib_ref[pl.ds(pl.multiple_of(j * tq, tq), tq), :]
        lg = _bdot_nt(kij, qis_ref[...])
        acc = jnp.zeros((tq, tq), F32)
        for h in range(IDX_HEADS):
            acc = acc + jnp.maximum(lg[:, h * tq:(h + 1) * tq], 0.0) * w8[h:h + 1, :]
        valid = (s_id + j * tq) <= (t_id + i * tq)
        key_ref[j] = _order_key(jnp.where(valid, acc, NEG_INF))
        return carry

    lax.fori_loop(0, nk, score_body, 0)
    n_pairs = (nk + 1) // 2

    @pl.when(nk % 2 == 1)
    def _():
        pad = jnp.minimum(nk, key_ref.shape[0] - 1)
        key_ref[pad] = jnp.full((tq, tq), INT_MIN, I32)
        selb_ref[pad] = jnp.full((tq, tq), NEG_INF, F32)

    def count(pred):
        def body(jj, acc8):
            ones = jnp.where(pred(key_ref[2 * jj]), 1.0, 0.0) + jnp.where(pred(key_ref[2 * jj + 1]), 1.0, 0.0)
            return acc8 + fold8(ones, jnp.sum)
        return jnp.sum(lax.fori_loop(0, n_pairs, body, jnp.zeros((8, tq), F32)), axis=0, keepdims=True)

    def bit_body(it, ans):
        cand = ans | (jnp.int32(1) << (31 - it))
        cand_s = cand ^ INT_MIN
        return jnp.where(count(lambda kj: kj >= cand_s) >= topk, cand, ans)

    thr = lax.fori_loop(0, 32, bit_body, jnp.zeros((1, tq), I32)) ^ INT_MIN
    need = topk - count(lambda kj: kj > thr)
    n_tie = count(lambda kj: kj == thr)

    def causal(j):
        return (s_id + j * tq) <= (t_id + i * tq)

    some_ties_dropped = jnp.max(n_tie - need) > 0.0

    @pl.when(jnp.logical_not(some_ties_dropped))
    def _():
        def sel_body(j, carry):
            selb_ref[j] = jnp.where(jnp.logical_and(key_ref[j] >= thr, causal(j)), 0.0, NEG_INF)
            return carry
        lax.fori_loop(0, nk, sel_body, 0)

    @pl.when(some_ties_dropped)
    def _():
        def sel_body(j, run):
            kj = key_ref[j]
            tie = kj == thr
            tie_f = jnp.where(tie, 1.0, 0.0)
            before = run + jnp.dot(tril_ref[...], tie_f.astype(BF16), preferred_element_type=F32)
            take = jnp.where(kj > thr, 0.0, jnp.where(tie, jnp.where(before < need, 0.0, NEG_INF), NEG_INF))
            selb_ref[j] = jnp.where(causal(j), take, NEG_INF)
            return run + jnp.sum(tie_f, axis=0, keepdims=True)
        lax.fori_loop(0, nk, sel_body, jnp.zeros((1, tq), F32))

    ones_rows = jnp.ones((16, 2 * tq), BF16)
    for g in range(ATT_KV_HEADS):
        qg = qg_ref[g]

        def logits_body(jj, mx8):
            base = pl.multiple_of(jj * 2 * tq, 2 * tq)
            s = _bdot_nt(kg_ref[g, pl.ds(base, 2 * tq), :], qg)
            j0 = 2 * jj
            halves = []
            for c in range(2):
                kind = jnp.clip(i - j0 - c, 0, 2)
                sel = selb_ref[j0 + c]
                halves.append(jnp.concatenate([bias_ref[g * g_heads + a, kind] + sel
                                               for a in range(g_heads)], axis=1))
            s = s + jnp.concatenate(halves, axis=0)
            sc_ref[jj] = s
            return jnp.maximum(mx8, fold8(s, jnp.max))

        mx8 = lax.fori_loop(0, n_pairs, logits_body, jnp.full((8, g_heads * tq), NEG_INF, F32))
        m = jnp.max(mx8, axis=0, keepdims=True)
        m = jnp.where(m == NEG_INF, 0.0, m)

        def pv_body(jj, acc):
            p = jnp.exp(sc_ref[jj] - m).astype(BF16)
            lhs = jnp.concatenate([vt_ref[jj, g * dh:(g + 1) * dh, :], ones_rows], axis=0)
            return acc + jnp.dot(lhs, p, preferred_element_type=F32)

        acc = lax.fori_loop(0, n_pairs, pv_body, jnp.zeros((dh + 16, g_heads * tq), F32))
        og = acc[:dh] / acc[dh:dh + 1]
        for a in range(g_heads):
            hh = g * g_heads + a
            ot_ref[hh * dh:(hh + 1) * dh, :] = og[:, a * tq:(a + 1) * tq]
    o_ref[...] = ot_ref[...].T.astype(o_ref.dtype)


def dsa_prompt(q, qi, wit, kib, kg, vt, bias_tab, batch, seq_len, tq=128):
    n, d = q.shape
    dh = d // ATT_HEADS
    nq = seq_len // tq
    g_heads = ATT_HEADS // ATT_KV_HEADS
    assert nq % 2 == 0 and vt.shape[2] == 2 * tq
    topk = min(TOPK_MAX, seq_len // 4)
    tril = np.tril(np.ones((tq, tq), np.float32), -1)
    blk_spec = lambda width: pl.BlockSpec((tq, width), lambda b, i: (b * nq + i, 0))
    return pl.pallas_call(
        functools.partial(_dsa_prompt_kernel, tq=tq, dh=dh, topk=float(topk)),
        grid=(batch, nq),
        in_specs=[blk_spec(d), blk_spec(qi.shape[1]),
                  pl.BlockSpec((IDX_HEADS, tq), lambda b, i: (0, b * nq + i)),
                  pl.BlockSpec((seq_len, IDX_DH), lambda b, i: (b, 0)),
                  pl.BlockSpec((ATT_KV_HEADS, seq_len, dh), lambda b, i: (0, b, 0)),
                  pl.BlockSpec((nq // 2,) + vt.shape[1:], lambda b, i: (b, 0, 0)),
                  pl.BlockSpec(bias_tab.shape, lambda b, i: (0, 0, 0, 0)),
                  pl.BlockSpec((tq, tq), lambda b, i: (0, 0))],
        out_specs=blk_spec(d),
        out_shape=jax.ShapeDtypeStruct((n, d), BF16),
        scratch_shapes=[pltpu.VMEM((nq, tq, tq), I32),
                        pltpu.VMEM((nq, tq, tq), F32),
                        pltpu.VMEM((IDX_HEADS * tq, IDX_DH), BF16),
                        pltpu.VMEM((ATT_KV_HEADS, g_heads * tq, dh), BF16),
                        pltpu.VMEM((nq // 2, 2 * tq, g_heads * tq), F32),
                        pltpu.VMEM((d, tq), F32)],
        compiler_params=_cparams("parallel", "arbitrary"),
        name="dsa_prompt",
    )(q, qi, wit, kib, kg, vt, bias_tab, jnp.asarray(tril, BF16))


def _router_kernel(h_ref, wr_ref, rb_ref, tri_ref, eidx_ref, wsel_ref, pos_ref, cnt_ref, carry_ref, *, n_exp):
    @pl.when(pl.program_id(0) == 0)
    def _():
        carry_ref[...] = jnp.zeros_like(carry_ref)

    h_hi, h_mid, h_lo = _split3(h_ref[...])
    w_hi, w_mid, w_lo = _split3(wr_ref[...])
    nt = lambda a, b: lax.dot_general(a, b, (((1,), (1,)), ((), ())), preferred_element_type=F32)
    logits = ((nt(w_lo, h_hi) + nt(w_hi, h_lo) + nt(w_mid, h_mid))
              + (nt(w_hi, h_mid) + nt(w_mid, h_hi))) + nt(w_hi, h_hi)
    s = _sigmoid(logits)
    sel = s + rb_ref[...]
    tm = sel.shape[1]
    gsz = n_exp // N_GROUPS
    io_g = lax.broadcasted_iota(I32, (gsz, tm), 0)

    gs = []
    for g in range(N_GROUPS):
        grp = sel[g * gsz:(g + 1) * gsz, :]
        m1 = jnp.max(grp, axis=0, keepdims=True)
        i1 = jnp.min(jnp.where(grp == m1, io_g, gsz), axis=0, keepdims=True)
        m2 = jnp.max(jnp.where(io_g == i1, NEG_INF, grp), axis=0, keepdims=True)
        gs.append(m1 + m2)
    masked = []
    for g in range(N_GROUPS):
        rank = jnp.zeros((1, tm), F32)
        for o in range(N_GROUPS):
            if o == g:
                continue
            ahead = (gs[o] >= gs[g]) if o < g else (gs[o] > gs[g])
            rank = rank + jnp.where(ahead, 1.0, 0.0)
        keep = jnp.where(rank < TOPK_GROUPS, 0.0, NEG_INF)
        masked.append(sel[g * gsz:(g + 1) * gsz, :] + keep)
    msel = jnp.concatenate(masked, axis=0)

    io_e = lax.broadcasted_iota(I32, (n_exp, tm), 0)
    chosen = jnp.zeros((n_exp, tm), F32)
    picks, weights = [], []
    for _ in range(TOP_K):
        m = jnp.max(msel, axis=0, keepdims=True)
        ei = jnp.min(jnp.where(msel == m, io_e, n_exp), axis=0, keepdims=True)
        pick = io_e == ei
        weights.append(jnp.sum(jnp.where(pick, s, 0.0), axis=0, keepdims=True))
        picks.append(ei)
        chosen = jnp.where(pick, 1.0, chosen)
        msel = jnp.where(pick, NEG_INF, msel)
    wsum = weights[0]
    for w in weights[1:]:
        wsum = wsum + w

    rank_in_expert = carry_ref[...] + jnp.dot(chosen.astype(BF16), tri_ref[...], preferred_element_type=F32)
    carry_new = carry_ref[...] + jnp.sum(chosen, axis=1, keepdims=True)
    carry_ref[...] = carry_new
    cnt_ref[...] = carry_new
    for kk in range(TOP_K):
        eidx_ref[kk:kk + 1, :] = picks[kk]
        wsel_ref[kk:kk + 1, :] = weights[kk] / wsum * ROUTE_SCALE
        pk = jnp.sum(jnp.where(io_e == picks[kk], rank_in_expert, 0.0), axis=0, keepdims=True)
        pos_ref[kk:kk + 1, :] = pk.astype(I32)


def moe_route(h, w_router, router_bias, tm):
    n, d = h.shape
    n_exp = w_router.shape[1]
    tri = np.triu(np.ones((tm, tm), np.float32), 1)
    row8 = lambda dt: jax.ShapeDtypeStruct((TOP_K, n), dt)
    return pl.pallas_call(
        functools.partial(_router_kernel, n_exp=n_exp),
        grid=(n // tm,),
        in_specs=[pl.BlockSpec((tm, d), lambda i: (i, 0)),
                  pl.BlockSpec((n_exp, d), lambda i: (0, 0)),
                  pl.BlockSpec((n_exp, 1), lambda i: (0, 0)),
                  pl.BlockSpec((tm, tm), lambda i: (0, 0))],
        out_specs=[pl.BlockSpec((TOP_K, tm), lambda i: (0, i)),
                   pl.BlockSpec((TOP_K, tm), lambda i: (0, i)),
                   pl.BlockSpec((TOP_K, tm), lambda i: (0, i)),
                   pl.BlockSpec((n_exp, 1), lambda i: (0, 0))],
        out_shape=[row8(I32), row8(F32), row8(I32), jax.ShapeDtypeStruct((n_exp, 1), F32)],
        scratch_shapes=[pltpu.VMEM((n_exp, 1), F32)],
        compiler_params=_cparams("arbitrary"),
        name="moe_router",
    )(h, w_router.T, router_bias.reshape(n_exp, 1), jnp.asarray(tri, BF16))


def _dispatch_kernel(dest_ref, h_ref, xs_ref, sem):
    tm = h_ref.shape[0]

    def row_copy(t, dst_row):
        return pltpu.make_async_copy(h_ref.at[pl.ds(t, 1), :], xs_ref.at[pl.ds(dst_row, 1), :], sem)

    def issue(t, c):
        for kk in range(TOP_K):
            row_copy(t, dest_ref[kk, t]).start()
        return c

    def drain(t, c):
        for kk in range(TOP_K):
            row_copy(t, dest_ref[kk, t]).wait()
        return c

    lax.fori_loop(0, tm, issue, 0)
    lax.fori_loop(0, tm, drain, 0)


def moe_dispatch(h, dest, tm):
    n, d = h.shape
    return pl.pallas_call(
        _dispatch_kernel,
        grid=(n // tm,),
        in_specs=[pl.BlockSpec((TOP_K, tm), lambda i: (0, i), memory_space=pltpu.SMEM),
                  pl.BlockSpec((tm, d), lambda i: (i, 0))],
        out_specs=pl.BlockSpec(memory_space=pl.ANY),
        out_shape=jax.ShapeDtypeStruct((n * TOP_K, d), F32),
        scratch_shapes=[pltpu.SemaphoreType.DMA(())],
        compiler_params=_cparams("arbitrary"),
        name="moe_dispatch",
    )(dest, h)


def _expert_kernel(blk_ref, exp_ref, lo_ref, hi_ref, x_ref, wg_ref, wu_ref, wd_ref, y_ref):
    j = pl.program_id(0)
    blk = x_ref.shape[0]
    lo = lo_ref[j] - blk_ref[j] * blk
    hi = hi_ref[j] - blk_ref[j] * blk

    @pl.when(hi > lo)
    def _():
        x = x_ref[...].astype(BF16)
        g = jnp.dot(x, wg_ref[...].astype(BF16), preferred_element_type=F32)
        u = jnp.dot(x, wu_ref[...].astype(BF16), preferred_element_type=F32)
        y = _bdot(_silu(g) * u, wd_ref[...])
        row = lax.broadcasted_iota(I32, y.shape, 0)
        y = jnp.where(jnp.logical_and(row >= lo, row < hi), y, 0.0)

        @pl.when(lo == 0)
        def _():
            y_ref[...] = y

        @pl.when(lo != 0)
        def _():
            y_ref[...] += y


def moe_experts(xs, seg_blk, seg_exp, seg_lo, seg_hi, w_gate, w_up, w_down, layer, blk):
    n_rows, d = xs.shape
    ff = w_gate.shape[3]
    wspec = lambda shape: pl.BlockSpec((None, None) + shape, lambda j, sb, se, lo, hi: (layer, se[j], 0, 0))
    return pl.pallas_call(
        _expert_kernel,
        grid_spec=pltpu.PrefetchScalarGridSpec(
            num_scalar_prefetch=4, grid=(seg_blk.shape[0],),
            in_specs=[pl.BlockSpec((blk, d), lambda j, sb, se, lo, hi: (sb[j], 0)),
                      wspec((d, ff)), wspec((d, ff)), wspec((ff, d))],
            out_specs=pl.BlockSpec((blk, d), lambda j, sb, se, lo, hi: (sb[j], 0))),
        out_shape=jax.ShapeDtypeStruct((n_rows, d), F32),
        compiler_params=_cparams("arbitrary"),
        name="moe_experts",
    )(seg_blk, seg_exp, seg_lo, seg_hi, xs, w_gate, w_up, w_down)


def _combine_kernel(dest_ref, ys_ref, wsel_ref, h_ref, x_ref, gt_ref, sg_ref, su_ref, sd_ref, nf_ref,
                    o_ref, buf_ref, sem, *, final_norm):
    tm = h_ref.shape[0]

    def row_copy(t, kk):
        return pltpu.make_async_copy(ys_ref.at[pl.ds(dest_ref[kk, t], 1), :],
                                     buf_ref.at[pl.ds(kk * tm + t, 1), :], sem)

    def issue(t, c):
        for kk in range(TOP_K):
            row_copy(t, kk).start()
        return c

    def drain(t, c):
        for kk in range(TOP_K):
            row_copy(t, kk).wait()
        return c

    lax.fori_loop(0, tm, issue, 0)
    h = h_ref[...].astype(BF16)
    g = jnp.dot(h, sg_ref[...], preferred_element_type=F32)
    u = jnp.dot(h, su_ref[...], preferred_element_type=F32)
    y = _bdot(_silu(g) * u, sd_ref[...])
    lax.fori_loop(0, tm, drain, 0)
    w = wsel_ref[...]
    for kk in range(TOP_K):
        y = y + w[:, kk:kk + 1] * buf_ref[kk * tm:(kk + 1) * tm, :]
    x_new = x_ref[...] + gt_ref[...] * y
    o_ref[...] = _rms(x_new, nf_ref[...]) if final_norm else x_new


def moe_combine(ys, dest, wsel_t, h, x, mod, ws_gate, ws_up, ws_down, norm_final, final_norm, tm, seq_len):
    n, d = x.shape
    ff = ws_gate.shape[1]
    return pl.pallas_call(
        functools.partial(_combine_kernel, final_norm=final_norm),
        grid=(n // tm,),
        in_specs=[pl.BlockSpec((TOP_K, tm), lambda i: (0, i), memory_space=pltpu.SMEM),
                  pl.BlockSpec(memory_space=pl.ANY),
                  pl.BlockSpec((tm, TOP_K), lambda i: (i, 0)),
                  pl.BlockSpec((tm, d), lambda i: (i, 0)),
                  pl.BlockSpec((tm, d), lambda i: (i, 0)),
                  _mod_spec(mod, 5, tm, seq_len, d),
                  pl.BlockSpec((d, ff), lambda i: (0, 0)),
                  pl.BlockSpec((d, ff), lambda i: (0, 0)),
                  pl.BlockSpec((ff, d), lambda i: (0, 0)),
                  pl.BlockSpec((1, d), lambda i: (0, 0))],
        out_specs=pl.BlockSpec((tm, d), lambda i: (i, 0)),
        out_shape=jax.ShapeDtypeStruct((n, d), F32),
        scratch_shapes=[pltpu.VMEM((TOP_K * tm, d), F32), pltpu.SemaphoreType.DMA(())],
        compiler_params=_cparams("arbitrary"),
        name="moe_combine",
    )(dest, ys, wsel_t, h, x, mod, ws_gate.astype(BF16), ws_up.astype(BF16), ws_down.astype(BF16),
      norm_final.reshape(1, d))


def _slot_kernel(start_ref, eidx_ref, pos_ref, dest_ref):
    e = eidx_ref[...]
    base = jnp.zeros(e.shape, I32)
    for ex in range(start_ref.shape[0]):
        base = jnp.where(e == ex, start_ref[ex], base)
    dest_ref[...] = base + pos_ref[...]


def moe_slots(start, eidx, pos, tm):
    n = eidx.shape[1]
    spec = pl.BlockSpec((TOP_K, tm), lambda i: (0, i))
    return pl.pallas_call(
        _slot_kernel,
        grid=(n // tm,),
        in_specs=[pl.BlockSpec(memory_space=pltpu.SMEM), spec, spec],
        out_specs=spec,
        out_shape=jax.ShapeDtypeStruct(eidx.shape, I32),
        compiler_params=_cparams("parallel"),
        name="moe_slots",
    )(start, eidx, pos)


def moe_ffn(x, mod, norm_g, w_router, router_bias, w_gate, w_up, w_down, layer, ws_gate, ws_up, ws_down,
            norm_final, final_norm, tm, seq_len, blk):
    n, d = x.shape
    n_exp = w_router.shape[1]
    h = modulate_rows(x, mod, norm_g, tm, seq_len)
    eidx, wsel, pos, counts = moe_route(h, w_router, router_bias, tm)
    n_rows = n * TOP_K
    n_blocks = n_rows // blk
    counts = counts.reshape(n_exp).astype(I32)
    end = jnp.cumsum(counts)
    start = end - counts
    dest = moe_slots(start, eidx, pos, min(tm, n))
    seg_lo = jnp.sort(jnp.concatenate([jnp.arange(n_blocks, dtype=I32) * blk, start[1:]]))
    seg_hi = jnp.concatenate([seg_lo[1:], jnp.full((1,), n_rows, I32)])
    seg_blk = jnp.minimum(seg_lo // blk, n_blocks - 1)
    seg_exp = jnp.minimum(jnp.sum((end[None, :] <= seg_lo[:, None]).astype(I32), axis=1), n_exp - 1)
    xs = moe_dispatch(h, dest, tm)
    ys = moe_experts(xs, seg_blk, seg_exp, seg_lo, seg_hi, w_gate, w_up, w_down, layer, blk)
    return moe_combine(ys, dest, wsel.T, h, x, mod, ws_gate, ws_up, ws_down, norm_final, final_norm,
                       min(tm, 128), seq_len)


def _sample_score_kernel(pt_ref, qi_ref, w_ref, *refs):
    page_refs, o_ref = refs[:-1], refs[-1]
    kp = jnp.concatenate([r[...] for r in page_refs], axis=1)
    lg = _bdot(qi_ref[...], kp)
    sc = jnp.sum(jnp.maximum(lg, 0.0) * w_ref[...], axis=0, keepdims=True)
    page = page_refs[0].shape[1]
    for p in range(len(page_refs)):
        o_ref[p:p + 1, :] = sc[:, p * page:(p + 1) * page]


def _page_specs(layer, pg, block):
    zeros = (0,) * (len(block) - 2)
    return [pl.BlockSpec(block, lambda b, g, pt, p=p: (layer, pt[b, g * pg + p]) + zeros) for p in range(pg)]


def dsa_sample_scores(qi3, wi3, cache_ki_t, layer, page_table, pg):
    b, n_pages = page_table.shape
    page = cache_ki_t.shape[3]
    return pl.pallas_call(
        _sample_score_kernel,
        grid_spec=pltpu.PrefetchScalarGridSpec(
            num_scalar_prefetch=1, grid=(b, n_pages // pg),
            in_specs=[pl.BlockSpec((None, IDX_HEADS, IDX_DH), lambda b, g, pt: (b, 0, 0)),
                      pl.BlockSpec((None, IDX_HEADS, 1), lambda b, g, pt: (b, 0, 0))]
                     + _page_specs(layer, pg, (None, None, IDX_DH, page)),
            out_specs=pl.BlockSpec((None, pg, page), lambda b, g, pt: (b, g, 0))),
        out_shape=jax.ShapeDtypeStruct((b, n_pages, page), F32),
        compiler_params=_cparams("parallel", "arbitrary"),
        name="dsa_sample_scores",
    )(page_table, qi3, wi3, *([cache_ki_t] * pg))


def _sample_select_kernel(sc_ref, qi_ref, w_ref, kin_ref, triu_ref, tril_ref, selb_ref, selbn_ref, *, topk):
    lg_new = jnp.sum(qi_ref[...].astype(F32) * kin_ref[...].astype(BF16).astype(F32),
                     axis=1, keepdims=True)
    s_new = jnp.sum(jnp.maximum(lg_new, 0.0) * w_ref[...], axis=0, keepdims=True)
    keys = _order_key(sc_ref[...])
    key_new = _order_key(s_new)
    total = lambda a: jnp.sum(jnp.sum(a, axis=0, keepdims=True), axis=1, keepdims=True)

    def count(pred):
        return total(jnp.where(pred(keys), 1.0, 0.0)) + jnp.where(pred(key_new), 1.0, 0.0)

    def bit_body(it, ans):
        cand = ans | (jnp.int32(1) << (31 - it))
        cand_s = cand ^ INT_MIN
        return jnp.where(count(lambda kj: kj >= cand_s) >= topk, cand, ans)

    thr = lax.fori_loop(0, 32, bit_body, jnp.zeros((1, 1), I32)) ^ INT_MIN
    need = topk - count(lambda kj: kj > thr)
    tie = keys == thr
    tie_f = jnp.where(tie, 1.0, 0.0)
    tie_b = tie_f.astype(BF16)
    in_row = jnp.dot(tie_b, triu_ref[...], preferred_element_type=F32)
    rows_before = jnp.sum(jnp.dot(tril_ref[...], tie_b, preferred_element_type=F32), axis=1, keepdims=True)
    before = in_row + rows_before
    selb_ref[...] = jnp.where(keys > thr, 0.0, jnp.where(tie, jnp.where(before < need, 0.0, NEG_INF), NEG_INF))
    selbn_ref[...] = jnp.where(key_new > thr, 0.0,
                               jnp.where(key_new == thr, jnp.where(total(tie_f) < need, 0.0, NEG_INF), NEG_INF))


def dsa_sample_select(scores, qi3, wi3, ki_new, topk):
    b, n_pages, page = scores.shape
    triu = np.triu(np.ones((page, page), np.float32), 1)
    tril = np.tril(np.ones((n_pages, n_pages), np.float32), -1)
    return pl.pallas_call(
        functools.partial(_sample_select_kernel, topk=float(topk)),
        grid=(b,),
        in_specs=[pl.BlockSpec((None, n_pages, page), lambda i: (i, 0, 0)),
                  pl.BlockSpec((None, IDX_HEADS, IDX_DH), lambda i: (i, 0, 0)),
                  pl.BlockSpec((None, IDX_HEADS, 1), lambda i: (i, 0, 0)),
                  pl.BlockSpec((None, 1, IDX_DH), lambda i: (i, 0, 0)),
                  pl.BlockSpec((page, page), lambda i: (0, 0)),
                  pl.BlockSpec((n_pages, n_pages), lambda i: (0, 0))],
        out_specs=[pl.BlockSpec((None, n_pages, page), lambda i: (i, 0, 0)),
                   pl.BlockSpec((None, 1, 1), lambda i: (i, 0, 0))],
        out_shape=[jax.ShapeDtypeStruct((b, n_pages, page), F32), jax.ShapeDtypeStruct((b, 1, 1), F32)],
        compiler_params=_cparams("parallel"),
        name="dsa_sample_select",
    )(scores, qi3, wi3, ki_new, jnp.asarray(triu, BF16), jnp.asarray(tril, BF16))


def _sample_attend_kernel(pt_ref, q_ref, selb_ref, selbn_ref, kn_ref, vn_ref, blast_ref, bfar_ref, bnew_ref,
                          *refs, pg, n_pages):
    k_refs, v_refs = refs[:pg], refs[pg:2 * pg]
    o_ref, m_ref, l_ref, acc_ref = refs[2 * pg:]
    step = pl.program_id(1)
    g_heads = ATT_HEADS // ATT_KV_HEADS
    page = k_refs[0].shape[2]
    dh = q_ref.shape[1]
    head_group = lax.broadcasted_iota(I32, (ATT_HEADS, 1), 0) // g_heads

    @pl.when(step == 0)
    def _():
        m_ref[...] = jnp.full(m_ref.shape, NEG_INF, F32)
        l_ref[...] = jnp.zeros(l_ref.shape, F32)
        acc_ref[...] = jnp.zeros(acc_ref.shape, F32)

    q = q_ref[...]
    s = jnp.zeros((ATT_HEADS, pg * page), F32)
    for g in range(ATT_KV_HEADS):
        kg = jnp.concatenate([r[g] for r in k_refs], axis=1)
        s = jnp.where(head_group == g, _bdot(q, kg), s)
    add = [jnp.where(step * pg + p == n_pages - 1, blast_ref[...], bfar_ref[...]) + selb_ref[p:p + 1, :]
           for p in range(pg)]
    s = s + jnp.concatenate(add, axis=1)
    m_old = m_ref[...]
    m_new = jnp.maximum(m_old, jnp.max(s, axis=1, keepdims=True))
    m_safe = jnp.where(m_new == NEG_INF, 0.0, m_new)
    alpha = jnp.exp(m_old - m_safe)
    p_ = jnp.exp(s - m_safe)
    pv = jnp.zeros((ATT_HEADS, dh), F32)
    for g in range(ATT_KV_HEADS):
        vg = jnp.concatenate([r[g] for r in v_refs], axis=1)
        pv = jnp.where(head_group == g, _bdot_nt(p_, vg), pv)
    l_new = alpha * l_ref[...] + jnp.sum(p_, axis=1, keepdims=True)
    acc_new = alpha * acc_ref[...] + pv
    m_ref[...] = m_new
    l_ref[...] = l_new
    acc_ref[...] = acc_new

    @pl.when(step == pl.num_programs(1) - 1)
    def _():
        kn = jnp.zeros((ATT_HEADS, dh), F32)
        vn = jnp.zeros((ATT_HEADS, dh), F32)
        for g in range(ATT_KV_HEADS):
            kn = jnp.where(head_group == g, kn_ref[g:g + 1, :], kn)
            vn = jnp.where(head_group == g, vn_ref[g:g + 1, :], vn)
        qk = q.astype(F32) * kn.astype(BF16).astype(F32)
        s_n = jnp.sum(qk, axis=1, keepdims=True) + bnew_ref[...] + selbn_ref[...]
        m_fin = jnp.maximum(m_new, s_n)
        m_fs = jnp.where(m_fin == NEG_INF, 0.0, m_fin)
        a2 = jnp.exp(m_new - m_fs)
        p_n = jnp.exp(s_n - m_fs)
        o_ref[...] = ((a2 * acc_new + p_n * vn) / (a2 * l_new + p_n)).astype(o_ref.dtype)


def dsa_sample_attend(q3, selb, selb_new, k_new, v_new, bias_tab, cache_k_t, cache_v_t, layer, page_table, pg):
    b, n_pages = page_table.shape
    kvh, dh, page = cache_k_t.shape[2:]
    assert page == bias_tab.shape[2] and page >= MAX_DISTANCE
    b_last = bias_tab[:, 1, :, 0]
    b_far = bias_tab[:, 2, 0, :1]
    b_new = bias_tab[:, 0, 0, :1]
    full = lambda shape: pl.BlockSpec(shape, lambda b, g, pt: (0,) * len(shape))
    per_b = lambda shape: pl.BlockSpec((None,) + shape, lambda b, g, pt: (b,) + (0,) * len(shape))
    return pl.pallas_call(
        functools.partial(_sample_attend_kernel, pg=pg, n_pages=n_pages),
        grid_spec=pltpu.PrefetchScalarGridSpec(
            num_scalar_prefetch=1, grid=(b, n_pages // pg),
            in_specs=[per_b((ATT_HEADS, dh)),
                      pl.BlockSpec((None, pg, page), lambda b, g, pt: (b, g, 0)),
                      per_b((1, 1)), per_b((kvh, dh)), per_b((kvh, dh)),
                      full((ATT_HEADS, page)), full((ATT_HEADS, 1)), full((ATT_HEADS, 1))]
                     + _page_specs(layer, pg, (None, None, kvh, dh, page))
                     + _page_specs(layer, pg, (None, None, kvh, dh, page)),
            out_specs=per_b((ATT_HEADS, dh)),
            scratch_shapes=[pltpu.VMEM((ATT_HEADS, 1), F32), pltpu.VMEM((ATT_HEADS, 1), F32),
                            pltpu.VMEM((ATT_HEADS, dh), F32)]),
        out_shape=jax.ShapeDtypeStruct((b, ATT_HEADS, dh), BF16),
        compiler_params=_cparams("parallel", "arbitrary"),
        name="dsa_sample_attend",
    )(page_table, q3, selb, selb_new, k_new, v_new, b_last, b_far, b_new,
      *([cache_k_t] * pg), *([cache_v_t] * pg))


def dsa_sample(q, k, v, qi, ki, wit, cache_k, cache_v, cache_ki, layer, page_table, bias_tab):
    b, d = q.shape
    dh = d // ATT_HEADS
    n_pages = page_table.shape[1]
    page = cache_k.shape[2]
    pg = 16 if n_pages % 16 == 0 else n_pages
    topk = min(TOPK_MAX, (n_pages * page + 1) // 4)
    qi3 = qi.reshape(b, IDX_HEADS, IDX_DH)
    wi3 = wit.T.reshape(b, IDX_HEADS, 1)
    cache_ki_t = jnp.transpose(cache_ki, (0, 1, 3, 2))
    cache_k_t = jnp.transpose(cache_k, (0, 1, 3, 4, 2))
    cache_v_t = jnp.transpose(cache_v, (0, 1, 3, 4, 2))
    scores = dsa_sample_scores(qi3, wi3, cache_ki_t, layer, page_table, pg)
    selb, selb_new = dsa_sample_select(scores, qi3, wi3, ki.reshape(b, 1, IDX_DH), topk)
    o = dsa_sample_attend(q.reshape(b, ATT_HEADS, dh), selb, selb_new, k.reshape(b, ATT_KV_HEADS, dh),
                          v.reshape(b, ATT_KV_HEADS, dh), bias_tab, cache_k_t, cache_v_t, layer, page_table, pg)
    return o.reshape(b, d)


def kernel(x_prompt, x_sample, c_prompt, c_sample, state_gla, cache_k, cache_v, cache_idx_k, page_table,
           rel_bias, w_ada, b_ada, norm_mix, norm_ffn, norm_final,
           gla_w_in, gla_w_g2, gla_b_g2, gla_gnorm, gla_w_out,
           dsa_w_in, dsa_idx_ln_g, dsa_idx_ln_b, dsa_w_out,
           moe_w_router, moe_router_bias, moe_w_gate, moe_w_up, moe_w_down,
           shared_w_gate, shared_w_up, shared_w_down):
    bp, t, d = x_prompt.shape
    bs = x_sample.shape[0]
    depth = w_ada.shape[0]
    dh = d // ATT_HEADS
    tm_p = min(256, t)
    tq = 128
    mods = ada_mod_all(jnp.concatenate([c_prompt, c_sample], axis=0), w_ada, b_ada)
    xp = x_prompt.reshape(bp * t, d)
    xs = x_sample.reshape(bs, d)
    bias_tab = bias_tables(rel_bias, tq)
    gla_p, gla_s, kp_l, vp_l, kip_l, ks_l, vs_l, kis_l = [], [], [], [], [], [], [], []
    for i in range(depth):
        mod_p = mods[i, :bp].reshape(bp, 1, 6 * d)
        mod_s = mods[i, bp:]
        j = i // 2
        if i % 2 == 0:
            qkvr_p, la_p = gla_project(xp, mod_p, norm_mix[i], gla_w_in[j], gla_w_g2[j], gla_b_g2[j], tm_p, t)
            o_p, s_p = gla_chunked(qkvr_p, la_p, gla_gnorm[j], bp, t)
            qkvr_s, la_s = gla_project(xs, mod_s, norm_mix[i], gla_w_in[j], gla_w_g2[j], gla_b_g2[j], bs, 1)
            o_s, s_s = gla_step(qkvr_s, la_s, state_gla[j], gla_gnorm[j])
            gla_p.append(s_p)
            gla_s.append(s_s)
            w_out = gla_w_out[j]
        else:
            dp = (dsa_w_in[j], dsa_idx_ln_g[j], dsa_idx_ln_b[j])
            q_p, qi_p, wit_p, kt_p, vt32_p, kit_p, kib_p, kg_p, vt_p = dsa_project(
                xp, mod_p, norm_mix[i], *dp, tm_p, t, True)
            o_p = dsa_prompt(q_p, qi_p, wit_p, kib_p, kg_p, vt_p, bias_tab, bp, t, tq)
            q_s, qi_s, wit_s, k_s, v_s, ki_s = dsa_project(xs, mod_s, norm_mix[i], *dp, bs, 1, False)
            o_s = dsa_sample(q_s, k_s, v_s, qi_s, ki_s, wit_s, cache_k, cache_v, cache_idx_k, j,
                             page_table, bias_tab)
            kp_l.append(kt_p.reshape(bp, ATT_KV_HEADS, dh, t).transpose(0, 3, 1, 2))
            vp_l.append(vt32_p.reshape(bp, ATT_KV_HEADS, dh, t).transpose(0, 3, 1, 2))
            kip_l.append(kit_p.transpose(0, 2, 1))
            ks_l.append(k_s.reshape(bs, 1, ATT_KV_HEADS, dh))
            vs_l.append(v_s.reshape(bs, 1, ATT_KV_HEADS, dh))
            kis_l.append(ki_s.reshape(bs, 1, IDX_DH))
            w_out = dsa_w_out[j]
        xp = out_project(o_p, w_out, xp, mod_p, tm_p, t)
        xs = out_project(o_s, w_out, xs, mod_s, bs, 1)
        last = i == depth - 1
        mo = (moe_w_router[i], moe_router_bias[i], moe_w_gate, moe_w_up, moe_w_down, i,
              shared_w_gate[i], shared_w_up[i], shared_w_down[i], norm_final, last)
        xp = moe_ffn(xp, mod_p, norm_ffn[i], *mo, tm_p, t, 256)
        xs = moe_ffn(xs, mod_s, norm_ffn[i], *mo, bs, 1, 32)
    return (xp.reshape(bp, t, d), xs.reshape(bs, 1, d), jnp.stack(gla_p), jnp.stack(gla_s),
            jnp.stack(kp_l), jnp.stack(vp_l), jnp.stack(kip_l),
            jnp.stack(ks_l), jnp.stack(vs_l), jnp.stack(kis_l))
```

```python
import functools
import math

import numpy as np
import jax
import jax.numpy as jnp
from jax import lax
from jax.experimental import pallas as pl
from jax.experimental.pallas import tpu as pltpu

F32 = jnp.float32
BF16 = jnp.bfloat16
I32 = jnp.int32

GLA_HEADS = 4
GLA_RANK = 16
GLA_NORMALIZER = 16.0
ATT_HEADS = 16
ATT_KV_HEADS = 4
IDX_HEADS = 8
IDX_DH = 64
TOPK_MAX = 256
N_BUCKETS = 32
MAX_DISTANCE = 128
N_GROUPS = 8
TOPK_GROUPS = 4
TOP_K = 8
ROUTE_SCALE = 2.5
EPS = 1e-6

LANES = 128
VMEM_LIMIT = 56 * 1024 * 1024
NEG_INF = float("-inf")
INT_MIN = -2 ** 31


def _cparams(*sem):
    return pltpu.CompilerParams(dimension_semantics=sem, vmem_limit_bytes=VMEM_LIMIT)


def _bdot(a, b):
    return jnp.dot(a.astype(BF16), b.astype(BF16), preferred_element_type=F32)


def _bdot_nt(a, b):
    return lax.dot_general(a.astype(BF16), b.astype(BF16), (((1,), (1,)), ((), ())),
                           preferred_element_type=F32)


def _bdot_tn(a, b):
    return lax.dot_general(a.astype(BF16), b.astype(BF16), (((0,), (0,)), ((), ())),
                           preferred_element_type=F32)


def _split3(a):
    hi = a.astype(BF16)
    r1 = a - hi.astype(F32)
    mid = r1.astype(BF16)
    lo = (r1 - mid.astype(F32)).astype(BF16)
    return hi, mid, lo


def _silu(x):
    return x * (1.0 / (1.0 + jnp.exp(-x)))


def _sigmoid(x):
    return 1.0 / (1.0 + jnp.exp(-x))


def _rms(x, g):
    return x * lax.rsqrt(jnp.mean(x * x, axis=-1, keepdims=True) + EPS) * g


def _ada_kernel(c_ref, w_ref, b_ref, o_ref):
    c_hi, c_mid, _ = _split3(_silu(c_ref[...]))
    w_hi, w_mid, _ = _split3(w_ref[...])
    dot = lambda a, b: jnp.dot(a, b, preferred_element_type=F32)
    o_ref[...] = (dot(c_hi, w_mid) + dot(c_mid, w_hi)) + dot(c_hi, w_hi) + b_ref[...]


def ada_mod_all(c, w_ada, b_ada, tn=512):
    depth, d, n6 = w_ada.shape
    rows = c.shape[0]
    return pl.pallas_call(
        _ada_kernel,
        grid=(depth, n6 // tn),
        in_specs=[pl.BlockSpec((rows, d), lambda l, j: (0, 0)),
                  pl.BlockSpec((None, d, tn), lambda l, j: (l, 0, j)),
                  pl.BlockSpec((None, 1, tn), lambda l, j: (l, 0, j))],
        out_specs=pl.BlockSpec((None, rows, tn), lambda l, j: (l, 0, j)),
        out_shape=jax.ShapeDtypeStruct((depth, rows, n6), F32),
        compiler_params=_cparams("parallel", "parallel"),
        name="ada_mod",
    )(c, w_ada, b_ada.reshape(depth, 1, n6))


def _mod_spec(mod, which, tm, seq_len, d):
    if mod.ndim == 3:
        per_seq = seq_len // tm
        return pl.BlockSpec((None, 1, d), lambda i: (i // per_seq, 0, which))
    return pl.BlockSpec((tm, d), lambda i: (i, which))


def _gla_proj_kernel(x_ref, g_ref, sh_ref, sc_ref, w_ref, wg2_ref, bg2_ref, qkvr_ref, la_ref, *, n_main):
    h = _rms(x_ref[...], g_ref[...]) * (1.0 + sc_ref[...]) + sh_ref[...]
    y = _bdot(h, w_ref[...])
    qkvr_ref[...] = y[:, :n_main]
    g1 = y[:, n_main:n_main + GLA_RANK]
    z = _bdot(g1, wg2_ref[...]) + bg2_ref[...]
    la_ref[...] = (jnp.minimum(z, 0.0) - jnp.log(1.0 + jnp.exp(-jnp.abs(z)))) * (1.0 / GLA_NORMALIZER)


def gla_project(x, mod, norm_g, w_in, w_g2, b_g2, tm, seq_len):
    n, d = x.shape
    hk = w_g2.shape[1]
    n_main = w_in.shape[1] - GLA_RANK
    n_pad = (-w_in.shape[1]) % LANES
    w = jnp.pad(w_in, ((0, 0), (0, n_pad))).astype(BF16)
    return pl.pallas_call(
        functools.partial(_gla_proj_kernel, n_main=n_main),
        grid=(n // tm,),
        in_specs=[pl.BlockSpec((tm, d), lambda i: (i, 0)),
                  pl.BlockSpec((1, d), lambda i: (0, 0)),
                  _mod_spec(mod, 0, tm, seq_len, d),
                  _mod_spec(mod, 1, tm, seq_len, d),
                  pl.BlockSpec(w.shape, lambda i: (0, 0)),
                  pl.BlockSpec(w_g2.shape, lambda i: (0, 0)),
                  pl.BlockSpec((1, hk), lambda i: (0, 0))],
        out_specs=[pl.BlockSpec((tm, n_main), lambda i: (i, 0)),
                   pl.BlockSpec((tm, hk), lambda i: (i, 0))],
        out_shape=[jax.ShapeDtypeStruct((n, n_main), F32),
                   jax.ShapeDtypeStruct((n, hk), F32)],
        compiler_params=_cparams("parallel"),
        name="gla_proj",
    )(x, norm_g.reshape(1, d), mod, mod, w, w_g2.astype(BF16), b_g2.reshape(1, hk))


def _gla_level_matrices(c):
    levels = int(math.log2(c))
    t = np.arange(c)[:, None]
    u = np.arange(c)[None, :]
    mats, masks = [], []
    for l in range(levels):
        m = 1 << l
        ref = (t // (2 * m)) * (2 * m) + m - 1
        right = (t % (2 * m)) >= m
        mat = np.where(right, (u > ref) & (u <= t), (u > t) & (u <= ref))
        mats.append(mat)
        masks.append((t // (2 * m) == u // (2 * m)) & right & ((u % (2 * m)) < m))
    mats.append(u <= t)
    masks.append(t == u)
    return (np.stack(mats).astype(np.float32).reshape(-1, c), np.stack(masks).astype(np.float32))


def _gla_chunk_kernel(q_ref, k_ref, v_ref, r_ref, la_ref, mat_ref, mask_ref, gn_ref,
                      o_ref, s_out_ref, st_ref, *, c, dk):
    ci = pl.program_id(1)
    nlev = mask_ref.shape[0] - 1
    hk = la_ref.shape[1]
    dv = v_ref.shape[1] // GLA_HEADS

    @pl.when(ci == 0)
    def _():
        st_ref[...] = jnp.zeros_like(st_ref)

    la = la_ref[...]
    hi = la.astype(BF16)
    lo = (la - hi.astype(F32)).astype(BF16)
    e_all = jnp.dot(mat_ref[...], jnp.concatenate([hi, lo], axis=1), preferred_element_type=F32)
    e_all = e_all[:, :hk] + e_all[:, hk:]

    for h in range(GLA_HEADS):
        ks, vs = slice(h * dk, (h + 1) * dk), slice(h * dv, (h + 1) * dv)
        q = q_ref[:, ks] * (dk ** -0.5)
        k = k_ref[:, ks]
        v = v_ref[:, vs]
        att = jnp.where(mask_ref[nlev] > 0.0, _bdot_nt(q, k), 0.0)
        for l in range(nlev):
            ex = jnp.exp(e_all[l * c:(l + 1) * c, ks])
            att = att + jnp.where(mask_ref[l] > 0.0, _bdot_nt(q * ex, k * ex), 0.0)

        b = e_all[nlev * c:(nlev + 1) * c, ks]
        b_end = b[c - 1:c, :]
        st = st_ref[h]
        o = _bdot(att, v) + _bdot_nt(q * jnp.exp(b), st)
        st_new = st * jnp.exp(b_end) + _bdot_tn(v, k * jnp.exp(b_end - b))
        st_ref[h] = st_new

        on = o * lax.rsqrt(jnp.mean(o * o, axis=-1, keepdims=True) + EPS) * gn_ref[...]
        o_ref[:, vs] = (on * _silu(r_ref[:, vs])).astype(o_ref.dtype)

        @pl.when(ci == pl.num_programs(1) - 1)
        def _():
            s_out_ref[h] = st_new.T


def gla_chunked(qkvr, log_a, gnorm, batch, seq_len, c=128):
    n = qkvr.shape[0]
    hk = log_a.shape[1]
    dk = hk // GLA_HEADS
    hv = (qkvr.shape[1] - 2 * hk) // 2
    dv = hv // GLA_HEADS
    nc = seq_len // c
    mats, masks = _gla_level_matrices(c)
    row = lambda b, ci: b * nc + ci
    return pl.pallas_call(
        functools.partial(_gla_chunk_kernel, c=c, dk=dk),
        grid=(batch, nc),
        in_specs=[pl.BlockSpec((c, hk), lambda b, ci: (row(b, ci), 0)),
                  pl.BlockSpec((c, hk), lambda b, ci: (row(b, ci), 1)),
                  pl.BlockSpec((c, hv), lambda b, ci: (row(b, ci), 2 * hk // hv)),
                  pl.BlockSpec((c, hv), lambda b, ci: (row(b, ci), 2 * hk // hv + 1)),
                  pl.BlockSpec((c, hk), lambda b, ci: (row(b, ci), 0)),
                  pl.BlockSpec(mats.shape, lambda b, ci: (0, 0)),
                  pl.BlockSpec(masks.shape, lambda b, ci: (0, 0, 0)),
                  pl.BlockSpec((1, dv), lambda b, ci: (0, 0))],
        out_specs=[pl.BlockSpec((c, hv), lambda b, ci: (row(b, ci), 0)),
                   pl.BlockSpec((None, GLA_HEADS, dk, dv), lambda b, ci: (b, 0, 0, 0))],
        out_shape=[jax.ShapeDtypeStruct((n, hv), BF16),
                   jax.ShapeDtypeStruct((batch, GLA_HEADS, dk, dv), F32)],
        scratch_shapes=[pltpu.VMEM((GLA_HEADS, dv, dk), F32)],
        compiler_params=_cparams("parallel", "arbitrary"),
        name="gla_chunked",
    )(qkvr, qkvr, qkvr, qkvr, log_a, jnp.asarray(mats, BF16), jnp.asarray(masks), gnorm.reshape(1, dv))


def _gla_step_kernel(q_ref, k_ref, g_ref, v_ref, r_ref, s_ref, gn_ref, o_ref, s_out_ref, *, dk):
    s_new = jnp.exp(g_ref[...]) * s_ref[...] + k_ref[...] * v_ref[...]
    s_out_ref[...] = s_new
    o = jnp.sum(q_ref[...] * (dk ** -0.5) * s_new, axis=0, keepdims=True)
    on = o * lax.rsqrt(jnp.mean(o * o, axis=-1, keepdims=True) + EPS) * gn_ref[...]
    o_ref[...] = (on * _silu(r_ref[...])).astype(o_ref.dtype)


def gla_step(qkvr, log_a, state, gnorm):
    batch = qkvr.shape[0]
    hk = log_a.shape[1]
    dk = hk // GLA_HEADS
    hv = (qkvr.shape[1] - 2 * hk) // 2
    dv = hv // GLA_HEADS
    col = lambda a: a.reshape(batch, GLA_HEADS, dk, 1)
    rowv = lambda a: a.reshape(batch, GLA_HEADS, 1, dv)
    q, k, v, r = (qkvr[:, :hk], qkvr[:, hk:2 * hk], qkvr[:, 2 * hk:2 * hk + hv], qkvr[:, 2 * hk + hv:])
    cspec = pl.BlockSpec((None, None, dk, 1), lambda b, h: (b, h, 0, 0))
    rspec = pl.BlockSpec((None, None, 1, dv), lambda b, h: (b, h, 0, 0))
    sspec = pl.BlockSpec((None, None, dk, dv), lambda b, h: (b, h, 0, 0))
    o, s_new = pl.pallas_call(
        functools.partial(_gla_step_kernel, dk=dk),
        grid=(batch, GLA_HEADS),
        in_specs=[cspec, cspec, cspec, rspec, rspec, sspec, pl.BlockSpec((1, dv), lambda b, h: (0, 0))],
        out_specs=[rspec, sspec],
        out_shape=[jax.ShapeDtypeStruct((batch, GLA_HEADS, 1, dv), BF16),
                   jax.ShapeDtypeStruct(state.shape, F32)],
        compiler_params=_cparams("parallel", "parallel"),
        name="gla_step",
    )(col(q), col(k), col(log_a), rowv(v), rowv(r), state, gnorm.reshape(1, dv))
    return o.reshape(batch, hv), s_new


def _out_proj_kernel(o_ref, w_ref, x_ref, gt_ref, xo_ref):
    xo_ref[...] = x_ref[...] + gt_ref[...] * _bdot(o_ref[...], w_ref[...])


def out_project(o, w_out, x, mod, tm, seq_len):
    n, d = x.shape
    kdim = o.shape[1]
    return pl.pallas_call(
        _out_proj_kernel,
        grid=(n // tm,),
        in_specs=[pl.BlockSpec((tm, kdim), lambda i: (i, 0)),
                  pl.BlockSpec((kdim, d), lambda i: (0, 0)),
                  pl.BlockSpec((tm, d), lambda i: (i, 0)),
                  _mod_spec(mod, 2, tm, seq_len, d)],
        out_specs=pl.BlockSpec((tm, d), lambda i: (i, 0)),
        out_shape=jax.ShapeDtypeStruct((n, d), F32),
        compiler_params=_cparams("parallel"),
        name="out_proj",
    )(o, w_out.astype(BF16), x, mod)


HI_MASK = -65536


def _pack_rows(val):
    half = val.shape[1] // 2
    bits = lambda a: pltpu.bitcast(a.astype(BF16).astype(F32), I32)
    return lax.shift_right_logical(bits(val[:, :half]), 16) | (bits(val[:, half:]) & HI_MASK)


def _unpack_rows(words):
    return jnp.concatenate([pltpu.bitcast(words << 16, F32), pltpu.bitcast(words & HI_MASK, F32)], axis=1)


def _rows_load(ref, per, first=0, rows=None):
    rows = ref.shape[0] // per - first if rows is None else rows
    return jnp.concatenate([ref[pl.ds(first * per + j, rows, stride=per), :] for j in range(per)], axis=1)


def _rows_store(ref, words):
    rows = words.shape[0]
    per = words.shape[1] // LANES
    for j in range(per):
        ref[pl.ds(j, rows, stride=per), :] = words[:, j * LANES:(j + 1) * LANES]


def _row_tile(ref, r, per):
    return ref.at[pl.ds(pl.multiple_of(r * per, per), per), :]


def _modulate_kernel(x_ref, g_ref, sh_ref, sc_ref, h_ref, hp_ref):
    h = _rms(x_ref[...], g_ref[...]) * (1.0 + sc_ref[...]) + sh_ref[...]
    h_ref[...] = h
    _rows_store(hp_ref, _pack_rows(h))


def modulate_rows(x, mod, norm_g, tm, seq_len):
    n, d = x.shape
    per = d // 2 // LANES
    return pl.pallas_call(
        _modulate_kernel,
        grid=(n // tm,),
        in_specs=[pl.BlockSpec((tm, d), lambda i: (i, 0)),
                  pl.BlockSpec((1, d), lambda i: (0, 0)),
                  _mod_spec(mod, 3, tm, seq_len, d),
                  _mod_spec(mod, 4, tm, seq_len, d)],
        out_specs=[pl.BlockSpec((tm, d), lambda i: (i, 0)),
                   pl.BlockSpec((tm * per, LANES), lambda i: (i, 0))],
        out_shape=[jax.ShapeDtypeStruct((n, d), F32), jax.ShapeDtypeStruct((n * per, LANES), I32)],
        compiler_params=_cparams("parallel"),
        name="modulate_ffn",
    )(x, norm_g.reshape(1, d), mod, mod)


def _dsa_proj_kernel(x_ref, g_ref, sh_ref, sc_ref, w_ref, wwi_ref, lg_ref, lb_ref,
                     q_ref, qi_ref, wit_ref, k_ref, v_ref, ki_ref, *seq_refs, cuts, dh):
    h = (_rms(x_ref[...], g_ref[...]) * (1.0 + sc_ref[...]) + sh_ref[...]).astype(BF16)
    y = jnp.dot(h, w_ref[...], preferred_element_type=F32)
    c0, c1, c2, c3, c4 = cuts
    q_ref[...] = (y[:, :c0] * (dh ** -0.5)).astype(BF16)
    qi_ref[...] = (y[:, c2:c3] * (IDX_DH ** -0.5)).astype(BF16)
    wit_ref[...] = _bdot_nt(wwi_ref[...], h) * (IDX_HEADS ** -0.5)
    k = y[:, c0:c1]
    v = y[:, c1:c2]
    ki = y[:, c3:c4]
    mu = jnp.mean(ki, axis=-1, keepdims=True)
    var = jnp.mean((ki - mu) * (ki - mu), axis=-1, keepdims=True)
    kin = (ki - mu) * lax.rsqrt(var + EPS) * lg_ref[...] + lb_ref[...]
    if not seq_refs:
        k_ref[...] = k
        v_ref[...] = v
        ki_ref[...] = kin
    else:
        kib_ref, kg_ref, vt_ref = seq_refs
        vt = v.T
        k_ref[...] = k.T
        v_ref[...] = vt
        ki_ref[...] = jnp.concatenate([kin, jnp.zeros_like(kin)], axis=1).T[:kin.shape[1]]
        kib_ref[...] = kin.astype(BF16)
        for g in range(ATT_KV_HEADS):
            kg_ref[g] = k[:, g * dh:(g + 1) * dh].astype(BF16)
        vt_ref[...] = vt.astype(BF16)


def dsa_project(x, mod, norm_g, w_in, ln_g, ln_b, tm, seq_len, seq_layouts):
    n, d = x.shape
    dh = d // ATT_HEADS
    nkv = ATT_KV_HEADS * dh
    nqi = IDX_HEADS * IDX_DH
    cuts = (d, d + nkv, d + 2 * nkv, d + 2 * nkv + nqi, d + 2 * nkv + nqi + IDX_DH)
    w = jnp.pad(w_in[:, :cuts[4]], ((0, 0), (0, (-cuts[4]) % LANES))).astype(BF16)
    w_wi_t = w_in[:, cuts[4]:cuts[4] + IDX_HEADS].T.astype(BF16)
    row = lambda wd: pl.BlockSpec((tm, wd), lambda i: (i, 0))
    out_specs = [row(d), row(nqi), pl.BlockSpec((IDX_HEADS, tm), lambda i: (0, i))]
    out_shape = [jax.ShapeDtypeStruct((n, d), BF16), jax.ShapeDtypeStruct((n, nqi), BF16),
                 jax.ShapeDtypeStruct((IDX_HEADS, n), F32)]
    if not seq_layouts:
        out_specs += [row(nkv), row(nkv), row(IDX_DH)]
        out_shape += [jax.ShapeDtypeStruct((n, wd), F32) for wd in (nkv, nkv, IDX_DH)]
    else:
        per_seq = seq_len // tm
        tmin = lambda wd: pl.BlockSpec((None, wd, tm), lambda i: (i // per_seq, 0, i % per_seq))
        out_specs += [tmin(nkv), tmin(nkv), tmin(IDX_DH),
                      row(IDX_DH),
                      pl.BlockSpec((ATT_KV_HEADS, tm, dh), lambda i: (0, i, 0)),
                      pl.BlockSpec((None, nkv, tm), lambda i: (i, 0, 0))]
        out_shape += [jax.ShapeDtypeStruct((n // seq_len, wd, seq_len), F32) for wd in (nkv, nkv, IDX_DH)]
        out_shape += [jax.ShapeDtypeStruct((n, IDX_DH), BF16),
                      jax.ShapeDtypeStruct((ATT_KV_HEADS, n, dh), BF16),
                      jax.ShapeDtypeStruct((n // tm, nkv, tm), BF16)]
    return pl.pallas_call(
        functools.partial(_dsa_proj_kernel, cuts=cuts, dh=dh),
        grid=(n // tm,),
        in_specs=[pl.BlockSpec((tm, d), lambda i: (i, 0)),
                  pl.BlockSpec((1, d), lambda i: (0, 0)),
                  _mod_spec(mod, 0, tm, seq_len, d),
                  _mod_spec(mod, 1, tm, seq_len, d),
                  pl.BlockSpec(w.shape, lambda i: (0, 0)),
                  pl.BlockSpec(w_wi_t.shape, lambda i: (0, 0)),
                  pl.BlockSpec((1, IDX_DH), lambda i: (0, 0)),
                  pl.BlockSpec((1, IDX_DH), lambda i: (0, 0))],
        out_specs=out_specs,
        out_shape=out_shape,
        compiler_params=_cparams("parallel"),
        name="dsa_proj",
    )(x, norm_g.reshape(1, d), mod, mod, w, w_wi_t, ln_g.reshape(1, IDX_DH), ln_b.reshape(1, IDX_DH))


def _t5_bucket_np(dist):
    dist = np.maximum(dist, 0)
    max_exact = N_BUCKETS // 2
    ratio = np.log(np.maximum(dist, max_exact).astype(np.float32) / max_exact) / math.log(MAX_DISTANCE / max_exact)
    large = np.minimum(max_exact + (ratio * (N_BUCKETS - max_exact)).astype(np.int32), N_BUCKETS - 1)
    return np.where(dist < max_exact, dist, large).astype(np.int32)


def _bias_table_kernel(rb_ref, bk_ref, o_ref):
    h = pl.program_id(0)
    for kind in range(3):
        bk = bk_ref[kind]
        acc = jnp.zeros(bk.shape, F32)
        for b in range(N_BUCKETS):
            acc = jnp.where(bk == b, rb_ref[b, h], acc)
        o_ref[kind] = acc


def bias_tables(rel_bias, tq):
    s = np.arange(tq)[:, None]
    t = np.arange(tq)[None, :]
    buckets = np.stack([_t5_bucket_np(t - s), _t5_bucket_np(tq + t - s), _t5_bucket_np(2 * tq + t - s)])
    assert tq >= MAX_DISTANCE and (buckets[2] == N_BUCKETS - 1).all()
    return pl.pallas_call(
        _bias_table_kernel,
        grid=(ATT_HEADS,),
        in_specs=[pl.BlockSpec(memory_space=pltpu.SMEM),
                  pl.BlockSpec(buckets.shape, lambda h: (0, 0, 0))],
        out_specs=pl.BlockSpec((None, 3, tq, tq), lambda h: (h, 0, 0, 0)),
        out_shape=jax.ShapeDtypeStruct((ATT_HEADS, 3, tq, tq), F32),
        compiler_params=_cparams("parallel"),
        name="t5_bias_tiles",
    )(rel_bias, jnp.asarray(buckets))


def _order_key(score):
    score = jnp.where(score == 0.0, 0.0, score)
    bits = pltpu.bitcast(score, I32)
    return bits ^ ((bits >> 31) & 0x7FFFFFFF)


def _dsa_prompt_kernel(q_ref, qi_ref, wit_ref, kib_ref, kg_ref, vt_ref, bias_ref, tril_ref, o_ref,
                       key_ref, selb_ref, qis_ref, qg_ref, sc_ref, ot_ref, *, tq, dh, topk):
    i = pl.program_id(1)
    nk = i + 1
    g_heads = ATT_HEADS // ATT_KV_HEADS
    s_id = lax.broadcasted_iota(I32, (tq, tq), 0)
    t_id = lax.broadcasted_iota(I32, (tq, tq), 1)
    fold8 = lambda a, op: op(a.reshape(a.shape[0] // 8, 8, a.shape[1]), axis=0)

    for h in range(IDX_HEADS):
        qis_ref[h * tq:(h + 1) * tq, :] = qi_ref[:, h * IDX_DH:(h + 1) * IDX_DH]
    for hh in range(ATT_HEADS):
        g, a = divmod(hh, g_heads)
        qg_ref[g, a * tq:(a + 1) * tq, :] = q_ref[:, hh * dh:(hh + 1) * dh]
    w8 = wit_ref[...]

    def score_body(j, carry):
        kij = kib_ref[pl.ds(pl.multiple_of(j * tq, tq), tq), :]
        lg = _bdot_nt(kij, qis_ref[...])
        acc = jnp.zeros((tq, tq), F32)
        for h in range(IDX_HEADS):
            acc = acc + jnp.maximum(lg[:, h * tq:(h + 1) * tq], 0.0) * w8[h:h + 1, :]
        valid = (s_id + j * tq) <= (t_id + i * tq)
        key_ref[j] = _order_key(jnp.where(valid, acc, NEG_INF))
        return carry

    lax.fori_loop(0, nk, score_body, 0)
    n_pairs = (nk + 1) // 2

    @pl.when(nk % 2 == 1)
    def _():
        pad = jnp.minimum(nk, key_ref.shape[0] - 1)
        key_ref[pad] = jnp.full((tq, tq), INT_MIN, I32)
        selb_ref[pad] = jnp.full((tq, tq), NEG_INF, F32)

    def count(pred):
        def body(jj, acc8):
            ones = jnp.where(pred(key_ref[2 * jj]), 1.0, 0.0) + jnp.where(pred(key_ref[2 * jj + 1]), 1.0, 0.0)
            return acc8 + fold8(ones, jnp.sum)
        return jnp.sum(lax.fori_loop(0, n_pairs, body, jnp.zeros((8, tq), F32)), axis=0, keepdims=True)

    def bit_body(it, ans):
        cand = ans | (jnp.int32(1) << (31 - it))
        cand_s = cand ^ INT_MIN
        return jnp.where(count(lambda kj: kj >= cand_s) >= topk, cand, ans)

    thr = lax.fori_loop(0, 32, bit_body, jnp.zeros((1, tq), I32)) ^ INT_MIN
    need = topk - count(lambda kj: kj > thr)
    n_tie = count(lambda kj: kj == thr)

    def causal(j):
        return (s_id + j * tq) <= (t_id + i * tq)

    some_ties_dropped = jnp.max(n_tie - need) > 0.0

    @pl.when(jnp.logical_not(some_ties_dropped))
    def _():
        def sel_body(j, carry):
            selb_ref[j] = jnp.where(jnp.logical_and(key_ref[j] >= thr, causal(j)), 0.0, NEG_INF)
            return carry
        lax.fori_loop(0, nk, sel_body, 0)

    @pl.when(some_ties_dropped)
    def _():
        def sel_body(j, run):
            kj = key_ref[j]
            tie = kj == thr
            tie_f = jnp.where(tie, 1.0, 0.0)
            before = run + jnp.dot(tril_ref[...], tie_f.astype(BF16), preferred_element_type=F32)
            take = jnp.where(kj > thr, 0.0, jnp.where(tie, jnp.where(before < need, 0.0, NEG_INF), NEG_INF))
            selb_ref[j] = jnp.where(causal(j), take, NEG_INF)
            return run + jnp.sum(tie_f, axis=0, keepdims=True)
        lax.fori_loop(0, nk, sel_body, jnp.zeros((1, tq), F32))

    ones_rows = jnp.ones((16, 2 * tq), BF16)
    for g in range(ATT_KV_HEADS):
        qg = qg_ref[g]

        def logits_body(jj, mx8):
            base = pl.multiple_of(jj * 2 * tq, 2 * tq)
            s = _bdot_nt(kg_ref[g, pl.ds(base, 2 * tq), :], qg)
            j0 = 2 * jj
            halves = []
            for c in range(2):
                kind = jnp.clip(i - j0 - c, 0, 2)
                sel = selb_ref[j0 + c]
                halves.append(jnp.concatenate([bias_ref[g * g_heads + a, kind] + sel
                                               for a in range(g_heads)], axis=1))
            s = s + jnp.concatenate(halves, axis=0)
            sc_ref[jj] = s
            return jnp.maximum(mx8, fold8(s, jnp.max))

        mx8 = lax.fori_loop(0, n_pairs, logits_body, jnp.full((8, g_heads * tq), NEG_INF, F32))
        m = jnp.max(mx8, axis=0, keepdims=True)
        m = jnp.where(m == NEG_INF, 0.0, m)

        def pv_body(jj, acc):
            p = jnp.exp(sc_ref[jj] - m).astype(BF16)
            lhs = jnp.concatenate([vt_ref[jj, g * dh:(g + 1) * dh, :], ones_rows], axis=0)
            return acc + jnp.dot(lhs, p, preferred_element_type=F32)

        acc = lax.fori_loop(0, n_pairs, pv_body, jnp.zeros((dh + 16, g_heads * tq), F32))
        og = acc[:dh] / acc[dh:dh + 1]
        for a in range(g_heads):
            hh = g * g_heads + a
            ot_ref[hh * dh:(hh + 1) * dh, :] = og[:, a * tq:(a + 1) * tq]
    o_ref[...] = ot_ref[...].T.astype(o_ref.dtype)


def dsa_prompt(q, qi, wit, kib, kg, vt, bias_tab, batch, seq_len, tq=128):
    n, d = q.shape
    dh = d // ATT_HEADS
    nq = seq_len // tq
    g_heads = ATT_HEADS // ATT_KV_HEADS
    assert nq % 2 == 0 and vt.shape[2] == 2 * tq
    topk = min(TOPK_MAX, seq_len // 4)
    tril = np.tril(np.ones((tq, tq), np.float32), -1)
    blk_spec = lambda width: pl.BlockSpec((tq, width), lambda b, i: (b * nq + i, 0))
    return pl.pallas_call(
        functools.partial(_dsa_prompt_kernel, tq=tq, dh=dh, topk=float(topk)),
        grid=(batch, nq),
        in_specs=[blk_spec(d), blk_spec(qi.shape[1]),
                  pl.BlockSpec((IDX_HEADS, tq), lambda b, i: (0, b * nq + i)),
                  pl.BlockSpec((seq_len, IDX_DH), lambda b, i: (b, 0)),
                  pl.BlockSpec((ATT_KV_HEADS, seq_len, dh), lambda b, i: (0, b, 0)),
                  pl.BlockSpec((nq // 2,) + vt.shape[1:], lambda b, i: (b, 0, 0)),
                  pl.BlockSpec(bias_tab.shape, lambda b, i: (0, 0, 0, 0)),
                  pl.BlockSpec((tq, tq), lambda b, i: (0, 0))],
        out_specs=blk_spec(d),
        out_shape=jax.ShapeDtypeStruct((n, d), BF16),
        scratch_shapes=[pltpu.VMEM((nq, tq, tq), I32),
                        pltpu.VMEM((nq, tq, tq), F32),
                        pltpu.VMEM((IDX_HEADS * tq, IDX_DH), BF16),
                        pltpu.VMEM((ATT_KV_HEADS, g_heads * tq, dh), BF16),
                        pltpu.VMEM((nq // 2, 2 * tq, g_heads * tq), F32),
                        pltpu.VMEM((d, tq), F32)],
        compiler_params=_cparams("parallel", "arbitrary"),
        name="dsa_prompt",
    )(q, qi, wit, kib, kg, vt, bias_tab, jnp.asarray(tril, BF16))


def _router_kernel(h_ref, wr_ref, rb_ref, tri_ref, eidx_ref, wsel_ref, pos_ref, cnt_ref, carry_ref, *, n_exp):
    @pl.when(pl.program_id(0) == 0)
    def _():
        carry_ref[...] = jnp.zeros_like(carry_ref)

    h_hi, h_mid, h_lo = _split3(h_ref[...])
    w_hi, w_mid, w_lo = _split3(wr_ref[...])
    nt = lambda a, b: lax.dot_general(a, b, (((1,), (1,)), ((), ())), preferred_element_type=F32)
    logits = ((nt(w_lo, h_hi) + nt(w_hi, h_lo) + nt(w_mid, h_mid))
              + (nt(w_hi, h_mid) + nt(w_mid, h_hi))) + nt(w_hi, h_hi)
    s = _sigmoid(logits)
    sel = s + rb_ref[...]
    tm = sel.shape[1]
    gsz = n_exp // N_GROUPS
    io_g = lax.broadcasted_iota(I32, (gsz, tm), 0)

    gs = []
    for g in range(N_GROUPS):
        grp = sel[g * gsz:(g + 1) * gsz, :]
        m1 = jnp.max(grp, axis=0, keepdims=True)
        i1 = jnp.min(jnp.where(grp == m1, io_g, gsz), axis=0, keepdims=True)
        m2 = jnp.max(jnp.where(io_g == i1, NEG_INF, grp), axis=0, keepdims=True)
        gs.append(m1 + m2)
    masked = []
    for g in range(N_GROUPS):
        rank = jnp.zeros((1, tm), F32)
        for o in range(N_GROUPS):
            if o == g:
                continue
            ahead = (gs[o] >= gs[g]) if o < g else (gs[o] > gs[g])
            rank = rank + jnp.where(ahead, 1.0, 0.0)
        keep = jnp.where(rank < TOPK_GROUPS, 0.0, NEG_INF)
        masked.append(sel[g * gsz:(g + 1) * gsz, :] + keep)
    msel = jnp.concatenate(masked, axis=0)

    io_e = lax.broadcasted_iota(I32, (n_exp, tm), 0)
    chosen = jnp.zeros((n_exp, tm), F32)
    picks, weights = [], []
    for _ in range(TOP_K):
        m = jnp.max(msel, axis=0, keepdims=True)
        ei = jnp.min(jnp.where(msel == m, io_e, n_exp), axis=0, keepdims=True)
        pick = io_e == ei
        weights.append(jnp.sum(jnp.where(pick, s, 0.0), axis=0, keepdims=True))
        picks.append(ei)
        chosen = jnp.where(pick, 1.0, chosen)
        msel = jnp.where(pick, NEG_INF, msel)
    wsum = weights[0]
    for w in weights[1:]:
        wsum = wsum + w

    rank_in_expert = carry_ref[...] + jnp.dot(chosen.astype(BF16), tri_ref[...], preferred_element_type=F32)
    carry_new = carry_ref[...] + jnp.sum(chosen, axis=1, keepdims=True)
    carry_ref[...] = carry_new
    cnt_ref[...] = carry_new
    for kk in range(TOP_K):
        eidx_ref[kk:kk + 1, :] = picks[kk]
        wsel_ref[kk:kk + 1, :] = weights[kk] / wsum * ROUTE_SCALE
        pk = jnp.sum(jnp.where(io_e == picks[kk], rank_in_expert, 0.0), axis=0, keepdims=True)
        pos_ref[kk:kk + 1, :] = pk.astype(I32)


def moe_route(h, w_router, router_bias, tm):
    n, d = h.shape
    n_exp = w_router.shape[1]
    tri = np.triu(np.ones((tm, tm), np.float32), 1)
    row8 = lambda dt: jax.ShapeDtypeStruct((TOP_K, n), dt)
    return pl.pallas_call(
        functools.partial(_router_kernel, n_exp=n_exp),
        grid=(n // tm,),
        in_specs=[pl.BlockSpec((tm, d), lambda i: (i, 0)),
                  pl.BlockSpec((n_exp, d), lambda i: (0, 0)),
                  pl.BlockSpec((n_exp, 1), lambda i: (0, 0)),
                  pl.BlockSpec((tm, tm), lambda i: (0, 0))],
        out_specs=[pl.BlockSpec((TOP_K, tm), lambda i: (0, i)),
                   pl.BlockSpec((TOP_K, tm), lambda i: (0, i)),
                   pl.BlockSpec((TOP_K, tm), lambda i: (0, i)),
                   pl.BlockSpec((n_exp, 1), lambda i: (0, 0))],
        out_shape=[row8(I32), row8(F32), row8(I32), jax.ShapeDtypeStruct((n_exp, 1), F32)],
        scratch_shapes=[pltpu.VMEM((n_exp, 1), F32)],
        compiler_params=_cparams("arbitrary"),
        name="moe_router",
    )(h, w_router.T, router_bias.reshape(n_exp, 1), jnp.asarray(tri, BF16))


def _dispatch_kernel(dest_ref, h_ref, xs_ref, sem, *, per):
    tm = h_ref.shape[0] // per

    def row_copy(t, dst_row):
        return pltpu.make_async_copy(_row_tile(h_ref, t, per), _row_tile(xs_ref, dst_row, per), sem)

    def issue(t, c):
        for kk in range(TOP_K):
            row_copy(t, dest_ref[kk, t]).start()
        return c

    def drain(t, c):
        for kk in range(TOP_K):
            row_copy(t, dest_ref[kk, t]).wait()
        return c

    lax.fori_loop(0, tm, issue, 0)
    lax.fori_loop(0, tm, drain, 0)


def moe_dispatch(h, dest, tm):
    n = dest.shape[1]
    per = h.shape[0] // n
    return pl.pallas_call(
        functools.partial(_dispatch_kernel, per=per),
        grid=(n // tm,),
        in_specs=[pl.BlockSpec((TOP_K, tm), lambda i: (0, i), memory_space=pltpu.SMEM),
                  pl.BlockSpec((tm * per, LANES), lambda i: (i, 0))],
        out_specs=pl.BlockSpec(memory_space=pl.ANY),
        out_shape=jax.ShapeDtypeStruct((n * TOP_K * per, LANES), h.dtype),
        scratch_shapes=[pltpu.SemaphoreType.DMA(())],
        compiler_params=_cparams("arbitrary"),
        name="moe_dispatch",
    )(dest, h)


def _expert_kernel(blk_ref, exp_ref, lo_ref, hi_ref, x_ref, wg_ref, wu_ref, wd_ref, y_ref):
    j = pl.program_id(0)
    per = wg_ref.shape[0] // 2 // LANES
    blk = x_ref.shape[0] // per
    lo = lo_ref[j] - blk_ref[j] * blk
    hi = hi_ref[j] - blk_ref[j] * blk

    @pl.when(hi > lo)
    def _():
        x = _unpack_rows(_rows_load(x_ref, per)).astype(BF16)
        g = jnp.dot(x, wg_ref[...].astype(BF16), preferred_element_type=F32)
        u = jnp.dot(x, wu_ref[...].astype(BF16), preferred_element_type=F32)
        y = _bdot(_silu(g) * u, wd_ref[...])
        row = lax.broadcasted_iota(I32, y.shape, 0)
        y = jnp.where(jnp.logical_and(row >= lo, row < hi), y, 0.0)

        @pl.when(lo == 0)
        def _():
            _rows_store(y_ref, _pack_rows(y))

        @pl.when(lo != 0)
        def _():
            _rows_store(y_ref, _pack_rows(_unpack_rows(_rows_load(y_ref, per)) + y))


def moe_experts(xs, seg_blk, seg_exp, seg_lo, seg_hi, w_gate, w_up, w_down, layer, blk):
    d, ff = w_gate.shape[2:]
    wspec = lambda shape: pl.BlockSpec((None, None) + shape, lambda j, sb, se, lo, hi: (layer, se[j], 0, 0))
    rows = pl.BlockSpec((blk * (d // 2 // LANES), LANES), lambda j, sb, se, lo, hi: (sb[j], 0))
    return pl.pallas_call(
        _expert_kernel,
        grid_spec=pltpu.PrefetchScalarGridSpec(
            num_scalar_prefetch=4, grid=(seg_blk.shape[0],),
            in_specs=[rows, wspec((d, ff)), wspec((d, ff)), wspec((ff, d))],
            out_specs=rows),
        out_shape=jax.ShapeDtypeStruct(xs.shape, xs.dtype),
        compiler_params=_cparams("arbitrary"),
        name="moe_experts",
    )(seg_blk, seg_exp, seg_lo, seg_hi, xs, w_gate, w_up, w_down)


def _combine_kernel(dest_ref, ys_ref, wsel_ref, h_ref, x_ref, gt_ref, sg_ref, su_ref, sd_ref, nf_ref,
                    o_ref, buf_ref, sem, *, final_norm):
    tm, d = x_ref.shape
    per = d // 2 // LANES

    def row_copy(t, kk):
        return pltpu.make_async_copy(_row_tile(ys_ref, dest_ref[kk, t], per),
                                     _row_tile(buf_ref, kk * tm + t, per), sem)

    def issue(t, c):
        for kk in range(TOP_K):
            row_copy(t, kk).start()
        return c

    def drain(t, c):
        for kk in range(TOP_K):
            row_copy(t, kk).wait()
        return c

    lax.fori_loop(0, tm, issue, 0)
    h = _unpack_rows(_rows_load(h_ref, per)).astype(BF16)
    g = jnp.dot(h, sg_ref[...], preferred_element_type=F32)
    u = jnp.dot(h, su_ref[...], preferred_element_type=F32)
    y = _bdot(_silu(g) * u, sd_ref[...])
    lax.fori_loop(0, tm, drain, 0)
    w = wsel_ref[...]
    for kk in range(TOP_K):
        y = y + w[:, kk:kk + 1] * _unpack_rows(_rows_load(buf_ref, per, first=kk * tm, rows=tm))
    x_new = x_ref[...] + gt_ref[...] * y
    o_ref[...] = _rms(x_new, nf_ref[...]) if final_norm else x_new


def moe_combine(ys, dest, wsel_t, h, x, mod, ws_gate, ws_up, ws_down, norm_final, final_norm, tm, seq_len):
    n, d = x.shape
    ff = ws_gate.shape[1]
    return pl.pallas_call(
        functools.partial(_combine_kernel, final_norm=final_norm),
        grid=(n // tm,),
        in_specs=[pl.BlockSpec((TOP_K, tm), lambda i: (0, i), memory_space=pltpu.SMEM),
                  pl.BlockSpec(memory_space=pl.ANY),
                  pl.BlockSpec((tm, TOP_K), lambda i: (i, 0)),
                  pl.BlockSpec((tm * (d // 2 // LANES), LANES), lambda i: (i, 0)),
                  pl.BlockSpec((tm, d), lambda i: (i, 0)),
                  _mod_spec(mod, 5, tm, seq_len, d),
                  pl.BlockSpec((d, ff), lambda i: (0, 0)),
                  pl.BlockSpec((d, ff), lambda i: (0, 0)),
                  pl.BlockSpec((ff, d), lambda i: (0, 0)),
                  pl.BlockSpec((1, d), lambda i: (0, 0))],
        out_specs=pl.BlockSpec((tm, d), lambda i: (i, 0)),
        out_shape=jax.ShapeDtypeStruct((n, d), F32),
        scratch_shapes=[pltpu.VMEM((TOP_K * tm * (d // 2 // LANES), LANES), I32), pltpu.SemaphoreType.DMA(())],
        compiler_params=_cparams("arbitrary"),
        name="moe_combine",
    )(dest, ys, wsel_t, h, x, mod, ws_gate.astype(BF16), ws_up.astype(BF16), ws_down.astype(BF16),
      norm_final.reshape(1, d))


def _slot_kernel(start_ref, eidx_ref, pos_ref, dest_ref):
    e = eidx_ref[...]
    base = jnp.zeros(e.shape, I32)
    for ex in range(start_ref.shape[0]):
        base = jnp.where(e == ex, start_ref[ex], base)
    dest_ref[...] = base + pos_ref[...]


def moe_slots(start, eidx, pos, tm):
    n = eidx.shape[1]
    spec = pl.BlockSpec((TOP_K, tm), lambda i: (0, i))
    return pl.pallas_call(
        _slot_kernel,
        grid=(n // tm,),
        in_specs=[pl.BlockSpec(memory_space=pltpu.SMEM), spec, spec],
        out_specs=spec,
        out_shape=jax.ShapeDtypeStruct(eidx.shape, I32),
        compiler_params=_cparams("parallel"),
        name="moe_slots",
    )(start, eidx, pos)


def moe_ffn(x, mod, norm_g, w_router, router_bias, w_gate, w_up, w_down, layer, ws_gate, ws_up, ws_down,
            norm_final, final_norm, tm, seq_len, blk):
    n, d = x.shape
    n_exp = w_router.shape[1]
    h32, h = modulate_rows(x, mod, norm_g, tm, seq_len)
    eidx, wsel, pos, counts = moe_route(h32, w_router, router_bias, tm)
    n_rows = n * TOP_K
    n_blocks = n_rows // blk
    counts = counts.reshape(n_exp).astype(I32)
    end = jnp.cumsum(counts)
    start = end - counts
    dest = moe_slots(start, eidx, pos, min(tm, n))
    seg_lo = jnp.sort(jnp.concatenate([jnp.arange(n_blocks, dtype=I32) * blk, start[1:]]))
    seg_hi = jnp.concatenate([seg_lo[1:], jnp.full((1,), n_rows, I32)])
    seg_blk = jnp.minimum(seg_lo // blk, n_blocks - 1)
    seg_exp = jnp.minimum(jnp.sum((end[None, :] <= seg_lo[:, None]).astype(I32), axis=1), n_exp - 1)
    xs = moe_dispatch(h, dest, tm)
    ys = moe_experts(xs, seg_blk, seg_exp, seg_lo, seg_hi, w_gate, w_up, w_down, layer, blk)
    return moe_combine(ys, dest, wsel.T, h, x, mod, ws_gate, ws_up, ws_down, norm_final, final_norm,
                       min(tm, 128), seq_len)


def _sample_score_kernel(pt_ref, qi_ref, w_ref, *refs):
    page_refs, o_ref = refs[:-1], refs[-1]
    kp = jnp.concatenate([r[...] for r in page_refs], axis=1)
    lg = _bdot(qi_ref[...], kp)
    sc = jnp.sum(jnp.maximum(lg, 0.0) * w_ref[...], axis=0, keepdims=True)
    page = page_refs[0].shape[1]
    for p in range(len(page_refs)):
        o_ref[p:p + 1, :] = sc[:, p * page:(p + 1) * page]


def _page_specs(layer, pg, block):
    zeros = (0,) * (len(block) - 2)
    return [pl.BlockSpec(block, lambda b, g, pt, p=p: (layer, pt[b, g * pg + p]) + zeros) for p in range(pg)]


def dsa_sample_scores(qi3, wi3, cache_ki_t, layer, page_table, pg):
    b, n_pages = page_table.shape
    page = cache_ki_t.shape[3]
    return pl.pallas_call(
        _sample_score_kernel,
        grid_spec=pltpu.PrefetchScalarGridSpec(
            num_scalar_prefetch=1, grid=(b, n_pages // pg),
            in_specs=[pl.BlockSpec((None, IDX_HEADS, IDX_DH), lambda b, g, pt: (b, 0, 0)),
                      pl.BlockSpec((None, IDX_HEADS, 1), lambda b, g, pt: (b, 0, 0))]
                     + _page_specs(layer, pg, (None, None, IDX_DH, page)),
            out_specs=pl.BlockSpec((None, pg, page), lambda b, g, pt: (b, g, 0))),
        out_shape=jax.ShapeDtypeStruct((b, n_pages, page), F32),
        compiler_params=_cparams("parallel", "arbitrary"),
        name="dsa_sample_scores",
    )(page_table, qi3, wi3, *([cache_ki_t] * pg))


def _sample_select_kernel(sc_ref, qi_ref, w_ref, kin_ref, triu_ref, tril_ref, selb_ref, selbn_ref, *, topk):
    lg_new = jnp.sum(qi_ref[...].astype(F32) * kin_ref[...].astype(BF16).astype(F32),
                     axis=1, keepdims=True)
    s_new = jnp.sum(jnp.maximum(lg_new, 0.0) * w_ref[...], axis=0, keepdims=True)
    keys = _order_key(sc_ref[...])
    key_new = _order_key(s_new)
    total = lambda a: jnp.sum(jnp.sum(a, axis=0, keepdims=True), axis=1, keepdims=True)

    def count(pred):
        return total(jnp.where(pred(keys), 1.0, 0.0)) + jnp.where(pred(key_new), 1.0, 0.0)

    def bit_body(it, ans):
        cand = ans | (jnp.int32(1) << (31 - it))
        cand_s = cand ^ INT_MIN
        return jnp.where(count(lambda kj: kj >= cand_s) >= topk, cand, ans)

    thr = lax.fori_loop(0, 32, bit_body, jnp.zeros((1, 1), I32)) ^ INT_MIN
    need = topk - count(lambda kj: kj > thr)
    tie = keys == thr
    tie_f = jnp.where(tie, 1.0, 0.0)
    tie_b = tie_f.astype(BF16)
    in_row = jnp.dot(tie_b, triu_ref[...], preferred_element_type=F32)
    rows_before = jnp.sum(jnp.dot(tril_ref[...], tie_b, preferred_element_type=F32), axis=1, keepdims=True)
    before = in_row + rows_before
    selb_ref[...] = jnp.where(keys > thr, 0.0, jnp.where(tie, jnp.where(before < need, 0.0, NEG_INF), NEG_INF))
    selbn_ref[...] = jnp.where(key_new > thr, 0.0,
                               jnp.where(key_new == thr, jnp.where(total(tie_f) < need, 0.0, NEG_INF), NEG_INF))


def dsa_sample_select(scores, qi3, wi3, ki_new, topk):
    b, n_pages, page = scores.shape
    triu = np.triu(np.ones((page, page), np.float32), 1)
    tril = np.tril(np.ones((n_pages, n_pages), np.float32), -1)
    return pl.pallas_call(
        functools.partial(_sample_select_kernel, topk=float(topk)),
        grid=(b,),
        in_specs=[pl.BlockSpec((None, n_pages, page), lambda i: (i, 0, 0)),
                  pl.BlockSpec((None, IDX_HEADS, IDX_DH), lambda i: (i, 0, 0)),
                  pl.BlockSpec((None, IDX_HEADS, 1), lambda i: (i, 0, 0)),
                  pl.BlockSpec((None, 1, IDX_DH), lambda i: (i, 0, 0)),
                  pl.BlockSpec((page, page), lambda i: (0, 0)),
                  pl.BlockSpec((n_pages, n_pages), lambda i: (0, 0))],
        out_specs=[pl.BlockSpec((None, n_pages, page), lambda i: (i, 0, 0)),
                   pl.BlockSpec((None, 1, 1), lambda i: (i, 0, 0))],
        out_shape=[jax.ShapeDtypeStruct((b, n_pages, page), F32), jax.ShapeDtypeStruct((b, 1, 1), F32)],
        compiler_params=_cparams("parallel"),
        name="dsa_sample_select",
    )(scores, qi3, wi3, ki_new, jnp.asarray(triu, BF16), jnp.asarray(tril, BF16))


def _sample_attend_kernel(pt_ref, q_ref, selb_ref, selbn_ref, kn_ref, vn_ref, blast_ref, bfar_ref, bnew_ref,
                          *refs, pg, n_pages):
    k_refs, v_refs = refs[:pg], refs[pg:2 * pg]
    o_ref, m_ref, l_ref, acc_ref = refs[2 * pg:]
    step = pl.program_id(1)
    g_heads = ATT_HEADS // ATT_KV_HEADS
    page = k_refs[0].shape[2]
    dh = q_ref.shape[1]
    head_group = lax.broadcasted_iota(I32, (ATT_HEADS, 1), 0) // g_heads

    @pl.when(step == 0)
    def _():
        m_ref[...] = jnp.full(m_ref.shape, NEG_INF, F32)
        l_ref[...] = jnp.zeros(l_ref.shape, F32)
        acc_ref[...] = jnp.zeros(acc_ref.shape, F32)

    q = q_ref[...]
    s = jnp.zeros((ATT_HEADS, pg * page), F32)
    for g in range(ATT_KV_HEADS):
        kg = jnp.concatenate([r[g] for r in k_refs], axis=1)
        s = jnp.where(head_group == g, _bdot(q, kg), s)
    add = [jnp.where(step * pg + p == n_pages - 1, blast_ref[...], bfar_ref[...]) + selb_ref[p:p + 1, :]
           for p in range(pg)]
    s = s + jnp.concatenate(add, axis=1)
    m_old = m_ref[...]
    m_new = jnp.maximum(m_old, jnp.max(s, axis=1, keepdims=True))
    m_safe = jnp.where(m_new == NEG_INF, 0.0, m_new)
    alpha = jnp.exp(m_old - m_safe)
    p_ = jnp.exp(s - m_safe)
    pv = jnp.zeros((ATT_HEADS, dh), F32)
    for g in range(ATT_KV_HEADS):
        vg = jnp.concatenate([r[g] for r in v_refs], axis=1)
        pv = jnp.where(head_group == g, _bdot_nt(p_, vg), pv)
    l_new = alpha * l_ref[...] + jnp.sum(p_, axis=1, keepdims=True)
    acc_new = alpha * acc_ref[...] + pv
    m_ref[...] = m_new
    l_ref[...] = l_new
    acc_ref[...] = acc_new

    @pl.when(step == pl.num_programs(1) - 1)
    def _():
        kn = jnp.zeros((ATT_HEADS, dh), F32)
        vn = jnp.zeros((ATT_HEADS, dh), F32)
        for g in range(ATT_KV_HEADS):
            kn = jnp.where(head_group == g, kn_ref[g:g + 1, :], kn)
            vn = jnp.where(head_group == g, vn_ref[g:g + 1, :], vn)
        qk = q.astype(F32) * kn.astype(BF16).astype(F32)
        s_n = jnp.sum(qk, axis=1, keepdims=True) + bnew_ref[...] + selbn_ref[...]
        m_fin = jnp.maximum(m_new, s_n)
        m_fs = jnp.where(m_fin == NEG_INF, 0.0, m_fin)
        a2 = jnp.exp(m_new - m_fs)
        p_n = jnp.exp(s_n - m_fs)
        o_ref[...] = ((a2 * acc_new + p_n * vn) / (a2 * l_new + p_n)).astype(o_ref.dtype)


def dsa_sample_attend(q3, selb, selb_new, k_new, v_new, bias_tab, cache_k_t, cache_v_t, layer, page_table, pg):
    b, n_pages = page_table.shape
    kvh, dh, page = cache_k_t.shape[2:]
    assert page == bias_tab.shape[2] and page >= MAX_DISTANCE
    b_last = bias_tab[:, 1, :, 0]
    b_far = bias_tab[:, 2, 0, :1]
    b_new = bias_tab[:, 0, 0, :1]
    full = lambda shape: pl.BlockSpec(shape, lambda b, g, pt: (0,) * len(shape))
    per_b = lambda shape: pl.BlockSpec((None,) + shape, lambda b, g, pt: (b,) + (0,) * len(shape))
    return pl.pallas_call(
        functools.partial(_sample_attend_kernel, pg=pg, n_pages=n_pages),
        grid_spec=pltpu.PrefetchScalarGridSpec(
            num_scalar_prefetch=1, grid=(b, n_pages // pg),
            in_specs=[per_b((ATT_HEADS, dh)),
                      pl.BlockSpec((None, pg, page), lambda b, g, pt: (b, g, 0)),
                      per_b((1, 1)), per_b((kvh, dh)), per_b((kvh, dh)),
                      full((ATT_HEADS, page)), full((ATT_HEADS, 1)), full((ATT_HEADS, 1))]
                     + _page_specs(layer, pg, (None, None, kvh, dh, page))
                     + _page_specs(layer, pg, (None, None, kvh, dh, page)),
            out_specs=per_b((ATT_HEADS, dh)),
            scratch_shapes=[pltpu.VMEM((ATT_HEADS, 1), F32), pltpu.VMEM((ATT_HEADS, 1), F32),
                            pltpu.VMEM((ATT_HEADS, dh), F32)]),
        out_shape=jax.ShapeDtypeStruct((b, ATT_HEADS, dh), BF16),
        compiler_params=_cparams("parallel", "arbitrary"),
        name="dsa_sample_attend",
    )(page_table, q3, selb, selb_new, k_new, v_new, b_last, b_far, b_new,
      *([cache_k_t] * pg), *([cache_v_t] * pg))


def dsa_sample(q, k, v, qi, ki, wit, cache_k, cache_v, cache_ki, layer, page_table, bias_tab):
    b, d = q.shape
    dh = d // ATT_HEADS
    n_pages = page_table.shape[1]
    page = cache_k.shape[2]
    pg = 16 if n_pages % 16 == 0 else n_pages
    topk = min(TOPK_MAX, (n_pages * page + 1) // 4)
    qi3 = qi.reshape(b, IDX_HEADS, IDX_DH)
    wi3 = wit.T.reshape(b, IDX_HEADS, 1)
    cache_ki_t = jnp.transpose(cache_ki, (0, 1, 3, 2))
    cache_k_t = jnp.transpose(cache_k, (0, 1, 3, 4, 2))
    cache_v_t = jnp.transpose(cache_v, (0, 1, 3, 4, 2))
    scores = dsa_sample_scores(qi3, wi3, cache_ki_t, layer, page_table, pg)
    selb, selb_new = dsa_sample_select(scores, qi3, wi3, ki.reshape(b, 1, IDX_DH), topk)
    o = dsa_sample_attend(q.reshape(b, ATT_HEADS, dh), selb, selb_new, k.reshape(b, ATT_KV_HEADS, dh),
                          v.reshape(b, ATT_KV_HEADS, dh), bias_tab, cache_k_t, cache_v_t, layer, page_table, pg)
    return o.reshape(b, d)


def kernel(x_prompt, x_sample, c_prompt, c_sample, state_gla, cache_k, cache_v, cache_idx_k, page_table,
           rel_bias, w_ada, b_ada, norm_mix, norm_ffn, norm_final,
           gla_w_in, gla_w_g2, gla_b_g2, gla_gnorm, gla_w_out,
           dsa_w_in, dsa_idx_ln_g, dsa_idx_ln_b, dsa_w_out,
           moe_w_router, moe_router_bias, moe_w_gate, moe_w_up, moe_w_down,
           shared_w_gate, shared_w_up, shared_w_down):
    bp, t, d = x_prompt.shape
    bs = x_sample.shape[0]
    depth = w_ada.shape[0]
    dh = d // ATT_HEADS
    tm_p = min(256, t)
    tq = 128
    mods = ada_mod_all(jnp.concatenate([c_prompt, c_sample], axis=0), w_ada, b_ada)
    xp = x_prompt.reshape(bp * t, d)
    xs = x_sample.reshape(bs, d)
    bias_tab = bias_tables(rel_bias, tq)
    gla_p, gla_s, kp_l, vp_l, kip_l, ks_l, vs_l, kis_l = [], [], [], [], [], [], [], []
    for i in range(depth):
        mod_p = mods[i, :bp].reshape(bp, 1, 6 * d)
        mod_s = mods[i, bp:]
        j = i // 2
        if i % 2 == 0:
            qkvr_p, la_p = gla_project(xp, mod_p, norm_mix[i], gla_w_in[j], gla_w_g2[j], gla_b_g2[j], tm_p, t)
            o_p, s_p = gla_chunked(qkvr_p, la_p, gla_gnorm[j], bp, t)
            qkvr_s, la_s = gla_project(xs, mod_s, norm_mix[i], gla_w_in[j], gla_w_g2[j], gla_b_g2[j], bs, 1)
            o_s, s_s = gla_step(qkvr_s, la_s, state_gla[j], gla_gnorm[j])
            gla_p.append(s_p)
            gla_s.append(s_s)
            w_out = gla_w_out[j]
        else:
            dp = (dsa_w_in[j], dsa_idx_ln_g[j], dsa_idx_ln_b[j])
            q_p, qi_p, wit_p, kt_p, vt32_p, kit_p, kib_p, kg_p, vt_p = dsa_project(
                xp, mod_p, norm_mix[i], *dp, tm_p, t, True)
            o_p = dsa_prompt(q_p, qi_p, wit_p, kib_p, kg_p, vt_p, bias_tab, bp, t, tq)
            q_s, qi_s, wit_s, k_s, v_s, ki_s = dsa_project(xs, mod_s, norm_mix[i], *dp, bs, 1, False)
            o_s = dsa_sample(q_s, k_s, v_s, qi_s, ki_s, wit_s, cache_k, cache_v, cache_idx_k, j,
                             page_table, bias_tab)
            kp_l.append(kt_p.reshape(bp, ATT_KV_HEADS, dh, t).transpose(0, 3, 1, 2))
            vp_l.append(vt32_p.reshape(bp, ATT_KV_HEADS, dh, t).transpose(0, 3, 1, 2))
            kip_l.append(kit_p.transpose(0, 2, 1))
            ks_l.append(k_s.reshape(bs, 1, ATT_KV_HEADS, dh))
            vs_l.append(v_s.reshape(bs, 1, ATT_KV_HEADS, dh))
            kis_l.append(ki_s.reshape(bs, 1, IDX_DH))
            w_out = dsa_w_out[j]
        xp = out_project(o_p, w_out, xp, mod_p, tm_p, t)
        xs = out_project(o_s, w_out, xs, mod_s, bs, 1)
        last = i == depth - 1
        mo = (moe_w_router[i], moe_router_bias[i], moe_w_gate, moe_w_up, moe_w_down, i,
              shared_w_gate[i], shared_w_up[i], shared_w_down[i], norm_final, last)
        xp = moe_ffn(xp, mod_p, norm_ffn[i], *mo, tm_p, t, 256)
        xs = moe_ffn(xs, mod_s, norm_ffn[i], *mo, bs, 1, 32)
    return (xp.reshape(bp, t, d), xs.reshape(bs, 1, d), jnp.stack(gla_p), jnp.stack(gla_s),
            jnp.stack(kp_l), jnp.stack(vp_l), jnp.stack(kip_l),
            jnp.stack(ks_l), jnp.stack(vs_l), jnp.stack(kis_l))
```

```python
import functools
import math

import numpy as np
import jax
import jax.numpy as jnp
from jax import lax
from jax.experimental import pallas as pl
from jax.experimental.pallas import tpu as pltpu

F32 = jnp.float32
BF16 = jnp.bfloat16
I32 = jnp.int32

GLA_HEADS = 4
GLA_RANK = 16
GLA_NORMALIZER = 16.0
ATT_HEADS = 16
ATT_KV_HEADS = 4
IDX_HEADS = 8
IDX_DH = 64
TOPK_MAX = 256
N_BUCKETS = 32
MAX_DISTANCE = 128
N_GROUPS = 8
TOPK_GROUPS = 4
TOP_K = 8
ROUTE_SCALE = 2.5
EPS = 1e-6

LANES = 128
VMEM_LIMIT = 56 * 1024 * 1024
NEG_INF = float("-inf")
INT_MIN = -2 ** 31


def _cparams(*sem):
    return pltpu.CompilerParams(dimension_semantics=sem, vmem_limit_bytes=VMEM_LIMIT)


def _bdot(a, b):
    return jnp.dot(a.astype(BF16), b.astype(BF16), preferred_element_type=F32)


def _bdot_nt(a, b):
    return lax.dot_general(a.astype(BF16), b.astype(BF16), (((1,), (1,)), ((), ())),
                           preferred_element_type=F32)


def _bdot_tn(a, b):
    return lax.dot_general(a.astype(BF16), b.astype(BF16), (((0,), (0,)), ((), ())),
                           preferred_element_type=F32)


def _split3(a):
    hi = a.astype(BF16)
    r1 = a - hi.astype(F32)
    mid = r1.astype(BF16)
    lo = (r1 - mid.astype(F32)).astype(BF16)
    return hi, mid, lo


def _silu(x):
    return x * (1.0 / (1.0 + jnp.exp(-x)))


def _sigmoid(x):
    return 1.0 / (1.0 + jnp.exp(-x))


def _rms(x, g):
    return x * lax.rsqrt(jnp.mean(x * x, axis=-1, keepdims=True) + EPS) * g


def _ada_kernel(c_ref, w_ref, b_ref, o_ref):
    c_hi, c_mid, _ = _split3(_silu(c_ref[...]))
    w_hi, w_mid, _ = _split3(w_ref[...])
    dot = lambda a, b: jnp.dot(a, b, preferred_element_type=F32)
    o_ref[...] = (dot(c_hi, w_mid) + dot(c_mid, w_hi)) + dot(c_hi, w_hi) + b_ref[...]


def ada_mod_all(c, w_ada, b_ada, tn=512):
    depth, d, n6 = w_ada.shape
    rows = c.shape[0]
    return pl.pallas_call(
        _ada_kernel,
        grid=(depth, n6 // tn),
        in_specs=[pl.BlockSpec((rows, d), lambda l, j: (0, 0)),
                  pl.BlockSpec((None, d, tn), lambda l, j: (l, 0, j)),
                  pl.BlockSpec((None, 1, tn), lambda l, j: (l, 0, j))],
        out_specs=pl.BlockSpec((None, rows, tn), lambda l, j: (l, 0, j)),
        out_shape=jax.ShapeDtypeStruct((depth, rows, n6), F32),
        compiler_params=_cparams("parallel", "parallel"),
        name="ada_mod",
    )(c, w_ada, b_ada.reshape(depth, 1, n6))


def _mod_spec(mod, which, tm, seq_len, d):
    if mod.ndim == 3:
        per_seq = seq_len // tm
        return pl.BlockSpec((None, 1, d), lambda i: (i // per_seq, 0, which))
    return pl.BlockSpec((tm, d), lambda i: (i, which))


def _gla_proj_kernel(x_ref, g_ref, sh_ref, sc_ref, w_ref, wg2_ref, bg2_ref, qkvr_ref, la_ref, *, n_main):
    h = _rms(x_ref[...], g_ref[...]) * (1.0 + sc_ref[...]) + sh_ref[...]
    y = _bdot(h, w_ref[...])
    qkvr_ref[...] = y[:, :n_main]
    g1 = y[:, n_main:n_main + GLA_RANK]
    z = _bdot(g1, wg2_ref[...]) + bg2_ref[...]
    la_ref[...] = (jnp.minimum(z, 0.0) - jnp.log(1.0 + jnp.exp(-jnp.abs(z)))) * (1.0 / GLA_NORMALIZER)


def gla_project(x, mod, norm_g, w_in, w_g2, b_g2, tm, seq_len):
    n, d = x.shape
    hk = w_g2.shape[1]
    n_main = w_in.shape[1] - GLA_RANK
    n_pad = (-w_in.shape[1]) % LANES
    w = jnp.pad(w_in, ((0, 0), (0, n_pad))).astype(BF16)
    return pl.pallas_call(
        functools.partial(_gla_proj_kernel, n_main=n_main),
        grid=(n // tm,),
        in_specs=[pl.BlockSpec((tm, d), lambda i: (i, 0)),
                  pl.BlockSpec((1, d), lambda i: (0, 0)),
                  _mod_spec(mod, 0, tm, seq_len, d),
                  _mod_spec(mod, 1, tm, seq_len, d),
                  pl.BlockSpec(w.shape, lambda i: (0, 0)),
                  pl.BlockSpec(w_g2.shape, lambda i: (0, 0)),
                  pl.BlockSpec((1, hk), lambda i: (0, 0))],
        out_specs=[pl.BlockSpec((tm, n_main), lambda i: (i, 0)),
                   pl.BlockSpec((tm, hk), lambda i: (i, 0))],
        out_shape=[jax.ShapeDtypeStruct((n, n_main), F32),
                   jax.ShapeDtypeStruct((n, hk), F32)],
        compiler_params=_cparams("parallel"),
        name="gla_proj",
    )(x, norm_g.reshape(1, d), mod, mod, w, w_g2.astype(BF16), b_g2.reshape(1, hk))


def _gla_level_matrices(c):
    levels = int(math.log2(c))
    t = np.arange(c)[:, None]
    u = np.arange(c)[None, :]
    mats, masks = [], []
    for l in range(levels):
        m = 1 << l
        ref = (t // (2 * m)) * (2 * m) + m - 1
        right = (t % (2 * m)) >= m
        mat = np.where(right, (u > ref) & (u <= t), (u > t) & (u <= ref))
        mats.append(mat)
        masks.append((t // (2 * m) == u // (2 * m)) & right & ((u % (2 * m)) < m))
    mats.append(u <= t)
    masks.append(t == u)
    return (np.stack(mats).astype(np.float32).reshape(-1, c), np.stack(masks).astype(np.float32))


def _gla_chunk_kernel(q_ref, k_ref, v_ref, r_ref, la_ref, mat_ref, mask_ref, gn_ref,
                      o_ref, s_out_ref, st_ref, *, c, dk):
    ci = pl.program_id(1)
    nlev = mask_ref.shape[0] - 1
    hk = la_ref.shape[1]
    dv = v_ref.shape[1] // GLA_HEADS

    @pl.when(ci == 0)
    def _():
        st_ref[...] = jnp.zeros_like(st_ref)

    la = la_ref[...]
    hi = la.astype(BF16)
    lo = (la - hi.astype(F32)).astype(BF16)
    e_all = jnp.dot(mat_ref[...], jnp.concatenate([hi, lo], axis=1), preferred_element_type=F32)
    e_all = e_all[:, :hk] + e_all[:, hk:]

    for h in range(GLA_HEADS):
        ks, vs = slice(h * dk, (h + 1) * dk), slice(h * dv, (h + 1) * dv)
        q = q_ref[:, ks] * (dk ** -0.5)
        k = k_ref[:, ks]
        v = v_ref[:, vs]
        att = jnp.where(mask_ref[nlev] > 0.0, _bdot_nt(q, k), 0.0)
        for l in range(nlev):
            ex = jnp.exp(e_all[l * c:(l + 1) * c, ks])
            att = att + jnp.where(mask_ref[l] > 0.0, _bdot_nt(q * ex, k * ex), 0.0)

        b = e_all[nlev * c:(nlev + 1) * c, ks]
        b_end = b[c - 1:c, :]
        st = st_ref[h]
        o = _bdot(att, v) + _bdot_nt(q * jnp.exp(b), st)
        st_new = st * jnp.exp(b_end) + _bdot_tn(v, k * jnp.exp(b_end - b))
        st_ref[h] = st_new

        on = o * lax.rsqrt(jnp.mean(o * o, axis=-1, keepdims=True) + EPS) * gn_ref[...]
        o_ref[:, vs] = (on * _silu(r_ref[:, vs])).astype(o_ref.dtype)

        @pl.when(ci == pl.num_programs(1) - 1)
        def _():
            s_out_ref[h] = st_new.T


def gla_chunked(qkvr, log_a, gnorm, batch, seq_len, c=128):
    n = qkvr.shape[0]
    hk = log_a.shape[1]
    dk = hk // GLA_HEADS
    hv = (qkvr.shape[1] - 2 * hk) // 2
    dv = hv // GLA_HEADS
    nc = seq_len // c
    mats, masks = _gla_level_matrices(c)
    row = lambda b, ci: b * nc + ci
    return pl.pallas_call(
        functools.partial(_gla_chunk_kernel, c=c, dk=dk),
        grid=(batch, nc),
        in_specs=[pl.BlockSpec((c, hk), lambda b, ci: (row(b, ci), 0)),
                  pl.BlockSpec((c, hk), lambda b, ci: (row(b, ci), 1)),
                  pl.BlockSpec((c, hv), lambda b, ci: (row(b, ci), 2 * hk // hv)),
                  pl.BlockSpec((c, hv), lambda b, ci: (row(b, ci), 2 * hk // hv + 1)),
                  pl.BlockSpec((c, hk), lambda b, ci: (row(b, ci), 0)),
                  pl.BlockSpec(mats.shape, lambda b, ci: (0, 0)),
                  pl.BlockSpec(masks.shape, lambda b, ci: (0, 0, 0)),
                  pl.BlockSpec((1, dv), lambda b, ci: (0, 0))],
        out_specs=[pl.BlockSpec((c, hv), lambda b, ci: (row(b, ci), 0)),
                   pl.BlockSpec((None, GLA_HEADS, dk, dv), lambda b, ci: (b, 0, 0, 0))],
        out_shape=[jax.ShapeDtypeStruct((n, hv), BF16),
                   jax.ShapeDtypeStruct((batch, GLA_HEADS, dk, dv), F32)],
        scratch_shapes=[pltpu.VMEM((GLA_HEADS, dv, dk), F32)],
        compiler_params=_cparams("parallel", "arbitrary"),
        name="gla_chunked",
    )(qkvr, qkvr, qkvr, qkvr, log_a, jnp.asarray(mats, BF16), jnp.asarray(masks), gnorm.reshape(1, dv))


def _gla_step_kernel(q_ref, k_ref, g_ref, v_ref, r_ref, s_ref, gn_ref, o_ref, s_out_ref, *, dk):
    s_new = jnp.exp(g_ref[...]) * s_ref[...] + k_ref[...] * v_ref[...]
    s_out_ref[...] = s_new
    o = jnp.sum(q_ref[...] * (dk ** -0.5) * s_new, axis=0, keepdims=True)
    on = o * lax.rsqrt(jnp.mean(o * o, axis=-1, keepdims=True) + EPS) * gn_ref[...]
    o_ref[...] = (on * _silu(r_ref[...])).astype(o_ref.dtype)


def gla_step(qkvr, log_a, state, gnorm):
    batch = qkvr.shape[0]
    hk = log_a.shape[1]
    dk = hk // GLA_HEADS
    hv = (qkvr.shape[1] - 2 * hk) // 2
    dv = hv // GLA_HEADS
    col = lambda a: a.reshape(batch, GLA_HEADS, dk, 1)
    rowv = lambda a: a.reshape(batch, GLA_HEADS, 1, dv)
    q, k, v, r = (qkvr[:, :hk], qkvr[:, hk:2 * hk], qkvr[:, 2 * hk:2 * hk + hv], qkvr[:, 2 * hk + hv:])
    cspec = pl.BlockSpec((None, None, dk, 1), lambda b, h: (b, h, 0, 0))
    rspec = pl.BlockSpec((None, None, 1, dv), lambda b, h: (b, h, 0, 0))
    sspec = pl.BlockSpec((None, None, dk, dv), lambda b, h: (b, h, 0, 0))
    o, s_new = pl.pallas_call(
        functools.partial(_gla_step_kernel, dk=dk),
        grid=(batch, GLA_HEADS),
        in_specs=[cspec, cspec, cspec, rspec, rspec, sspec, pl.BlockSpec((1, dv), lambda b, h: (0, 0))],
        out_specs=[rspec, sspec],
        out_shape=[jax.ShapeDtypeStruct((batch, GLA_HEADS, 1, dv), BF16),
                   jax.ShapeDtypeStruct(state.shape, F32)],
        compiler_params=_cparams("parallel", "parallel"),
        name="gla_step",
    )(col(q), col(k), col(log_a), rowv(v), rowv(r), state, gnorm.reshape(1, dv))
    return o.reshape(batch, hv), s_new


def _out_proj_kernel(o_ref, w_ref, x_ref, gt_ref, xo_ref):
    xo_ref[...] = x_ref[...] + gt_ref[...] * _bdot(o_ref[...], w_ref[...])


def out_project(o, w_out, x, mod, tm, seq_len):
    n, d = x.shape
    kdim = o.shape[1]
    return pl.pallas_call(
        _out_proj_kernel,
        grid=(n // tm,),
        in_specs=[pl.BlockSpec((tm, kdim), lambda i: (i, 0)),
                  pl.BlockSpec((kdim, d), lambda i: (0, 0)),
                  pl.BlockSpec((tm, d), lambda i: (i, 0)),
                  _mod_spec(mod, 2, tm, seq_len, d)],
        out_specs=pl.BlockSpec((tm, d), lambda i: (i, 0)),
        out_shape=jax.ShapeDtypeStruct((n, d), F32),
        compiler_params=_cparams("parallel"),
        name="out_proj",
    )(o, w_out.astype(BF16), x, mod)


HI_MASK = -65536


def _pack_rows(val):
    half = val.shape[1] // 2
    bits = lambda a: pltpu.bitcast(a.astype(BF16).astype(F32), I32)
    return lax.shift_right_logical(bits(val[:, :half]), 16) | (bits(val[:, half:]) & HI_MASK)


def _unpack_rows(words):
    return jnp.concatenate([pltpu.bitcast(words << 16, F32), pltpu.bitcast(words & HI_MASK, F32)], axis=1)


def _rows_load(ref, per, first=0, rows=None):
    rows = ref.shape[0] // per - first if rows is None else rows
    return jnp.concatenate([ref[pl.ds(first * per + j, rows, stride=per), :] for j in range(per)], axis=1)


def _rows_store(ref, words):
    rows = words.shape[0]
    per = words.shape[1] // LANES
    for j in range(per):
        ref[pl.ds(j, rows, stride=per), :] = words[:, j * LANES:(j + 1) * LANES]


def _row_tile(ref, r, per):
    return ref.at[pl.ds(pl.multiple_of(r * per, per), per), :]


def _modulate_kernel(x_ref, g_ref, sh_ref, sc_ref, h_ref, hp_ref):
    h = _rms(x_ref[...], g_ref[...]) * (1.0 + sc_ref[...]) + sh_ref[...]
    h_ref[...] = h
    _rows_store(hp_ref, _pack_rows(h))


def modulate_rows(x, mod, norm_g, tm, seq_len):
    n, d = x.shape
    per = d // 2 // LANES
    return pl.pallas_call(
        _modulate_kernel,
        grid=(n // tm,),
        in_specs=[pl.BlockSpec((tm, d), lambda i: (i, 0)),
                  pl.BlockSpec((1, d), lambda i: (0, 0)),
                  _mod_spec(mod, 3, tm, seq_len, d),
                  _mod_spec(mod, 4, tm, seq_len, d)],
        out_specs=[pl.BlockSpec((tm, d), lambda i: (i, 0)),
                   pl.BlockSpec((tm * per, LANES), lambda i: (i, 0))],
        out_shape=[jax.ShapeDtypeStruct((n, d), F32), jax.ShapeDtypeStruct((n * per, LANES), I32)],
        compiler_params=_cparams("parallel"),
        name="modulate_ffn",
    )(x, norm_g.reshape(1, d), mod, mod)


def _dsa_proj_kernel(x_ref, g_ref, sh_ref, sc_ref, w_ref, wwi_ref, lg_ref, lb_ref,
                     q_ref, qi_ref, wit_ref, k_ref, v_ref, ki_ref, *seq_refs, cuts, dh):
    h = (_rms(x_ref[...], g_ref[...]) * (1.0 + sc_ref[...]) + sh_ref[...]).astype(BF16)
    y = jnp.dot(h, w_ref[...], preferred_element_type=F32)
    c0, c1, c2, c3, c4 = cuts
    q_ref[...] = (y[:, :c0] * (dh ** -0.5)).astype(BF16)
    qi_ref[...] = (y[:, c2:c3] * (IDX_DH ** -0.5)).astype(BF16)
    wit_ref[...] = _bdot_nt(wwi_ref[...], h) * (IDX_HEADS ** -0.5)
    k = y[:, c0:c1]
    v = y[:, c1:c2]
    ki = y[:, c3:c4]
    mu = jnp.mean(ki, axis=-1, keepdims=True)
    var = jnp.mean((ki - mu) * (ki - mu), axis=-1, keepdims=True)
    kin = (ki - mu) * lax.rsqrt(var + EPS) * lg_ref[...] + lb_ref[...]
    if not seq_refs:
        k_ref[...] = k
        v_ref[...] = v
        ki_ref[...] = kin
    else:
        kib_ref, kg_ref, vt_ref = seq_refs
        vt = v.T
        k_ref[...] = k.T
        v_ref[...] = vt
        ki_ref[...] = jnp.concatenate([kin, jnp.zeros_like(kin)], axis=1).T[:kin.shape[1]]
        kib_ref[...] = kin.astype(BF16)
        for g in range(ATT_KV_HEADS):
            kg_ref[g] = k[:, g * dh:(g + 1) * dh].astype(BF16)
        vt_ref[...] = vt.astype(BF16)


def dsa_project(x, mod, norm_g, w_in, ln_g, ln_b, tm, seq_len, seq_layouts):
    n, d = x.shape
    dh = d // ATT_HEADS
    nkv = ATT_KV_HEADS * dh
    nqi = IDX_HEADS * IDX_DH
    cuts = (d, d + nkv, d + 2 * nkv, d + 2 * nkv + nqi, d + 2 * nkv + nqi + IDX_DH)
    w = jnp.pad(w_in[:, :cuts[4]], ((0, 0), (0, (-cuts[4]) % LANES))).astype(BF16)
    w_wi_t = w_in[:, cuts[4]:cuts[4] + IDX_HEADS].T.astype(BF16)
    row = lambda wd: pl.BlockSpec((tm, wd), lambda i: (i, 0))
    out_specs = [row(d), row(nqi), pl.BlockSpec((IDX_HEADS, tm), lambda i: (0, i))]
    out_shape = [jax.ShapeDtypeStruct((n, d), BF16), jax.ShapeDtypeStruct((n, nqi), BF16),
                 jax.ShapeDtypeStruct((IDX_HEADS, n), F32)]
    if not seq_layouts:
        out_specs += [row(nkv), row(nkv), row(IDX_DH)]
        out_shape += [jax.ShapeDtypeStruct((n, wd), F32) for wd in (nkv, nkv, IDX_DH)]
    else:
        per_seq = seq_len // tm
        tmin = lambda wd: pl.BlockSpec((None, wd, tm), lambda i: (i // per_seq, 0, i % per_seq))
        out_specs += [tmin(nkv), tmin(nkv), tmin(IDX_DH),
                      row(IDX_DH),
                      pl.BlockSpec((ATT_KV_HEADS, tm, dh), lambda i: (0, i, 0)),
                      pl.BlockSpec((None, nkv, tm), lambda i: (i, 0, 0))]
        out_shape += [jax.ShapeDtypeStruct((n // seq_len, wd, seq_len), F32) for wd in (nkv, nkv, IDX_DH)]
        out_shape += [jax.ShapeDtypeStruct((n, IDX_DH), BF16),
                      jax.ShapeDtypeStruct((ATT_KV_HEADS, n, dh), BF16),
                      jax.ShapeDtypeStruct((n // tm, nkv, tm), BF16)]
    return pl.pallas_call(
        functools.partial(_dsa_proj_kernel, cuts=cuts, dh=dh),
        grid=(n // tm,),
        in_specs=[pl.BlockSpec((tm, d), lambda i: (i, 0)),
                  pl.BlockSpec((1, d), lambda i: (0, 0)),
                  _mod_spec(mod, 0, tm, seq_len, d),
                  _mod_spec(mod, 1, tm, seq_len, d),
                  pl.BlockSpec(w.shape, lambda i: (0, 0)),
                  pl.BlockSpec(w_wi_t.shape, lambda i: (0, 0)),
                  pl.BlockSpec((1, IDX_DH), lambda i: (0, 0)),
                  pl.BlockSpec((1, IDX_DH), lambda i: (0, 0))],
        out_specs=out_specs,
        out_shape=out_shape,
        compiler_params=_cparams("parallel"),
        name="dsa_proj",
    )(x, norm_g.reshape(1, d), mod, mod, w, w_wi_t, ln_g.reshape(1, IDX_DH), ln_b.reshape(1, IDX_DH))


def _t5_bucket_np(dist):
    dist = np.maximum(dist, 0)
    max_exact = N_BUCKETS // 2
    ratio = np.log(np.maximum(dist, max_exact).astype(np.float32) / max_exact) / math.log(MAX_DISTANCE / max_exact)
    large = np.minimum(max_exact + (ratio * (N_BUCKETS - max_exact)).astype(np.int32), N_BUCKETS - 1)
    return np.where(dist < max_exact, dist, large).astype(np.int32)


def _bias_table_kernel(rb_ref, bk_ref, o_ref):
    h = pl.program_id(0)
    for kind in range(3):
        bk = bk_ref[kind]
        acc = jnp.zeros(bk.shape, F32)
        for b in range(N_BUCKETS):
            acc = jnp.where(bk == b, rb_ref[b, h], acc)
        o_ref[kind] = acc


def bias_tables(rel_bias, tq):
    s = np.arange(tq)[:, None]
    t = np.arange(tq)[None, :]
    buckets = np.stack([_t5_bucket_np(t - s), _t5_bucket_np(tq + t - s), _t5_bucket_np(2 * tq + t - s)])
    assert tq >= MAX_DISTANCE and (buckets[2] == N_BUCKETS - 1).all()
    return pl.pallas_call(
        _bias_table_kernel,
        grid=(ATT_HEADS,),
        in_specs=[pl.BlockSpec(memory_space=pltpu.SMEM),
                  pl.BlockSpec(buckets.shape, lambda h: (0, 0, 0))],
        out_specs=pl.BlockSpec((None, 3, tq, tq), lambda h: (h, 0, 0, 0)),
        out_shape=jax.ShapeDtypeStruct((ATT_HEADS, 3, tq, tq), F32),
        compiler_params=_cparams("parallel"),
        name="t5_bias_tiles",
    )(rel_bias, jnp.asarray(buckets))


def _order_key(score):
    score = jnp.where(score == 0.0, 0.0, score)
    bits = pltpu.bitcast(score, I32)
    return bits ^ ((bits >> 31) & 0x7FFFFFFF)


def _dsa_prompt_kernel(q_ref, qi_ref, wit_ref, kib_ref, kg_ref, vt_ref, bias_ref, tril_ref, o_ref,
                       key_ref, selb_ref, qis_ref, qg_ref, sc_ref, ot_ref, *, tq, dh, topk):
    i = pl.program_id(1)
    nk = i + 1
    g_heads = ATT_HEADS // ATT_KV_HEADS
    s_id = lax.broadcasted_iota(I32, (tq, tq), 0)
    t_id = lax.broadcasted_iota(I32, (tq, tq), 1)
    fold8 = lambda a, op: op(a.reshape(a.shape[0] // 8, 8, a.shape[1]), axis=0)

    for h in range(IDX_HEADS):
        qis_ref[h * tq:(h + 1) * tq, :] = qi_ref[:, h * IDX_DH:(h + 1) * IDX_DH]
    for hh in range(ATT_HEADS):
        g, a = divmod(hh, g_heads)
        qg_ref[g, a * tq:(a + 1) * tq, :] = q_ref[:, hh * dh:(hh + 1) * dh]
    w8 = wit_ref[...]

    def score_body(j, carry):
        kij = kib_ref[pl.ds(pl.multiple_of(j * tq, tq), tq), :]
        lg = _bdot_nt(kij, qis_ref[...])
        acc = jnp.zeros((tq, tq), F32)
        for h in range(IDX_HEADS):
            acc = acc + jnp.maximum(lg[:, h * tq:(h + 1) * tq], 0.0) * w8[h:h + 1, :]
        valid = (s_id + j * tq) <= (t_id + i * tq)
        key_ref[j] = _order_key(jnp.where(valid, acc, NEG_INF))
        return carry

    lax.fori_loop(0, nk, score_body, 0)
    n_pairs = (nk + 1) // 2

    @pl.when(nk % 2 == 1)
    def _():
        pad = jnp.minimum(nk, key_ref.shape[0] - 1)
        key_ref[pad] = jnp.full((tq, tq), INT_MIN, I32)
        selb_ref[pad] = jnp.full((tq, tq), NEG_INF, F32)

    def count(pred):
        def body(jj, acc8):
            ones = jnp.where(pred(key_ref[2 * jj]), 1.0, 0.0) + jnp.where(pred(key_ref[2 * jj + 1]), 1.0, 0.0)
            return acc8 + fold8(ones, jnp.sum)
        return jnp.sum(lax.fori_loop(0, n_pairs, body, jnp.zeros((8, tq), F32)), axis=0, keepdims=True)

    def bit_body(it, ans):
        cand = ans | (jnp.int32(1) << (31 - it))
        cand_s = cand ^ INT_MIN
        return jnp.where(count(lambda kj: kj >= cand_s) >= topk, cand, ans)

    thr = lax.fori_loop(0, 32, bit_body, jnp.zeros((1, tq), I32)) ^ INT_MIN
    need = topk - count(lambda kj: kj > thr)
    n_tie = count(lambda kj: kj == thr)

    def causal(j):
        return (s_id + j * tq) <= (t_id + i * tq)

    some_ties_dropped = jnp.max(n_tie - need) > 0.0

    @pl.when(jnp.logical_not(some_ties_dropped))
    def _():
        def sel_body(j, carry):
            selb_ref[j] = jnp.where(jnp.logical_and(key_ref[j] >= thr, causal(j)), 0.0, NEG_INF)
            return carry
        lax.fori_loop(0, nk, sel_body, 0)

    @pl.when(some_ties_dropped)
    def _():
        def sel_body(j, run):
            kj = key_ref[j]
            tie = kj == thr
            tie_f = jnp.where(tie, 1.0, 0.0)
            before = run + jnp.dot(tril_ref[...], tie_f.astype(BF16), preferred_element_type=F32)
            take = jnp.where(kj > thr, 0.0, jnp.where(tie, jnp.where(before < need, 0.0, NEG_INF), NEG_INF))
            selb_ref[j] = jnp.where(causal(j), take, NEG_INF)
            return run + jnp.sum(tie_f, axis=0, keepdims=True)
        lax.fori_loop(0, nk, sel_body, jnp.zeros((1, tq), F32))

    ones_rows = jnp.ones((16, 2 * tq), BF16)
    for g in range(ATT_KV_HEADS):
        qg = qg_ref[g]

        def logits_body(jj, mx8):
            base = pl.multiple_of(jj * 2 * tq, 2 * tq)
            s = _bdot_nt(kg_ref[g, pl.ds(base, 2 * tq), :], qg)
            j0 = 2 * jj
            halves = []
            for c in range(2):
                kind = jnp.clip(i - j0 - c, 0, 2)
                sel = selb_ref[j0 + c]
                halves.append(jnp.concatenate([bias_ref[g * g_heads + a, kind] + sel
                                               for a in range(g_heads)], axis=1))
            s = s + jnp.concatenate(halves, axis=0)
            sc_ref[jj] = s
            return jnp.maximum(mx8, fold8(s, jnp.max))

        mx8 = lax.fori_loop(0, n_pairs, logits_body, jnp.full((8, g_heads * tq), NEG_INF, F32))
        m = jnp.max(mx8, axis=0, keepdims=True)
        m = jnp.where(m == NEG_INF, 0.0, m)

        def pv_body(jj, acc):
            p = jnp.exp(sc_ref[jj] - m).astype(BF16)
            lhs = jnp.concatenate([vt_ref[jj, g * dh:(g + 1) * dh, :], ones_rows], axis=0)
            return acc + jnp.dot(lhs, p, preferred_element_type=F32)

        acc = lax.fori_loop(0, n_pairs, pv_body, jnp.zeros((dh + 16, g_heads * tq), F32))
        og = acc[:dh] / acc[dh:dh + 1]
        for a in range(g_heads):
            hh = g * g_heads + a
            ot_ref[hh * dh:(hh + 1) * dh, :] = og[:, a * tq:(a + 1) * tq]
    o_ref[...] = ot_ref[...].T.astype(o_ref.dtype)


def dsa_prompt(q, qi, wit, kib, kg, vt, bias_tab, batch, seq_len, tq=128):
    n, d = q.shape
    dh = d // ATT_HEADS
    nq = seq_len // tq
    g_heads = ATT_HEADS // ATT_KV_HEADS
    assert nq % 2 == 0 and vt.shape[2] == 2 * tq
    topk = min(TOPK_MAX, seq_len // 4)
    tril = np.tril(np.ones((tq, tq), np.float32), -1)
    blk_spec = lambda width: pl.BlockSpec((tq, width), lambda b, i: (b * nq + i, 0))
    return pl.pallas_call(
        functools.partial(_dsa_prompt_kernel, tq=tq, dh=dh, topk=float(topk)),
        grid=(batch, nq),
        in_specs=[blk_spec(d), blk_spec(qi.shape[1]),
                  pl.BlockSpec((IDX_HEADS, tq), lambda b, i: (0, b * nq + i)),
                  pl.BlockSpec((seq_len, IDX_DH), lambda b, i: (b, 0)),
                  pl.BlockSpec((ATT_KV_HEADS, seq_len, dh), lambda b, i: (0, b, 0)),
                  pl.BlockSpec((nq // 2,) + vt.shape[1:], lambda b, i: (b, 0, 0)),
                  pl.BlockSpec(bias_tab.shape, lambda b, i: (0, 0, 0, 0)),
                  pl.BlockSpec((tq, tq), lambda b, i: (0, 0))],
        out_specs=blk_spec(d),
        out_shape=jax.ShapeDtypeStruct((n, d), BF16),
        scratch_shapes=[pltpu.VMEM((nq, tq, tq), I32),
                        pltpu.VMEM((nq, tq, tq), F32),
                        pltpu.VMEM((IDX_HEADS * tq, IDX_DH), BF16),
                        pltpu.VMEM((ATT_KV_HEADS, g_heads * tq, dh), BF16),
                        pltpu.VMEM((nq // 2, 2 * tq, g_heads * tq), F32),
                        pltpu.VMEM((d, tq), F32)],
        compiler_params=_cparams("parallel", "arbitrary"),
        name="dsa_prompt",
    )(q, qi, wit, kib, kg, vt, bias_tab, jnp.asarray(tril, BF16))


def _router_kernel(h_ref, wr_ref, rb_ref, tri_ref, eidx_ref, wsel_ref, pos_ref, cnt_ref, carry_ref, *, n_exp):
    @pl.when(pl.program_id(0) == 0)
    def _():
        carry_ref[...] = jnp.zeros_like(carry_ref)

    h_hi, h_mid, h_lo = _split3(h_ref[...])
    w_hi, w_mid, w_lo = _split3(wr_ref[...])
    nt = lambda a, b: lax.dot_general(a, b, (((1,), (1,)), ((), ())), preferred_element_type=F32)
    logits = ((nt(w_lo, h_hi) + nt(w_hi, h_lo) + nt(w_mid, h_mid))
              + (nt(w_hi, h_mid) + nt(w_mid, h_hi))) + nt(w_hi, h_hi)
    s = _sigmoid(logits)
    sel = s + rb_ref[...]
    tm = sel.shape[1]
    gsz = n_exp // N_GROUPS
    io_g = lax.broadcasted_iota(I32, (gsz, tm), 0)

    gs = []
    for g in range(N_GROUPS):
        grp = sel[g * gsz:(g + 1) * gsz, :]
        m1 = jnp.max(grp, axis=0, keepdims=True)
        i1 = jnp.min(jnp.where(grp == m1, io_g, gsz), axis=0, keepdims=True)
        m2 = jnp.max(jnp.where(io_g == i1, NEG_INF, grp), axis=0, keepdims=True)
        gs.append(m1 + m2)
    masked = []
    for g in range(N_GROUPS):
        rank = jnp.zeros((1, tm), F32)
        for o in range(N_GROUPS):
            if o == g:
                continue
            ahead = (gs[o] >= gs[g]) if o < g else (gs[o] > gs[g])
            rank = rank + jnp.where(ahead, 1.0, 0.0)
        keep = jnp.where(rank < TOPK_GROUPS, 0.0, NEG_INF)
        masked.append(sel[g * gsz:(g + 1) * gsz, :] + keep)
    msel = jnp.concatenate(masked, axis=0)

    io_e = lax.broadcasted_iota(I32, (n_exp, tm), 0)
    chosen = jnp.zeros((n_exp, tm), F32)
    picks, weights = [], []
    for _ in range(TOP_K):
        m = jnp.max(msel, axis=0, keepdims=True)
        ei = jnp.min(jnp.where(msel == m, io_e, n_exp), axis=0, keepdims=True)
        pick = io_e == ei
        weights.append(jnp.sum(jnp.where(pick, s, 0.0), axis=0, keepdims=True))
        picks.append(ei)
        chosen = jnp.where(pick, 1.0, chosen)
        msel = jnp.where(pick, NEG_INF, msel)
    wsum = weights[0]
    for w in weights[1:]:
        wsum = wsum + w

    rank_in_expert = carry_ref[...] + jnp.dot(chosen.astype(BF16), tri_ref[...], preferred_element_type=F32)
    carry_new = carry_ref[...] + jnp.sum(chosen, axis=1, keepdims=True)
    carry_ref[...] = carry_new
    cnt_ref[...] = carry_new
    for kk in range(TOP_K):
        eidx_ref[kk:kk + 1, :] = picks[kk]
        wsel_ref[kk:kk + 1, :] = weights[kk] / wsum * ROUTE_SCALE
        pk = jnp.sum(jnp.where(io_e == picks[kk], rank_in_expert, 0.0), axis=0, keepdims=True)
        pos_ref[kk:kk + 1, :] = pk.astype(I32)


def moe_route(h, w_router, router_bias, tm):
    n, d = h.shape
    n_exp = w_router.shape[1]
    tri = np.triu(np.ones((tm, tm), np.float32), 1)
    row8 = lambda dt: jax.ShapeDtypeStruct((TOP_K, n), dt)
    return pl.pallas_call(
        functools.partial(_router_kernel, n_exp=n_exp),
        grid=(n // tm,),
        in_specs=[pl.BlockSpec((tm, d), lambda i: (i, 0)),
                  pl.BlockSpec((n_exp, d), lambda i: (0, 0)),
                  pl.BlockSpec((n_exp, 1), lambda i: (0, 0)),
                  pl.BlockSpec((tm, tm), lambda i: (0, 0))],
        out_specs=[pl.BlockSpec((TOP_K, tm), lambda i: (0, i)),
                   pl.BlockSpec((TOP_K, tm), lambda i: (0, i)),
                   pl.BlockSpec((TOP_K, tm), lambda i: (0, i)),
                   pl.BlockSpec((n_exp, 1), lambda i: (0, 0))],
        out_shape=[row8(I32), row8(F32), row8(I32), jax.ShapeDtypeStruct((n_exp, 1), F32)],
        scratch_shapes=[pltpu.VMEM((n_exp, 1), F32)],
        compiler_params=_cparams("arbitrary"),
        name="moe_router",
    )(h, w_router.T, router_bias.reshape(n_exp, 1), jnp.asarray(tri, BF16))


def _dispatch_kernel(dest_ref, h_ref, xs_ref, sem, *, per):
    tm = h_ref.shape[0] // per

    def row_copy(t, dst_row):
        return pltpu.make_async_copy(_row_tile(h_ref, t, per), _row_tile(xs_ref, dst_row, per), sem)

    def issue(t, c):
        for kk in range(TOP_K):
            row_copy(t, dest_ref[kk, t]).start()
        return c

    def drain(t, c):
        for kk in range(TOP_K):
            row_copy(t, dest_ref[kk, t]).wait()
        return c

    lax.fori_loop(0, tm, issue, 0)
    lax.fori_loop(0, tm, drain, 0)


def moe_dispatch(h, dest, tm):
    n = dest.shape[1]
    per = h.shape[0] // n
    return pl.pallas_call(
        functools.partial(_dispatch_kernel, per=per),
        grid=(n // tm,),
        in_specs=[pl.BlockSpec((TOP_K, tm), lambda i: (0, i), memory_space=pltpu.SMEM),
                  pl.BlockSpec((tm * per, LANES), lambda i: (i, 0))],
        out_specs=pl.BlockSpec(memory_space=pl.ANY),
        out_shape=jax.ShapeDtypeStruct((n * TOP_K * per, LANES), h.dtype),
        scratch_shapes=[pltpu.SemaphoreType.DMA(())],
        compiler_params=_cparams("arbitrary"),
        name="moe_dispatch",
    )(dest, h)


def _expert_kernel(blk_ref, exp_ref, lo_ref, hi_ref, x_ref, wg_ref, wu_ref, wd_ref, y_ref,
                   wgb_ref, wub_ref, wdb_ref):
    j = pl.program_id(0)
    per = wg_ref.shape[0] // 2 // LANES
    blk = x_ref.shape[0] // per
    lo = lo_ref[j] - blk_ref[j] * blk
    hi = hi_ref[j] - blk_ref[j] * blk

    @pl.when(jnp.logical_or(j == 0, exp_ref[j] != exp_ref[jnp.maximum(j - 1, 0)]))
    def _():
        wgb_ref[...] = wg_ref[...].astype(BF16)
        wub_ref[...] = wu_ref[...].astype(BF16)
        wdb_ref[...] = wd_ref[...].astype(BF16)

    @pl.when(hi > lo)
    def _():
        x = _unpack_rows(_rows_load(x_ref, per)).astype(BF16)
        g = jnp.dot(x, wgb_ref[...], preferred_element_type=F32)
        u = jnp.dot(x, wub_ref[...], preferred_element_type=F32)
        y = jnp.dot((_silu(g) * u).astype(BF16), wdb_ref[...], preferred_element_type=F32)
        row = lax.broadcasted_iota(I32, y.shape, 0)
        y = jnp.where(jnp.logical_and(row >= lo, row < hi), y, 0.0)

        @pl.when(lo == 0)
        def _():
            _rows_store(y_ref, _pack_rows(y))

        @pl.when(lo != 0)
        def _():
            _rows_store(y_ref, _pack_rows(_unpack_rows(_rows_load(y_ref, per)) + y))


def moe_experts(xs, seg_blk, seg_exp, seg_lo, seg_hi, w_gate, w_up, w_down, layer, blk):
    d, ff = w_gate.shape[2:]
    wspec = lambda shape: pl.BlockSpec((None, None) + shape, lambda j, sb, se, lo, hi: (layer, se[j], 0, 0))
    rows = pl.BlockSpec((blk * (d // 2 // LANES), LANES), lambda j, sb, se, lo, hi: (sb[j], 0))
    return pl.pallas_call(
        _expert_kernel,
        grid_spec=pltpu.PrefetchScalarGridSpec(
            num_scalar_prefetch=4, grid=(seg_blk.shape[0],),
            in_specs=[rows, wspec((d, ff)), wspec((d, ff)), wspec((ff, d))],
            out_specs=rows,
            scratch_shapes=[pltpu.VMEM((d, ff), BF16), pltpu.VMEM((d, ff), BF16), pltpu.VMEM((ff, d), BF16)]),
        out_shape=jax.ShapeDtypeStruct(xs.shape, xs.dtype),
        compiler_params=_cparams("arbitrary"),
        name="moe_experts",
    )(seg_blk, seg_exp, seg_lo, seg_hi, xs, w_gate, w_up, w_down)


def _combine_kernel(dest_ref, dnext_ref, ys_ref, wsel_ref, h_ref, x_ref, gt_ref, sg_ref, su_ref, sd_ref, nf_ref,
                    o_ref, buf_ref, sems, *, final_norm):
    i = pl.program_id(0)
    tm, d = x_ref.shape
    per = d // 2 // LANES

    def row_copy(idx_ref, slot, t, kk):
        return pltpu.make_async_copy(_row_tile(ys_ref, idx_ref[kk, t], per),
                                     _row_tile(buf_ref.at[slot], kk * tm + t, per), sems.at[slot])

    def start_tile(idx_ref, slot):
        def issue(t, c):
            for kk in range(TOP_K):
                row_copy(idx_ref, slot, t, kk).start()
            return c
        lax.fori_loop(0, tm, issue, 0)

    slot = i % 2

    @pl.when(i == 0)
    def _():
        start_tile(dest_ref, 0)

    @pl.when(i + 1 < pl.num_programs(0))
    def _():
        start_tile(dnext_ref, 1 - slot)

    h = _unpack_rows(_rows_load(h_ref, per)).astype(BF16)
    g = jnp.dot(h, sg_ref[...], preferred_element_type=F32)
    u = jnp.dot(h, su_ref[...], preferred_element_type=F32)
    y = _bdot(_silu(g) * u, sd_ref[...])

    def drain(t, c):
        for kk in range(TOP_K):
            row_copy(dest_ref, slot, t, kk).wait()
        return c

    lax.fori_loop(0, tm, drain, 0)
    w = wsel_ref[...]
    rows = buf_ref.at[slot]
    for kk in range(TOP_K):
        y = y + w[:, kk:kk + 1] * _unpack_rows(_rows_load(rows, per, first=kk * tm, rows=tm))
    x_new = x_ref[...] + gt_ref[...] * y
    o_ref[...] = _rms(x_new, nf_ref[...]) if final_norm else x_new


def moe_combine(ys, dest, wsel_t, h, x, mod, ws_gate, ws_up, ws_down, norm_final, final_norm, tm, seq_len):
    n, d = x.shape
    ff = ws_gate.shape[1]
    last = n // tm - 1
    return pl.pallas_call(
        functools.partial(_combine_kernel, final_norm=final_norm),
        grid=(n // tm,),
        in_specs=[pl.BlockSpec((TOP_K, tm), lambda i: (0, i), memory_space=pltpu.SMEM),
                  pl.BlockSpec((TOP_K, tm), lambda i: (0, jnp.minimum(i + 1, last)), memory_space=pltpu.SMEM),
                  pl.BlockSpec(memory_space=pl.ANY),
                  pl.BlockSpec((tm, TOP_K), lambda i: (i, 0)),
                  pl.BlockSpec((tm * (d // 2 // LANES), LANES), lambda i: (i, 0)),
                  pl.BlockSpec((tm, d), lambda i: (i, 0)),
                  _mod_spec(mod, 5, tm, seq_len, d),
                  pl.BlockSpec((d, ff), lambda i: (0, 0)),
                  pl.BlockSpec((d, ff), lambda i: (0, 0)),
                  pl.BlockSpec((ff, d), lambda i: (0, 0)),
                  pl.BlockSpec((1, d), lambda i: (0, 0))],
        out_specs=pl.BlockSpec((tm, d), lambda i: (i, 0)),
        out_shape=jax.ShapeDtypeStruct((n, d), F32),
        scratch_shapes=[pltpu.VMEM((2, TOP_K * tm * (d // 2 // LANES), LANES), I32),
                        pltpu.SemaphoreType.DMA((2,))],
        compiler_params=_cparams("arbitrary"),
        name="moe_combine",
    )(dest, dest, ys, wsel_t, h, x, mod, ws_gate.astype(BF16), ws_up.astype(BF16), ws_down.astype(BF16),
      norm_final.reshape(1, d))


def _slot_kernel(start_ref, eidx_ref, pos_ref, dest_ref):
    e = eidx_ref[...]
    base = jnp.zeros(e.shape, I32)
    for ex in range(start_ref.shape[0]):
        base = jnp.where(e == ex, start_ref[ex], base)
    dest_ref[...] = base + pos_ref[...]


def moe_slots(start, eidx, pos, tm):
    n = eidx.shape[1]
    spec = pl.BlockSpec((TOP_K, tm), lambda i: (0, i))
    return pl.pallas_call(
        _slot_kernel,
        grid=(n // tm,),
        in_specs=[pl.BlockSpec(memory_space=pltpu.SMEM), spec, spec],
        out_specs=spec,
        out_shape=jax.ShapeDtypeStruct(eidx.shape, I32),
        compiler_params=_cparams("parallel"),
        name="moe_slots",
    )(start, eidx, pos)


def moe_ffn(x, mod, norm_g, w_router, router_bias, w_gate, w_up, w_down, layer, ws_gate, ws_up, ws_down,
            norm_final, final_norm, tm, seq_len, blk):
    n, d = x.shape
    n_exp = w_router.shape[1]
    h32, h = modulate_rows(x, mod, norm_g, tm, seq_len)
    eidx, wsel, pos, counts = moe_route(h32, w_router, router_bias, tm)
    n_rows = n * TOP_K
    n_blocks = n_rows // blk
    counts = counts.reshape(n_exp).astype(I32)
    end = jnp.cumsum(counts)
    start = end - counts
    dest = moe_slots(start, eidx, pos, min(tm, n))
    seg_lo = jnp.sort(jnp.concatenate([jnp.arange(n_blocks, dtype=I32) * blk, start[1:]]))
    seg_hi = jnp.concatenate([seg_lo[1:], jnp.full((1,), n_rows, I32)])
    seg_blk = jnp.minimum(seg_lo // blk, n_blocks - 1)
    seg_exp = jnp.minimum(jnp.sum((end[None, :] <= seg_lo[:, None]).astype(I32), axis=1), n_exp - 1)
    xs = moe_dispatch(h, dest, tm)
    ys = moe_experts(xs, seg_blk, seg_exp, seg_lo, seg_hi, w_gate, w_up, w_down, layer, blk)
    return moe_combine(ys, dest, wsel.T, h, x, mod, ws_gate, ws_up, ws_down, norm_final, final_norm,
                       min(tm, 128), seq_len)


def _sample_score_kernel(pt_ref, qi_ref, w_ref, *refs):
    page_refs, o_ref = refs[:-1], refs[-1]
    kp = jnp.concatenate([r[...] for r in page_refs], axis=1)
    lg = _bdot(qi_ref[...], kp)
    sc = jnp.sum(jnp.maximum(lg, 0.0) * w_ref[...], axis=0, keepdims=True)
    page = page_refs[0].shape[1]
    for p in range(len(page_refs)):
        o_ref[p:p + 1, :] = sc[:, p * page:(p + 1) * page]


def _page_specs(layer, pg, block):
    zeros = (0,) * (len(block) - 2)
    return [pl.BlockSpec(block, lambda b, g, pt, p=p: (layer, pt[b, g * pg + p]) + zeros) for p in range(pg)]


def dsa_sample_scores(qi3, wi3, cache_ki_t, layer, page_table, pg):
    b, n_pages = page_table.shape
    page = cache_ki_t.shape[3]
    return pl.pallas_call(
        _sample_score_kernel,
        grid_spec=pltpu.PrefetchScalarGridSpec(
            num_scalar_prefetch=1, grid=(b, n_pages // pg),
            in_specs=[pl.BlockSpec((None, IDX_HEADS, IDX_DH), lambda b, g, pt: (b, 0, 0)),
                      pl.BlockSpec((None, IDX_HEADS, 1), lambda b, g, pt: (b, 0, 0))]
                     + _page_specs(layer, pg, (None, None, IDX_DH, page)),
            out_specs=pl.BlockSpec((None, pg, page), lambda b, g, pt: (b, g, 0))),
        out_shape=jax.ShapeDtypeStruct((b, n_pages, page), F32),
        compiler_params=_cparams("parallel", "arbitrary"),
        name="dsa_sample_scores",
    )(page_table, qi3, wi3, *([cache_ki_t] * pg))


def _sample_select_kernel(sc_ref, qi_ref, w_ref, kin_ref, triu_ref, tril_ref, selb_ref, selbn_ref, *, topk):
    lg_new = jnp.sum(qi_ref[...].astype(F32) * kin_ref[...].astype(BF16).astype(F32),
                     axis=1, keepdims=True)
    s_new = jnp.sum(jnp.maximum(lg_new, 0.0) * w_ref[...], axis=0, keepdims=True)
    keys = _order_key(sc_ref[...])
    key_new = _order_key(s_new)
    total = lambda a: jnp.sum(jnp.sum(a, axis=0, keepdims=True), axis=1, keepdims=True)

    def count(pred):
        return total(jnp.where(pred(keys), 1.0, 0.0)) + jnp.where(pred(key_new), 1.0, 0.0)

    def bit_body(it, ans):
        cand = ans | (jnp.int32(1) << (31 - it))
        cand_s = cand ^ INT_MIN
        return jnp.where(count(lambda kj: kj >= cand_s) >= topk, cand, ans)

    thr = lax.fori_loop(0, 32, bit_body, jnp.zeros((1, 1), I32)) ^ INT_MIN
    need = topk - count(lambda kj: kj > thr)
    tie = keys == thr
    tie_f = jnp.where(tie, 1.0, 0.0)
    tie_b = tie_f.astype(BF16)
    in_row = jnp.dot(tie_b, triu_ref[...], preferred_element_type=F32)
    rows_before = jnp.sum(jnp.dot(tril_ref[...], tie_b, preferred_element_type=F32), axis=1, keepdims=True)
    before = in_row + rows_before
    selb_ref[...] = jnp.where(keys > thr, 0.0, jnp.where(tie, jnp.where(before < need, 0.0, NEG_INF), NEG_INF))
    selbn_ref[...] = jnp.where(key_new > thr, 0.0,
                               jnp.where(key_new == thr, jnp.where(total(tie_f) < need, 0.0, NEG_INF), NEG_INF))


def dsa_sample_select(scores, qi3, wi3, ki_new, topk):
    b, n_pages, page = scores.shape
    triu = np.triu(np.ones((page, page), np.float32), 1)
    tril = np.tril(np.ones((n_pages, n_pages), np.float32), -1)
    return pl.pallas_call(
        functools.partial(_sample_select_kernel, topk=float(topk)),
        grid=(b,),
        in_specs=[pl.BlockSpec((None, n_pages, page), lambda i: (i, 0, 0)),
                  pl.BlockSpec((None, IDX_HEADS, IDX_DH), lambda i: (i, 0, 0)),
                  pl.BlockSpec((None, IDX_HEADS, 1), lambda i: (i, 0, 0)),
                  pl.BlockSpec((None, 1, IDX_DH), lambda i: (i, 0, 0)),
                  pl.BlockSpec((page, page), lambda i: (0, 0)),
                  pl.BlockSpec((n_pages, n_pages), lambda i: (0, 0))],
        out_specs=[pl.BlockSpec((None, n_pages, page), lambda i: (i, 0, 0)),
                   pl.BlockSpec((None, 1, 1), lambda i: (i, 0, 0))],
        out_shape=[jax.ShapeDtypeStruct((b, n_pages, page), F32), jax.ShapeDtypeStruct((b, 1, 1), F32)],
        compiler_params=_cparams("parallel"),
        name="dsa_sample_select",
    )(scores, qi3, wi3, ki_new, jnp.asarray(triu, BF16), jnp.asarray(tril, BF16))


def _sample_attend_kernel(pt_ref, q_ref, selb_ref, selbn_ref, kn_ref, vn_ref, blast_ref, bfar_ref, bnew_ref,
                          *refs, pg, n_pages):
    k_refs, v_refs = refs[:pg], refs[pg:2 * pg]
    o_ref, m_ref, l_ref, acc_ref = refs[2 * pg:]
    step = pl.program_id(1)
    g_heads = ATT_HEADS // ATT_KV_HEADS
    page = k_refs[0].shape[2]
    dh = q_ref.shape[1]
    head_group = lax.broadcasted_iota(I32, (ATT_HEADS, 1), 0) // g_heads

    @pl.when(step == 0)
    def _():
        m_ref[...] = jnp.full(m_ref.shape, NEG_INF, F32)
        l_ref[...] = jnp.zeros(l_ref.shape, F32)
        acc_ref[...] = jnp.zeros(acc_ref.shape, F32)

    q = q_ref[...]
    s = jnp.zeros((ATT_HEADS, pg * page), F32)
    for g in range(ATT_KV_HEADS):
        kg = jnp.concatenate([r[g] for r in k_refs], axis=1)
        s = jnp.where(head_group == g, _bdot(q, kg), s)
    add = [jnp.where(step * pg + p == n_pages - 1, blast_ref[...], bfar_ref[...]) + selb_ref[p:p + 1, :]
           for p in range(pg)]
    s = s + jnp.concatenate(add, axis=1)
    m_old = m_ref[...]
    m_new = jnp.maximum(m_old, jnp.max(s, axis=1, keepdims=True))
    m_safe = jnp.where(m_new == NEG_INF, 0.0, m_new)
    alpha = jnp.exp(m_old - m_safe)
    p_ = jnp.exp(s - m_safe)
    pv = jnp.zeros((ATT_HEADS, dh), F32)
    for g in range(ATT_KV_HEADS):
        vg = jnp.concatenate([r[g] for r in v_refs], axis=1)
        pv = jnp.where(head_group == g, _bdot_nt(p_, vg), pv)
    l_new = alpha * l_ref[...] + jnp.sum(p_, axis=1, keepdims=True)
    acc_new = alpha * acc_ref[...] + pv
    m_ref[...] = m_new
    l_ref[...] = l_new
    acc_ref[...] = acc_new

    @pl.when(step == pl.num_programs(1) - 1)
    def _():
        kn = jnp.zeros((ATT_HEADS, dh), F32)
        vn = jnp.zeros((ATT_HEADS, dh), F32)
        for g in range(ATT_KV_HEADS):
            kn = jnp.where(head_group == g, kn_ref[g:g + 1, :], kn)
            vn = jnp.where(head_group == g, vn_ref[g:g + 1, :], vn)
        qk = q.astype(F32) * kn.astype(BF16).astype(F32)
        s_n = jnp.sum(qk, axis=1, keepdims=True) + bnew_ref[...] + selbn_ref[...]
        m_fin = jnp.maximum(m_new, s_n)
        m_fs = jnp.where(m_fin == NEG_INF, 0.0, m_fin)
        a2 = jnp.exp(m_new - m_fs)
        p_n = jnp.exp(s_n - m_fs)
        o_ref[...] = ((a2 * acc_new + p_n * vn) / (a2 * l_new + p_n)).astype(o_ref.dtype)


def dsa_sample_attend(q3, selb, selb_new, k_new, v_new, bias_tab, cache_k_t, cache_v_t, layer, page_table, pg):
    b, n_pages = page_table.shape
    kvh, dh, page = cache_k_t.shape[2:]
    assert page == bias_tab.shape[2] and page >= MAX_DISTANCE
    b_last = bias_tab[:, 1, :, 0]
    b_far = bias_tab[:, 2, 0, :1]
    b_new = bias_tab[:, 0, 0, :1]
    full = lambda shape: pl.BlockSpec(shape, lambda b, g, pt: (0,) * len(shape))
    per_b = lambda shape: pl.BlockSpec((None,) + shape, lambda b, g, pt: (b,) + (0,) * len(shape))
    return pl.pallas_call(
        functools.partial(_sample_attend_kernel, pg=pg, n_pages=n_pages),
        grid_spec=pltpu.PrefetchScalarGridSpec(
            num_scalar_prefetch=1, grid=(b, n_pages // pg),
            in_specs=[per_b((ATT_HEADS, dh)),
                      pl.BlockSpec((None, pg, page), lambda b, g, pt: (b, g, 0)),
                      per_b((1, 1)), per_b((kvh, dh)), per_b((kvh, dh)),
                      full((ATT_HEADS, page)), full((ATT_HEADS, 1)), full((ATT_HEADS, 1))]
                     + _page_specs(layer, pg, (None, None, kvh, dh, page))
                     + _page_specs(layer, pg, (None, None, kvh, dh, page)),
            out_specs=per_b((ATT_HEADS, dh)),
            scratch_shapes=[pltpu.VMEM((ATT_HEADS, 1), F32), pltpu.VMEM((ATT_HEADS, 1), F32),
                            pltpu.VMEM((ATT_HEADS, dh), F32)]),
        out_shape=jax.ShapeDtypeStruct((b, ATT_HEADS, dh), BF16),
        compiler_params=_cparams("parallel", "arbitrary"),
        name="dsa_sample_attend",
    )(page_table, q3, selb, selb_new, k_new, v_new, b_last, b_far, b_new,
      *([cache_k_t] * pg), *([cache_v_t] * pg))


def dsa_sample(q, k, v, qi, ki, wit, cache_k, cache_v, cache_ki, layer, page_table, bias_tab):
    b, d = q.shape
    dh = d // ATT_HEADS
    n_pages = page_table.shape[1]
    page = cache_k.shape[2]
    pg = 16 if n_pages % 16 == 0 else n_pages
    topk = min(TOPK_MAX, (n_pages * page + 1) // 4)
    qi3 = qi.reshape(b, IDX_HEADS, IDX_DH)
    wi3 = wit.T.reshape(b, IDX_HEADS, 1)
    cache_ki_t = jnp.transpose(cache_ki, (0, 1, 3, 2))
    cache_k_t = jnp.transpose(cache_k, (0, 1, 3, 4, 2))
    cache_v_t = jnp.transpose(cache_v, (0, 1, 3, 4, 2))
    scores = dsa_sample_scores(qi3, wi3, cache_ki_t, layer, page_table, pg)
    selb, selb_new = dsa_sample_select(scores, qi3, wi3, ki.reshape(b, 1, IDX_DH), topk)
    o = dsa_sample_attend(q.reshape(b, ATT_HEADS, dh), selb, selb_new, k.reshape(b, ATT_KV_HEADS, dh),
                          v.reshape(b, ATT_KV_HEADS, dh), bias_tab, cache_k_t, cache_v_t, layer, page_table, pg)
    return o.reshape(b, d)


def kernel(x_prompt, x_sample, c_prompt, c_sample, state_gla, cache_k, cache_v, cache_idx_k, page_table,
           rel_bias, w_ada, b_ada, norm_mix, norm_ffn, norm_final,
           gla_w_in, gla_w_g2, gla_b_g2, gla_gnorm, gla_w_out,
           dsa_w_in, dsa_idx_ln_g, dsa_idx_ln_b, dsa_w_out,
           moe_w_router, moe_router_bias, moe_w_gate, moe_w_up, moe_w_down,
           shared_w_gate, shared_w_up, shared_w_down):
    bp, t, d = x_prompt.shape
    bs = x_sample.shape[0]
    depth = w_ada.shape[0]
    dh = d // ATT_HEADS
    tm_p = min(256, t)
    tq = 128
    mods = ada_mod_all(jnp.concatenate([c_prompt, c_sample], axis=0), w_ada, b_ada)
    xp = x_prompt.reshape(bp * t, d)
    xs = x_sample.reshape(bs, d)
    bias_tab = bias_tables(rel_bias, tq)
    gla_p, gla_s, kp_l, vp_l, kip_l, ks_l, vs_l, kis_l = [], [], [], [], [], [], [], []
    for i in range(depth):
        mod_p = mods[i, :bp].reshape(bp, 1, 6 * d)
        mod_s = mods[i, bp:]
        j = i // 2
        if i % 2 == 0:
            qkvr_p, la_p = gla_project(xp, mod_p, norm_mix[i], gla_w_in[j], gla_w_g2[j], gla_b_g2[j], tm_p, t)
            o_p, s_p = gla_chunked(qkvr_p, la_p, gla_gnorm[j], bp, t)
            qkvr_s, la_s = gla_project(xs, mod_s, norm_mix[i], gla_w_in[j], gla_w_g2[j], gla_b_g2[j], bs, 1)
            o_s, s_s = gla_step(qkvr_s, la_s, state_gla[j], gla_gnorm[j])
            gla_p.append(s_p)
            gla_s.append(s_s)
            w_out = gla_w_out[j]
        else:
            dp = (dsa_w_in[j], dsa_idx_ln_g[j], dsa_idx_ln_b[j])
            q_p, qi_p, wit_p, kt_p, vt32_p, kit_p, kib_p, kg_p, vt_p = dsa_project(
                xp, mod_p, norm_mix[i], *dp, tm_p, t, True)
            o_p = dsa_prompt(q_p, qi_p, wit_p, kib_p, kg_p, vt_p, bias_tab, bp, t, tq)
            q_s, qi_s, wit_s, k_s, v_s, ki_s = dsa_project(xs, mod_s, norm_mix[i], *dp, bs, 1, False)
            o_s = dsa_sample(q_s, k_s, v_s, qi_s, ki_s, wit_s, cache_k, cache_v, cache_idx_k, j,
                             page_table, bias_tab)
            kp_l.append(kt_p.reshape(bp, ATT_KV_HEADS, dh, t).transpose(0, 3, 1, 2))
            vp_l.append(vt32_p.reshape(bp, ATT_KV_HEADS, dh, t).transpose(0, 3, 1, 2))
            kip_l.append(kit_p.transpose(0, 2, 1))
            ks_l.append(k_s.reshape(bs, 1, ATT_KV_HEADS, dh))
            vs_l.append(v_s.reshape(bs, 1, ATT_KV_HEADS, dh))
            kis_l.append(ki_s.reshape(bs, 1, IDX_DH))
            w_out = dsa_w_out[j]
        xp = out_project(o_p, w_out, xp, mod_p, tm_p, t)
        xs = out_project(o_s, w_out, xs, mod_s, bs, 1)
        last = i == depth - 1
        mo = (moe_w_router[i], moe_router_bias[i], moe_w_gate, moe_w_up, moe_w_down, i,
              shared_w_gate[i], shared_w_up[i], shared_w_down[i], norm_final, last)
        xp = moe_ffn(xp, mod_p, norm_ffn[i], *mo, tm_p, t, 256)
        xs = moe_ffn(xs, mod_s, norm_ffn[i], *mo, bs, 1, 32)
    return (xp.reshape(bp, t, d), xs.reshape(bs, 1, d), jnp.stack(gla_p), jnp.stack(gla_s),
            jnp.stack(kp_l), jnp.stack(vp_l), jnp.stack(kip_l),
            jnp.stack(ks_l), jnp.stack(vs_l), jnp.stack(kis_l))
```

```python
import functools
import math

import numpy as np
import jax
import jax.numpy as jnp
from jax import lax
from jax.experimental import pallas as pl
from jax.experimental.pallas import tpu as pltpu

F32 = jnp.float32
BF16 = jnp.bfloat16
I32 = jnp.int32

GLA_HEADS = 4
GLA_RANK = 16
GLA_NORMALIZER = 16.0
ATT_HEADS = 16
ATT_KV_HEADS = 4
IDX_HEADS = 8
IDX_DH = 64
TOPK_MAX = 256
N_BUCKETS = 32
MAX_DISTANCE = 128
N_GROUPS = 8
TOPK_GROUPS = 4
TOP_K = 8
ROUTE_SCALE = 2.5
EPS = 1e-6

LANES = 128
VMEM_LIMIT = 56 * 1024 * 1024
NEG_INF = float("-inf")
INT_MIN = -2 ** 31


def _cparams(*sem):
    return pltpu.CompilerParams(dimension_semantics=sem, vmem_limit_bytes=VMEM_LIMIT)


def _bdot(a, b):
    return jnp.dot(a.astype(BF16), b.astype(BF16), preferred_element_type=F32)


def _bdot_nt(a, b):
    return lax.dot_general(a.astype(BF16), b.astype(BF16), (((1,), (1,)), ((), ())),
                           preferred_element_type=F32)


def _bdot_tn(a, b):
    return lax.dot_general(a.astype(BF16), b.astype(BF16), (((0,), (0,)), ((), ())),
                           preferred_element_type=F32)


def _split3(a):
    hi = a.astype(BF16)
    r1 = a - hi.astype(F32)
    mid = r1.astype(BF16)
    lo = (r1 - mid.astype(F32)).astype(BF16)
    return hi, mid, lo


def _silu(x):
    return x * (1.0 / (1.0 + jnp.exp(-x)))


def _sigmoid(x):
    return 1.0 / (1.0 + jnp.exp(-x))


def _rms(x, g):
    return x * lax.rsqrt(jnp.mean(x * x, axis=-1, keepdims=True) + EPS) * g


def _ada_kernel(c_ref, w_ref, b_ref, o_ref):
    c_hi, c_mid, _ = _split3(_silu(c_ref[...]))
    w_hi, w_mid, _ = _split3(w_ref[...])
    dot = lambda a, b: jnp.dot(a, b, preferred_element_type=F32)
    o_ref[...] = (dot(c_hi, w_mid) + dot(c_mid, w_hi)) + dot(c_hi, w_hi) + b_ref[...]


def ada_mod_all(c, w_ada, b_ada, tn=512):
    depth, d, n6 = w_ada.shape
    rows = c.shape[0]
    return pl.pallas_call(
        _ada_kernel,
        grid=(depth, n6 // tn),
        in_specs=[pl.BlockSpec((rows, d), lambda l, j: (0, 0)),
                  pl.BlockSpec((None, d, tn), lambda l, j: (l, 0, j)),
                  pl.BlockSpec((None, 1, tn), lambda l, j: (l, 0, j))],
        out_specs=pl.BlockSpec((None, rows, tn), lambda l, j: (l, 0, j)),
        out_shape=jax.ShapeDtypeStruct((depth, rows, n6), F32),
        compiler_params=_cparams("parallel", "parallel"),
        name="ada_mod",
    )(c, w_ada, b_ada.reshape(depth, 1, n6))


def _mod_spec(mod, which, tm, seq_len, d):
    if mod.ndim == 3:
        per_seq = seq_len // tm
        return pl.BlockSpec((None, 1, d), lambda i: (i // per_seq, 0, which))
    return pl.BlockSpec((tm, d), lambda i: (i, which))


def _gla_proj_kernel(x_ref, g_ref, sh_ref, sc_ref, w_ref, wg2_ref, bg2_ref, qkvr_ref, la_ref, *, n_main):
    h = _rms(x_ref[...], g_ref[...]) * (1.0 + sc_ref[...]) + sh_ref[...]
    y = _bdot(h, w_ref[...])
    qkvr_ref[...] = y[:, :n_main]
    g1 = y[:, n_main:n_main + GLA_RANK]
    z = _bdot(g1, wg2_ref[...]) + bg2_ref[...]
    la_ref[...] = (jnp.minimum(z, 0.0) - jnp.log(1.0 + jnp.exp(-jnp.abs(z)))) * (1.0 / GLA_NORMALIZER)


def gla_project(x, mod, norm_g, w_in, w_g2, b_g2, tm, seq_len):
    n, d = x.shape
    hk = w_g2.shape[1]
    n_main = w_in.shape[1] - GLA_RANK
    n_pad = (-w_in.shape[1]) % LANES
    w = jnp.pad(w_in, ((0, 0), (0, n_pad))).astype(BF16)
    return pl.pallas_call(
        functools.partial(_gla_proj_kernel, n_main=n_main),
        grid=(n // tm,),
        in_specs=[pl.BlockSpec((tm, d), lambda i: (i, 0)),
                  pl.BlockSpec((1, d), lambda i: (0, 0)),
                  _mod_spec(mod, 0, tm, seq_len, d),
                  _mod_spec(mod, 1, tm, seq_len, d),
                  pl.BlockSpec(w.shape, lambda i: (0, 0)),
                  pl.BlockSpec(w_g2.shape, lambda i: (0, 0)),
                  pl.BlockSpec((1, hk), lambda i: (0, 0))],
        out_specs=[pl.BlockSpec((tm, n_main), lambda i: (i, 0)),
                   pl.BlockSpec((tm, hk), lambda i: (i, 0))],
        out_shape=[jax.ShapeDtypeStruct((n, n_main), F32),
                   jax.ShapeDtypeStruct((n, hk), F32)],
        compiler_params=_cparams("parallel"),
        name="gla_proj",
    )(x, norm_g.reshape(1, d), mod, mod, w, w_g2.astype(BF16), b_g2.reshape(1, hk))


def _gla_level_matrices(c):
    levels = int(math.log2(c))
    t = np.arange(c)[:, None]
    u = np.arange(c)[None, :]
    mats, masks = [], []
    for l in range(levels):
        m = 1 << l
        ref = (t // (2 * m)) * (2 * m) + m - 1
        right = (t % (2 * m)) >= m
        mat = np.where(right, (u > ref) & (u <= t), (u > t) & (u <= ref))
        mats.append(mat)
        masks.append((t // (2 * m) == u // (2 * m)) & right & ((u % (2 * m)) < m))
    mats.append(u <= t)
    masks.append(t == u)
    return (np.stack(mats).astype(np.float32).reshape(-1, c), np.stack(masks).astype(np.float32))


def _gla_chunk_kernel(q_ref, k_ref, v_ref, r_ref, la_ref, mat_ref, mask_ref, gn_ref,
                      o_ref, s_out_ref, st_ref, *, c, dk):
    ci = pl.program_id(1)
    nlev = mask_ref.shape[0] - 1
    hk = la_ref.shape[1]
    dv = v_ref.shape[1] // GLA_HEADS

    @pl.when(ci == 0)
    def _():
        st_ref[...] = jnp.zeros_like(st_ref)

    la = la_ref[...]
    hi = la.astype(BF16)
    lo = (la - hi.astype(F32)).astype(BF16)
    e_all = jnp.dot(mat_ref[...], jnp.concatenate([hi, lo], axis=1), preferred_element_type=F32)
    e_all = e_all[:, :hk] + e_all[:, hk:]

    for h in range(GLA_HEADS):
        ks, vs = slice(h * dk, (h + 1) * dk), slice(h * dv, (h + 1) * dv)
        q = q_ref[:, ks] * (dk ** -0.5)
        k = k_ref[:, ks]
        v = v_ref[:, vs]
        att = jnp.where(mask_ref[nlev] > 0.0, _bdot_nt(q, k), 0.0)
        for l in range(nlev):
            ex = jnp.exp(e_all[l * c:(l + 1) * c, ks])
            att = att + jnp.where(mask_ref[l] > 0.0, _bdot_nt(q * ex, k * ex), 0.0)

        b = e_all[nlev * c:(nlev + 1) * c, ks]
        b_end = b[c - 1:c, :]
        st = st_ref[h]
        o = _bdot(att, v) + _bdot_nt(q * jnp.exp(b), st)
        st_new = st * jnp.exp(b_end) + _bdot_tn(v, k * jnp.exp(b_end - b))
        st_ref[h] = st_new

        on = o * lax.rsqrt(jnp.mean(o * o, axis=-1, keepdims=True) + EPS) * gn_ref[...]
        o_ref[:, vs] = (on * _silu(r_ref[:, vs])).astype(o_ref.dtype)

        @pl.when(ci == pl.num_programs(1) - 1)
        def _():
            s_out_ref[h] = st_new.T


def gla_chunked(qkvr, log_a, gnorm, batch, seq_len, c=128):
    n = qkvr.shape[0]
    hk = log_a.shape[1]
    dk = hk // GLA_HEADS
    hv = (qkvr.shape[1] - 2 * hk) // 2
    dv = hv // GLA_HEADS
    nc = seq_len // c
    mats, masks = _gla_level_matrices(c)
    row = lambda b, ci: b * nc + ci
    return pl.pallas_call(
        functools.partial(_gla_chunk_kernel, c=c, dk=dk),
        grid=(batch, nc),
        in_specs=[pl.BlockSpec((c, hk), lambda b, ci: (row(b, ci), 0)),
                  pl.BlockSpec((c, hk), lambda b, ci: (row(b, ci), 1)),
                  pl.BlockSpec((c, hv), lambda b, ci: (row(b, ci), 2 * hk // hv)),
                  pl.BlockSpec((c, hv), lambda b, ci: (row(b, ci), 2 * hk // hv + 1)),
                  pl.BlockSpec((c, hk), lambda b, ci: (row(b, ci), 0)),
                  pl.BlockSpec(mats.shape, lambda b, ci: (0, 0)),
                  pl.BlockSpec(masks.shape, lambda b, ci: (0, 0, 0)),
                  pl.BlockSpec((1, dv), lambda b, ci: (0, 0))],
        out_specs=[pl.BlockSpec((c, hv), lambda b, ci: (row(b, ci), 0)),
                   pl.BlockSpec((None, GLA_HEADS, dk, dv), lambda b, ci: (b, 0, 0, 0))],
        out_shape=[jax.ShapeDtypeStruct((n, hv), BF16),
                   jax.ShapeDtypeStruct((batch, GLA_HEADS, dk, dv), F32)],
        scratch_shapes=[pltpu.VMEM((GLA_HEADS, dv, dk), F32)],
        compiler_params=_cparams("parallel", "arbitrary"),
        name="gla_chunked",
    )(qkvr, qkvr, qkvr, qkvr, log_a, jnp.asarray(mats, BF16), jnp.asarray(masks), gnorm.reshape(1, dv))


def _gla_step_kernel(q_ref, k_ref, g_ref, v_ref, r_ref, s_ref, gn_ref, o_ref, s_out_ref, *, dk):
    s_new = jnp.exp(g_ref[...]) * s_ref[...] + k_ref[...] * v_ref[...]
    s_out_ref[...] = s_new
    o = jnp.sum(q_ref[...] * (dk ** -0.5) * s_new, axis=0, keepdims=True)
    on = o * lax.rsqrt(jnp.mean(o * o, axis=-1, keepdims=True) + EPS) * gn_ref[...]
    o_ref[...] = (on * _silu(r_ref[...])).astype(o_ref.dtype)


def gla_step(qkvr, log_a, state, gnorm):
    batch = qkvr.shape[0]
    hk = log_a.shape[1]
    dk = hk // GLA_HEADS
    hv = (qkvr.shape[1] - 2 * hk) // 2
    dv = hv // GLA_HEADS
    col = lambda a: a.reshape(batch, GLA_HEADS, dk, 1)
    rowv = lambda a: a.reshape(batch, GLA_HEADS, 1, dv)
    q, k, v, r = (qkvr[:, :hk], qkvr[:, hk:2 * hk], qkvr[:, 2 * hk:2 * hk + hv], qkvr[:, 2 * hk + hv:])
    cspec = pl.BlockSpec((None, None, dk, 1), lambda b, h: (b, h, 0, 0))
    rspec = pl.BlockSpec((None, None, 1, dv), lambda b, h: (b, h, 0, 0))
    sspec = pl.BlockSpec((None, None, dk, dv), lambda b, h: (b, h, 0, 0))
    o, s_new = pl.pallas_call(
        functools.partial(_gla_step_kernel, dk=dk),
        grid=(batch, GLA_HEADS),
        in_specs=[cspec, cspec, cspec, rspec, rspec, sspec, pl.BlockSpec((1, dv), lambda b, h: (0, 0))],
        out_specs=[rspec, sspec],
        out_shape=[jax.ShapeDtypeStruct((batch, GLA_HEADS, 1, dv), BF16),
                   jax.ShapeDtypeStruct(state.shape, F32)],
        compiler_params=_cparams("parallel", "parallel"),
        name="gla_step",
    )(col(q), col(k), col(log_a), rowv(v), rowv(r), state, gnorm.reshape(1, dv))
    return o.reshape(batch, hv), s_new


def _out_proj_kernel(o_ref, w_ref, x_ref, gt_ref, xo_ref):
    xo_ref[...] = x_ref[...] + gt_ref[...] * _bdot(o_ref[...], w_ref[...])


def out_project(o, w_out, x, mod, tm, seq_len):
    n, d = x.shape
    kdim = o.shape[1]
    return pl.pallas_call(
        _out_proj_kernel,
        grid=(n // tm,),
        in_specs=[pl.BlockSpec((tm, kdim), lambda i: (i, 0)),
                  pl.BlockSpec((kdim, d), lambda i: (0, 0)),
                  pl.BlockSpec((tm, d), lambda i: (i, 0)),
                  _mod_spec(mod, 2, tm, seq_len, d)],
        out_specs=pl.BlockSpec((tm, d), lambda i: (i, 0)),
        out_shape=jax.ShapeDtypeStruct((n, d), F32),
        compiler_params=_cparams("parallel"),
        name="out_proj",
    )(o, w_out.astype(BF16), x, mod)


HI_MASK = -65536


def _pack_rows(val):
    half = val.shape[1] // 2
    bits = lambda a: pltpu.bitcast(a.astype(BF16).astype(F32), I32)
    return lax.shift_right_logical(bits(val[:, :half]), 16) | (bits(val[:, half:]) & HI_MASK)


def _unpack_rows(words):
    return jnp.concatenate([pltpu.bitcast(words << 16, F32), pltpu.bitcast(words & HI_MASK, F32)], axis=1)


def _rows_load(ref, per, first=0, rows=None):
    rows = ref.shape[0] // per - first if rows is None else rows
    return jnp.concatenate([ref[pl.ds(first * per + j, rows, stride=per), :] for j in range(per)], axis=1)


def _rows_store(ref, words):
    rows = words.shape[0]
    per = words.shape[1] // LANES
    for j in range(per):
        ref[pl.ds(j, rows, stride=per), :] = words[:, j * LANES:(j + 1) * LANES]


def _row_tile(ref, r, per):
    return ref.at[pl.ds(pl.multiple_of(r * per, per), per), :]


def _modulate_kernel(x_ref, g_ref, sh_ref, sc_ref, h_ref, hp_ref):
    h = _rms(x_ref[...], g_ref[...]) * (1.0 + sc_ref[...]) + sh_ref[...]
    h_ref[...] = h
    _rows_store(hp_ref, _pack_rows(h))


def modulate_rows(x, mod, norm_g, tm, seq_len):
    n, d = x.shape
    per = d // 2 // LANES
    return pl.pallas_call(
        _modulate_kernel,
        grid=(n // tm,),
        in_specs=[pl.BlockSpec((tm, d), lambda i: (i, 0)),
                  pl.BlockSpec((1, d), lambda i: (0, 0)),
                  _mod_spec(mod, 3, tm, seq_len, d),
                  _mod_spec(mod, 4, tm, seq_len, d)],
        out_specs=[pl.BlockSpec((tm, d), lambda i: (i, 0)),
                   pl.BlockSpec((tm * per, LANES), lambda i: (i, 0))],
        out_shape=[jax.ShapeDtypeStruct((n, d), F32), jax.ShapeDtypeStruct((n * per, LANES), I32)],
        compiler_params=_cparams("parallel"),
        name="modulate_ffn",
    )(x, norm_g.reshape(1, d), mod, mod)


def _dsa_proj_kernel(x_ref, g_ref, sh_ref, sc_ref, w_ref, wwi_ref, lg_ref, lb_ref,
                     q_ref, qi_ref, wit_ref, k_ref, v_ref, ki_ref, *seq_refs, cuts, dh):
    h = (_rms(x_ref[...], g_ref[...]) * (1.0 + sc_ref[...]) + sh_ref[...]).astype(BF16)
    y = jnp.dot(h, w_ref[...], preferred_element_type=F32)
    c0, c1, c2, c3, c4 = cuts
    q_ref[...] = (y[:, :c0] * (dh ** -0.5)).astype(BF16)
    qi_ref[...] = (y[:, c2:c3] * (IDX_DH ** -0.5)).astype(BF16)
    wit_ref[...] = _bdot_nt(wwi_ref[...], h) * (IDX_HEADS ** -0.5)
    k = y[:, c0:c1]
    v = y[:, c1:c2]
    ki = y[:, c3:c4]
    mu = jnp.mean(ki, axis=-1, keepdims=True)
    var = jnp.mean((ki - mu) * (ki - mu), axis=-1, keepdims=True)
    kin = (ki - mu) * lax.rsqrt(var + EPS) * lg_ref[...] + lb_ref[...]
    if not seq_refs:
        k_ref[...] = k
        v_ref[...] = v
        ki_ref[...] = kin
    else:
        kib_ref, kg_ref, vt_ref = seq_refs
        vt = v.T
        k_ref[...] = k.T
        v_ref[...] = vt
        ki_ref[...] = jnp.concatenate([kin, jnp.zeros_like(kin)], axis=1).T[:kin.shape[1]]
        kib_ref[...] = kin.astype(BF16)
        for g in range(ATT_KV_HEADS):
            kg_ref[g] = k[:, g * dh:(g + 1) * dh].astype(BF16)
        vt_ref[...] = vt.astype(BF16)


def dsa_project(x, mod, norm_g, w_in, ln_g, ln_b, tm, seq_len, seq_layouts):
    n, d = x.shape
    dh = d // ATT_HEADS
    nkv = ATT_KV_HEADS * dh
    nqi = IDX_HEADS * IDX_DH
    cuts = (d, d + nkv, d + 2 * nkv, d + 2 * nkv + nqi, d + 2 * nkv + nqi + IDX_DH)
    w = jnp.pad(w_in[:, :cuts[4]], ((0, 0), (0, (-cuts[4]) % LANES))).astype(BF16)
    w_wi_t = w_in[:, cuts[4]:cuts[4] + IDX_HEADS].T.astype(BF16)
    row = lambda wd: pl.BlockSpec((tm, wd), lambda i: (i, 0))
    out_specs = [row(d), row(nqi), pl.BlockSpec((IDX_HEADS, tm), lambda i: (0, i))]
    out_shape = [jax.ShapeDtypeStruct((n, d), BF16), jax.ShapeDtypeStruct((n, nqi), BF16),
                 jax.ShapeDtypeStruct((IDX_HEADS, n), F32)]
    if not seq_layouts:
        out_specs += [row(nkv), row(nkv), row(IDX_DH)]
        out_shape += [jax.ShapeDtypeStruct((n, wd), F32) for wd in (nkv, nkv, IDX_DH)]
    else:
        per_seq = seq_len // tm
        tmin = lambda wd: pl.BlockSpec((None, wd, tm), lambda i: (i // per_seq, 0, i % per_seq))
        out_specs += [tmin(nkv), tmin(nkv), tmin(IDX_DH),
                      row(IDX_DH),
                      pl.BlockSpec((ATT_KV_HEADS, tm, dh), lambda i: (0, i, 0)),
                      pl.BlockSpec((None, nkv, tm), lambda i: (i, 0, 0))]
        out_shape += [jax.ShapeDtypeStruct((n // seq_len, wd, seq_len), F32) for wd in (nkv, nkv, IDX_DH)]
        out_shape += [jax.ShapeDtypeStruct((n, IDX_DH), BF16),
                      jax.ShapeDtypeStruct((ATT_KV_HEADS, n, dh), BF16),
                      jax.ShapeDtypeStruct((n // tm, nkv, tm), BF16)]
    return pl.pallas_call(
        functools.partial(_dsa_proj_kernel, cuts=cuts, dh=dh),
        grid=(n // tm,),
        in_specs=[pl.BlockSpec((tm, d), lambda i: (i, 0)),
                  pl.BlockSpec((1, d), lambda i: (0, 0)),
                  _mod_spec(mod, 0, tm, seq_len, d),
                  _mod_spec(mod, 1, tm, seq_len, d),
                  pl.BlockSpec(w.shape, lambda i: (0, 0)),
                  pl.BlockSpec(w_wi_t.shape, lambda i: (0, 0)),
                  pl.BlockSpec((1, IDX_DH), lambda i: (0, 0)),
                  pl.BlockSpec((1, IDX_DH), lambda i: (0, 0))],
        out_specs=out_specs,
        out_shape=out_shape,
        compiler_params=_cparams("parallel"),
        name="dsa_proj",
    )(x, norm_g.reshape(1, d), mod, mod, w, w_wi_t, ln_g.reshape(1, IDX_DH), ln_b.reshape(1, IDX_DH))


def _t5_bucket_np(dist):
    dist = np.maximum(dist, 0)
    max_exact = N_BUCKETS // 2
    ratio = np.log(np.maximum(dist, max_exact).astype(np.float32) / max_exact) / math.log(MAX_DISTANCE / max_exact)
    large = np.minimum(max_exact + (ratio * (N_BUCKETS - max_exact)).astype(np.int32), N_BUCKETS - 1)
    return np.where(dist < max_exact, dist, large).astype(np.int32)


def _bias_table_kernel(rb_ref, bk_ref, o_ref):
    h = pl.program_id(0)
    for kind in range(3):
        bk = bk_ref[kind]
        acc = jnp.zeros(bk.shape, F32)
        for b in range(N_BUCKETS):
            acc = jnp.where(bk == b, rb_ref[b, h], acc)
        o_ref[kind] = acc


def bias_tables(rel_bias, tq):
    s = np.arange(tq)[:, None]
    t = np.arange(tq)[None, :]
    buckets = np.stack([_t5_bucket_np(t - s), _t5_bucket_np(tq + t - s), _t5_bucket_np(2 * tq + t - s)])
    assert tq >= MAX_DISTANCE and (buckets[2] == N_BUCKETS - 1).all()
    return pl.pallas_call(
        _bias_table_kernel,
        grid=(ATT_HEADS,),
        in_specs=[pl.BlockSpec(memory_space=pltpu.SMEM),
                  pl.BlockSpec(buckets.shape, lambda h: (0, 0, 0))],
        out_specs=pl.BlockSpec((None, 3, tq, tq), lambda h: (h, 0, 0, 0)),
        out_shape=jax.ShapeDtypeStruct((ATT_HEADS, 3, tq, tq), F32),
        compiler_params=_cparams("parallel"),
        name="t5_bias_tiles",
    )(rel_bias, jnp.asarray(buckets))


def _order_key(score):
    score = jnp.where(score == 0.0, 0.0, score)
    bits = pltpu.bitcast(score, I32)
    return bits ^ ((bits >> 31) & 0x7FFFFFFF)


def _dsa_prompt_kernel(q_ref, qi_ref, wit_ref, kib_ref, kg_ref, vt_ref, bias_ref, tril_ref, o_ref,
                       key_ref, selb_ref, qis_ref, qg_ref, sc0_ref, sc1_ref, sc2_ref, sc3_ref, ot_ref,
                       *, tq, dh, topk):
    i = pl.program_id(1)
    nk = i + 1
    g_heads = ATT_HEADS // ATT_KV_HEADS
    s_id = lax.broadcasted_iota(I32, (tq, tq), 0)
    t_id = lax.broadcasted_iota(I32, (tq, tq), 1)
    fold8 = lambda a, op: op(a.reshape(a.shape[0] // 8, 8, a.shape[1]), axis=0)

    for h in range(IDX_HEADS):
        qis_ref[h * tq:(h + 1) * tq, :] = qi_ref[:, h * IDX_DH:(h + 1) * IDX_DH]
    for hh in range(ATT_HEADS):
        g, a = divmod(hh, g_heads)
        qg_ref[g, a * tq:(a + 1) * tq, :] = q_ref[:, hh * dh:(hh + 1) * dh]
    w8 = wit_ref[...]

    def score_body(j, carry):
        kij = kib_ref[pl.ds(pl.multiple_of(j * tq, tq), tq), :]
        lg = _bdot_nt(kij, qis_ref[...])
        acc = jnp.zeros((tq, tq), F32)
        for h in range(IDX_HEADS):
            acc = acc + jnp.maximum(lg[:, h * tq:(h + 1) * tq], 0.0) * w8[h:h + 1, :]
        valid = (s_id + j * tq) <= (t_id + i * tq)
        key_ref[j] = _order_key(jnp.where(valid, acc, NEG_INF))
        return carry

    lax.fori_loop(0, nk, score_body, 0)
    n_pairs = (nk + 1) // 2

    @pl.when(nk % 2 == 1)
    def _():
        pad = jnp.minimum(nk, key_ref.shape[0] - 1)
        key_ref[pad] = jnp.full((tq, tq), INT_MIN, I32)
        selb_ref[pad] = jnp.full((tq, tq), NEG_INF, F32)

    def count(pred):
        def body(jj, acc8):
            ones = jnp.where(pred(key_ref[2 * jj]), 1.0, 0.0) + jnp.where(pred(key_ref[2 * jj + 1]), 1.0, 0.0)
            return acc8 + fold8(ones, jnp.sum)
        return jnp.sum(lax.fori_loop(0, n_pairs, body, jnp.zeros((8, tq), F32)), axis=0, keepdims=True)

    def bit_body(it, ans):
        cand = ans | (jnp.int32(1) << (31 - it))
        cand_s = cand ^ INT_MIN
        return jnp.where(count(lambda kj: kj >= cand_s) >= topk, cand, ans)

    thr = lax.fori_loop(0, 32, bit_body, jnp.zeros((1, tq), I32)) ^ INT_MIN
    need = topk - count(lambda kj: kj > thr)
    n_tie = count(lambda kj: kj == thr)

    def causal(j):
        return (s_id + j * tq) <= (t_id + i * tq)

    some_ties_dropped = jnp.max(n_tie - need) > 0.0

    @pl.when(jnp.logical_not(some_ties_dropped))
    def _():
        def sel_body(j, carry):
            selb_ref[j] = jnp.where(jnp.logical_and(key_ref[j] >= thr, causal(j)), 0.0, NEG_INF)
            return carry
        lax.fori_loop(0, nk, sel_body, 0)

    @pl.when(some_ties_dropped)
    def _():
        def sel_body(j, run):
            kj = key_ref[j]
            tie = kj == thr
            tie_f = jnp.where(tie, 1.0, 0.0)
            before = run + jnp.dot(tril_ref[...], tie_f.astype(BF16), preferred_element_type=F32)
            take = jnp.where(kj > thr, 0.0, jnp.where(tie, jnp.where(before < need, 0.0, NEG_INF), NEG_INF))
            selb_ref[j] = jnp.where(causal(j), take, NEG_INF)
            return run + jnp.sum(tie_f, axis=0, keepdims=True)
        lax.fori_loop(0, nk, sel_body, jnp.zeros((1, tq), F32))

    ones_rows = jnp.ones((16, 2 * tq), BF16)
    sc_refs = (sc0_ref, sc1_ref, sc2_ref, sc3_ref)
    for g0 in range(0, ATT_KV_HEADS, len(sc_refs)):
        groups = tuple((g0 + n, scr) for n, scr in enumerate(sc_refs))

        def logits_body(jj, mx):
            base = pl.multiple_of(jj * 2 * tq, 2 * tq)
            j0 = 2 * jj
            sels = [selb_ref[j0], selb_ref[j0 + 1]]
            kinds = [jnp.clip(i - j0 - c, 0, 2) for c in range(2)]
            out = []
            for (g, scr), mx8 in zip(groups, mx):
                s = _bdot_nt(kg_ref[g, pl.ds(base, 2 * tq), :], qg_ref[g])
                add = jnp.concatenate(
                    [jnp.concatenate([bias_ref[g * g_heads + a, kinds[c]] + sels[c] for a in range(g_heads)], axis=1)
                     for c in range(2)], axis=0)
                s = s + add
                scr[jj] = s
                out.append(jnp.maximum(mx8, fold8(s, jnp.max)))
            return tuple(out)

        neg = jnp.full((8, g_heads * tq), NEG_INF, F32)
        mx = lax.fori_loop(0, n_pairs, logits_body, (neg,) * len(groups))
        ms = []
        for mx8 in mx:
            m = jnp.max(mx8, axis=0, keepdims=True)
            ms.append(jnp.where(m == NEG_INF, 0.0, m))

        def pv_body(jj, accs):
            out = []
            for (g, scr), m, acc in zip(groups, ms, accs):
                p = jnp.exp(scr[jj] - m).astype(BF16)
                lhs = jnp.concatenate([vt_ref[jj, g * dh:(g + 1) * dh, :], ones_rows], axis=0)
                out.append(acc + jnp.dot(lhs, p, preferred_element_type=F32))
            return tuple(out)

        zero = jnp.zeros((dh + 16, g_heads * tq), F32)
        accs = lax.fori_loop(0, n_pairs, pv_body, (zero,) * len(groups))
        for (g, _), acc in zip(groups, accs):
            og = acc[:dh] / acc[dh:dh + 1]
            for a in range(g_heads):
                hh = g * g_heads + a
                ot_ref[hh * dh:(hh + 1) * dh, :] = og[:, a * tq:(a + 1) * tq]
    o_ref[...] = ot_ref[...].T.astype(o_ref.dtype)


def dsa_prompt(q, qi, wit, kib, kg, vt, bias_tab, batch, seq_len, tq=128):
    n, d = q.shape
    dh = d // ATT_HEADS
    nq = seq_len // tq
    g_heads = ATT_HEADS // ATT_KV_HEADS
    assert nq % 2 == 0 and vt.shape[2] == 2 * tq
    topk = min(TOPK_MAX, seq_len // 4)
    tril = np.tril(np.ones((tq, tq), np.float32), -1)
    blk_spec = lambda width: pl.BlockSpec((tq, width), lambda b, i: (b * nq + i, 0))
    return pl.pallas_call(
        functools.partial(_dsa_prompt_kernel, tq=tq, dh=dh, topk=float(topk)),
        grid=(batch, nq),
        in_specs=[blk_spec(d), blk_spec(qi.shape[1]),
                  pl.BlockSpec((IDX_HEADS, tq), lambda b, i: (0, b * nq + i)),
                  pl.BlockSpec((seq_len, IDX_DH), lambda b, i: (b, 0)),
                  pl.BlockSpec((ATT_KV_HEADS, seq_len, dh), lambda b, i: (0, b, 0)),
                  pl.BlockSpec((nq // 2,) + vt.shape[1:], lambda b, i: (b, 0, 0)),
                  pl.BlockSpec(bias_tab.shape, lambda b, i: (0, 0, 0, 0)),
                  pl.BlockSpec((tq, tq), lambda b, i: (0, 0))],
        out_specs=blk_spec(d),
        out_shape=jax.ShapeDtypeStruct((n, d), BF16),
        scratch_shapes=[pltpu.VMEM((nq, tq, tq), I32),
                        pltpu.VMEM((nq, tq, tq), F32),
                        pltpu.VMEM((IDX_HEADS * tq, IDX_DH), BF16),
                        pltpu.VMEM((ATT_KV_HEADS, g_heads * tq, dh), BF16),
                        *([pltpu.VMEM((nq // 2, 2 * tq, g_heads * tq), F32)] * ATT_KV_HEADS),
                        pltpu.VMEM((d, tq), F32)],
        compiler_params=_cparams("parallel", "arbitrary"),
        name="dsa_prompt",
    )(q, qi, wit, kib, kg, vt, bias_tab, jnp.asarray(tril, BF16))


def _router_kernel(h_ref, wr_ref, rb_ref, tri_ref, eidx_ref, wsel_ref, pos_ref, cnt_ref, carry_ref, *, n_exp):
    @pl.when(pl.program_id(0) == 0)
    def _():
        carry_ref[...] = jnp.zeros_like(carry_ref)

    h_hi, h_mid, h_lo = _split3(h_ref[...])
    w_hi, w_mid, w_lo = _split3(wr_ref[...])
    nt = lambda a, b: lax.dot_general(a, b, (((1,), (1,)), ((), ())), preferred_element_type=F32)
    logits = ((nt(w_lo, h_hi) + nt(w_hi, h_lo) + nt(w_mid, h_mid))
              + (nt(w_hi, h_mid) + nt(w_mid, h_hi))) + nt(w_hi, h_hi)
    s = _sigmoid(logits)
    sel = s + rb_ref[...]
    tm = sel.shape[1]
    gsz = n_exp // N_GROUPS
    io_g = lax.broadcasted_iota(I32, (gsz, tm), 0)

    gs = []
    for g in range(N_GROUPS):
        grp = sel[g * gsz:(g + 1) * gsz, :]
        m1 = jnp.max(grp, axis=0, keepdims=True)
        i1 = jnp.min(jnp.where(grp == m1, io_g, gsz), axis=0, keepdims=True)
        m2 = jnp.max(jnp.where(io_g == i1, NEG_INF, grp), axis=0, keepdims=True)
        gs.append(m1 + m2)
    masked = []
    for g in range(N_GROUPS):
        rank = jnp.zeros((1, tm), F32)
        for o in range(N_GROUPS):
            if o == g:
                continue
            ahead = (gs[o] >= gs[g]) if o < g else (gs[o] > gs[g])
            rank = rank + jnp.where(ahead, 1.0, 0.0)
        keep = jnp.where(rank < TOPK_GROUPS, 0.0, NEG_INF)
        masked.append(sel[g * gsz:(g + 1) * gsz, :] + keep)
    msel = jnp.concatenate(masked, axis=0)

    io_e = lax.broadcasted_iota(I32, (n_exp, tm), 0)
    chosen = jnp.zeros((n_exp, tm), F32)
    picks, weights = [], []
    for _ in range(TOP_K):
        m = jnp.max(msel, axis=0, keepdims=True)
        ei = jnp.min(jnp.where(msel == m, io_e, n_exp), axis=0, keepdims=True)
        pick = io_e == ei
        weights.append(jnp.sum(jnp.where(pick, s, 0.0), axis=0, keepdims=True))
        picks.append(ei)
        chosen = jnp.where(pick, 1.0, chosen)
        msel = jnp.where(pick, NEG_INF, msel)
    wsum = weights[0]
    for w in weights[1:]:
        wsum = wsum + w

    rank_in_expert = carry_ref[...] + jnp.dot(chosen.astype(BF16), tri_ref[...], preferred_element_type=F32)
    carry_new = carry_ref[...] + jnp.sum(chosen, axis=1, keepdims=True)
    carry_ref[...] = carry_new
    cnt_ref[...] = carry_new
    for kk in range(TOP_K):
        eidx_ref[kk:kk + 1, :] = picks[kk]
        wsel_ref[kk:kk + 1, :] = weights[kk] / wsum * ROUTE_SCALE
        pk = jnp.sum(jnp.where(io_e == picks[kk], rank_in_expert, 0.0), axis=0, keepdims=True)
        pos_ref[kk:kk + 1, :] = pk.astype(I32)


def moe_route(h, w_router, router_bias, tm):
    n, d = h.shape
    n_exp = w_router.shape[1]
    tri = np.triu(np.ones((tm, tm), np.float32), 1)
    row8 = lambda dt: jax.ShapeDtypeStruct((TOP_K, n), dt)
    return pl.pallas_call(
        functools.partial(_router_kernel, n_exp=n_exp),
        grid=(n // tm,),
        in_specs=[pl.BlockSpec((tm, d), lambda i: (i, 0)),
                  pl.BlockSpec((n_exp, d), lambda i: (0, 0)),
                  pl.BlockSpec((n_exp, 1), lambda i: (0, 0)),
                  pl.BlockSpec((tm, tm), lambda i: (0, 0))],
        out_specs=[pl.BlockSpec((TOP_K, tm), lambda i: (0, i)),
                   pl.BlockSpec((TOP_K, tm), lambda i: (0, i)),
                   pl.BlockSpec((TOP_K, tm), lambda i: (0, i)),
                   pl.BlockSpec((n_exp, 1), lambda i: (0, 0))],
        out_shape=[row8(I32), row8(F32), row8(I32), jax.ShapeDtypeStruct((n_exp, 1), F32)],
        scratch_shapes=[pltpu.VMEM((n_exp, 1), F32)],
        compiler_params=_cparams("arbitrary"),
        name="moe_router",
    )(h, w_router.T, router_bias.reshape(n_exp, 1), jnp.asarray(tri, BF16))


def _dispatch_kernel(dest_ref, h_ref, xs_ref, sem, *, per):
    tm = h_ref.shape[0] // per

    def row_copy(t, dst_row):
        return pltpu.make_async_copy(_row_tile(h_ref, t, per), _row_tile(xs_ref, dst_row, per), sem)

    def issue(t, c):
        for kk in range(TOP_K):
            row_copy(t, dest_ref[kk, t]).start()
        return c

    def drain(t, c):
        for kk in range(TOP_K):
            row_copy(t, dest_ref[kk, t]).wait()
        return c

    lax.fori_loop(0, tm, issue, 0)
    lax.fori_loop(0, tm, drain, 0)


def moe_dispatch(h, dest, tm):
    n = dest.shape[1]
    per = h.shape[0] // n
    return pl.pallas_call(
        functools.partial(_dispatch_kernel, per=per),
        grid=(n // tm,),
        in_specs=[pl.BlockSpec((TOP_K, tm), lambda i: (0, i), memory_space=pltpu.SMEM),
                  pl.BlockSpec((tm * per, LANES), lambda i: (i, 0))],
        out_specs=pl.BlockSpec(memory_space=pl.ANY),
        out_shape=jax.ShapeDtypeStruct((n * TOP_K * per, LANES), h.dtype),
        scratch_shapes=[pltpu.SemaphoreType.DMA(())],
        compiler_params=_cparams("arbitrary"),
        name="moe_dispatch",
    )(dest, h)


def _expert_kernel(blk_ref, exp_ref, lo_ref, hi_ref, x_ref, wg_ref, wu_ref, wd_ref, y_ref,
                   wgb_ref, wub_ref, wdb_ref):
    j = pl.program_id(0)
    per = wg_ref.shape[0] // 2 // LANES
    blk = x_ref.shape[0] // per
    lo = lo_ref[j] - blk_ref[j] * blk
    hi = hi_ref[j] - blk_ref[j] * blk

    @pl.when(jnp.logical_or(j == 0, exp_ref[j] != exp_ref[jnp.maximum(j - 1, 0)]))
    def _():
        wgb_ref[...] = wg_ref[...].astype(BF16)
        wub_ref[...] = wu_ref[...].astype(BF16)
        wdb_ref[...] = wd_ref[...].astype(BF16)

    @pl.when(hi > lo)
    def _():
        x = _unpack_rows(_rows_load(x_ref, per)).astype(BF16)
        g = jnp.dot(x, wgb_ref[...], preferred_element_type=F32)
        u = jnp.dot(x, wub_ref[...], preferred_element_type=F32)
        y = jnp.dot((_silu(g) * u).astype(BF16), wdb_ref[...], preferred_element_type=F32)

        @pl.when(jnp.logical_and(lo == 0, hi == blk))
        def _():
            _rows_store(y_ref, _pack_rows(y))

        @pl.when(jnp.logical_not(jnp.logical_and(lo == 0, hi == blk)))
        def _():
            row = lax.broadcasted_iota(I32, y.shape, 0)
            ym = jnp.where(jnp.logical_and(row >= lo, row < hi), y, 0.0)

            @pl.when(lo == 0)
            def _():
                _rows_store(y_ref, _pack_rows(ym))

            @pl.when(lo != 0)
            def _():
                _rows_store(y_ref, _pack_rows(_unpack_rows(_rows_load(y_ref, per)) + ym))


def moe_experts(xs, seg_blk, seg_exp, seg_lo, seg_hi, w_gate, w_up, w_down, layer, blk):
    d, ff = w_gate.shape[2:]
    wspec = lambda shape: pl.BlockSpec((None, None) + shape, lambda j, sb, se, lo, hi: (layer, se[j], 0, 0))
    rows = pl.BlockSpec((blk * (d // 2 // LANES), LANES), lambda j, sb, se, lo, hi: (sb[j], 0))
    return pl.pallas_call(
        _expert_kernel,
        grid_spec=pltpu.PrefetchScalarGridSpec(
            num_scalar_prefetch=4, grid=(seg_blk.shape[0],),
            in_specs=[rows, wspec((d, ff)), wspec((d, ff)), wspec((ff, d))],
            out_specs=rows,
            scratch_shapes=[pltpu.VMEM((d, ff), BF16), pltpu.VMEM((d, ff), BF16), pltpu.VMEM((ff, d), BF16)]),
        out_shape=jax.ShapeDtypeStruct(xs.shape, xs.dtype),
        compiler_params=_cparams("arbitrary"),
        name="moe_experts",
    )(seg_blk, seg_exp, seg_lo, seg_hi, xs, w_gate, w_up, w_down)


def _combine_kernel(dest_ref, dnext_ref, ys_ref, wsel_ref, h_ref, x_ref, gt_ref, sg_ref, su_ref, sd_ref, nf_ref,
                    o_ref, buf_ref, sems, *, final_norm):
    i = pl.program_id(0)
    tm, d = x_ref.shape
    per = d // 2 // LANES

    def row_copy(idx_ref, slot, t, kk):
        return pltpu.make_async_copy(_row_tile(ys_ref, idx_ref[kk, t], per),
                                     _row_tile(buf_ref.at[slot], kk * tm + t, per), sems.at[slot])

    def start_tile(idx_ref, slot):
        def issue(t, c):
            for kk in range(TOP_K):
                row_copy(idx_ref, slot, t, kk).start()
            return c
        lax.fori_loop(0, tm, issue, 0)

    slot = i % 2

    @pl.when(i == 0)
    def _():
        start_tile(dest_ref, 0)

    @pl.when(i + 1 < pl.num_programs(0))
    def _():
        start_tile(dnext_ref, 1 - slot)

    h = _unpack_rows(_rows_load(h_ref, per)).astype(BF16)
    g = jnp.dot(h, sg_ref[...], preferred_element_type=F32)
    u = jnp.dot(h, su_ref[...], preferred_element_type=F32)
    y = _bdot(_silu(g) * u, sd_ref[...])

    def drain(t, c):
        for kk in range(TOP_K):
            row_copy(dest_ref, slot, t, kk).wait()
        return c

    lax.fori_loop(0, tm, drain, 0)
    w = wsel_ref[...]
    rows = buf_ref.at[slot]
    for kk in range(TOP_K):
        y = y + w[:, kk:kk + 1] * _unpack_rows(_rows_load(rows, per, first=kk * tm, rows=tm))
    x_new = x_ref[...] + gt_ref[...] * y
    o_ref[...] = _rms(x_new, nf_ref[...]) if final_norm else x_new


def moe_combine(ys, dest, wsel_t, h, x, mod, ws_gate, ws_up, ws_down, norm_final, final_norm, tm, seq_len):
    n, d = x.shape
    ff = ws_gate.shape[1]
    last = n // tm - 1
    return pl.pallas_call(
        functools.partial(_combine_kernel, final_norm=final_norm),
        grid=(n // tm,),
        in_specs=[pl.BlockSpec((TOP_K, tm), lambda i: (0, i), memory_space=pltpu.SMEM),
                  pl.BlockSpec((TOP_K, tm), lambda i: (0, jnp.minimum(i + 1, last)), memory_space=pltpu.SMEM),
                  pl.BlockSpec(memory_space=pl.ANY),
                  pl.BlockSpec((tm, TOP_K), lambda i: (i, 0)),
                  pl.BlockSpec((tm * (d // 2 // LANES), LANES), lambda i: (i, 0)),
                  pl.BlockSpec((tm, d), lambda i: (i, 0)),
                  _mod_spec(mod, 5, tm, seq_len, d),
                  pl.BlockSpec((d, ff), lambda i: (0, 0)),
                  pl.BlockSpec((d, ff), lambda i: (0, 0)),
                  pl.BlockSpec((ff, d), lambda i: (0, 0)),
                  pl.BlockSpec((1, d), lambda i: (0, 0))],
        out_specs=pl.BlockSpec((tm, d), lambda i: (i, 0)),
        out_shape=jax.ShapeDtypeStruct((n, d), F32),
        scratch_shapes=[pltpu.VMEM((2, TOP_K * tm * (d // 2 // LANES), LANES), I32),
                        pltpu.SemaphoreType.DMA((2,))],
        compiler_params=_cparams("arbitrary"),
        name="moe_combine",
    )(dest, dest, ys, wsel_t, h, x, mod, ws_gate.astype(BF16), ws_up.astype(BF16), ws_down.astype(BF16),
      norm_final.reshape(1, d))


def _slot_kernel(start_ref, eidx_ref, pos_ref, dest_ref):
    e = eidx_ref[...]
    base = jnp.zeros(e.shape, I32)
    for ex in range(start_ref.shape[0]):
        base = jnp.where(e == ex, start_ref[ex], base)
    dest_ref[...] = base + pos_ref[...]


def moe_slots(start, eidx, pos, tm):
    n = eidx.shape[1]
    spec = pl.BlockSpec((TOP_K, tm), lambda i: (0, i))
    return pl.pallas_call(
        _slot_kernel,
        grid=(n // tm,),
        in_specs=[pl.BlockSpec(memory_space=pltpu.SMEM), spec, spec],
        out_specs=spec,
        out_shape=jax.ShapeDtypeStruct(eidx.shape, I32),
        compiler_params=_cparams("parallel"),
        name="moe_slots",
    )(start, eidx, pos)


def moe_ffn(x, mod, norm_g, w_router, router_bias, w_gate, w_up, w_down, layer, ws_gate, ws_up, ws_down,
            norm_final, final_norm, tm, seq_len, blk):
    n, d = x.shape
    n_exp = w_router.shape[1]
    h32, h = modulate_rows(x, mod, norm_g, tm, seq_len)
    eidx, wsel, pos, counts = moe_route(h32, w_router, router_bias, tm)
    n_rows = n * TOP_K
    n_blocks = n_rows // blk
    counts = counts.reshape(n_exp).astype(I32)
    end = jnp.cumsum(counts)
    start = end - counts
    dest = moe_slots(start, eidx, pos, min(tm, n))
    seg_lo = jnp.sort(jnp.concatenate([jnp.arange(n_blocks, dtype=I32) * blk, start[1:]]))
    seg_hi = jnp.concatenate([seg_lo[1:], jnp.full((1,), n_rows, I32)])
    seg_blk = jnp.minimum(seg_lo // blk, n_blocks - 1)
    seg_exp = jnp.minimum(jnp.sum((end[None, :] <= seg_lo[:, None]).astype(I32), axis=1), n_exp - 1)
    xs = moe_dispatch(h, dest, tm)
    ys = moe_experts(xs, seg_blk, seg_exp, seg_lo, seg_hi, w_gate, w_up, w_down, layer, blk)
    return moe_combine(ys, dest, wsel.T, h, x, mod, ws_gate, ws_up, ws_down, norm_final, final_norm,
                       min(tm, 128), seq_len)


def _sample_score_kernel(pt_ref, qi_ref, w_ref, *refs):
    page_refs, o_ref = refs[:-1], refs[-1]
    kp = jnp.concatenate([r[...] for r in page_refs], axis=1)
    lg = _bdot(qi_ref[...], kp)
    sc = jnp.sum(jnp.maximum(lg, 0.0) * w_ref[...], axis=0, keepdims=True)
    page = page_refs[0].shape[1]
    for p in range(len(page_refs)):
        o_ref[p:p + 1, :] = sc[:, p * page:(p + 1) * page]


def _page_specs(layer, pg, block):
    zeros = (0,) * (len(block) - 2)
    return [pl.BlockSpec(block, lambda b, g, pt, p=p: (layer, pt[b, g * pg + p]) + zeros) for p in range(pg)]


def dsa_sample_scores(qi3, wi3, cache_ki_t, layer, page_table, pg):
    b, n_pages = page_table.shape
    page = cache_ki_t.shape[3]
    return pl.pallas_call(
        _sample_score_kernel,
        grid_spec=pltpu.PrefetchScalarGridSpec(
            num_scalar_prefetch=1, grid=(b, n_pages // pg),
            in_specs=[pl.BlockSpec((None, IDX_HEADS, IDX_DH), lambda b, g, pt: (b, 0, 0)),
                      pl.BlockSpec((None, IDX_HEADS, 1), lambda b, g, pt: (b, 0, 0))]
                     + _page_specs(layer, pg, (None, None, IDX_DH, page)),
            out_specs=pl.BlockSpec((None, pg, page), lambda b, g, pt: (b, g, 0))),
        out_shape=jax.ShapeDtypeStruct((b, n_pages, page), F32),
        compiler_params=_cparams("parallel", "arbitrary"),
        name="dsa_sample_scores",
    )(page_table, qi3, wi3, *([cache_ki_t] * pg))


def _sample_select_kernel(sc_ref, qi_ref, w_ref, kin_ref, triu_ref, tril_ref, selb_ref, selbn_ref, *, topk):
    lg_new = jnp.sum(qi_ref[...].astype(F32) * kin_ref[...].astype(BF16).astype(F32),
                     axis=1, keepdims=True)
    s_new = jnp.sum(jnp.maximum(lg_new, 0.0) * w_ref[...], axis=0, keepdims=True)
    keys = _order_key(sc_ref[...])
    key_new = _order_key(s_new)
    total = lambda a: jnp.sum(jnp.sum(a, axis=0, keepdims=True), axis=1, keepdims=True)

    def count(pred):
        return total(jnp.where(pred(keys), 1.0, 0.0)) + jnp.where(pred(key_new), 1.0, 0.0)

    def bit_body(it, ans):
        cand = ans | (jnp.int32(1) << (31 - it))
        cand_s = cand ^ INT_MIN
        return jnp.where(count(lambda kj: kj >= cand_s) >= topk, cand, ans)

    thr = lax.fori_loop(0, 32, bit_body, jnp.zeros((1, 1), I32)) ^ INT_MIN
    need = topk - count(lambda kj: kj > thr)
    tie = keys == thr
    tie_f = jnp.where(tie, 1.0, 0.0)
    tie_b = tie_f.astype(BF16)
    in_row = jnp.dot(tie_b, triu_ref[...], preferred_element_type=F32)
    rows_before = jnp.sum(jnp.dot(tril_ref[...], tie_b, preferred_element_type=F32), axis=1, keepdims=True)
    before = in_row + rows_before
    selb_ref[...] = jnp.where(keys > thr, 0.0, jnp.where(tie, jnp.where(before < need, 0.0, NEG_INF), NEG_INF))
    selbn_ref[...] = jnp.where(key_new > thr, 0.0,
                               jnp.where(key_new == thr, jnp.where(total(tie_f) < need, 0.0, NEG_INF), NEG_INF))


def dsa_sample_select(scores, qi3, wi3, ki_new, topk):
    b, n_pages, page = scores.shape
    triu = np.triu(np.ones((page, page), np.float32), 1)
    tril = np.tril(np.ones((n_pages, n_pages), np.float32), -1)
    return pl.pallas_call(
        functools.partial(_sample_select_kernel, topk=float(topk)),
        grid=(b,),
        in_specs=[pl.BlockSpec((None, n_pages, page), lambda i: (i, 0, 0)),
                  pl.BlockSpec((None, IDX_HEADS, IDX_DH), lambda i: (i, 0, 0)),
                  pl.BlockSpec((None, IDX_HEADS, 1), lambda i: (i, 0, 0)),
                  pl.BlockSpec((None, 1, IDX_DH), lambda i: (i, 0, 0)),
                  pl.BlockSpec((page, page), lambda i: (0, 0)),
                  pl.BlockSpec((n_pages, n_pages), lambda i: (0, 0))],
        out_specs=[pl.BlockSpec((None, n_pages, page), lambda i: (i, 0, 0)),
                   pl.BlockSpec((None, 1, 1), lambda i: (i, 0, 0))],
        out_shape=[jax.ShapeDtypeStruct((b, n_pages, page), F32), jax.ShapeDtypeStruct((b, 1, 1), F32)],
        compiler_params=_cparams("parallel"),
        name="dsa_sample_select",
    )(scores, qi3, wi3, ki_new, jnp.asarray(triu, BF16), jnp.asarray(tril, BF16))


def _sample_attend_kernel(pt_ref, q_ref, selb_ref, selbn_ref, kn_ref, vn_ref, blast_ref, bfar_ref, bnew_ref,
                          *refs, pg, n_pages):
    k_refs, v_refs = refs[:pg], refs[pg:2 * pg]
    o_ref, m_ref, l_ref, acc_ref = refs[2 * pg:]
    step = pl.program_id(1)
    g_heads = ATT_HEADS // ATT_KV_HEADS
    page = k_refs[0].shape[2]
    dh = q_ref.shape[1]
    head_group = lax.broadcasted_iota(I32, (ATT_HEADS, 1), 0) // g_heads

    @pl.when(step == 0)
    def _():
        m_ref[...] = jnp.full(m_ref.shape, NEG_INF, F32)
        l_ref[...] = jnp.zeros(l_ref.shape, F32)
        acc_ref[...] = jnp.zeros(acc_ref.shape, F32)

    q = q_ref[...]
    s = jnp.zeros((ATT_HEADS, pg * page), F32)
    for g in range(ATT_KV_HEADS):
        kg = jnp.concatenate([r[g] for r in k_refs], axis=1)
        s = jnp.where(head_group == g, _bdot(q, kg), s)
    add = [jnp.where(step * pg + p == n_pages - 1, blast_ref[...], bfar_ref[...]) + selb_ref[p:p + 1, :]
           for p in range(pg)]
    s = s + jnp.concatenate(add, axis=1)
    m_old = m_ref[...]
    m_new = jnp.maximum(m_old, jnp.max(s, axis=1, keepdims=True))
    m_safe = jnp.where(m_new == NEG_INF, 0.0, m_new)
    alpha = jnp.exp(m_old - m_safe)
    p_ = jnp.exp(s - m_safe)
    pv = jnp.zeros((ATT_HEADS, dh), F32)
    for g in range(ATT_KV_HEADS):
        vg = jnp.concatenate([r[g] for r in v_refs], axis=1)
        pv = jnp.where(head_group == g, _bdot_nt(p_, vg), pv)
    l_new = alpha * l_ref[...] + jnp.sum(p_, axis=1, keepdims=True)
    acc_new = alpha * acc_ref[...] + pv
    m_ref[...] = m_new
    l_ref[...] = l_new
    acc_ref[...] = acc_new

    @pl.when(step == pl.num_programs(1) - 1)
    def _():
        kn = jnp.zeros((ATT_HEADS, dh), F32)
        vn = jnp.zeros((ATT_HEADS, dh), F32)
        for g in range(ATT_KV_HEADS):
            kn = jnp.where(head_group == g, kn_ref[g:g + 1, :], kn)
            vn = jnp.where(head_group == g, vn_ref[g:g + 1, :], vn)
        qk = q.astype(F32) * kn.astype(BF16).astype(F32)
        s_n = jnp.sum(qk, axis=1, keepdims=True) + bnew_ref[...] + selbn_ref[...]
        m_fin = jnp.maximum(m_new, s_n)
        m_fs = jnp.where(m_fin == NEG_INF, 0.0, m_fin)
        a2 = jnp.exp(m_new - m_fs)
        p_n = jnp.exp(s_n - m_fs)
        o_ref[...] = ((a2 * acc_new + p_n * vn) / (a2 * l_new + p_n)).astype(o_ref.dtype)


def dsa_sample_attend(q3, selb, selb_new, k_new, v_new, bias_tab, cache_k_t, cache_v_t, layer, page_table, pg):
    b, n_pages = page_table.shape
    kvh, dh, page = cache_k_t.shape[2:]
    assert page == bias_tab.shape[2] and page >= MAX_DISTANCE
    b_last = bias_tab[:, 1, :, 0]
    b_far = bias_tab[:, 2, 0, :1]
    b_new = bias_tab[:, 0, 0, :1]
    full = lambda shape: pl.BlockSpec(shape, lambda b, g, pt: (0,) * len(shape))
    per_b = lambda shape: pl.BlockSpec((None,) + shape, lambda b, g, pt: (b,) + (0,) * len(shape))
    return pl.pallas_call(
        functools.partial(_sample_attend_kernel, pg=pg, n_pages=n_pages),
        grid_spec=pltpu.PrefetchScalarGridSpec(
            num_scalar_prefetch=1, grid=(b, n_pages // pg),
            in_specs=[per_b((ATT_HEADS, dh)),
                      pl.BlockSpec((None, pg, page), lambda b, g, pt: (b, g, 0)),
                      per_b((1, 1)), per_b((kvh, dh)), per_b((kvh, dh)),
                      full((ATT_HEADS, page)), full((ATT_HEADS, 1)), full((ATT_HEADS, 1))]
                     + _page_specs(layer, pg, (None, None, kvh, dh, page))
                     + _page_specs(layer, pg, (None, None, kvh, dh, page)),
            out_specs=per_b((ATT_HEADS, dh)),
            scratch_shapes=[pltpu.VMEM((ATT_HEADS, 1), F32), pltpu.VMEM((ATT_HEADS, 1), F32),
                            pltpu.VMEM((ATT_HEADS, dh), F32)]),
        out_shape=jax.ShapeDtypeStruct((b, ATT_HEADS, dh), BF16),
        compiler_params=_cparams("parallel", "arbitrary"),
        name="dsa_sample_attend",
    )(page_table, q3, selb, selb_new, k_new, v_new, b_last, b_far, b_new,
      *([cache_k_t] * pg), *([cache_v_t] * pg))


def dsa_sample(q, k, v, qi, ki, wit, cache_k, cache_v, cache_ki, layer, page_table, bias_tab):
    b, d = q.shape
    dh = d // ATT_HEADS
    n_pages = page_table.shape[1]
    page = cache_k.shape[2]
    pg = 16 if n_pages % 16 == 0 else n_pages
    topk = min(TOPK_MAX, (n_pages * page + 1) // 4)
    qi3 = qi.reshape(b, IDX_HEADS, IDX_DH)
    wi3 = wit.T.reshape(b, IDX_HEADS, 1)
    cache_ki_t = jnp.transpose(cache_ki, (0, 1, 3, 2))
    cache_k_t = jnp.transpose(cache_k, (0, 1, 3, 4, 2))
    cache_v_t = jnp.transpose(cache_v, (0, 1, 3, 4, 2))
    scores = dsa_sample_scores(qi3, wi3, cache_ki_t, layer, page_table, pg)
    selb, selb_new = dsa_sample_select(scores, qi3, wi3, ki.reshape(b, 1, IDX_DH), topk)
    o = dsa_sample_attend(q.reshape(b, ATT_HEADS, dh), selb, selb_new, k.reshape(b, ATT_KV_HEADS, dh),
                          v.reshape(b, ATT_KV_HEADS, dh), bias_tab, cache_k_t, cache_v_t, layer, page_table, pg)
    return o.reshape(b, d)


def kernel(x_prompt, x_sample, c_prompt, c_sample, state_gla, cache_k, cache_v, cache_idx_k, page_table,
           rel_bias, w_ada, b_ada, norm_mix, norm_ffn, norm_final,
           gla_w_in, gla_w_g2, gla_b_g2, gla_gnorm, gla_w_out,
           dsa_w_in, dsa_idx_ln_g, dsa_idx_ln_b, dsa_w_out,
           moe_w_router, moe_router_bias, moe_w_gate, moe_w_up, moe_w_down,
           shared_w_gate, shared_w_up, shared_w_down):
    bp, t, d = x_prompt.shape
    bs = x_sample.shape[0]
    depth = w_ada.shape[0]
    dh = d // ATT_HEADS
    tm_p = min(256, t)
    tq = 128
    mods = ada_mod_all(jnp.concatenate([c_prompt, c_sample], axis=0), w_ada, b_ada)
    xp = x_prompt.reshape(bp * t, d)
    xs = x_sample.reshape(bs, d)
    bias_tab = bias_tables(rel_bias, tq)
    gla_p, gla_s, kp_l, vp_l, kip_l, ks_l, vs_l, kis_l = [], [], [], [], [], [], [], []
    for i in range(depth):
        mod_p = mods[i, :bp].reshape(bp, 1, 6 * d)
        mod_s = mods[i, bp:]
        j = i // 2
        if i % 2 == 0:
            qkvr_p, la_p = gla_project(xp, mod_p, norm_mix[i], gla_w_in[j], gla_w_g2[j], gla_b_g2[j], tm_p, t)
            o_p, s_p = gla_chunked(qkvr_p, la_p, gla_gnorm[j], bp, t)
            qkvr_s, la_s = gla_project(xs, mod_s, norm_mix[i], gla_w_in[j], gla_w_g2[j], gla_b_g2[j], bs, 1)
            o_s, s_s = gla_step(qkvr_s, la_s, state_gla[j], gla_gnorm[j])
            gla_p.append(s_p)
            gla_s.append(s_s)
            w_out = gla_w_out[j]
        else:
            dp = (dsa_w_in[j], dsa_idx_ln_g[j], dsa_idx_ln_b[j])
            q_p, qi_p, wit_p, kt_p, vt32_p, kit_p, kib_p, kg_p, vt_p = dsa_project(
                xp, mod_p, norm_mix[i], *dp, tm_p, t, True)
            o_p = dsa_prompt(q_p, qi_p, wit_p, kib_p, kg_p, vt_p, bias_tab, bp, t, tq)
            q_s, qi_s, wit_s, k_s, v_s, ki_s = dsa_project(xs, mod_s, norm_mix[i], *dp, bs, 1, False)
            o_s = dsa_sample(q_s, k_s, v_s, qi_s, ki_s, wit_s, cache_k, cache_v, cache_idx_k, j,
                             page_table, bias_tab)
            kp_l.append(kt_p.reshape(bp, ATT_KV_HEADS, dh, t).transpose(0, 3, 1, 2))
            vp_l.append(vt32_p.reshape(bp, ATT_KV_HEADS, dh, t).transpose(0, 3, 1, 2))
            kip_l.append(kit_p.transpose(0, 2, 1))
            ks_l.append(k_s.reshape(bs, 1, ATT_KV_HEADS, dh))
            vs_l.append(v_s.reshape(bs, 1, ATT_KV_HEADS, dh))
            kis_l.append(ki_s.reshape(bs, 1, IDX_DH))
            w_out = dsa_w_out[j]
        xp = out_project(o_p, w_out, xp, mod_p, tm_p, t)
        xs = out_project(o_s, w_out, xs, mod_s, bs, 1)
        last = i == depth - 1
        mo = (moe_w_router[i], moe_router_bias[i], moe_w_gate, moe_w_up, moe_w_down, i,
              shared_w_gate[i], shared_w_up[i], shared_w_down[i], norm_final, last)
        xp = moe_ffn(xp, mod_p, norm_ffn[i], *mo, tm_p, t, 512)
        xs = moe_ffn(xs, mod_s, norm_ffn[i], *mo, bs, 1, 32)
    return (xp.reshape(bp, t, d), xs.reshape(bs, 1, d), jnp.stack(gla_p), jnp.stack(gla_s),
            jnp.stack(kp_l), jnp.stack(vp_l), jnp.stack(kip_l),
            jnp.stack(ks_l), jnp.stack(vs_l), jnp.stack(kis_l))
```

```python
import functools
import math

import numpy as np
import jax
import jax.numpy as jnp
from jax import lax
from jax.experimental import pallas as pl
from jax.experimental.pallas import tpu as pltpu

F32 = jnp.float32
BF16 = jnp.bfloat16
I32 = jnp.int32

GLA_HEADS = 4
GLA_RANK = 16
GLA_NORMALIZER = 16.0
ATT_HEADS = 16
ATT_KV_HEADS = 4
IDX_HEADS = 8
IDX_DH = 64
TOPK_MAX = 256
N_BUCKETS = 32
MAX_DISTANCE = 128
N_GROUPS = 8
TOPK_GROUPS = 4
TOP_K = 8
ROUTE_SCALE = 2.5
EPS = 1e-6

LANES = 128
VMEM_LIMIT = 56 * 1024 * 1024
NEG_INF = float("-inf")
INT_MIN = -2 ** 31


def _cparams(*sem):
    return pltpu.CompilerParams(dimension_semantics=sem, vmem_limit_bytes=VMEM_LIMIT)


def _bdot(a, b):
    return jnp.dot(a.astype(BF16), b.astype(BF16), preferred_element_type=F32)


def _bdot_nt(a, b):
    return lax.dot_general(a.astype(BF16), b.astype(BF16), (((1,), (1,)), ((), ())),
                           preferred_element_type=F32)


def _bdot_tn(a, b):
    return lax.dot_general(a.astype(BF16), b.astype(BF16), (((0,), (0,)), ((), ())),
                           preferred_element_type=F32)


def _split3(a):
    hi = a.astype(BF16)
    r1 = a - hi.astype(F32)
    mid = r1.astype(BF16)
    lo = (r1 - mid.astype(F32)).astype(BF16)
    return hi, mid, lo


def _silu(x):
    return x * (1.0 / (1.0 + jnp.exp(-x)))


def _sigmoid(x):
    return 1.0 / (1.0 + jnp.exp(-x))


def _rms(x, g):
    return x * lax.rsqrt(jnp.mean(x * x, axis=-1, keepdims=True) + EPS) * g


def _ada_kernel(c_ref, w_ref, b_ref, o_ref):
    c_hi, c_mid, _ = _split3(_silu(c_ref[...]))
    w_hi, w_mid, _ = _split3(w_ref[...])
    dot = lambda a, b: jnp.dot(a, b, preferred_element_type=F32)
    o_ref[...] = (dot(c_hi, w_mid) + dot(c_mid, w_hi)) + dot(c_hi, w_hi) + b_ref[...]


def ada_mod_all(c, w_ada, b_ada, tn=512):
    depth, d, n6 = w_ada.shape
    rows = c.shape[0]
    return pl.pallas_call(
        _ada_kernel,
        grid=(depth, n6 // tn),
        in_specs=[pl.BlockSpec((rows, d), lambda l, j: (0, 0)),
                  pl.BlockSpec((None, d, tn), lambda l, j: (l, 0, j)),
                  pl.BlockSpec((None, 1, tn), lambda l, j: (l, 0, j))],
        out_specs=pl.BlockSpec((None, rows, tn), lambda l, j: (l, 0, j)),
        out_shape=jax.ShapeDtypeStruct((depth, rows, n6), F32),
        compiler_params=_cparams("parallel", "parallel"),
        name="ada_mod",
    )(c, w_ada, b_ada.reshape(depth, 1, n6))


def _mod_spec(mod, which, tm, seq_len, d):
    if mod.ndim == 3:
        per_seq = seq_len // tm
        return pl.BlockSpec((None, 1, d), lambda i: (i // per_seq, 0, which))
    return pl.BlockSpec((tm, d), lambda i: (i, which))


def _gla_proj_kernel(x_ref, g_ref, sh_ref, sc_ref, w_ref, wg2_ref, bg2_ref, qkvr_ref, la_ref, *, n_main):
    h = _rms(x_ref[...], g_ref[...]) * (1.0 + sc_ref[...]) + sh_ref[...]
    y = _bdot(h, w_ref[...])
    qkvr_ref[...] = y[:, :n_main]
    g1 = y[:, n_main:n_main + GLA_RANK]
    z = _bdot(g1, wg2_ref[...]) + bg2_ref[...]
    la_ref[...] = (jnp.minimum(z, 0.0) - jnp.log(1.0 + jnp.exp(-jnp.abs(z)))) * (1.0 / GLA_NORMALIZER)


def gla_project(x, mod, norm_g, w_in, w_g2, b_g2, tm, seq_len):
    n, d = x.shape
    hk = w_g2.shape[1]
    n_main = w_in.shape[1] - GLA_RANK
    n_pad = (-w_in.shape[1]) % LANES
    w = jnp.pad(w_in, ((0, 0), (0, n_pad))).astype(BF16)
    return pl.pallas_call(
        functools.partial(_gla_proj_kernel, n_main=n_main),
        grid=(n // tm,),
        in_specs=[pl.BlockSpec((tm, d), lambda i: (i, 0)),
                  pl.BlockSpec((1, d), lambda i: (0, 0)),
                  _mod_spec(mod, 0, tm, seq_len, d),
                  _mod_spec(mod, 1, tm, seq_len, d),
                  pl.BlockSpec(w.shape, lambda i: (0, 0)),
                  pl.BlockSpec(w_g2.shape, lambda i: (0, 0)),
                  pl.BlockSpec((1, hk), lambda i: (0, 0))],
        out_specs=[pl.BlockSpec((tm, n_main), lambda i: (i, 0)),
                   pl.BlockSpec((tm, hk), lambda i: (i, 0))],
        out_shape=[jax.ShapeDtypeStruct((n, n_main), F32),
                   jax.ShapeDtypeStruct((n, hk), F32)],
        compiler_params=_cparams("parallel"),
        name="gla_proj",
    )(x, norm_g.reshape(1, d), mod, mod, w, w_g2.astype(BF16), b_g2.reshape(1, hk))


def _gla_level_matrices(c):
    levels = int(math.log2(c))
    t = np.arange(c)[:, None]
    u = np.arange(c)[None, :]
    mats, masks = [], []
    for l in range(levels):
        m = 1 << l
        ref = (t // (2 * m)) * (2 * m) + m - 1
        right = (t % (2 * m)) >= m
        mat = np.where(right, (u > ref) & (u <= t), (u > t) & (u <= ref))
        mats.append(mat)
        masks.append((t // (2 * m) == u // (2 * m)) & right & ((u % (2 * m)) < m))
    mats.append(u <= t)
    masks.append(t == u)
    return (np.stack(mats).astype(np.float32).reshape(-1, c), np.stack(masks).astype(np.float32))


def _gla_chunk_kernel(q_ref, k_ref, v_ref, r_ref, la_ref, mat_ref, mask_ref, gn_ref,
                      o_ref, s_out_ref, st_ref, *, c, dk):
    ci = pl.program_id(1)
    nlev = mask_ref.shape[0] - 1
    hk = la_ref.shape[1]
    dv = v_ref.shape[1] // GLA_HEADS

    @pl.when(ci == 0)
    def _():
        st_ref[...] = jnp.zeros_like(st_ref)

    la = la_ref[...]
    hi = la.astype(BF16)
    lo = (la - hi.astype(F32)).astype(BF16)
    e_all = jnp.dot(mat_ref[...], jnp.concatenate([hi, lo], axis=1), preferred_element_type=F32)
    e_all = e_all[:, :hk] + e_all[:, hk:]

    for h in range(GLA_HEADS):
        ks, vs = slice(h * dk, (h + 1) * dk), slice(h * dv, (h + 1) * dv)
        q = q_ref[:, ks] * (dk ** -0.5)
        k = k_ref[:, ks]
        v = v_ref[:, vs]
        att = jnp.where(mask_ref[nlev] > 0.0, _bdot_nt(q, k), 0.0)
        for l in range(nlev):
            ex = jnp.exp(e_all[l * c:(l + 1) * c, ks])
            att = att + jnp.where(mask_ref[l] > 0.0, _bdot_nt(q * ex, k * ex), 0.0)

        b = e_all[nlev * c:(nlev + 1) * c, ks]
        b_end = b[c - 1:c, :]
        st = st_ref[h]
        o = _bdot(att, v) + _bdot_nt(q * jnp.exp(b), st)
        st_new = st * jnp.exp(b_end) + _bdot_tn(v, k * jnp.exp(b_end - b))
        st_ref[h] = st_new

        on = o * lax.rsqrt(jnp.mean(o * o, axis=-1, keepdims=True) + EPS) * gn_ref[...]
        o_ref[:, vs] = (on * _silu(r_ref[:, vs])).astype(o_ref.dtype)

        @pl.when(ci == pl.num_programs(1) - 1)
        def _():
            s_out_ref[h] = st_new.T


def gla_chunked(qkvr, log_a, gnorm, batch, seq_len, c=128):
    n = qkvr.shape[0]
    hk = log_a.shape[1]
    dk = hk // GLA_HEADS
    hv = (qkvr.shape[1] - 2 * hk) // 2
    dv = hv // GLA_HEADS
    nc = seq_len // c
    mats, masks = _gla_level_matrices(c)
    row = lambda b, ci: b * nc + ci
    return pl.pallas_call(
        functools.partial(_gla_chunk_kernel, c=c, dk=dk),
        grid=(batch, nc),
        in_specs=[pl.BlockSpec((c, hk), lambda b, ci: (row(b, ci), 0)),
                  pl.BlockSpec((c, hk), lambda b, ci: (row(b, ci), 1)),
                  pl.BlockSpec((c, hv), lambda b, ci: (row(b, ci), 2 * hk // hv)),
                  pl.BlockSpec((c, hv), lambda b, ci: (row(b, ci), 2 * hk // hv + 1)),
                  pl.BlockSpec((c, hk), lambda b, ci: (row(b, ci), 0)),
                  pl.BlockSpec(mats.shape, lambda b, ci: (0, 0)),
                  pl.BlockSpec(masks.shape, lambda b, ci: (0, 0, 0)),
                  pl.BlockSpec((1, dv), lambda b, ci: (0, 0))],
        out_specs=[pl.BlockSpec((c, hv), lambda b, ci: (row(b, ci), 0)),
                   pl.BlockSpec((None, GLA_HEADS, dk, dv), lambda b, ci: (b, 0, 0, 0))],
        out_shape=[jax.ShapeDtypeStruct((n, hv), BF16),
                   jax.ShapeDtypeStruct((batch, GLA_HEADS, dk, dv), F32)],
        scratch_shapes=[pltpu.VMEM((GLA_HEADS, dv, dk), F32)],
        compiler_params=_cparams("parallel", "arbitrary"),
        name="gla_chunked",
    )(qkvr, qkvr, qkvr, qkvr, log_a, jnp.asarray(mats, BF16), jnp.asarray(masks), gnorm.reshape(1, dv))


def _gla_step_kernel(q_ref, k_ref, g_ref, v_ref, r_ref, s_ref, gn_ref, o_ref, s_out_ref, *, dk):
    s_new = jnp.exp(g_ref[...]) * s_ref[...] + k_ref[...] * v_ref[...]
    s_out_ref[...] = s_new
    o = jnp.sum(q_ref[...] * (dk ** -0.5) * s_new, axis=0, keepdims=True)
    on = o * lax.rsqrt(jnp.mean(o * o, axis=-1, keepdims=True) + EPS) * gn_ref[...]
    o_ref[...] = (on * _silu(r_ref[...])).astype(o_ref.dtype)


def gla_step(qkvr, log_a, state, gnorm):
    batch = qkvr.shape[0]
    hk = log_a.shape[1]
    dk = hk // GLA_HEADS
    hv = (qkvr.shape[1] - 2 * hk) // 2
    dv = hv // GLA_HEADS
    col = lambda a: a.reshape(batch, GLA_HEADS, dk, 1)
    rowv = lambda a: a.reshape(batch, GLA_HEADS, 1, dv)
    q, k, v, r = (qkvr[:, :hk], qkvr[:, hk:2 * hk], qkvr[:, 2 * hk:2 * hk + hv], qkvr[:, 2 * hk + hv:])
    cspec = pl.BlockSpec((None, None, dk, 1), lambda b, h: (b, h, 0, 0))
    rspec = pl.BlockSpec((None, None, 1, dv), lambda b, h: (b, h, 0, 0))
    sspec = pl.BlockSpec((None, None, dk, dv), lambda b, h: (b, h, 0, 0))
    o, s_new = pl.pallas_call(
        functools.partial(_gla_step_kernel, dk=dk),
        grid=(batch, GLA_HEADS),
        in_specs=[cspec, cspec, cspec, rspec, rspec, sspec, pl.BlockSpec((1, dv), lambda b, h: (0, 0))],
        out_specs=[rspec, sspec],
        out_shape=[jax.ShapeDtypeStruct((batch, GLA_HEADS, 1, dv), BF16),
                   jax.ShapeDtypeStruct(state.shape, F32)],
        compiler_params=_cparams("parallel", "parallel"),
        name="gla_step",
    )(col(q), col(k), col(log_a), rowv(v), rowv(r), state, gnorm.reshape(1, dv))
    return o.reshape(batch, hv), s_new


def _out_proj_kernel(o_ref, w_ref, x_ref, gt_ref, xo_ref):
    xo_ref[...] = x_ref[...] + gt_ref[...] * _bdot(o_ref[...], w_ref[...])


def out_project(o, w_out, x, mod, tm, seq_len):
    n, d = x.shape
    kdim = o.shape[1]
    return pl.pallas_call(
        _out_proj_kernel,
        grid=(n // tm,),
        in_specs=[pl.BlockSpec((tm, kdim), lambda i: (i, 0)),
                  pl.BlockSpec((kdim, d), lambda i: (0, 0)),
                  pl.BlockSpec((tm, d), lambda i: (i, 0)),
                  _mod_spec(mod, 2, tm, seq_len, d)],
        out_specs=pl.BlockSpec((tm, d), lambda i: (i, 0)),
        out_shape=jax.ShapeDtypeStruct((n, d), F32),
        compiler_params=_cparams("parallel"),
        name="out_proj",
    )(o, w_out.astype(BF16), x, mod)


HI_MASK = -65536


def _pack_rows(val):
    half = val.shape[1] // 2
    bits = lambda a: pltpu.bitcast(a.astype(BF16).astype(F32), I32)
    return lax.shift_right_logical(bits(val[:, :half]), 16) | (bits(val[:, half:]) & HI_MASK)


def _unpack_rows(words):
    return jnp.concatenate([pltpu.bitcast(words << 16, F32), pltpu.bitcast(words & HI_MASK, F32)], axis=1)


def _rows_load(ref, per, first=0, rows=None):
    rows = ref.shape[0] // per - first if rows is None else rows
    return jnp.concatenate([ref[pl.ds(first * per + j, rows, stride=per), :] for j in range(per)], axis=1)


def _rows_store(ref, words):
    rows = words.shape[0]
    per = words.shape[1] // LANES
    for j in range(per):
        ref[pl.ds(j, rows, stride=per), :] = words[:, j * LANES:(j + 1) * LANES]


def _row_tile(ref, r, per):
    return ref.at[pl.ds(pl.multiple_of(r * per, per), per), :]


def _modulate_kernel(x_ref, g_ref, sh_ref, sc_ref, h_ref, hp_ref):
    h = _rms(x_ref[...], g_ref[...]) * (1.0 + sc_ref[...]) + sh_ref[...]
    h_ref[...] = h
    _rows_store(hp_ref, _pack_rows(h))


def modulate_rows(x, mod, norm_g, tm, seq_len):
    n, d = x.shape
    per = d // 2 // LANES
    return pl.pallas_call(
        _modulate_kernel,
        grid=(n // tm,),
        in_specs=[pl.BlockSpec((tm, d), lambda i: (i, 0)),
                  pl.BlockSpec((1, d), lambda i: (0, 0)),
                  _mod_spec(mod, 3, tm, seq_len, d),
                  _mod_spec(mod, 4, tm, seq_len, d)],
        out_specs=[pl.BlockSpec((tm, d), lambda i: (i, 0)),
                   pl.BlockSpec((tm * per, LANES), lambda i: (i, 0))],
        out_shape=[jax.ShapeDtypeStruct((n, d), F32), jax.ShapeDtypeStruct((n * per, LANES), I32)],
        compiler_params=_cparams("parallel"),
        name="modulate_ffn",
    )(x, norm_g.reshape(1, d), mod, mod)


def _dsa_proj_kernel(x_ref, g_ref, sh_ref, sc_ref, w_ref, wwi_ref, lg_ref, lb_ref,
                     q_ref, qi_ref, wit_ref, k_ref, v_ref, ki_ref, *seq_refs, cuts, dh):
    h = (_rms(x_ref[...], g_ref[...]) * (1.0 + sc_ref[...]) + sh_ref[...]).astype(BF16)
    y = jnp.dot(h, w_ref[...], preferred_element_type=F32)
    c0, c1, c2, c3, c4 = cuts
    q_ref[...] = (y[:, :c0] * (dh ** -0.5)).astype(BF16)
    qi_ref[...] = (y[:, c2:c3] * (IDX_DH ** -0.5)).astype(BF16)
    wit_ref[...] = _bdot_nt(wwi_ref[...], h) * (IDX_HEADS ** -0.5)
    k = y[:, c0:c1]
    v = y[:, c1:c2]
    ki = y[:, c3:c4]
    mu = jnp.mean(ki, axis=-1, keepdims=True)
    var = jnp.mean((ki - mu) * (ki - mu), axis=-1, keepdims=True)
    kin = (ki - mu) * lax.rsqrt(var + EPS) * lg_ref[...] + lb_ref[...]
    if not seq_refs:
        k_ref[...] = k
        v_ref[...] = v
        ki_ref[...] = kin
    else:
        kib_ref, kg_ref, vt_ref = seq_refs
        vt = v.T
        k_ref[...] = k.T
        v_ref[...] = vt
        ki_ref[...] = jnp.concatenate([kin, jnp.zeros_like(kin)], axis=1).T[:kin.shape[1]]
        kib_ref[...] = kin.astype(BF16)
        for g in range(ATT_KV_HEADS):
            kg_ref[g] = k[:, g * dh:(g + 1) * dh].astype(BF16)
        vt_ref[...] = vt.astype(BF16)


def dsa_project(x, mod, norm_g, w_in, ln_g, ln_b, tm, seq_len, seq_layouts):
    n, d = x.shape
    dh = d // ATT_HEADS
    nkv = ATT_KV_HEADS * dh
    nqi = IDX_HEADS * IDX_DH
    cuts = (d, d + nkv, d + 2 * nkv, d + 2 * nkv + nqi, d + 2 * nkv + nqi + IDX_DH)
    w = jnp.pad(w_in[:, :cuts[4]], ((0, 0), (0, (-cuts[4]) % LANES))).astype(BF16)
    w_wi_t = w_in[:, cuts[4]:cuts[4] + IDX_HEADS].T.astype(BF16)
    row = lambda wd: pl.BlockSpec((tm, wd), lambda i: (i, 0))
    out_specs = [row(d), row(nqi), pl.BlockSpec((IDX_HEADS, tm), lambda i: (0, i))]
    out_shape = [jax.ShapeDtypeStruct((n, d), BF16), jax.ShapeDtypeStruct((n, nqi), BF16),
                 jax.ShapeDtypeStruct((IDX_HEADS, n), F32)]
    if not seq_layouts:
        out_specs += [row(nkv), row(nkv), row(IDX_DH)]
        out_shape += [jax.ShapeDtypeStruct((n, wd), F32) for wd in (nkv, nkv, IDX_DH)]
    else:
        per_seq = seq_len // tm
        tmin = lambda wd: pl.BlockSpec((None, wd, tm), lambda i: (i // per_seq, 0, i % per_seq))
        out_specs += [tmin(nkv), tmin(nkv), tmin(IDX_DH),
                      row(IDX_DH),
                      pl.BlockSpec((ATT_KV_HEADS, tm, dh), lambda i: (0, i, 0)),
                      pl.BlockSpec((None, nkv, tm), lambda i: (i, 0, 0))]
        out_shape += [jax.ShapeDtypeStruct((n // seq_len, wd, seq_len), F32) for wd in (nkv, nkv, IDX_DH)]
        out_shape += [jax.ShapeDtypeStruct((n, IDX_DH), BF16),
                      jax.ShapeDtypeStruct((ATT_KV_HEADS, n, dh), BF16),
                      jax.ShapeDtypeStruct((n // tm, nkv, tm), BF16)]
    return pl.pallas_call(
        functools.partial(_dsa_proj_kernel, cuts=cuts, dh=dh),
        grid=(n // tm,),
        in_specs=[pl.BlockSpec((tm, d), lambda i: (i, 0)),
                  pl.BlockSpec((1, d), lambda i: (0, 0)),
                  _mod_spec(mod, 0, tm, seq_len, d),
                  _mod_spec(mod, 1, tm, seq_len, d),
                  pl.BlockSpec(w.shape, lambda i: (0, 0)),
                  pl.BlockSpec(w_wi_t.shape, lambda i: (0, 0)),
                  pl.BlockSpec((1, IDX_DH), lambda i: (0, 0)),
                  pl.BlockSpec((1, IDX_DH), lambda i: (0, 0))],
        out_specs=out_specs,
        out_shape=out_shape,
        compiler_params=_cparams("parallel"),
        name="dsa_proj",
    )(x, norm_g.reshape(1, d), mod, mod, w, w_wi_t, ln_g.reshape(1, IDX_DH), ln_b.reshape(1, IDX_DH))


def _t5_bucket_np(dist):
    dist = np.maximum(dist, 0)
    max_exact = N_BUCKETS // 2
    ratio = np.log(np.maximum(dist, max_exact).astype(np.float32) / max_exact) / math.log(MAX_DISTANCE / max_exact)
    large = np.minimum(max_exact + (ratio * (N_BUCKETS - max_exact)).astype(np.int32), N_BUCKETS - 1)
    return np.where(dist < max_exact, dist, large).astype(np.int32)


def _bias_table_kernel(rb_ref, bk_ref, o_ref):
    h = pl.program_id(0)
    for kind in range(3):
        bk = bk_ref[kind]
        acc = jnp.zeros(bk.shape, F32)
        for b in range(N_BUCKETS):
            acc = jnp.where(bk == b, rb_ref[b, h], acc)
        o_ref[kind] = acc


def bias_tables(rel_bias, tq):
    s = np.arange(tq)[:, None]
    t = np.arange(tq)[None, :]
    buckets = np.stack([_t5_bucket_np(t - s), _t5_bucket_np(tq + t - s), _t5_bucket_np(2 * tq + t - s)])
    assert tq >= MAX_DISTANCE and (buckets[2] == N_BUCKETS - 1).all()
    return pl.pallas_call(
        _bias_table_kernel,
        grid=(ATT_HEADS,),
        in_specs=[pl.BlockSpec(memory_space=pltpu.SMEM),
                  pl.BlockSpec(buckets.shape, lambda h: (0, 0, 0))],
        out_specs=pl.BlockSpec((None, 3, tq, tq), lambda h: (h, 0, 0, 0)),
        out_shape=jax.ShapeDtypeStruct((ATT_HEADS, 3, tq, tq), F32),
        compiler_params=_cparams("parallel"),
        name="t5_bias_tiles",
    )(rel_bias, jnp.asarray(buckets))


def _order_key(score):
    score = jnp.where(score == 0.0, 0.0, score)
    bits = pltpu.bitcast(score, I32)
    return bits ^ ((bits >> 31) & 0x7FFFFFFF)


def _dsa_prompt_kernel(q_ref, qi_ref, wit_ref, kib_ref, kg_ref, vt_ref, bias_ref, tril_ref, o_ref,
                       key_ref, selb_ref, qis_ref, qg_ref, sc0_ref, sc1_ref, sc2_ref, sc3_ref, ot_ref,
                       *, tq, dh, topk):
    i = pl.program_id(1)
    nk = i + 1
    g_heads = ATT_HEADS // ATT_KV_HEADS
    s_id = lax.broadcasted_iota(I32, (tq, tq), 0)
    t_id = lax.broadcasted_iota(I32, (tq, tq), 1)
    fold8 = lambda a, op: op(a.reshape(a.shape[0] // 8, 8, a.shape[1]), axis=0)

    for h in range(IDX_HEADS):
        qis_ref[h * tq:(h + 1) * tq, :] = qi_ref[:, h * IDX_DH:(h + 1) * IDX_DH]
    for hh in range(ATT_HEADS):
        g, a = divmod(hh, g_heads)
        qg_ref[g, a * tq:(a + 1) * tq, :] = q_ref[:, hh * dh:(hh + 1) * dh]
    w8 = wit_ref[...]

    def score_body(j, carry):
        kij = kib_ref[pl.ds(pl.multiple_of(j * tq, tq), tq), :]
        lg = _bdot_nt(kij, qis_ref[...])
        acc = jnp.zeros((tq, tq), F32)
        for h in range(IDX_HEADS):
            acc = acc + jnp.maximum(lg[:, h * tq:(h + 1) * tq], 0.0) * w8[h:h + 1, :]
        valid = (s_id + j * tq) <= (t_id + i * tq)
        key_ref[j] = _order_key(jnp.where(valid, acc, NEG_INF))
        return carry

    lax.fori_loop(0, nk, score_body, 0)
    n_pairs = (nk + 1) // 2

    @pl.when(nk % 2 == 1)
    def _():
        pad = jnp.minimum(nk, key_ref.shape[0] - 1)
        key_ref[pad] = jnp.full((tq, tq), INT_MIN, I32)
        selb_ref[pad] = jnp.full((tq, tq), NEG_INF, F32)

    def counts(preds):
        def body(jj, accs):
            ka, kb = key_ref[2 * jj], key_ref[2 * jj + 1]
            return tuple(acc8 + fold8(jnp.where(p(ka), 1.0, 0.0) + jnp.where(p(kb), 1.0, 0.0), jnp.sum)
                         for p, acc8 in zip(preds, accs))
        accs = lax.fori_loop(0, n_pairs, body, (jnp.zeros((8, tq), F32),) * len(preds))
        return [jnp.sum(a, axis=0, keepdims=True) for a in accs]

    def bits_body(it, ans):
        shift = 30 - 2 * it
        cands = [ans | (jnp.int32(v) << shift) for v in (3, 2, 1)]
        c3, c2, c1 = counts([lambda kj, c=c: kj >= (c ^ INT_MIN) for c in cands])
        return jnp.where(c3 >= topk, cands[0], jnp.where(c2 >= topk, cands[1], jnp.where(c1 >= topk, cands[2], ans)))

    thr = lax.fori_loop(0, 16, bits_body, jnp.zeros((1, tq), I32)) ^ INT_MIN
    n_gt, n_tie = counts([lambda kj: kj > thr, lambda kj: kj == thr])
    need = topk - n_gt

    def causal(j):
        return (s_id + j * tq) <= (t_id + i * tq)

    some_ties_dropped = jnp.max(n_tie - need) > 0.0

    @pl.when(jnp.logical_not(some_ties_dropped))
    def _():
        def sel_body(j, carry):
            selb_ref[j] = jnp.where(jnp.logical_and(key_ref[j] >= thr, causal(j)), 0.0, NEG_INF)
            return carry
        lax.fori_loop(0, nk, sel_body, 0)

    @pl.when(some_ties_dropped)
    def _():
        def sel_body(j, run):
            kj = key_ref[j]
            tie = kj == thr
            tie_f = jnp.where(tie, 1.0, 0.0)
            before = run + jnp.dot(tril_ref[...], tie_f.astype(BF16), preferred_element_type=F32)
            take = jnp.where(kj > thr, 0.0, jnp.where(tie, jnp.where(before < need, 0.0, NEG_INF), NEG_INF))
            selb_ref[j] = jnp.where(causal(j), take, NEG_INF)
            return run + jnp.sum(tie_f, axis=0, keepdims=True)
        lax.fori_loop(0, nk, sel_body, jnp.zeros((1, tq), F32))

    ones_rows = jnp.ones((16, 2 * tq), BF16)
    sc_refs = (sc0_ref, sc1_ref, sc2_ref, sc3_ref)
    for g0 in range(0, ATT_KV_HEADS, len(sc_refs)):
        groups = tuple((g0 + n, scr) for n, scr in enumerate(sc_refs))

        def logits_body(jj, mx):
            base = pl.multiple_of(jj * 2 * tq, 2 * tq)
            j0 = 2 * jj
            sels = [selb_ref[j0], selb_ref[j0 + 1]]
            kinds = [jnp.clip(i - j0 - c, 0, 2) for c in range(2)]
            out = []
            for (g, scr), mx8 in zip(groups, mx):
                s = _bdot_nt(kg_ref[g, pl.ds(base, 2 * tq), :], qg_ref[g])
                add = jnp.concatenate(
                    [jnp.concatenate([bias_ref[g * g_heads + a, kinds[c]] + sels[c] for a in range(g_heads)], axis=1)
                     for c in range(2)], axis=0)
                s = s + add
                scr[jj] = s
                out.append(jnp.maximum(mx8, fold8(s, jnp.max)))
            return tuple(out)

        neg = jnp.full((8, g_heads * tq), NEG_INF, F32)
        mx = lax.fori_loop(0, n_pairs, logits_body, (neg,) * len(groups))
        ms = []
        for mx8 in mx:
            m = jnp.max(mx8, axis=0, keepdims=True)
            ms.append(jnp.where(m == NEG_INF, 0.0, m))

        def pv_body(jj, accs):
            out = []
            for (g, scr), m, acc in zip(groups, ms, accs):
                p = jnp.exp(scr[jj] - m).astype(BF16)
                lhs = jnp.concatenate([vt_ref[jj, g * dh:(g + 1) * dh, :], ones_rows], axis=0)
                out.append(acc + jnp.dot(lhs, p, preferred_element_type=F32))
            return tuple(out)

        zero = jnp.zeros((dh + 16, g_heads * tq), F32)
        accs = lax.fori_loop(0, n_pairs, pv_body, (zero,) * len(groups))
        for (g, _), acc in zip(groups, accs):
            og = acc[:dh] / acc[dh:dh + 1]
            for a in range(g_heads):
                hh = g * g_heads + a
                ot_ref[hh * dh:(hh + 1) * dh, :] = og[:, a * tq:(a + 1) * tq]
    o_ref[...] = ot_ref[...].T.astype(o_ref.dtype)


def dsa_prompt(q, qi, wit, kib, kg, vt, bias_tab, batch, seq_len, tq=128):
    n, d = q.shape
    dh = d // ATT_HEADS
    nq = seq_len // tq
    g_heads = ATT_HEADS // ATT_KV_HEADS
    assert nq % 2 == 0 and vt.shape[2] == 2 * tq
    topk = min(TOPK_MAX, seq_len // 4)
    tril = np.tril(np.ones((tq, tq), np.float32), -1)
    blk_spec = lambda width: pl.BlockSpec((tq, width), lambda b, i: (b * nq + i, 0))
    return pl.pallas_call(
        functools.partial(_dsa_prompt_kernel, tq=tq, dh=dh, topk=float(topk)),
        grid=(batch, nq),
        in_specs=[blk_spec(d), blk_spec(qi.shape[1]),
                  pl.BlockSpec((IDX_HEADS, tq), lambda b, i: (0, b * nq + i)),
                  pl.BlockSpec((seq_len, IDX_DH), lambda b, i: (b, 0)),
                  pl.BlockSpec((ATT_KV_HEADS, seq_len, dh), lambda b, i: (0, b, 0)),
                  pl.BlockSpec((nq // 2,) + vt.shape[1:], lambda b, i: (b, 0, 0)),
                  pl.BlockSpec(bias_tab.shape, lambda b, i: (0, 0, 0, 0)),
                  pl.BlockSpec((tq, tq), lambda b, i: (0, 0))],
        out_specs=blk_spec(d),
        out_shape=jax.ShapeDtypeStruct((n, d), BF16),
        scratch_shapes=[pltpu.VMEM((nq, tq, tq), I32),
                        pltpu.VMEM((nq, tq, tq), F32),
                        pltpu.VMEM((IDX_HEADS * tq, IDX_DH), BF16),
                        pltpu.VMEM((ATT_KV_HEADS, g_heads * tq, dh), BF16),
                        *([pltpu.VMEM((nq // 2, 2 * tq, g_heads * tq), F32)] * ATT_KV_HEADS),
                        pltpu.VMEM((d, tq), F32)],
        compiler_params=_cparams("parallel", "arbitrary"),
        name="dsa_prompt",
    )(q, qi, wit, kib, kg, vt, bias_tab, jnp.asarray(tril, BF16))


def _router_kernel(h_ref, wr_ref, rb_ref, tri_ref, eidx_ref, wsel_ref, pos_ref, cnt_ref, carry_ref, *, n_exp):
    @pl.when(pl.program_id(0) == 0)
    def _():
        carry_ref[...] = jnp.zeros_like(carry_ref)

    h_hi, h_mid, h_lo = _split3(h_ref[...])
    w_hi, w_mid, w_lo = _split3(wr_ref[...])
    nt = lambda a, b: lax.dot_general(a, b, (((1,), (1,)), ((), ())), preferred_element_type=F32)
    logits = ((nt(w_lo, h_hi) + nt(w_hi, h_lo) + nt(w_mid, h_mid))
              + (nt(w_hi, h_mid) + nt(w_mid, h_hi))) + nt(w_hi, h_hi)
    s = _sigmoid(logits)
    sel = s + rb_ref[...]
    tm = sel.shape[1]
    gsz = n_exp // N_GROUPS
    io_g = lax.broadcasted_iota(I32, (gsz, tm), 0)

    gs = []
    for g in range(N_GROUPS):
        grp = sel[g * gsz:(g + 1) * gsz, :]
        m1 = jnp.max(grp, axis=0, keepdims=True)
        i1 = jnp.min(jnp.where(grp == m1, io_g, gsz), axis=0, keepdims=True)
        m2 = jnp.max(jnp.where(io_g == i1, NEG_INF, grp), axis=0, keepdims=True)
        gs.append(m1 + m2)
    masked = []
    for g in range(N_GROUPS):
        rank = jnp.zeros((1, tm), F32)
        for o in range(N_GROUPS):
            if o == g:
                continue
            ahead = (gs[o] >= gs[g]) if o < g else (gs[o] > gs[g])
            rank = rank + jnp.where(ahead, 1.0, 0.0)
        keep = jnp.where(rank < TOPK_GROUPS, 0.0, NEG_INF)
        masked.append(sel[g * gsz:(g + 1) * gsz, :] + keep)
    msel = jnp.concatenate(masked, axis=0)

    io_e = lax.broadcasted_iota(I32, (n_exp, tm), 0)
    chosen = jnp.zeros((n_exp, tm), F32)
    picks, weights = [], []
    for _ in range(TOP_K):
        m = jnp.max(msel, axis=0, keepdims=True)
        ei = jnp.min(jnp.where(msel == m, io_e, n_exp), axis=0, keepdims=True)
        pick = io_e == ei
        weights.append(jnp.sum(jnp.where(pick, s, 0.0), axis=0, keepdims=True))
        picks.append(ei)
        chosen = jnp.where(pick, 1.0, chosen)
        msel = jnp.where(pick, NEG_INF, msel)
    wsum = weights[0]
    for w in weights[1:]:
        wsum = wsum + w

    rank_in_expert = carry_ref[...] + jnp.dot(chosen.astype(BF16), tri_ref[...], preferred_element_type=F32)
    carry_new = carry_ref[...] + jnp.sum(chosen, axis=1, keepdims=True)
    carry_ref[...] = carry_new
    cnt_ref[...] = carry_new
    for kk in range(TOP_K):
        eidx_ref[kk:kk + 1, :] = picks[kk]
        wsel_ref[kk:kk + 1, :] = weights[kk] / wsum * ROUTE_SCALE
        pk = jnp.sum(jnp.where(io_e == picks[kk], rank_in_expert, 0.0), axis=0, keepdims=True)
        pos_ref[kk:kk + 1, :] = pk.astype(I32)


def moe_route(h, w_router, router_bias, tm):
    n, d = h.shape
    n_exp = w_router.shape[1]
    tri = np.triu(np.ones((tm, tm), np.float32), 1)
    row8 = lambda dt: jax.ShapeDtypeStruct((TOP_K, n), dt)
    return pl.pallas_call(
        functools.partial(_router_kernel, n_exp=n_exp),
        grid=(n // tm,),
        in_specs=[pl.BlockSpec((tm, d), lambda i: (i, 0)),
                  pl.BlockSpec((n_exp, d), lambda i: (0, 0)),
                  pl.BlockSpec((n_exp, 1), lambda i: (0, 0)),
                  pl.BlockSpec((tm, tm), lambda i: (0, 0))],
        out_specs=[pl.BlockSpec((TOP_K, tm), lambda i: (0, i)),
                   pl.BlockSpec((TOP_K, tm), lambda i: (0, i)),
                   pl.BlockSpec((TOP_K, tm), lambda i: (0, i)),
                   pl.BlockSpec((n_exp, 1), lambda i: (0, 0))],
        out_shape=[row8(I32), row8(F32), row8(I32), jax.ShapeDtypeStruct((n_exp, 1), F32)],
        scratch_shapes=[pltpu.VMEM((n_exp, 1), F32)],
        compiler_params=_cparams("arbitrary"),
        name="moe_router",
    )(h, w_router.T, router_bias.reshape(n_exp, 1), jnp.asarray(tri, BF16))


def _dispatch_kernel(dest_ref, h_ref, xs_ref, sem, *, per):
    tm = h_ref.shape[0] // per

    def row_copy(t, dst_row):
        return pltpu.make_async_copy(_row_tile(h_ref, t, per), _row_tile(xs_ref, dst_row, per), sem)

    def issue(t, c):
        for kk in range(TOP_K):
            row_copy(t, dest_ref[kk, t]).start()
        return c

    def drain(t, c):
        for kk in range(TOP_K):
            row_copy(t, dest_ref[kk, t]).wait()
        return c

    lax.fori_loop(0, tm, issue, 0)
    lax.fori_loop(0, tm, drain, 0)


def moe_dispatch(h, dest, tm):
    n = dest.shape[1]
    per = h.shape[0] // n
    return pl.pallas_call(
        functools.partial(_dispatch_kernel, per=per),
        grid=(n // tm,),
        in_specs=[pl.BlockSpec((TOP_K, tm), lambda i: (0, i), memory_space=pltpu.SMEM),
                  pl.BlockSpec((tm * per, LANES), lambda i: (i, 0))],
        out_specs=pl.BlockSpec(memory_space=pl.ANY),
        out_shape=jax.ShapeDtypeStruct((n * TOP_K * per, LANES), h.dtype),
        scratch_shapes=[pltpu.SemaphoreType.DMA(())],
        compiler_params=_cparams("arbitrary"),
        name="moe_dispatch",
    )(dest, h)


def _expert_kernel(blk_ref, exp_ref, lo_ref, hi_ref, x_ref, wg_ref, wu_ref, wd_ref, y_ref,
                   wgb_ref, wub_ref, wdb_ref):
    j = pl.program_id(0)
    per = wg_ref.shape[0] // 2 // LANES
    blk = x_ref.shape[0] // per
    lo = lo_ref[j] - blk_ref[j] * blk
    hi = hi_ref[j] - blk_ref[j] * blk

    @pl.when(jnp.logical_or(j == 0, exp_ref[j] != exp_ref[jnp.maximum(j - 1, 0)]))
    def _():
        wgb_ref[...] = wg_ref[...].astype(BF16)
        wub_ref[...] = wu_ref[...].astype(BF16)
        wdb_ref[...] = wd_ref[...].astype(BF16)

    @pl.when(hi > lo)
    def _():
        x = _unpack_rows(_rows_load(x_ref, per)).astype(BF16)
        g = jnp.dot(x, wgb_ref[...], preferred_element_type=F32)
        u = jnp.dot(x, wub_ref[...], preferred_element_type=F32)
        y = jnp.dot((_silu(g) * u).astype(BF16), wdb_ref[...], preferred_element_type=F32)

        @pl.when(jnp.logical_and(lo == 0, hi == blk))
        def _():
            _rows_store(y_ref, _pack_rows(y))

        @pl.when(jnp.logical_not(jnp.logical_and(lo == 0, hi == blk)))
        def _():
            row = lax.broadcasted_iota(I32, y.shape, 0)
            ym = jnp.where(jnp.logical_and(row >= lo, row < hi), y, 0.0)

            @pl.when(lo == 0)
            def _():
                _rows_store(y_ref, _pack_rows(ym))

            @pl.when(lo != 0)
            def _():
                _rows_store(y_ref, _pack_rows(_unpack_rows(_rows_load(y_ref, per)) + ym))


def moe_experts(xs, seg_blk, seg_exp, seg_lo, seg_hi, w_gate, w_up, w_down, layer, blk):
    d, ff = w_gate.shape[2:]
    wspec = lambda shape: pl.BlockSpec((None, None) + shape, lambda j, sb, se, lo, hi: (layer, se[j], 0, 0))
    rows = pl.BlockSpec((blk * (d // 2 // LANES), LANES), lambda j, sb, se, lo, hi: (sb[j], 0))
    return pl.pallas_call(
        _expert_kernel,
        grid_spec=pltpu.PrefetchScalarGridSpec(
            num_scalar_prefetch=4, grid=(seg_blk.shape[0],),
            in_specs=[rows, wspec((d, ff)), wspec((d, ff)), wspec((ff, d))],
            out_specs=rows,
            scratch_shapes=[pltpu.VMEM((d, ff), BF16), pltpu.VMEM((d, ff), BF16), pltpu.VMEM((ff, d), BF16)]),
        out_shape=jax.ShapeDtypeStruct(xs.shape, xs.dtype),
        compiler_params=_cparams("arbitrary"),
        name="moe_experts",
    )(seg_blk, seg_exp, seg_lo, seg_hi, xs, w_gate, w_up, w_down)


def _combine_kernel(dest_ref, dnext_ref, ys_ref, wsel_ref, h_ref, x_ref, gt_ref, sg_ref, su_ref, sd_ref, nf_ref,
                    o_ref, buf_ref, sems, *, final_norm):
    i = pl.program_id(0)
    tm, d = x_ref.shape
    per = d // 2 // LANES

    def row_copy(idx_ref, slot, t, kk):
        return pltpu.make_async_copy(_row_tile(ys_ref, idx_ref[kk, t], per),
                                     _row_tile(buf_ref.at[slot], kk * tm + t, per), sems.at[slot])

    def start_rows(idx_ref, slot, t0, t1):
        def issue(t, c):
            for kk in range(TOP_K):
                row_copy(idx_ref, slot, t, kk).start()
            return c
        lax.fori_loop(t0, t1, issue, 0)

    slot = i % 2
    has_next = i + 1 < pl.num_programs(0)
    quarter = tm // 4

    def start_next(part):
        @pl.when(has_next)
        def _():
            start_rows(dnext_ref, 1 - slot, part * quarter, tm if part == 3 else (part + 1) * quarter)

    @pl.when(i == 0)
    def _():
        start_rows(dest_ref, 0, 0, tm)

    start_next(0)
    h = _unpack_rows(_rows_load(h_ref, per)).astype(BF16)
    g = jnp.dot(h, sg_ref[...], preferred_element_type=F32)
    u = jnp.dot(h, su_ref[...], preferred_element_type=F32)
    start_next(1)
    y = _bdot(_silu(g) * u, sd_ref[...])

    def drain(t, c):
        for kk in range(TOP_K):
            row_copy(dest_ref, slot, t, kk).wait()
        return c

    lax.fori_loop(0, tm, drain, 0)
    w = wsel_ref[...]
    rows = buf_ref.at[slot]
    for kk in range(TOP_K):
        if kk in (0, TOP_K // 2):
            start_next(2 + kk // (TOP_K // 2))
        y = y + w[:, kk:kk + 1] * _unpack_rows(_rows_load(rows, per, first=kk * tm, rows=tm))
    x_new = x_ref[...] + gt_ref[...] * y
    o_ref[...] = _rms(x_new, nf_ref[...]) if final_norm else x_new


def moe_combine(ys, dest, wsel_t, h, x, mod, ws_gate, ws_up, ws_down, norm_final, final_norm, tm, seq_len):
    n, d = x.shape
    ff = ws_gate.shape[1]
    last = n // tm - 1
    return pl.pallas_call(
        functools.partial(_combine_kernel, final_norm=final_norm),
        grid=(n // tm,),
        in_specs=[pl.BlockSpec((TOP_K, tm), lambda i: (0, i), memory_space=pltpu.SMEM),
                  pl.BlockSpec((TOP_K, tm), lambda i: (0, jnp.minimum(i + 1, last)), memory_space=pltpu.SMEM),
                  pl.BlockSpec(memory_space=pl.ANY),
                  pl.BlockSpec((tm, TOP_K), lambda i: (i, 0)),
                  pl.BlockSpec((tm * (d // 2 // LANES), LANES), lambda i: (i, 0)),
                  pl.BlockSpec((tm, d), lambda i: (i, 0)),
                  _mod_spec(mod, 5, tm, seq_len, d),
                  pl.BlockSpec((d, ff), lambda i: (0, 0)),
                  pl.BlockSpec((d, ff), lambda i: (0, 0)),
                  pl.BlockSpec((ff, d), lambda i: (0, 0)),
                  pl.BlockSpec((1, d), lambda i: (0, 0))],
        out_specs=pl.BlockSpec((tm, d), lambda i: (i, 0)),
        out_shape=jax.ShapeDtypeStruct((n, d), F32),
        scratch_shapes=[pltpu.VMEM((2, TOP_K * tm * (d // 2 // LANES), LANES), I32),
                        pltpu.SemaphoreType.DMA((2,))],
        compiler_params=_cparams("arbitrary"),
        name="moe_combine",
    )(dest, dest, ys, wsel_t, h, x, mod, ws_gate.astype(BF16), ws_up.astype(BF16), ws_down.astype(BF16),
      norm_final.reshape(1, d))


def _slot_kernel(start_ref, eidx_ref, pos_ref, dest_ref):
    e = eidx_ref[...]
    base = jnp.zeros(e.shape, I32)
    for ex in range(start_ref.shape[0]):
        base = jnp.where(e == ex, start_ref[ex], base)
    dest_ref[...] = base + pos_ref[...]


def moe_slots(start, eidx, pos, tm):
    n = eidx.shape[1]
    spec = pl.BlockSpec((TOP_K, tm), lambda i: (0, i))
    return pl.pallas_call(
        _slot_kernel,
        grid=(n // tm,),
        in_specs=[pl.BlockSpec(memory_space=pltpu.SMEM), spec, spec],
        out_specs=spec,
        out_shape=jax.ShapeDtypeStruct(eidx.shape, I32),
        compiler_params=_cparams("parallel"),
        name="moe_slots",
    )(start, eidx, pos)


def moe_ffn(x, mod, norm_g, w_router, router_bias, w_gate, w_up, w_down, layer, ws_gate, ws_up, ws_down,
            norm_final, final_norm, tm, seq_len, blk):
    n, d = x.shape
    n_exp = w_router.shape[1]
    h32, h = modulate_rows(x, mod, norm_g, tm, seq_len)
    eidx, wsel, pos, counts = moe_route(h32, w_router, router_bias, tm)
    n_rows = n * TOP_K
    n_blocks = n_rows // blk
    counts = counts.reshape(n_exp).astype(I32)
    end = jnp.cumsum(counts)
    start = end - counts
    dest = moe_slots(start, eidx, pos, min(tm, n))
    seg_lo = jnp.sort(jnp.concatenate([jnp.arange(n_blocks, dtype=I32) * blk, start[1:]]))
    seg_hi = jnp.concatenate([seg_lo[1:], jnp.full((1,), n_rows, I32)])
    seg_blk = jnp.minimum(seg_lo // blk, n_blocks - 1)
    seg_exp = jnp.minimum(jnp.sum((end[None, :] <= seg_lo[:, None]).astype(I32), axis=1), n_exp - 1)
    xs = moe_dispatch(h, dest, tm)
    ys = moe_experts(xs, seg_blk, seg_exp, seg_lo, seg_hi, w_gate, w_up, w_down, layer, blk)
    return moe_combine(ys, dest, wsel.T, h, x, mod, ws_gate, ws_up, ws_down, norm_final, final_norm,
                       min(tm, 128), seq_len)


def _sample_score_kernel(pt_ref, qi_ref, w_ref, *refs):
    page_refs, o_ref = refs[:-1], refs[-1]
    kp = jnp.concatenate([r[...] for r in page_refs], axis=1)
    lg = _bdot(qi_ref[...], kp)
    sc = jnp.sum(jnp.maximum(lg, 0.0) * w_ref[...], axis=0, keepdims=True)
    page = page_refs[0].shape[1]
    for p in range(len(page_refs)):
        o_ref[p:p + 1, :] = sc[:, p * page:(p + 1) * page]


def _page_specs(layer, pg, block):
    zeros = (0,) * (len(block) - 2)
    return [pl.BlockSpec(block, lambda b, g, pt, p=p: (layer, pt[b, g * pg + p]) + zeros) for p in range(pg)]


def dsa_sample_scores(qi3, wi3, cache_ki_t, layer, page_table, pg):
    b, n_pages = page_table.shape
    page = cache_ki_t.shape[3]
    return pl.pallas_call(
        _sample_score_kernel,
        grid_spec=pltpu.PrefetchScalarGridSpec(
            num_scalar_prefetch=1, grid=(b, n_pages // pg),
            in_specs=[pl.BlockSpec((None, IDX_HEADS, IDX_DH), lambda b, g, pt: (b, 0, 0)),
                      pl.BlockSpec((None, IDX_HEADS, 1), lambda b, g, pt: (b, 0, 0))]
                     + _page_specs(layer, pg, (None, None, IDX_DH, page)),
            out_specs=pl.BlockSpec((None, pg, page), lambda b, g, pt: (b, g, 0))),
        out_shape=jax.ShapeDtypeStruct((b, n_pages, page), F32),
        compiler_params=_cparams("parallel", "arbitrary"),
        name="dsa_sample_scores",
    )(page_table, qi3, wi3, *([cache_ki_t] * pg))


def _sample_select_kernel(sc_ref, qi_ref, w_ref, kin_ref, triu_ref, tril_ref, selb_ref, selbn_ref, *, topk):
    nb = sc_ref.shape[0]
    total = lambda a: jnp.sum(jnp.sum(a, axis=0, keepdims=True), axis=1, keepdims=True)
    keys, keys_new = [], []
    for b in range(nb):
        lg_new = jnp.sum(qi_ref[b].astype(F32) * kin_ref[b].astype(BF16).astype(F32),
                         axis=1, keepdims=True)
        s_new = jnp.sum(jnp.maximum(lg_new, 0.0) * w_ref[b], axis=0, keepdims=True)
        keys.append(_order_key(sc_ref[b]))
        keys_new.append(_order_key(s_new))

    def count(b, pred):
        return total(jnp.where(pred(keys[b]), 1.0, 0.0)) + jnp.where(pred(keys_new[b]), 1.0, 0.0)

    def bits_body(it, answers):
        shift = 30 - 2 * it
        out = []
        for b, ans in enumerate(answers):
            cands = [ans | (jnp.int32(v) << shift) for v in (3, 2, 1)]
            c3, c2, c1 = [count(b, lambda kj, c=c: kj >= (c ^ INT_MIN)) for c in cands]
            out.append(jnp.where(c3 >= topk, cands[0],
                                 jnp.where(c2 >= topk, cands[1], jnp.where(c1 >= topk, cands[2], ans))))
        return tuple(out)

    answers = lax.fori_loop(0, 16, bits_body, (jnp.zeros((1, 1), I32),) * nb)
    for b in range(nb):
        thr = answers[b] ^ INT_MIN
        need = topk - count(b, lambda kj: kj > thr)
        tie = keys[b] == thr
        tie_f = jnp.where(tie, 1.0, 0.0)
        tie_b = tie_f.astype(BF16)
        in_row = jnp.dot(tie_b, triu_ref[...], preferred_element_type=F32)
        rows_before = jnp.sum(jnp.dot(tril_ref[...], tie_b, preferred_element_type=F32), axis=1, keepdims=True)
        before = in_row + rows_before
        selb_ref[b] = jnp.where(keys[b] > thr, 0.0,
                                jnp.where(tie, jnp.where(before < need, 0.0, NEG_INF), NEG_INF))
        selbn_ref[b] = jnp.where(keys_new[b] > thr, 0.0,
                                 jnp.where(keys_new[b] == thr,
                                           jnp.where(total(tie_f) < need, 0.0, NEG_INF), NEG_INF))


def dsa_sample_select(scores, qi3, wi3, ki_new, topk):
    b, n_pages, page = scores.shape
    nb = 8 if b % 8 == 0 else 1
    triu = np.triu(np.ones((page, page), np.float32), 1)
    tril = np.tril(np.ones((n_pages, n_pages), np.float32), -1)
    per_b = lambda *tail: pl.BlockSpec((nb,) + tail, lambda i: (i,) + (0,) * len(tail))
    return pl.pallas_call(
        functools.partial(_sample_select_kernel, topk=float(topk)),
        grid=(b // nb,),
        in_specs=[per_b(n_pages, page), per_b(IDX_HEADS, IDX_DH), per_b(IDX_HEADS, 1), per_b(1, IDX_DH),
                  pl.BlockSpec((page, page), lambda i: (0, 0)),
                  pl.BlockSpec((n_pages, n_pages), lambda i: (0, 0))],
        out_specs=[per_b(n_pages, page), per_b(1, 1)],
        out_shape=[jax.ShapeDtypeStruct((b, n_pages, page), F32), jax.ShapeDtypeStruct((b, 1, 1), F32)],
        compiler_params=_cparams("parallel"),
        name="dsa_sample_select",
    )(scores, qi3, wi3, ki_new, jnp.asarray(triu, BF16), jnp.asarray(tril, BF16))


def _sample_attend_kernel(pt_ref, q_ref, selb_ref, selbn_ref, kn_ref, vn_ref, blast_ref, bfar_ref, bnew_ref,
                          *refs, pg, n_pages):
    k_refs, v_refs = refs[:pg], refs[pg:2 * pg]
    o_ref, m_ref, l_ref, acc_ref = refs[2 * pg:]
    step = pl.program_id(1)
    g_heads = ATT_HEADS // ATT_KV_HEADS
    page = k_refs[0].shape[2]
    dh = q_ref.shape[1]
    head_group = lax.broadcasted_iota(I32, (ATT_HEADS, 1), 0) // g_heads

    @pl.when(step == 0)
    def _():
        m_ref[...] = jnp.full(m_ref.shape, NEG_INF, F32)
        l_ref[...] = jnp.zeros(l_ref.shape, F32)
        acc_ref[...] = jnp.zeros(acc_ref.shape, F32)

    q = q_ref[...]
    s = jnp.zeros((ATT_HEADS, pg * page), F32)
    for g in range(ATT_KV_HEADS):
        kg = jnp.concatenate([r[g] for r in k_refs], axis=1)
        s = jnp.where(head_group == g, _bdot(q, kg), s)
    add = [jnp.where(step * pg + p == n_pages - 1, blast_ref[...], bfar_ref[...]) + selb_ref[p:p + 1, :]
           for p in range(pg)]
    s = s + jnp.concatenate(add, axis=1)
    m_old = m_ref[...]
    m_new = jnp.maximum(m_old, jnp.max(s, axis=1, keepdims=True))
    m_safe = jnp.where(m_new == NEG_INF, 0.0, m_new)
    alpha = jnp.exp(m_old - m_safe)
    p_ = jnp.exp(s - m_safe)
    pv = jnp.zeros((ATT_HEADS, dh), F32)
    for g in range(ATT_KV_HEADS):
        vg = jnp.concatenate([r[g] for r in v_refs], axis=1)
        pv = jnp.where(head_group == g, _bdot_nt(p_, vg), pv)
    l_new = alpha * l_ref[...] + jnp.sum(p_, axis=1, keepdims=True)
    acc_new = alpha * acc_ref[...] + pv
    m_ref[...] = m_new
    l_ref[...] = l_new
    acc_ref[...] = acc_new

    @pl.when(step == pl.num_programs(1) - 1)
    def _():
        kn = jnp.zeros((ATT_HEADS, dh), F32)
        vn = jnp.zeros((ATT_HEADS, dh), F32)
        for g in range(ATT_KV_HEADS):
            kn = jnp.where(head_group == g, kn_ref[g:g + 1, :], kn)
            vn = jnp.where(head_group == g, vn_ref[g:g + 1, :], vn)
        qk = q.astype(F32) * kn.astype(BF16).astype(F32)
        s_n = jnp.sum(qk, axis=1, keepdims=True) + bnew_ref[...] + selbn_ref[...]
        m_fin = jnp.maximum(m_new, s_n)
        m_fs = jnp.where(m_fin == NEG_INF, 0.0, m_fin)
        a2 = jnp.exp(m_new - m_fs)
        p_n = jnp.exp(s_n - m_fs)
        o_ref[...] = ((a2 * acc_new + p_n * vn) / (a2 * l_new + p_n)).astype(o_ref.dtype)


def dsa_sample_attend(q3, selb, selb_new, k_new, v_new, bias_tab, cache_k_t, cache_v_t, layer, page_table, pg):
    b, n_pages = page_table.shape
    kvh, dh, page = cache_k_t.shape[2:]
    assert page == bias_tab.shape[2] and page >= MAX_DISTANCE
    b_last = bias_tab[:, 1, :, 0]
    b_far = bias_tab[:, 2, 0, :1]
    b_new = bias_tab[:, 0, 0, :1]
    full = lambda shape: pl.BlockSpec(shape, lambda b, g, pt: (0,) * len(shape))
    per_b = lambda shape: pl.BlockSpec((None,) + shape, lambda b, g, pt: (b,) + (0,) * len(shape))
    return pl.pallas_call(
        functools.partial(_sample_attend_kernel, pg=pg, n_pages=n_pages),
        grid_spec=pltpu.PrefetchScalarGridSpec(
            num_scalar_prefetch=1, grid=(b, n_pages // pg),
            in_specs=[per_b((ATT_HEADS, dh)),
                      pl.BlockSpec((None, pg, page), lambda b, g, pt: (b, g, 0)),
                      per_b((1, 1)), per_b((kvh, dh)), per_b((kvh, dh)),
                      full((ATT_HEADS, page)), full((ATT_HEADS, 1)), full((ATT_HEADS, 1))]
                     + _page_specs(layer, pg, (None, None, kvh, dh, page))
                     + _page_specs(layer, pg, (None, None, kvh, dh, page)),
            out_specs=per_b((ATT_HEADS, dh)),
            scratch_shapes=[pltpu.VMEM((ATT_HEADS, 1), F32), pltpu.VMEM((ATT_HEADS, 1), F32),
                            pltpu.VMEM((ATT_HEADS, dh), F32)]),
        out_shape=jax.ShapeDtypeStruct((b, ATT_HEADS, dh), BF16),
        compiler_params=_cparams("parallel", "arbitrary"),
        name="dsa_sample_attend",
    )(page_table, q3, selb, selb_new, k_new, v_new, b_last, b_far, b_new,
      *([cache_k_t] * pg), *([cache_v_t] * pg))


def dsa_sample(q, k, v, qi, ki, wit, cache_k, cache_v, cache_ki, layer, page_table, bias_tab):
    b, d = q.shape
    dh = d // ATT_HEADS
    n_pages = page_table.shape[1]
    page = cache_k.shape[2]
    pg = 16 if n_pages % 16 == 0 else n_pages
    topk = min(TOPK_MAX, (n_pages * page + 1) // 4)
    qi3 = qi.reshape(b, IDX_HEADS, IDX_DH)
    wi3 = wit.T.reshape(b, IDX_HEADS, 1)
    cache_ki_t = jnp.transpose(cache_ki, (0, 1, 3, 2))
    cache_k_t = jnp.transpose(cache_k, (0, 1, 3, 4, 2))
    cache_v_t = jnp.transpose(cache_v, (0, 1, 3, 4, 2))
    scores = dsa_sample_scores(qi3, wi3, cache_ki_t, layer, page_table, pg)
    selb, selb_new = dsa_sample_select(scores, qi3, wi3, ki.reshape(b, 1, IDX_DH), topk)
    o = dsa_sample_attend(q.reshape(b, ATT_HEADS, dh), selb, selb_new, k.reshape(b, ATT_KV_HEADS, dh),
                          v.reshape(b, ATT_KV_HEADS, dh), bias_tab, cache_k_t, cache_v_t, layer, page_table, pg)
    return o.reshape(b, d)


def kernel(x_prompt, x_sample, c_prompt, c_sample, state_gla, cache_k, cache_v, cache_idx_k, page_table,
           rel_bias, w_ada, b_ada, norm_mix, norm_ffn, norm_final,
           gla_w_in, gla_w_g2, gla_b_g2, gla_gnorm, gla_w_out,
           dsa_w_in, dsa_idx_ln_g, dsa_idx_ln_b, dsa_w_out,
           moe_w_router, moe_router_bias, moe_w_gate, moe_w_up, moe_w_down,
           shared_w_gate, shared_w_up, shared_w_down):
    bp, t, d = x_prompt.shape
    bs = x_sample.shape[0]
    depth = w_ada.shape[0]
    dh = d // ATT_HEADS
    tm_p = min(256, t)
    tq = 128
    mods = ada_mod_all(jnp.concatenate([c_prompt, c_sample], axis=0), w_ada, b_ada)
    xp = x_prompt.reshape(bp * t, d)
    xs = x_sample.reshape(bs, d)
    bias_tab = bias_tables(rel_bias, tq)
    gla_p, gla_s, kp_l, vp_l, kip_l, ks_l, vs_l, kis_l = [], [], [], [], [], [], [], []
    for i in range(depth):
        mod_p = mods[i, :bp].reshape(bp, 1, 6 * d)
        mod_s = mods[i, bp:]
        j = i // 2
        if i % 2 == 0:
            qkvr_p, la_p = gla_project(xp, mod_p, norm_mix[i], gla_w_in[j], gla_w_g2[j], gla_b_g2[j], tm_p, t)
            o_p, s_p = gla_chunked(qkvr_p, la_p, gla_gnorm[j], bp, t)
            qkvr_s, la_s = gla_project(xs, mod_s, norm_mix[i], gla_w_in[j], gla_w_g2[j], gla_b_g2[j], bs, 1)
            o_s, s_s = gla_step(qkvr_s, la_s, state_gla[j], gla_gnorm[j])
            gla_p.append(s_p)
            gla_s.append(s_s)
            w_out = gla_w_out[j]
        else:
            dp = (dsa_w_in[j], dsa_idx_ln_g[j], dsa_idx_ln_b[j])
            q_p, qi_p, wit_p, kt_p, vt32_p, kit_p, kib_p, kg_p, vt_p = dsa_project(
                xp, mod_p, norm_mix[i], *dp, tm_p, t, True)
            o_p = dsa_prompt(q_p, qi_p, wit_p, kib_p, kg_p, vt_p, bias_tab, bp, t, tq)
            q_s, qi_s, wit_s, k_s, v_s, ki_s = dsa_project(xs, mod_s, norm_mix[i], *dp, bs, 1, False)
            o_s = dsa_sample(q_s, k_s, v_s, qi_s, ki_s, wit_s, cache_k, cache_v, cache_idx_k, j,
                             page_table, bias_tab)
            kp_l.append(kt_p.reshape(bp, ATT_KV_HEADS, dh, t).transpose(0, 3, 1, 2))
            vp_l.append(vt32_p.reshape(bp, ATT_KV_HEADS, dh, t).transpose(0, 3, 1, 2))
            kip_l.append(kit_p.transpose(0, 2, 1))
            ks_l.append(k_s.reshape(bs, 1, ATT_KV_HEADS, dh))
            vs_l.append(v_s.reshape(bs, 1, ATT_KV_HEADS, dh))
            kis_l.append(ki_s.reshape(bs, 1, IDX_DH))
            w_out = dsa_w_out[j]
        xp = out_project(o_p, w_out, xp, mod_p, tm_p, t)
        xs = out_project(o_s, w_out, xs, mod_s, bs, 1)
        last = i == depth - 1
        mo = (moe_w_router[i], moe_router_bias[i], moe_w_gate, moe_w_up, moe_w_down, i,
              shared_w_gate[i], shared_w_up[i], shared_w_down[i], norm_final, last)
        xp = moe_ffn(xp, mod_p, norm_ffn[i], *mo, tm_p, t, 512)
        xs = moe_ffn(xs, mod_s, norm_ffn[i], *mo, bs, 1, 32)
    return (xp.reshape(bp, t, d), xs.reshape(bs, 1, d), jnp.stack(gla_p), jnp.stack(gla_s),
            jnp.stack(kp_l), jnp.stack(vp_l), jnp.stack(kip_l),
            jnp.stack(ks_l), jnp.stack(vs_l), jnp.stack(kis_l))
```

```python
import functools
import math

import numpy as np
import jax
import jax.numpy as jnp
from jax import lax
from jax.experimental import pallas as pl
from jax.experimental.pallas import tpu as pltpu

F32 = jnp.float32
BF16 = jnp.bfloat16
I32 = jnp.int32

GLA_HEADS = 4
GLA_RANK = 16
GLA_NORMALIZER = 16.0
ATT_HEADS = 16
ATT_KV_HEADS = 4
IDX_HEADS = 8
IDX_DH = 64
TOPK_MAX = 256
N_BUCKETS = 32
MAX_DISTANCE = 128
N_GROUPS = 8
TOPK_GROUPS = 4
TOP_K = 8
ROUTE_SCALE = 2.5
EPS = 1e-6

LANES = 128
VMEM_LIMIT = 56 * 1024 * 1024
NEG_INF = float("-inf")
INT_MIN = -2 ** 31


def _cparams(*sem):
    return pltpu.CompilerParams(dimension_semantics=sem, vmem_limit_bytes=VMEM_LIMIT)


def _bdot(a, b):
    return jnp.dot(a.astype(BF16), b.astype(BF16), preferred_element_type=F32)


def _bdot_nt(a, b):
    return lax.dot_general(a.astype(BF16), b.astype(BF16), (((1,), (1,)), ((), ())),
                           preferred_element_type=F32)


def _bdot_tn(a, b):
    return lax.dot_general(a.astype(BF16), b.astype(BF16), (((0,), (0,)), ((), ())),
                           preferred_element_type=F32)


def _split3(a):
    hi = a.astype(BF16)
    r1 = a - hi.astype(F32)
    mid = r1.astype(BF16)
    lo = (r1 - mid.astype(F32)).astype(BF16)
    return hi, mid, lo


def _silu(x):
    return x * (1.0 / (1.0 + jnp.exp(-x)))


def _sigmoid(x):
    return 1.0 / (1.0 + jnp.exp(-x))


def _rms(x, g):
    return x * lax.rsqrt(jnp.mean(x * x, axis=-1, keepdims=True) + EPS) * g


def _ada_kernel(c_ref, w_ref, b_ref, o_ref):
    c_hi, c_mid, _ = _split3(_silu(c_ref[...]))
    w_hi, w_mid, _ = _split3(w_ref[...])
    dot = lambda a, b: jnp.dot(a, b, preferred_element_type=F32)
    o_ref[...] = (dot(c_hi, w_mid) + dot(c_mid, w_hi)) + dot(c_hi, w_hi) + b_ref[...]


def ada_mod_all(c, w_ada, b_ada, tn=512):
    depth, d, n6 = w_ada.shape
    rows = c.shape[0]
    return pl.pallas_call(
        _ada_kernel,
        grid=(depth, n6 // tn),
        in_specs=[pl.BlockSpec((rows, d), lambda l, j: (0, 0)),
                  pl.BlockSpec((None, d, tn), lambda l, j: (l, 0, j)),
                  pl.BlockSpec((None, 1, tn), lambda l, j: (l, 0, j))],
        out_specs=pl.BlockSpec((None, rows, tn), lambda l, j: (l, 0, j)),
        out_shape=jax.ShapeDtypeStruct((depth, rows, n6), F32),
        compiler_params=_cparams("parallel", "parallel"),
        name="ada_mod",
    )(c, w_ada, b_ada.reshape(depth, 1, n6))


def _mod_spec(mod, which, tm, seq_len, d):
    if mod.ndim == 3:
        per_seq = seq_len // tm
        return pl.BlockSpec((None, 1, d), lambda i: (i // per_seq, 0, which))
    return pl.BlockSpec((tm, d), lambda i: (i, which))


def _gla_proj_kernel(x_ref, g_ref, sh_ref, sc_ref, w_ref, wg2_ref, bg2_ref, qkvr_ref, la_ref, *, n_main):
    h = _rms(x_ref[...], g_ref[...]) * (1.0 + sc_ref[...]) + sh_ref[...]
    y = _bdot(h, w_ref[...])
    qkvr_ref[...] = y[:, :n_main]
    g1 = y[:, n_main:n_main + GLA_RANK]
    z = _bdot(g1, wg2_ref[...]) + bg2_ref[...]
    la_ref[...] = (jnp.minimum(z, 0.0) - jnp.log(1.0 + jnp.exp(-jnp.abs(z)))) * (1.0 / GLA_NORMALIZER)


def gla_project(x, mod, norm_g, w_in, w_g2, b_g2, tm, seq_len):
    n, d = x.shape
    hk = w_g2.shape[1]
    n_main = w_in.shape[1] - GLA_RANK
    n_pad = (-w_in.shape[1]) % LANES
    w = jnp.pad(w_in, ((0, 0), (0, n_pad))).astype(BF16)
    return pl.pallas_call(
        functools.partial(_gla_proj_kernel, n_main=n_main),
        grid=(n // tm,),
        in_specs=[pl.BlockSpec((tm, d), lambda i: (i, 0)),
                  pl.BlockSpec((1, d), lambda i: (0, 0)),
                  _mod_spec(mod, 0, tm, seq_len, d),
                  _mod_spec(mod, 1, tm, seq_len, d),
                  pl.BlockSpec(w.shape, lambda i: (0, 0)),
                  pl.BlockSpec(w_g2.shape, lambda i: (0, 0)),
                  pl.BlockSpec((1, hk), lambda i: (0, 0))],
        out_specs=[pl.BlockSpec((tm, n_main), lambda i: (i, 0)),
                   pl.BlockSpec((tm, hk), lambda i: (i, 0))],
        out_shape=[jax.ShapeDtypeStruct((n, n_main), F32),
                   jax.ShapeDtypeStruct((n, hk), F32)],
        compiler_params=_cparams("parallel"),
        name="gla_proj",
    )(x, norm_g.reshape(1, d), mod, mod, w, w_g2.astype(BF16), b_g2.reshape(1, hk))


def _gla_level_matrices(c):
    levels = int(math.log2(c))
    t = np.arange(c)[:, None]
    u = np.arange(c)[None, :]
    mats, masks = [], []
    for l in range(levels):
        m = 1 << l
        ref = (t // (2 * m)) * (2 * m) + m - 1
        right = (t % (2 * m)) >= m
        mat = np.where(right, (u > ref) & (u <= t), (u > t) & (u <= ref))
        mats.append(mat)
        masks.append((t // (2 * m) == u // (2 * m)) & right & ((u % (2 * m)) < m))
    mats.append(u <= t)
    masks.append(t == u)
    return (np.stack(mats).astype(np.float32).reshape(-1, c), np.stack(masks).astype(np.float32))


def _gla_chunk_kernel(q_ref, k_ref, v_ref, r_ref, la_ref, mat_ref, mask_ref, gn_ref,
                      o_ref, s_out_ref, st_ref, *, c, dk):
    ci = pl.program_id(1)
    nlev = mask_ref.shape[0] - 1
    hk = la_ref.shape[1]
    dv = v_ref.shape[1] // GLA_HEADS

    @pl.when(ci == 0)
    def _():
        st_ref[...] = jnp.zeros_like(st_ref)

    la = la_ref[...]
    hi = la.astype(BF16)
    lo = (la - hi.astype(F32)).astype(BF16)
    e_all = jnp.dot(mat_ref[...], jnp.concatenate([hi, lo], axis=1), preferred_element_type=F32)
    e_all = e_all[:, :hk] + e_all[:, hk:]

    for h in range(GLA_HEADS):
        ks, vs = slice(h * dk, (h + 1) * dk), slice(h * dv, (h + 1) * dv)
        q = q_ref[:, ks] * (dk ** -0.5)
        k = k_ref[:, ks]
        v = v_ref[:, vs]
        att = jnp.where(mask_ref[nlev] > 0.0, _bdot_nt(q, k), 0.0)
        for l in range(nlev):
            ex = jnp.exp(e_all[l * c:(l + 1) * c, ks])
            att = att + jnp.where(mask_ref[l] > 0.0, _bdot_nt(q * ex, k * ex), 0.0)

        b = e_all[nlev * c:(nlev + 1) * c, ks]
        b_end = b[c - 1:c, :]
        st = st_ref[h]
        o = _bdot(att, v) + _bdot_nt(q * jnp.exp(b), st)
        st_new = st * jnp.exp(b_end) + _bdot_tn(v, k * jnp.exp(b_end - b))
        st_ref[h] = st_new

        on = o * lax.rsqrt(jnp.mean(o * o, axis=-1, keepdims=True) + EPS) * gn_ref[...]
        o_ref[:, vs] = (on * _silu(r_ref[:, vs])).astype(o_ref.dtype)

        @pl.when(ci == pl.num_programs(1) - 1)
        def _():
            s_out_ref[h] = st_new.T


def gla_chunked(qkvr, log_a, gnorm, batch, seq_len, c=128):
    n = qkvr.shape[0]
    hk = log_a.shape[1]
    dk = hk // GLA_HEADS
    hv = (qkvr.shape[1] - 2 * hk) // 2
    dv = hv // GLA_HEADS
    nc = seq_len // c
    mats, masks = _gla_level_matrices(c)
    row = lambda b, ci: b * nc + ci
    return pl.pallas_call(
        functools.partial(_gla_chunk_kernel, c=c, dk=dk),
        grid=(batch, nc),
        in_specs=[pl.BlockSpec((c, hk), lambda b, ci: (row(b, ci), 0)),
                  pl.BlockSpec((c, hk), lambda b, ci: (row(b, ci), 1)),
                  pl.BlockSpec((c, hv), lambda b, ci: (row(b, ci), 2 * hk // hv)),
                  pl.BlockSpec((c, hv), lambda b, ci: (row(b, ci), 2 * hk // hv + 1)),
                  pl.BlockSpec((c, hk), lambda b, ci: (row(b, ci), 0)),
                  pl.BlockSpec(mats.shape, lambda b, ci: (0, 0)),
                  pl.BlockSpec(masks.shape, lambda b, ci: (0, 0, 0)),
                  pl.BlockSpec((1, dv), lambda b, ci: (0, 0))],
        out_specs=[pl.BlockSpec((c, hv), lambda b, ci: (row(b, ci), 0)),
                   pl.BlockSpec((None, GLA_HEADS, dk, dv), lambda b, ci: (b, 0, 0, 0))],
        out_shape=[jax.ShapeDtypeStruct((n, hv), BF16),
                   jax.ShapeDtypeStruct((batch, GLA_HEADS, dk, dv), F32)],
        scratch_shapes=[pltpu.VMEM((GLA_HEADS, dv, dk), F32)],
        compiler_params=_cparams("parallel", "arbitrary"),
        name="gla_chunked",
    )(qkvr, qkvr, qkvr, qkvr, log_a, jnp.asarray(mats, BF16), jnp.asarray(masks), gnorm.reshape(1, dv))


def _gla_step_kernel(q_ref, k_ref, g_ref, v_ref, r_ref, s_ref, gn_ref, o_ref, s_out_ref, *, dk):
    s_new = jnp.exp(g_ref[...]) * s_ref[...] + k_ref[...] * v_ref[...]
    s_out_ref[...] = s_new
    o = jnp.sum(q_ref[...] * (dk ** -0.5) * s_new, axis=0, keepdims=True)
    on = o * lax.rsqrt(jnp.mean(o * o, axis=-1, keepdims=True) + EPS) * gn_ref[...]
    o_ref[...] = (on * _silu(r_ref[...])).astype(o_ref.dtype)


def gla_step(qkvr, log_a, state, gnorm):
    batch = qkvr.shape[0]
    hk = log_a.shape[1]
    dk = hk // GLA_HEADS
    hv = (qkvr.shape[1] - 2 * hk) // 2
    dv = hv // GLA_HEADS
    col = lambda a: a.reshape(batch, GLA_HEADS, dk, 1)
    rowv = lambda a: a.reshape(batch, GLA_HEADS, 1, dv)
    q, k, v, r = (qkvr[:, :hk], qkvr[:, hk:2 * hk], qkvr[:, 2 * hk:2 * hk + hv], qkvr[:, 2 * hk + hv:])
    cspec = pl.BlockSpec((None, None, dk, 1), lambda b, h: (b, h, 0, 0))
    rspec = pl.BlockSpec((None, None, 1, dv), lambda b, h: (b, h, 0, 0))
    sspec = pl.BlockSpec((None, None, dk, dv), lambda b, h: (b, h, 0, 0))
    o, s_new = pl.pallas_call(
        functools.partial(_gla_step_kernel, dk=dk),
        grid=(batch, GLA_HEADS),
        in_specs=[cspec, cspec, cspec, rspec, rspec, sspec, pl.BlockSpec((1, dv), lambda b, h: (0, 0))],
        out_specs=[rspec, sspec],
        out_shape=[jax.ShapeDtypeStruct((batch, GLA_HEADS, 1, dv), BF16),
                   jax.ShapeDtypeStruct(state.shape, F32)],
        compiler_params=_cparams("parallel", "parallel"),
        name="gla_step",
    )(col(q), col(k), col(log_a), rowv(v), rowv(r), state, gnorm.reshape(1, dv))
    return o.reshape(batch, hv), s_new


HI_MASK = -65536


def _pack_rows(val):
    half = val.shape[1] // 2
    bits = lambda a: pltpu.bitcast(a.astype(BF16).astype(F32), I32)
    return lax.shift_right_logical(bits(val[:, :half]), 16) | (bits(val[:, half:]) & HI_MASK)


def _unpack_rows(words):
    return jnp.concatenate([pltpu.bitcast(words << 16, F32), pltpu.bitcast(words & HI_MASK, F32)], axis=1)


def _rows_load(ref, per, first=0, rows=None):
    rows = ref.shape[0] // per - first if rows is None else rows
    return jnp.concatenate([ref[pl.ds(first * per + j, rows, stride=per), :] for j in range(per)], axis=1)


def _rows_store(ref, words):
    rows = words.shape[0]
    per = words.shape[1] // LANES
    for j in range(per):
        ref[pl.ds(j, rows, stride=per), :] = words[:, j * LANES:(j + 1) * LANES]


def _row_tile(ref, r, per):
    return ref.at[pl.ds(pl.multiple_of(r * per, per), per), :]


def _dsa_proj_kernel(x_ref, g_ref, sh_ref, sc_ref, w_ref, wwi_ref, lg_ref, lb_ref,
                     q_ref, qi_ref, wit_ref, k_ref, v_ref, ki_ref, *seq_refs, cuts, dh):
    h = (_rms(x_ref[...], g_ref[...]) * (1.0 + sc_ref[...]) + sh_ref[...]).astype(BF16)
    y = jnp.dot(h, w_ref[...], preferred_element_type=F32)
    c0, c1, c2, c3, c4 = cuts
    q_ref[...] = (y[:, :c0] * (dh ** -0.5)).astype(BF16)
    qi_ref[...] = (y[:, c2:c3] * (IDX_DH ** -0.5)).astype(BF16)
    wit_ref[...] = _bdot_nt(wwi_ref[...], h) * (IDX_HEADS ** -0.5)
    k = y[:, c0:c1]
    v = y[:, c1:c2]
    ki = y[:, c3:c4]
    mu = jnp.mean(ki, axis=-1, keepdims=True)
    var = jnp.mean((ki - mu) * (ki - mu), axis=-1, keepdims=True)
    kin = (ki - mu) * lax.rsqrt(var + EPS) * lg_ref[...] + lb_ref[...]
    if not seq_refs:
        k_ref[...] = k
        v_ref[...] = v
        ki_ref[...] = kin
    else:
        kib_ref, kg_ref, vt_ref = seq_refs
        vt = v.T
        k_ref[...] = k.T
        v_ref[...] = vt
        ki_ref[...] = jnp.concatenate([kin, jnp.zeros_like(kin)], axis=1).T[:kin.shape[1]]
        kib_ref[...] = kin.astype(BF16)
        for g in range(ATT_KV_HEADS):
            kg_ref[g] = k[:, g * dh:(g + 1) * dh].astype(BF16)
        vt_ref[...] = vt.astype(BF16)


def dsa_project(x, mod, norm_g, w_in, ln_g, ln_b, tm, seq_len, seq_layouts):
    n, d = x.shape
    dh = d // ATT_HEADS
    nkv = ATT_KV_HEADS * dh
    nqi = IDX_HEADS * IDX_DH
    cuts = (d, d + nkv, d + 2 * nkv, d + 2 * nkv + nqi, d + 2 * nkv + nqi + IDX_DH)
    w = jnp.pad(w_in[:, :cuts[4]], ((0, 0), (0, (-cuts[4]) % LANES))).astype(BF16)
    w_wi_t = w_in[:, cuts[4]:cuts[4] + IDX_HEADS].T.astype(BF16)
    row = lambda wd: pl.BlockSpec((tm, wd), lambda i: (i, 0))
    out_specs = [row(d), row(nqi), pl.BlockSpec((IDX_HEADS, tm), lambda i: (0, i))]
    out_shape = [jax.ShapeDtypeStruct((n, d), BF16), jax.ShapeDtypeStruct((n, nqi), BF16),
                 jax.ShapeDtypeStruct((IDX_HEADS, n), F32)]
    if not seq_layouts:
        out_specs += [row(nkv), row(nkv), row(IDX_DH)]
        out_shape += [jax.ShapeDtypeStruct((n, wd), F32) for wd in (nkv, nkv, IDX_DH)]
    else:
        per_seq = seq_len // tm
        tmin = lambda wd: pl.BlockSpec((None, wd, tm), lambda i: (i // per_seq, 0, i % per_seq))
        out_specs += [tmin(nkv), tmin(nkv), tmin(IDX_DH),
                      row(IDX_DH),
                      pl.BlockSpec((ATT_KV_HEADS, tm, dh), lambda i: (0, i, 0)),
                      pl.BlockSpec((None, nkv, tm), lambda i: (i, 0, 0))]
        out_shape += [jax.ShapeDtypeStruct((n // seq_len, wd, seq_len), F32) for wd in (nkv, nkv, IDX_DH)]
        out_shape += [jax.ShapeDtypeStruct((n, IDX_DH), BF16),
                      jax.ShapeDtypeStruct((ATT_KV_HEADS, n, dh), BF16),
                      jax.ShapeDtypeStruct((n // tm, nkv, tm), BF16)]
    return pl.pallas_call(
        functools.partial(_dsa_proj_kernel, cuts=cuts, dh=dh),
        grid=(n // tm,),
        in_specs=[pl.BlockSpec((tm, d), lambda i: (i, 0)),
                  pl.BlockSpec((1, d), lambda i: (0, 0)),
                  _mod_spec(mod, 0, tm, seq_len, d),
                  _mod_spec(mod, 1, tm, seq_len, d),
                  pl.BlockSpec(w.shape, lambda i: (0, 0)),
                  pl.BlockSpec(w_wi_t.shape, lambda i: (0, 0)),
                  pl.BlockSpec((1, IDX_DH), lambda i: (0, 0)),
                  pl.BlockSpec((1, IDX_DH), lambda i: (0, 0))],
        out_specs=out_specs,
        out_shape=out_shape,
        compiler_params=_cparams("parallel"),
        name="dsa_proj",
    )(x, norm_g.reshape(1, d), mod, mod, w, w_wi_t, ln_g.reshape(1, IDX_DH), ln_b.reshape(1, IDX_DH))


def _t5_bucket_np(dist):
    dist = np.maximum(dist, 0)
    max_exact = N_BUCKETS // 2
    ratio = np.log(np.maximum(dist, max_exact).astype(np.float32) / max_exact) / math.log(MAX_DISTANCE / max_exact)
    large = np.minimum(max_exact + (ratio * (N_BUCKETS - max_exact)).astype(np.int32), N_BUCKETS - 1)
    return np.where(dist < max_exact, dist, large).astype(np.int32)


def _bias_table_kernel(rb_ref, bk_ref, o_ref):
    h = pl.program_id(0)
    for kind in range(3):
        bk = bk_ref[kind]
        acc = jnp.zeros(bk.shape, F32)
        for b in range(N_BUCKETS):
            acc = jnp.where(bk == b, rb_ref[b, h], acc)
        o_ref[kind] = acc


def bias_tables(rel_bias, tq):
    s = np.arange(tq)[:, None]
    t = np.arange(tq)[None, :]
    buckets = np.stack([_t5_bucket_np(t - s), _t5_bucket_np(tq + t - s), _t5_bucket_np(2 * tq + t - s)])
    assert tq >= MAX_DISTANCE and (buckets[2] == N_BUCKETS - 1).all()
    return pl.pallas_call(
        _bias_table_kernel,
        grid=(ATT_HEADS,),
        in_specs=[pl.BlockSpec(memory_space=pltpu.SMEM),
                  pl.BlockSpec(buckets.shape, lambda h: (0, 0, 0))],
        out_specs=pl.BlockSpec((None, 3, tq, tq), lambda h: (h, 0, 0, 0)),
        out_shape=jax.ShapeDtypeStruct((ATT_HEADS, 3, tq, tq), F32),
        compiler_params=_cparams("parallel"),
        name="t5_bias_tiles",
    )(rel_bias, jnp.asarray(buckets))


def _order_key(score):
    score = jnp.where(score == 0.0, 0.0, score)
    bits = pltpu.bitcast(score, I32)
    return bits ^ ((bits >> 31) & 0x7FFFFFFF)


def _dsa_prompt_kernel(q_ref, qi_ref, wit_ref, kib_ref, kg_ref, vt_ref, bias_ref, tril_ref, o_ref,
                       key_ref, selb_ref, qis_ref, qg_ref, sc0_ref, sc1_ref, sc2_ref, sc3_ref, ot_ref,
                       *, tq, dh, topk):
    i = pl.program_id(1)
    nk = i + 1
    g_heads = ATT_HEADS // ATT_KV_HEADS
    s_id = lax.broadcasted_iota(I32, (tq, tq), 0)
    t_id = lax.broadcasted_iota(I32, (tq, tq), 1)
    fold8 = lambda a, op: op(a.reshape(a.shape[0] // 8, 8, a.shape[1]), axis=0)

    for h in range(IDX_HEADS):
        qis_ref[h * tq:(h + 1) * tq, :] = qi_ref[:, h * IDX_DH:(h + 1) * IDX_DH]
    for hh in range(ATT_HEADS):
        g, a = divmod(hh, g_heads)
        qg_ref[g, a * tq:(a + 1) * tq, :] = q_ref[:, hh * dh:(hh + 1) * dh]
    w8 = wit_ref[...]

    def score_body(j, carry):
        kij = kib_ref[pl.ds(pl.multiple_of(j * tq, tq), tq), :]
        lg = _bdot_nt(kij, qis_ref[...])
        acc = jnp.zeros((tq, tq), F32)
        for h in range(IDX_HEADS):
            acc = acc + jnp.maximum(lg[:, h * tq:(h + 1) * tq], 0.0) * w8[h:h + 1, :]
        valid = (s_id + j * tq) <= (t_id + i * tq)
        key_ref[j] = _order_key(jnp.where(valid, acc, NEG_INF))
        return carry

    lax.fori_loop(0, nk, score_body, 0)
    n_pairs = (nk + 1) // 2

    @pl.when(nk % 2 == 1)
    def _():
        pad = jnp.minimum(nk, key_ref.shape[0] - 1)
        key_ref[pad] = jnp.full((tq, tq), INT_MIN, I32)
        selb_ref[pad] = jnp.full((tq, tq), NEG_INF, F32)

    def counts(preds):
        def body(jj, accs):
            ka, kb = key_ref[2 * jj], key_ref[2 * jj + 1]
            return tuple(acc8 + fold8(jnp.where(p(ka), 1.0, 0.0) + jnp.where(p(kb), 1.0, 0.0), jnp.sum)
                         for p, acc8 in zip(preds, accs))
        accs = lax.fori_loop(0, n_pairs, body, (jnp.zeros((8, tq), F32),) * len(preds))
        return [jnp.sum(a, axis=0, keepdims=True) for a in accs]

    def bits_body(it, ans):
        shift = 30 - 2 * it
        cands = [ans | (jnp.int32(v) << shift) for v in (3, 2, 1)]
        c3, c2, c1 = counts([lambda kj, c=c: kj >= (c ^ INT_MIN) for c in cands])
        return jnp.where(c3 >= topk, cands[0], jnp.where(c2 >= topk, cands[1], jnp.where(c1 >= topk, cands[2], ans)))

    thr = lax.fori_loop(0, 16, bits_body, jnp.zeros((1, tq), I32)) ^ INT_MIN
    n_gt, n_tie = counts([lambda kj: kj > thr, lambda kj: kj == thr])
    need = topk - n_gt

    def causal(j):
        return (s_id + j * tq) <= (t_id + i * tq)

    some_ties_dropped = jnp.max(n_tie - need) > 0.0

    @pl.when(jnp.logical_not(some_ties_dropped))
    def _():
        def sel_body(j, carry):
            selb_ref[j] = jnp.where(jnp.logical_and(key_ref[j] >= thr, causal(j)), 0.0, NEG_INF)
            return carry
        lax.fori_loop(0, nk, sel_body, 0)

    @pl.when(some_ties_dropped)
    def _():
        def sel_body(j, run):
            kj = key_ref[j]
            tie = kj == thr
            tie_f = jnp.where(tie, 1.0, 0.0)
            before = run + jnp.dot(tril_ref[...], tie_f.astype(BF16), preferred_element_type=F32)
            take = jnp.where(kj > thr, 0.0, jnp.where(tie, jnp.where(before < need, 0.0, NEG_INF), NEG_INF))
            selb_ref[j] = jnp.where(causal(j), take, NEG_INF)
            return run + jnp.sum(tie_f, axis=0, keepdims=True)
        lax.fori_loop(0, nk, sel_body, jnp.zeros((1, tq), F32))

    ones_rows = jnp.ones((16, 2 * tq), BF16)
    sc_refs = (sc0_ref, sc1_ref, sc2_ref, sc3_ref)
    for g0 in range(0, ATT_KV_HEADS, len(sc_refs)):
        groups = tuple((g0 + n, scr) for n, scr in enumerate(sc_refs))

        def logits_body(jj, mx):
            base = pl.multiple_of(jj * 2 * tq, 2 * tq)
            j0 = 2 * jj
            sels = [selb_ref[j0], selb_ref[j0 + 1]]
            kinds = [jnp.clip(i - j0 - c, 0, 2) for c in range(2)]
            out = []
            for (g, scr), mx8 in zip(groups, mx):
                s = _bdot_nt(kg_ref[g, pl.ds(base, 2 * tq), :], qg_ref[g])
                add = jnp.concatenate(
                    [jnp.concatenate([bias_ref[g * g_heads + a, kinds[c]] + sels[c] for a in range(g_heads)], axis=1)
                     for c in range(2)], axis=0)
                s = s + add
                scr[jj] = s
                out.append(jnp.maximum(mx8, fold8(s, jnp.max)))
            return tuple(out)

        neg = jnp.full((8, g_heads * tq), NEG_INF, F32)
        mx = lax.fori_loop(0, n_pairs, logits_body, (neg,) * len(groups))
        ms = []
        for mx8 in mx:
            m = jnp.max(mx8, axis=0, keepdims=True)
            ms.append(jnp.where(m == NEG_INF, 0.0, m))

        def pv_body(jj, accs):
            out = []
            for (g, scr), m, acc in zip(groups, ms, accs):
                p = jnp.exp(scr[jj] - m).astype(BF16)
                lhs = jnp.concatenate([vt_ref[jj, g * dh:(g + 1) * dh, :], ones_rows], axis=0)
                out.append(acc + jnp.dot(lhs, p, preferred_element_type=F32))
            return tuple(out)

        zero = jnp.zeros((dh + 16, g_heads * tq), F32)
        accs = lax.fori_loop(0, n_pairs, pv_body, (zero,) * len(groups))
        for (g, _), acc in zip(groups, accs):
            og = acc[:dh] / acc[dh:dh + 1]
            for a in range(g_heads):
                hh = g * g_heads + a
                ot_ref[hh * dh:(hh + 1) * dh, :] = og[:, a * tq:(a + 1) * tq]
    o_ref[...] = ot_ref[...].T.astype(o_ref.dtype)


def dsa_prompt(q, qi, wit, kib, kg, vt, bias_tab, batch, seq_len, tq=128):
    n, d = q.shape
    dh = d // ATT_HEADS
    nq = seq_len // tq
    g_heads = ATT_HEADS // ATT_KV_HEADS
    assert nq % 2 == 0 and vt.shape[2] == 2 * tq
    topk = min(TOPK_MAX, seq_len // 4)
    tril = np.tril(np.ones((tq, tq), np.float32), -1)
    blk_spec = lambda width: pl.BlockSpec((tq, width), lambda b, i: (b * nq + i, 0))
    return pl.pallas_call(
        functools.partial(_dsa_prompt_kernel, tq=tq, dh=dh, topk=float(topk)),
        grid=(batch, nq),
        in_specs=[blk_spec(d), blk_spec(qi.shape[1]),
                  pl.BlockSpec((IDX_HEADS, tq), lambda b, i: (0, b * nq + i)),
                  pl.BlockSpec((seq_len, IDX_DH), lambda b, i: (b, 0)),
                  pl.BlockSpec((ATT_KV_HEADS, seq_len, dh), lambda b, i: (0, b, 0)),
                  pl.BlockSpec((nq // 2,) + vt.shape[1:], lambda b, i: (b, 0, 0)),
                  pl.BlockSpec(bias_tab.shape, lambda b, i: (0, 0, 0, 0)),
                  pl.BlockSpec((tq, tq), lambda b, i: (0, 0))],
        out_specs=blk_spec(d),
        out_shape=jax.ShapeDtypeStruct((n, d), BF16),
        scratch_shapes=[pltpu.VMEM((nq, tq, tq), I32),
                        pltpu.VMEM((nq, tq, tq), F32),
                        pltpu.VMEM((IDX_HEADS * tq, IDX_DH), BF16),
                        pltpu.VMEM((ATT_KV_HEADS, g_heads * tq, dh), BF16),
                        *([pltpu.VMEM((nq // 2, 2 * tq, g_heads * tq), F32)] * ATT_KV_HEADS),
                        pltpu.VMEM((d, tq), F32)],
        compiler_params=_cparams("parallel", "arbitrary"),
        name="dsa_prompt",
    )(q, qi, wit, kib, kg, vt, bias_tab, jnp.asarray(tril, BF16))


def _mix_route_kernel(o_ref, wo_ref, x_ref, gt_ref, g_ref, sh_ref, sc_ref, wr_ref, rb_ref, tri_ref,
                      xo_ref, hp_ref, eidx_ref, wsel_ref, pos_ref, cnt_ref, carry_ref, *, n_exp):
    @pl.when(pl.program_id(0) == 0)
    def _():
        carry_ref[...] = jnp.zeros_like(carry_ref)

    x_new = x_ref[...] + gt_ref[...] * _bdot(o_ref[...], wo_ref[...])
    xo_ref[...] = x_new
    h = _rms(x_new, g_ref[...]) * (1.0 + sc_ref[...]) + sh_ref[...]
    _rows_store(hp_ref, _pack_rows(h))

    h_hi, h_mid, h_lo = _split3(h)
    w_hi, w_mid, w_lo = _split3(wr_ref[...])
    nt = lambda a, b: lax.dot_general(a, b, (((1,), (1,)), ((), ())), preferred_element_type=F32)
    logits = ((nt(w_lo, h_hi) + nt(w_hi, h_lo) + nt(w_mid, h_mid))
              + (nt(w_hi, h_mid) + nt(w_mid, h_hi))) + nt(w_hi, h_hi)
    s = _sigmoid(logits)
    sel = s + rb_ref[...]
    tm = sel.shape[1]
    gsz = n_exp // N_GROUPS
    io_g = lax.broadcasted_iota(I32, (gsz, tm), 0)

    gs = []
    for g in range(N_GROUPS):
        grp = sel[g * gsz:(g + 1) * gsz, :]
        m1 = jnp.max(grp, axis=0, keepdims=True)
        i1 = jnp.min(jnp.where(grp == m1, io_g, gsz), axis=0, keepdims=True)
        m2 = jnp.max(jnp.where(io_g == i1, NEG_INF, grp), axis=0, keepdims=True)
        gs.append(m1 + m2)
    masked = []
    for g in range(N_GROUPS):
        rank = jnp.zeros((1, tm), F32)
        for o in range(N_GROUPS):
            if o == g:
                continue
            ahead = (gs[o] >= gs[g]) if o < g else (gs[o] > gs[g])
            rank = rank + jnp.where(ahead, 1.0, 0.0)
        keep = jnp.where(rank < TOPK_GROUPS, 0.0, NEG_INF)
        masked.append(sel[g * gsz:(g + 1) * gsz, :] + keep)
    msel = jnp.concatenate(masked, axis=0)

    io_e = lax.broadcasted_iota(I32, (n_exp, tm), 0)
    chosen = jnp.zeros((n_exp, tm), F32)
    picks, weights = [], []
    for _ in range(TOP_K):
        m = jnp.max(msel, axis=0, keepdims=True)
        ei = jnp.min(jnp.where(msel == m, io_e, n_exp), axis=0, keepdims=True)
        pick = io_e == ei
        weights.append(jnp.sum(jnp.where(pick, s, 0.0), axis=0, keepdims=True))
        picks.append(ei)
        chosen = jnp.where(pick, 1.0, chosen)
        msel = jnp.where(pick, NEG_INF, msel)
    wsum = weights[0]
    for w in weights[1:]:
        wsum = wsum + w

    rank_in_expert = carry_ref[...] + jnp.dot(chosen.astype(BF16), tri_ref[...], preferred_element_type=F32)
    carry_new = carry_ref[...] + jnp.sum(chosen, axis=1, keepdims=True)
    carry_ref[...] = carry_new
    cnt_ref[...] = carry_new
    for kk in range(TOP_K):
        eidx_ref[kk:kk + 1, :] = picks[kk]
        wsel_ref[kk:kk + 1, :] = weights[kk] / wsum * ROUTE_SCALE
        pk = jnp.sum(jnp.where(io_e == picks[kk], rank_in_expert, 0.0), axis=0, keepdims=True)
        pos_ref[kk:kk + 1, :] = pk.astype(I32)


def mix_out_and_route(o, w_out, x, mod, norm_g, w_router, router_bias, tm, seq_len):
    n, d = x.shape
    kdim = o.shape[1]
    n_exp = w_router.shape[1]
    per = d // 2 // LANES
    tri = np.triu(np.ones((tm, tm), np.float32), 1)
    row8 = lambda dt: jax.ShapeDtypeStruct((TOP_K, n), dt)
    tok8 = pl.BlockSpec((TOP_K, tm), lambda i: (0, i))
    return pl.pallas_call(
        functools.partial(_mix_route_kernel, n_exp=n_exp),
        grid=(n // tm,),
        in_specs=[pl.BlockSpec((tm, kdim), lambda i: (i, 0)),
                  pl.BlockSpec((kdim, d), lambda i: (0, 0)),
                  pl.BlockSpec((tm, d), lambda i: (i, 0)),
                  _mod_spec(mod, 2, tm, seq_len, d),
                  pl.BlockSpec((1, d), lambda i: (0, 0)),
                  _mod_spec(mod, 3, tm, seq_len, d),
                  _mod_spec(mod, 4, tm, seq_len, d),
                  pl.BlockSpec((n_exp, d), lambda i: (0, 0)),
                  pl.BlockSpec((n_exp, 1), lambda i: (0, 0)),
                  pl.BlockSpec((tm, tm), lambda i: (0, 0))],
        out_specs=[pl.BlockSpec((tm, d), lambda i: (i, 0)),
                   pl.BlockSpec((tm * per, LANES), lambda i: (i, 0)),
                   tok8, tok8, tok8,
                   pl.BlockSpec((n_exp, 1), lambda i: (0, 0))],
        out_shape=[jax.ShapeDtypeStruct((n, d), F32), jax.ShapeDtypeStruct((n * per, LANES), I32),
                   row8(I32), row8(F32), row8(I32), jax.ShapeDtypeStruct((n_exp, 1), F32)],
        scratch_shapes=[pltpu.VMEM((n_exp, 1), F32)],
        compiler_params=_cparams("arbitrary"),
        name="mix_out_route",
    )(o, w_out.astype(BF16), x, mod, norm_g.reshape(1, d), mod, mod,
      w_router.T, router_bias.reshape(n_exp, 1), jnp.asarray(tri, BF16))


def _dispatch_kernel(dest_ref, h_ref, xs_ref, sem, *, per):
    tm = h_ref.shape[0] // per

    def row_copy(t, dst_row):
        return pltpu.make_async_copy(_row_tile(h_ref, t, per), _row_tile(xs_ref, dst_row, per), sem)

    def issue(t, c):
        for kk in range(TOP_K):
            row_copy(t, dest_ref[kk, t]).start()
        return c

    def drain(t, c):
        for kk in range(TOP_K):
            row_copy(t, dest_ref[kk, t]).wait()
        return c

    lax.fori_loop(0, tm, issue, 0)
    lax.fori_loop(0, tm, drain, 0)


def moe_dispatch(h, dest, tm):
    n = dest.shape[1]
    per = h.shape[0] // n
    return pl.pallas_call(
        functools.partial(_dispatch_kernel, per=per),
        grid=(n // tm,),
        in_specs=[pl.BlockSpec((TOP_K, tm), lambda i: (0, i), memory_space=pltpu.SMEM),
                  pl.BlockSpec((tm * per, LANES), lambda i: (i, 0))],
        out_specs=pl.BlockSpec(memory_space=pl.ANY),
        out_shape=jax.ShapeDtypeStruct((n * TOP_K * per, LANES), h.dtype),
        scratch_shapes=[pltpu.SemaphoreType.DMA(())],
        compiler_params=_cparams("arbitrary"),
        name="moe_dispatch",
    )(dest, h)


def _expert_kernel(blk_ref, exp_ref, lo_ref, hi_ref, x_ref, wg_ref, wu_ref, wd_ref, y_ref,
                   wgb_ref, wub_ref, wdb_ref):
    j = pl.program_id(0)
    per = wg_ref.shape[0] // 2 // LANES
    blk = x_ref.shape[0] // per
    lo = lo_ref[j] - blk_ref[j] * blk
    hi = hi_ref[j] - blk_ref[j] * blk

    @pl.when(jnp.logical_or(j == 0, exp_ref[j] != exp_ref[jnp.maximum(j - 1, 0)]))
    def _():
        wgb_ref[...] = wg_ref[...].astype(BF16)
        wub_ref[...] = wu_ref[...].astype(BF16)
        wdb_ref[...] = wd_ref[...].astype(BF16)

    @pl.when(hi > lo)
    def _():
        x = _unpack_rows(_rows_load(x_ref, per)).astype(BF16)
        g = jnp.dot(x, wgb_ref[...], preferred_element_type=F32)
        u = jnp.dot(x, wub_ref[...], preferred_element_type=F32)
        y = jnp.dot((_silu(g) * u).astype(BF16), wdb_ref[...], preferred_element_type=F32)

        @pl.when(jnp.logical_and(lo == 0, hi == blk))
        def _():
            _rows_store(y_ref, _pack_rows(y))

        @pl.when(jnp.logical_not(jnp.logical_and(lo == 0, hi == blk)))
        def _():
            row = lax.broadcasted_iota(I32, y.shape, 0)
            ym = jnp.where(jnp.logical_and(row >= lo, row < hi), y, 0.0)

            @pl.when(lo == 0)
            def _():
                _rows_store(y_ref, _pack_rows(ym))

            @pl.when(lo != 0)
            def _():
                _rows_store(y_ref, _pack_rows(_unpack_rows(_rows_load(y_ref, per)) + ym))


def moe_experts(xs, seg_blk, seg_exp, seg_lo, seg_hi, w_gate, w_up, w_down, layer, blk):
    d, ff = w_gate.shape[2:]
    wspec = lambda shape: pl.BlockSpec((None, None) + shape, lambda j, sb, se, lo, hi: (layer, se[j], 0, 0))
    rows = pl.BlockSpec((blk * (d // 2 // LANES), LANES), lambda j, sb, se, lo, hi: (sb[j], 0))
    return pl.pallas_call(
        _expert_kernel,
        grid_spec=pltpu.PrefetchScalarGridSpec(
            num_scalar_prefetch=4, grid=(seg_blk.shape[0],),
            in_specs=[rows, wspec((d, ff)), wspec((d, ff)), wspec((ff, d))],
            out_specs=rows,
            scratch_shapes=[pltpu.VMEM((d, ff), BF16), pltpu.VMEM((d, ff), BF16), pltpu.VMEM((ff, d), BF16)]),
        out_shape=jax.ShapeDtypeStruct(xs.shape, xs.dtype),
        compiler_params=_cparams("arbitrary"),
        name="moe_experts",
    )(seg_blk, seg_exp, seg_lo, seg_hi, xs, w_gate, w_up, w_down)


def _combine_kernel(dest_ref, dnext_ref, ys_ref, wsel_ref, h_ref, x_ref, gt_ref, sg_ref, su_ref, sd_ref, nf_ref,
                    o_ref, buf_ref, sems, *, final_norm):
    i = pl.program_id(0)
    tm, d = x_ref.shape
    per = d // 2 // LANES

    def row_copy(idx_ref, slot, t, kk):
        return pltpu.make_async_copy(_row_tile(ys_ref, idx_ref[kk, t], per),
                                     _row_tile(buf_ref.at[slot], kk * tm + t, per), sems.at[slot])

    def start_rows(idx_ref, slot, t0, t1):
        def issue(t, c):
            for kk in range(TOP_K):
                row_copy(idx_ref, slot, t, kk).start()
            return c
        lax.fori_loop(t0, t1, issue, 0)

    slot = i % 2
    has_next = i + 1 < pl.num_programs(0)
    quarter = tm // 4

    def start_next(part):
        @pl.when(has_next)
        def _():
            start_rows(dnext_ref, 1 - slot, part * quarter, tm if part == 3 else (part + 1) * quarter)

    @pl.when(i == 0)
    def _():
        start_rows(dest_ref, 0, 0, tm)

    start_next(0)
    h = _unpack_rows(_rows_load(h_ref, per)).astype(BF16)
    g = jnp.dot(h, sg_ref[...], preferred_element_type=F32)
    u = jnp.dot(h, su_ref[...], preferred_element_type=F32)
    start_next(1)
    y = _bdot(_silu(g) * u, sd_ref[...])

    def drain(t, c):
        for kk in range(TOP_K):
            row_copy(dest_ref, slot, t, kk).wait()
        return c

    lax.fori_loop(0, tm, drain, 0)
    w = wsel_ref[...]
    rows = buf_ref.at[slot]
    for kk in range(TOP_K):
        if kk in (0, TOP_K // 2):
            start_next(2 + kk // (TOP_K // 2))
        y = y + w[:, kk:kk + 1] * _unpack_rows(_rows_load(rows, per, first=kk * tm, rows=tm))
    x_new = x_ref[...] + gt_ref[...] * y
    o_ref[...] = _rms(x_new, nf_ref[...]) if final_norm else x_new


def moe_combine(ys, dest, wsel_t, h, x, mod, ws_gate, ws_up, ws_down, norm_final, final_norm, tm, seq_len):
    n, d = x.shape
    ff = ws_gate.shape[1]
    last = n // tm - 1
    return pl.pallas_call(
        functools.partial(_combine_kernel, final_norm=final_norm),
        grid=(n // tm,),
        in_specs=[pl.BlockSpec((TOP_K, tm), lambda i: (0, i), memory_space=pltpu.SMEM),
                  pl.BlockSpec((TOP_K, tm), lambda i: (0, jnp.minimum(i + 1, last)), memory_space=pltpu.SMEM),
                  pl.BlockSpec(memory_space=pl.ANY),
                  pl.BlockSpec((tm, TOP_K), lambda i: (i, 0)),
                  pl.BlockSpec((tm * (d // 2 // LANES), LANES), lambda i: (i, 0)),
                  pl.BlockSpec((tm, d), lambda i: (i, 0)),
                  _mod_spec(mod, 5, tm, seq_len, d),
                  pl.BlockSpec((d, ff), lambda i: (0, 0)),
                  pl.BlockSpec((d, ff), lambda i: (0, 0)),
                  pl.BlockSpec((ff, d), lambda i: (0, 0)),
                  pl.BlockSpec((1, d), lambda i: (0, 0))],
        out_specs=pl.BlockSpec((tm, d), lambda i: (i, 0)),
        out_shape=jax.ShapeDtypeStruct((n, d), F32),
        scratch_shapes=[pltpu.VMEM((2, TOP_K * tm * (d // 2 // LANES), LANES), I32),
                        pltpu.SemaphoreType.DMA((2,))],
        compiler_params=_cparams("arbitrary"),
        name="moe_combine",
    )(dest, dest, ys, wsel_t, h, x, mod, ws_gate.astype(BF16), ws_up.astype(BF16), ws_down.astype(BF16),
      norm_final.reshape(1, d))


def _slot_kernel(start_ref, eidx_ref, pos_ref, dest_ref):
    e = eidx_ref[...]
    base = jnp.zeros(e.shape, I32)
    for ex in range(start_ref.shape[0]):
        base = jnp.where(e == ex, start_ref[ex], base)
    dest_ref[...] = base + pos_ref[...]


def moe_slots(start, eidx, pos, tm):
    n = eidx.shape[1]
    spec = pl.BlockSpec((TOP_K, tm), lambda i: (0, i))
    return pl.pallas_call(
        _slot_kernel,
        grid=(n // tm,),
        in_specs=[pl.BlockSpec(memory_space=pltpu.SMEM), spec, spec],
        out_specs=spec,
        out_shape=jax.ShapeDtypeStruct(eidx.shape, I32),
        compiler_params=_cparams("parallel"),
        name="moe_slots",
    )(start, eidx, pos)


def mixer_out_and_moe(o, w_out, x, mod, norm_g, w_router, router_bias, w_gate, w_up, w_down, layer,
                      ws_gate, ws_up, ws_down, norm_final, final_norm, tm, seq_len, blk):
    n, d = x.shape
    n_exp = w_router.shape[1]
    x, h, eidx, wsel, pos, counts = mix_out_and_route(o, w_out, x, mod, norm_g, w_router, router_bias, tm, seq_len)
    n_rows = n * TOP_K
    n_blocks = n_rows // blk
    counts = counts.reshape(n_exp).astype(I32)
    end = jnp.cumsum(counts)
    start = end - counts
    dest = moe_slots(start, eidx, pos, min(tm, n))
    seg_lo = jnp.sort(jnp.concatenate([jnp.arange(n_blocks, dtype=I32) * blk, start[1:]]))
    seg_hi = jnp.concatenate([seg_lo[1:], jnp.full((1,), n_rows, I32)])
    seg_blk = jnp.minimum(seg_lo // blk, n_blocks - 1)
    seg_exp = jnp.minimum(jnp.sum((end[None, :] <= seg_lo[:, None]).astype(I32), axis=1), n_exp - 1)
    xs = moe_dispatch(h, dest, tm)
    ys = moe_experts(xs, seg_blk, seg_exp, seg_lo, seg_hi, w_gate, w_up, w_down, layer, blk)
    return moe_combine(ys, dest, wsel.T, h, x, mod, ws_gate, ws_up, ws_down, norm_final, final_norm,
                       min(tm, 128), seq_len)


def _sample_score_kernel(pt_ref, qi_ref, w_ref, *refs):
    page_refs, o_ref = refs[:-1], refs[-1]
    kp = jnp.concatenate([r[...] for r in page_refs], axis=1)
    lg = _bdot(qi_ref[...], kp)
    sc = jnp.sum(jnp.maximum(lg, 0.0) * w_ref[...], axis=0, keepdims=True)
    page = page_refs[0].shape[1]
    for p in range(len(page_refs)):
        o_ref[p:p + 1, :] = sc[:, p * page:(p + 1) * page]


def _page_specs(layer, pg, block):
    zeros = (0,) * (len(block) - 2)
    return [pl.BlockSpec(block, lambda b, g, pt, p=p: (layer, pt[b, g * pg + p]) + zeros) for p in range(pg)]


def dsa_sample_scores(qi3, wi3, cache_ki_t, layer, page_table, pg):
    b, n_pages = page_table.shape
    page = cache_ki_t.shape[3]
    return pl.pallas_call(
        _sample_score_kernel,
        grid_spec=pltpu.PrefetchScalarGridSpec(
            num_scalar_prefetch=1, grid=(b, n_pages // pg),
            in_specs=[pl.BlockSpec((None, IDX_HEADS, IDX_DH), lambda b, g, pt: (b, 0, 0)),
                      pl.BlockSpec((None, IDX_HEADS, 1), lambda b, g, pt: (b, 0, 0))]
                     + _page_specs(layer, pg, (None, None, IDX_DH, page)),
            out_specs=pl.BlockSpec((None, pg, page), lambda b, g, pt: (b, g, 0))),
        out_shape=jax.ShapeDtypeStruct((b, n_pages, page), F32),
        compiler_params=_cparams("parallel", "arbitrary"),
        name="dsa_sample_scores",
    )(page_table, qi3, wi3, *([cache_ki_t] * pg))


def _sample_select_kernel(sc_ref, qi_ref, w_ref, kin_ref, triu_ref, tril_ref, selb_ref, selbn_ref, *, topk):
    nb = sc_ref.shape[0]
    total = lambda a: jnp.sum(jnp.sum(a, axis=0, keepdims=True), axis=1, keepdims=True)
    keys, keys_new = [], []
    for b in range(nb):
        lg_new = jnp.sum(qi_ref[b].astype(F32) * kin_ref[b].astype(BF16).astype(F32),
                         axis=1, keepdims=True)
        s_new = jnp.sum(jnp.maximum(lg_new, 0.0) * w_ref[b], axis=0, keepdims=True)
        keys.append(_order_key(sc_ref[b]))
        keys_new.append(_order_key(s_new))

    def count(b, pred):
        return total(jnp.where(pred(keys[b]), 1.0, 0.0)) + jnp.where(pred(keys_new[b]), 1.0, 0.0)

    def bits_body(it, answers):
        shift = 30 - 2 * it
        out = []
        for b, ans in enumerate(answers):
            cands = [ans | (jnp.int32(v) << shift) for v in (3, 2, 1)]
            c3, c2, c1 = [count(b, lambda kj, c=c: kj >= (c ^ INT_MIN)) for c in cands]
            out.append(jnp.where(c3 >= topk, cands[0],
                                 jnp.where(c2 >= topk, cands[1], jnp.where(c1 >= topk, cands[2], ans))))
        return tuple(out)

    answers = lax.fori_loop(0, 16, bits_body, (jnp.zeros((1, 1), I32),) * nb)
    for b in range(nb):
        thr = answers[b] ^ INT_MIN
        need = topk - count(b, lambda kj: kj > thr)
        tie = keys[b] == thr
        tie_f = jnp.where(tie, 1.0, 0.0)
        tie_b = tie_f.astype(BF16)
        in_row = jnp.dot(tie_b, triu_ref[...], preferred_element_type=F32)
        rows_before = jnp.sum(jnp.dot(tril_ref[...], tie_b, preferred_element_type=F32), axis=1, keepdims=True)
        before = in_row + rows_before
        selb_ref[b] = jnp.where(keys[b] > thr, 0.0,
                                jnp.where(tie, jnp.where(before < need, 0.0, NEG_INF), NEG_INF))
        selbn_ref[b] = jnp.where(keys_new[b] > thr, 0.0,
                                 jnp.where(keys_new[b] == thr,
                                           jnp.where(total(tie_f) < need, 0.0, NEG_INF), NEG_INF))


def dsa_sample_select(scores, qi3, wi3, ki_new, topk):
    b, n_pages, page = scores.shape
    nb = 8 if b % 8 == 0 else 1
    triu = np.triu(np.ones((page, page), np.float32), 1)
    tril = np.tril(np.ones((n_pages, n_pages), np.float32), -1)
    per_b = lambda *tail: pl.BlockSpec((nb,) + tail, lambda i: (i,) + (0,) * len(tail))
    return pl.pallas_call(
        functools.partial(_sample_select_kernel, topk=float(topk)),
        grid=(b // nb,),
        in_specs=[per_b(n_pages, page), per_b(IDX_HEADS, IDX_DH), per_b(IDX_HEADS, 1), per_b(1, IDX_DH),
                  pl.BlockSpec((page, page), lambda i: (0, 0)),
                  pl.BlockSpec((n_pages, n_pages), lambda i: (0, 0))],
        out_specs=[per_b(n_pages, page), per_b(1, 1)],
        out_shape=[jax.ShapeDtypeStruct((b, n_pages, page), F32), jax.ShapeDtypeStruct((b, 1, 1), F32)],
        compiler_params=_cparams("parallel"),
        name="dsa_sample_select",
    )(scores, qi3, wi3, ki_new, jnp.asarray(triu, BF16), jnp.asarray(tril, BF16))


def _sample_attend_kernel(pt_ref, q_ref, selb_ref, selbn_ref, kn_ref, vn_ref, blast_ref, bfar_ref, bnew_ref,
                          *refs, pg, n_pages):
    k_refs, v_refs = refs[:pg], refs[pg:2 * pg]
    o_ref, m_ref, l_ref, acc_ref = refs[2 * pg:]
    step = pl.program_id(1)
    g_heads = ATT_HEADS // ATT_KV_HEADS
    page = k_refs[0].shape[2]
    dh = q_ref.shape[1]
    head_group = lax.broadcasted_iota(I32, (ATT_HEADS, 1), 0) // g_heads

    @pl.when(step == 0)
    def _():
        m_ref[...] = jnp.full(m_ref.shape, NEG_INF, F32)
        l_ref[...] = jnp.zeros(l_ref.shape, F32)
        acc_ref[...] = jnp.zeros(acc_ref.shape, F32)

    q = q_ref[...]
    s = jnp.zeros((ATT_HEADS, pg * page), F32)
    for g in range(ATT_KV_HEADS):
        kg = jnp.concatenate([r[g] for r in k_refs], axis=1)
        s = jnp.where(head_group == g, _bdot(q, kg), s)
    add = [jnp.where(step * pg + p == n_pages - 1, blast_ref[...], bfar_ref[...]) + selb_ref[p:p + 1, :]
           for p in range(pg)]
    s = s + jnp.concatenate(add, axis=1)
    m_old = m_ref[...]
    m_new = jnp.maximum(m_old, jnp.max(s, axis=1, keepdims=True))
    m_safe = jnp.where(m_new == NEG_INF, 0.0, m_new)
    alpha = jnp.exp(m_old - m_safe)
    p_ = jnp.exp(s - m_safe)
    pv = jnp.zeros((ATT_HEADS, dh), F32)
    for g in range(ATT_KV_HEADS):
        vg = jnp.concatenate([r[g] for r in v_refs], axis=1)
        pv = jnp.where(head_group == g, _bdot_nt(p_, vg), pv)
    l_new = alpha * l_ref[...] + jnp.sum(p_, axis=1, keepdims=True)
    acc_new = alpha * acc_ref[...] + pv
    m_ref[...] = m_new
    l_ref[...] = l_new
    acc_ref[...] = acc_new

    @pl.when(step == pl.num_programs(1) - 1)
    def _():
        kn = jnp.zeros((ATT_HEADS, dh), F32)
        vn = jnp.zeros((ATT_HEADS, dh), F32)
        for g in range(ATT_KV_HEADS):
            kn = jnp.where(head_group == g, kn_ref[g:g + 1, :], kn)
            vn = jnp.where(head_group == g, vn_ref[g:g + 1, :], vn)
        qk = q.astype(F32) * kn.astype(BF16).astype(F32)
        s_n = jnp.sum(qk, axis=1, keepdims=True) + bnew_ref[...] + selbn_ref[...]
        m_fin = jnp.maximum(m_new, s_n)
        m_fs = jnp.where(m_fin == NEG_INF, 0.0, m_fin)
        a2 = jnp.exp(m_new - m_fs)
        p_n = jnp.exp(s_n - m_fs)
        o_ref[...] = ((a2 * acc_new + p_n * vn) / (a2 * l_new + p_n)).astype(o_ref.dtype)


def dsa_sample_attend(q3, selb, selb_new, k_new, v_new, bias_tab, cache_k_t, cache_v_t, layer, page_table, pg):
    b, n_pages = page_table.shape
    kvh, dh, page = cache_k_t.shape[2:]
    assert page == bias_tab.shape[2] and page >= MAX_DISTANCE
    b_last = bias_tab[:, 1, :, 0]
    b_far = bias_tab[:, 2, 0, :1]
    b_new = bias_tab[:, 0, 0, :1]
    full = lambda shape: pl.BlockSpec(shape, lambda b, g, pt: (0,) * len(shape))
    per_b = lambda shape: pl.BlockSpec((None,) + shape, lambda b, g, pt: (b,) + (0,) * len(shape))
    return pl.pallas_call(
        functools.partial(_sample_attend_kernel, pg=pg, n_pages=n_pages),
        grid_spec=pltpu.PrefetchScalarGridSpec(
            num_scalar_prefetch=1, grid=(b, n_pages // pg),
            in_specs=[per_b((ATT_HEADS, dh)),
                      pl.BlockSpec((None, pg, page), lambda b, g, pt: (b, g, 0)),
                      per_b((1, 1)), per_b((kvh, dh)), per_b((kvh, dh)),
                      full((ATT_HEADS, page)), full((ATT_HEADS, 1)), full((ATT_HEADS, 1))]
                     + _page_specs(layer, pg, (None, None, kvh, dh, page))
                     + _page_specs(layer, pg, (None, None, kvh, dh, page)),
            out_specs=per_b((ATT_HEADS, dh)),
            scratch_shapes=[pltpu.VMEM((ATT_HEADS, 1), F32), pltpu.VMEM((ATT_HEADS, 1), F32),
                            pltpu.VMEM((ATT_HEADS, dh), F32)]),
        out_shape=jax.ShapeDtypeStruct((b, ATT_HEADS, dh), BF16),
        compiler_params=_cparams("parallel", "arbitrary"),
        name="dsa_sample_attend",
    )(page_table, q3, selb, selb_new, k_new, v_new, b_last, b_far, b_new,
      *([cache_k_t] * pg), *([cache_v_t] * pg))


def dsa_sample(q, k, v, qi, ki, wit, cache_k, cache_v, cache_ki, layer, page_table, bias_tab):
    b, d = q.shape
    dh = d // ATT_HEADS
    n_pages = page_table.shape[1]
    page = cache_k.shape[2]
    pg = 16 if n_pages % 16 == 0 else n_pages
    topk = min(TOPK_MAX, (n_pages * page + 1) // 4)
    qi3 = qi.reshape(b, IDX_HEADS, IDX_DH)
    wi3 = wit.T.reshape(b, IDX_HEADS, 1)
    cache_ki_t = jnp.transpose(cache_ki, (0, 1, 3, 2))
    cache_k_t = jnp.transpose(cache_k, (0, 1, 3, 4, 2))
    cache_v_t = jnp.transpose(cache_v, (0, 1, 3, 4, 2))
    scores = dsa_sample_scores(qi3, wi3, cache_ki_t, layer, page_table, 2 * pg if n_pages % (2 * pg) == 0 else pg)
    selb, selb_new = dsa_sample_select(scores, qi3, wi3, ki.reshape(b, 1, IDX_DH), topk)
    o = dsa_sample_attend(q.reshape(b, ATT_HEADS, dh), selb, selb_new, k.reshape(b, ATT_KV_HEADS, dh),
                          v.reshape(b, ATT_KV_HEADS, dh), bias_tab, cache_k_t, cache_v_t, layer, page_table, pg)
    return o.reshape(b, d)


def kernel(x_prompt, x_sample, c_prompt, c_sample, state_gla, cache_k, cache_v, cache_idx_k, page_table,
           rel_bias, w_ada, b_ada, norm_mix, norm_ffn, norm_final,
           gla_w_in, gla_w_g2, gla_b_g2, gla_gnorm, gla_w_out,
           dsa_w_in, dsa_idx_ln_g, dsa_idx_ln_b, dsa_w_out,
           moe_w_router, moe_router_bias, moe_w_gate, moe_w_up, moe_w_down,
           shared_w_gate, shared_w_up, shared_w_down):
    bp, t, d = x_prompt.shape
    bs = x_sample.shape[0]
    depth = w_ada.shape[0]
    dh = d // ATT_HEADS
    tm_p = min(256, t)
    tq = 128
    mods = ada_mod_all(jnp.concatenate([c_prompt, c_sample], axis=0), w_ada, b_ada)
    xp = x_prompt.reshape(bp * t, d)
    xs = x_sample.reshape(bs, d)
    bias_tab = bias_tables(rel_bias, tq)
    gla_p, gla_s, kp_l, vp_l, kip_l, ks_l, vs_l, kis_l = [], [], [], [], [], [], [], []
    for i in range(depth):
        mod_p = mods[i, :bp].reshape(bp, 1, 6 * d)
        mod_s = mods[i, bp:]
        j = i // 2
        if i % 2 == 0:
            qkvr_p, la_p = gla_project(xp, mod_p, norm_mix[i], gla_w_in[j], gla_w_g2[j], gla_b_g2[j], tm_p, t)
            o_p, s_p = gla_chunked(qkvr_p, la_p, gla_gnorm[j], bp, t)
            qkvr_s, la_s = gla_project(xs, mod_s, norm_mix[i], gla_w_in[j], gla_w_g2[j], gla_b_g2[j], bs, 1)
            o_s, s_s = gla_step(qkvr_s, la_s, state_gla[j], gla_gnorm[j])
            gla_p.append(s_p)
            gla_s.append(s_s)
            w_out = gla_w_out[j]
        else:
            dp = (dsa_w_in[j], dsa_idx_ln_g[j], dsa_idx_ln_b[j])
            q_p, qi_p, wit_p, kt_p, vt32_p, kit_p, kib_p, kg_p, vt_p = dsa_project(
                xp, mod_p, norm_mix[i], *dp, tm_p, t, True)
            o_p = dsa_prompt(q_p, qi_p, wit_p, kib_p, kg_p, vt_p, bias_tab, bp, t, tq)
            q_s, qi_s, wit_s, k_s, v_s, ki_s = dsa_project(xs, mod_s, norm_mix[i], *dp, bs, 1, False)
            o_s = dsa_sample(q_s, k_s, v_s, qi_s, ki_s, wit_s, cache_k, cache_v, cache_idx_k, j,
                             page_table, bias_tab)
            kp_l.append(kt_p.reshape(bp, ATT_KV_HEADS, dh, t).transpose(0, 3, 1, 2))
            vp_l.append(vt32_p.reshape(bp, ATT_KV_HEADS, dh, t).transpose(0, 3, 1, 2))
            kip_l.append(kit_p.transpose(0, 2, 1))
            ks_l.append(k_s.reshape(bs, 1, ATT_KV_HEADS, dh))
            vs_l.append(v_s.reshape(bs, 1, ATT_KV_HEADS, dh))
            kis_l.append(ki_s.reshape(bs, 1, IDX_DH))
            w_out = dsa_w_out[j]
        last = i == depth - 1
        mo = (norm_ffn[i], moe_w_router[i], moe_router_bias[i], moe_w_gate, moe_w_up, moe_w_down, i,
              shared_w_gate[i], shared_w_up[i], shared_w_down[i], norm_final, last)
        xp = mixer_out_and_moe(o_p, w_out, xp, mod_p, *mo, tm_p, t, 512)
        xs = mixer_out_and_moe(o_s, w_out, xs, mod_s, *mo, bs, 1, 32)
    return (xp.reshape(bp, t, d), xs.reshape(bs, 1, d), jnp.stack(gla_p), jnp.stack(gla_s),
            jnp.stack(kp_l), jnp.stack(vp_l), jnp.stack(kip_l),
            jnp.stack(ks_l), jnp.stack(vs_l), jnp.stack(kis_l))
```

```python
import functools
import math

import numpy as np
import jax
import jax.numpy as jnp
from jax import lax
from jax.experimental import pallas as pl
from jax.experimental.pallas import tpu as pltpu

F32 = jnp.float32
BF16 = jnp.bfloat16
I32 = jnp.int32

GLA_HEADS = 4
GLA_RANK = 16
GLA_NORMALIZER = 16.0
ATT_HEADS = 16
ATT_KV_HEADS = 4
IDX_HEADS = 8
IDX_DH = 64
TOPK_MAX = 256
N_BUCKETS = 32
MAX_DISTANCE = 128
N_GROUPS = 8
TOPK_GROUPS = 4
TOP_K = 8
ROUTE_SCALE = 2.5
EPS = 1e-6

LANES = 128
VMEM_LIMIT = 56 * 1024 * 1024
NEG_INF = float("-inf")
INT_MIN = -2 ** 31


def _cparams(*sem):
    return pltpu.CompilerParams(dimension_semantics=sem, vmem_limit_bytes=VMEM_LIMIT)


def _bdot(a, b):
    return jnp.dot(a.astype(BF16), b.astype(BF16), preferred_element_type=F32)


def _bdot_nt(a, b):
    return lax.dot_general(a.astype(BF16), b.astype(BF16), (((1,), (1,)), ((), ())),
                           preferred_element_type=F32)


def _bdot_tn(a, b):
    return lax.dot_general(a.astype(BF16), b.astype(BF16), (((0,), (0,)), ((), ())),
                           preferred_element_type=F32)


def _split3(a):
    hi = a.astype(BF16)
    r1 = a - hi.astype(F32)
    mid = r1.astype(BF16)
    lo = (r1 - mid.astype(F32)).astype(BF16)
    return hi, mid, lo


def _silu(x):
    return x * (1.0 / (1.0 + jnp.exp(-x)))


def _sigmoid(x):
    return 1.0 / (1.0 + jnp.exp(-x))


def _rms(x, g):
    return x * lax.rsqrt(jnp.mean(x * x, axis=-1, keepdims=True) + EPS) * g


def _ada_kernel(c_ref, w_ref, b_ref, o_ref):
    c_hi, c_mid, _ = _split3(_silu(c_ref[...]))
    w_hi, w_mid, _ = _split3(w_ref[...])
    dot = lambda a, b: jnp.dot(a, b, preferred_element_type=F32)
    o_ref[...] = (dot(c_hi, w_mid) + dot(c_mid, w_hi)) + dot(c_hi, w_hi) + b_ref[...]


def ada_mod_all(c, w_ada, b_ada, tn=512):
    depth, d, n6 = w_ada.shape
    rows = c.shape[0]
    return pl.pallas_call(
        _ada_kernel,
        grid=(depth, n6 // tn),
        in_specs=[pl.BlockSpec((rows, d), lambda l, j: (0, 0)),
                  pl.BlockSpec((None, d, tn), lambda l, j: (l, 0, j)),
                  pl.BlockSpec((None, 1, tn), lambda l, j: (l, 0, j))],
        out_specs=pl.BlockSpec((None, rows, tn), lambda l, j: (l, 0, j)),
        out_shape=jax.ShapeDtypeStruct((depth, rows, n6), F32),
        compiler_params=_cparams("parallel", "parallel"),
        name="ada_mod",
    )(c, w_ada, b_ada.reshape(depth, 1, n6))


def _mod_spec(mod, which, tm, seq_len, d):
    if mod.ndim == 3:
        per_seq = seq_len // tm
        return pl.BlockSpec((None, 1, d), lambda i: (i // per_seq, 0, which))
    return pl.BlockSpec((tm, d), lambda i: (i, which))


def _gla_proj_kernel(x_ref, g_ref, sh_ref, sc_ref, w_ref, wg2_ref, bg2_ref, qkvr_ref, la_ref, *, n_main):
    h = _rms(x_ref[...], g_ref[...]) * (1.0 + sc_ref[...]) + sh_ref[...]
    y = _bdot(h, w_ref[...])
    qkvr_ref[...] = y[:, :n_main]
    g1 = y[:, n_main:n_main + GLA_RANK]
    z = _bdot(g1, wg2_ref[...]) + bg2_ref[...]
    la_ref[...] = (jnp.minimum(z, 0.0) - jnp.log(1.0 + jnp.exp(-jnp.abs(z)))) * (1.0 / GLA_NORMALIZER)


def gla_project(x, mod, norm_g, w_in, w_g2, b_g2, tm, seq_len):
    n, d = x.shape
    hk = w_g2.shape[1]
    n_main = w_in.shape[1] - GLA_RANK
    n_pad = (-w_in.shape[1]) % LANES
    w = jnp.pad(w_in, ((0, 0), (0, n_pad))).astype(BF16)
    return pl.pallas_call(
        functools.partial(_gla_proj_kernel, n_main=n_main),
        grid=(n // tm,),
        in_specs=[pl.BlockSpec((tm, d), lambda i: (i, 0)),
                  pl.BlockSpec((1, d), lambda i: (0, 0)),
                  _mod_spec(mod, 0, tm, seq_len, d),
                  _mod_spec(mod, 1, tm, seq_len, d),
                  pl.BlockSpec(w.shape, lambda i: (0, 0)),
                  pl.BlockSpec(w_g2.shape, lambda i: (0, 0)),
                  pl.BlockSpec((1, hk), lambda i: (0, 0))],
        out_specs=[pl.BlockSpec((tm, n_main), lambda i: (i, 0)),
                   pl.BlockSpec((tm, hk), lambda i: (i, 0))],
        out_shape=[jax.ShapeDtypeStruct((n, n_main), F32),
                   jax.ShapeDtypeStruct((n, hk), F32)],
        compiler_params=_cparams("parallel"),
        name="gla_proj",
    )(x, norm_g.reshape(1, d), mod, mod, w, w_g2.astype(BF16), b_g2.reshape(1, hk))


def _gla_level_matrices(c):
    levels = int(math.log2(c))
    t = np.arange(c)[:, None]
    u = np.arange(c)[None, :]
    mats, masks = [], []
    for l in range(levels):
        m = 1 << l
        ref = (t // (2 * m)) * (2 * m) + m - 1
        right = (t % (2 * m)) >= m
        mat = np.where(right, (u > ref) & (u <= t), (u > t) & (u <= ref))
        mats.append(mat)
        masks.append((t // (2 * m) == u // (2 * m)) & right & ((u % (2 * m)) < m))
    mats.append(u <= t)
    masks.append(t == u)
    return (np.stack(mats).astype(np.float32).reshape(-1, c), np.stack(masks).astype(np.float32))


def _gla_chunk_kernel(q_ref, k_ref, v_ref, r_ref, la_ref, mat_ref, mask_ref, gn_ref,
                      o_ref, s_out_ref, st_ref, *, c, dk):
    ci = pl.program_id(1)
    nlev = mask_ref.shape[0] - 1
    hk = la_ref.shape[1]
    dv = v_ref.shape[1] // GLA_HEADS

    @pl.when(ci == 0)
    def _():
        st_ref[...] = jnp.zeros_like(st_ref)

    la = la_ref[...]
    hi = la.astype(BF16)
    lo = (la - hi.astype(F32)).astype(BF16)
    e_all = jnp.dot(mat_ref[...], jnp.concatenate([hi, lo], axis=1), preferred_element_type=F32)
    e_all = e_all[:, :hk] + e_all[:, hk:]

    for h in range(GLA_HEADS):
        ks, vs = slice(h * dk, (h + 1) * dk), slice(h * dv, (h + 1) * dv)
        q = q_ref[:, ks] * (dk ** -0.5)
        k = k_ref[:, ks]
        v = v_ref[:, vs]
        att = jnp.where(mask_ref[nlev] > 0.0, _bdot_nt(q, k), 0.0)
        for l in range(nlev):
            ex = jnp.exp(e_all[l * c:(l + 1) * c, ks])
            att = att + jnp.where(mask_ref[l] > 0.0, _bdot_nt(q * ex, k * ex), 0.0)

        b = e_all[nlev * c:(nlev + 1) * c, ks]
        b_end = b[c - 1:c, :]
        st = st_ref[h]
        o = _bdot(att, v) + _bdot_nt(q * jnp.exp(b), st)
        st_new = st * jnp.exp(b_end) + _bdot_tn(v, k * jnp.exp(b_end - b))
        st_ref[h] = st_new

        on = o * lax.rsqrt(jnp.mean(o * o, axis=-1, keepdims=True) + EPS) * gn_ref[...]
        o_ref[:, vs] = (on * _silu(r_ref[:, vs])).astype(o_ref.dtype)

        @pl.when(ci == pl.num_programs(1) - 1)
        def _():
            s_out_ref[h] = st_new.T


def gla_chunked(qkvr, log_a, gnorm, batch, seq_len, c=128):
    n = qkvr.shape[0]
    hk = log_a.shape[1]
    dk = hk // GLA_HEADS
    hv = (qkvr.shape[1] - 2 * hk) // 2
    dv = hv // GLA_HEADS
    nc = seq_len // c
    mats, masks = _gla_level_matrices(c)
    row = lambda b, ci: b * nc + ci
    return pl.pallas_call(
        functools.partial(_gla_chunk_kernel, c=c, dk=dk),
        grid=(batch, nc),
        in_specs=[pl.BlockSpec((c, hk), lambda b, ci: (row(b, ci), 0)),
                  pl.BlockSpec((c, hk), lambda b, ci: (row(b, ci), 1)),
                  pl.BlockSpec((c, hv), lambda b, ci: (row(b, ci), 2 * hk // hv)),
                  pl.BlockSpec((c, hv), lambda b, ci: (row(b, ci), 2 * hk // hv + 1)),
                  pl.BlockSpec((c, hk), lambda b, ci: (row(b, ci), 0)),
                  pl.BlockSpec(mats.shape, lambda b, ci: (0, 0)),
                  pl.BlockSpec(masks.shape, lambda b, ci: (0, 0, 0)),
                  pl.BlockSpec((1, dv), lambda b, ci: (0, 0))],
        out_specs=[pl.BlockSpec((c, hv), lambda b, ci: (row(b, ci), 0)),
                   pl.BlockSpec((None, GLA_HEADS, dk, dv), lambda b, ci: (b, 0, 0, 0))],
        out_shape=[jax.ShapeDtypeStruct((n, hv), BF16),
                   jax.ShapeDtypeStruct((batch, GLA_HEADS, dk, dv), F32)],
        scratch_shapes=[pltpu.VMEM((GLA_HEADS, dv, dk), F32)],
        compiler_params=_cparams("parallel", "arbitrary"),
        name="gla_chunked",
    )(qkvr, qkvr, qkvr, qkvr, log_a, jnp.asarray(mats, BF16), jnp.asarray(masks), gnorm.reshape(1, dv))


def _gla_step_kernel(q_ref, k_ref, g_ref, v_ref, r_ref, s_ref, gn_ref, o_ref, s_out_ref, *, dk):
    s_new = jnp.exp(g_ref[...]) * s_ref[...] + k_ref[...] * v_ref[...]
    s_out_ref[...] = s_new
    o = jnp.sum(q_ref[...] * (dk ** -0.5) * s_new, axis=0, keepdims=True)
    on = o * lax.rsqrt(jnp.mean(o * o, axis=-1, keepdims=True) + EPS) * gn_ref[...]
    o_ref[...] = (on * _silu(r_ref[...])).astype(o_ref.dtype)


def gla_step(qkvr, log_a, state, gnorm):
    batch = qkvr.shape[0]
    hk = log_a.shape[1]
    dk = hk // GLA_HEADS
    hv = (qkvr.shape[1] - 2 * hk) // 2
    dv = hv // GLA_HEADS
    col = lambda a: a.reshape(batch, GLA_HEADS, dk, 1)
    rowv = lambda a: a.reshape(batch, GLA_HEADS, 1, dv)
    q, k, v, r = (qkvr[:, :hk], qkvr[:, hk:2 * hk], qkvr[:, 2 * hk:2 * hk + hv], qkvr[:, 2 * hk + hv:])
    cspec = pl.BlockSpec((None, None, dk, 1), lambda b, h: (b, h, 0, 0))
    rspec = pl.BlockSpec((None, None, 1, dv), lambda b, h: (b, h, 0, 0))
    sspec = pl.BlockSpec((None, None, dk, dv), lambda b, h: (b, h, 0, 0))
    o, s_new = pl.pallas_call(
        functools.partial(_gla_step_kernel, dk=dk),
        grid=(batch, GLA_HEADS),
        in_specs=[cspec, cspec, cspec, rspec, rspec, sspec, pl.BlockSpec((1, dv), lambda b, h: (0, 0))],
        out_specs=[rspec, sspec],
        out_shape=[jax.ShapeDtypeStruct((batch, GLA_HEADS, 1, dv), BF16),
                   jax.ShapeDtypeStruct(state.shape, F32)],
        compiler_params=_cparams("parallel", "parallel"),
        name="gla_step",
    )(col(q), col(k), col(log_a), rowv(v), rowv(r), state, gnorm.reshape(1, dv))
    return o.reshape(batch, hv), s_new


HI_MASK = -65536


def _pack_rows(val):
    half = val.shape[1] // 2
    bits = lambda a: pltpu.bitcast(a.astype(BF16).astype(F32), I32)
    return lax.shift_right_logical(bits(val[:, :half]), 16) | (bits(val[:, half:]) & HI_MASK)


def _unpack_rows(words):
    return jnp.concatenate([pltpu.bitcast(words << 16, F32), pltpu.bitcast(words & HI_MASK, F32)], axis=1)


def _rows_load(ref, per, first=0, rows=None):
    rows = ref.shape[0] // per - first if rows is None else rows
    return jnp.concatenate([ref[pl.ds(first * per + j, rows, stride=per), :] for j in range(per)], axis=1)


def _rows_store(ref, words):
    rows = words.shape[0]
    per = words.shape[1] // LANES
    for j in range(per):
        ref[pl.ds(j, rows, stride=per), :] = words[:, j * LANES:(j + 1) * LANES]


def _row_tile(ref, r, per):
    return ref.at[pl.ds(pl.multiple_of(r * per, per), per), :]


def _dsa_proj_kernel(x_ref, g_ref, sh_ref, sc_ref, w_ref, wwi_ref, lg_ref, lb_ref,
                     q_ref, qi_ref, wit_ref, k_ref, v_ref, ki_ref, *seq_refs, cuts, dh):
    h = (_rms(x_ref[...], g_ref[...]) * (1.0 + sc_ref[...]) + sh_ref[...]).astype(BF16)
    y = jnp.dot(h, w_ref[...], preferred_element_type=F32)
    c0, c1, c2, c3, c4 = cuts
    q_ref[...] = (y[:, :c0] * (dh ** -0.5)).astype(BF16)
    qi_ref[...] = (y[:, c2:c3] * (IDX_DH ** -0.5)).astype(BF16)
    wit_ref[...] = _bdot_nt(wwi_ref[...], h) * (IDX_HEADS ** -0.5)
    k = y[:, c0:c1]
    v = y[:, c1:c2]
    ki = y[:, c3:c4]
    mu = jnp.mean(ki, axis=-1, keepdims=True)
    var = jnp.mean((ki - mu) * (ki - mu), axis=-1, keepdims=True)
    kin = (ki - mu) * lax.rsqrt(var + EPS) * lg_ref[...] + lb_ref[...]
    if not seq_refs:
        k_ref[...] = k
        v_ref[...] = v
        ki_ref[...] = kin
    else:
        kib_ref, kg_ref, vt_ref = seq_refs
        vt = v.T
        k_ref[...] = k.T
        v_ref[...] = vt
        ki_ref[...] = jnp.concatenate([kin, jnp.zeros_like(kin)], axis=1).T[:kin.shape[1]]
        kib_ref[...] = kin.astype(BF16)
        for g in range(ATT_KV_HEADS):
            kg_ref[g] = k[:, g * dh:(g + 1) * dh].astype(BF16)
        vt_ref[...] = vt.astype(BF16)


def dsa_project(x, mod, norm_g, w_in, ln_g, ln_b, tm, seq_len, seq_layouts):
    n, d = x.shape
    dh = d // ATT_HEADS
    nkv = ATT_KV_HEADS * dh
    nqi = IDX_HEADS * IDX_DH
    cuts = (d, d + nkv, d + 2 * nkv, d + 2 * nkv + nqi, d + 2 * nkv + nqi + IDX_DH)
    w = jnp.pad(w_in[:, :cuts[4]], ((0, 0), (0, (-cuts[4]) % LANES))).astype(BF16)
    w_wi_t = w_in[:, cuts[4]:cuts[4] + IDX_HEADS].T.astype(BF16)
    row = lambda wd: pl.BlockSpec((tm, wd), lambda i: (i, 0))
    out_specs = [row(d), row(nqi), pl.BlockSpec((IDX_HEADS, tm), lambda i: (0, i))]
    out_shape = [jax.ShapeDtypeStruct((n, d), BF16), jax.ShapeDtypeStruct((n, nqi), BF16),
                 jax.ShapeDtypeStruct((IDX_HEADS, n), F32)]
    if not seq_layouts:
        out_specs += [row(nkv), row(nkv), row(IDX_DH)]
        out_shape += [jax.ShapeDtypeStruct((n, wd), F32) for wd in (nkv, nkv, IDX_DH)]
    else:
        per_seq = seq_len // tm
        tmin = lambda wd: pl.BlockSpec((None, wd, tm), lambda i: (i // per_seq, 0, i % per_seq))
        out_specs += [tmin(nkv), tmin(nkv), tmin(IDX_DH),
                      row(IDX_DH),
                      pl.BlockSpec((ATT_KV_HEADS, tm, dh), lambda i: (0, i, 0)),
                      pl.BlockSpec((None, nkv, tm), lambda i: (i, 0, 0))]
        out_shape += [jax.ShapeDtypeStruct((n // seq_len, wd, seq_len), F32) for wd in (nkv, nkv, IDX_DH)]
        out_shape += [jax.ShapeDtypeStruct((n, IDX_DH), BF16),
                      jax.ShapeDtypeStruct((ATT_KV_HEADS, n, dh), BF16),
                      jax.ShapeDtypeStruct((n // tm, nkv, tm), BF16)]
    return pl.pallas_call(
        functools.partial(_dsa_proj_kernel, cuts=cuts, dh=dh),
        grid=(n // tm,),
        in_specs=[pl.BlockSpec((tm, d), lambda i: (i, 0)),
                  pl.BlockSpec((1, d), lambda i: (0, 0)),
                  _mod_spec(mod, 0, tm, seq_len, d),
                  _mod_spec(mod, 1, tm, seq_len, d),
                  pl.BlockSpec(w.shape, lambda i: (0, 0)),
                  pl.BlockSpec(w_wi_t.shape, lambda i: (0, 0)),
                  pl.BlockSpec((1, IDX_DH), lambda i: (0, 0)),
                  pl.BlockSpec((1, IDX_DH), lambda i: (0, 0))],
        out_specs=out_specs,
        out_shape=out_shape,
        compiler_params=_cparams("parallel"),
        name="dsa_proj",
    )(x, norm_g.reshape(1, d), mod, mod, w, w_wi_t, ln_g.reshape(1, IDX_DH), ln_b.reshape(1, IDX_DH))


def _t5_bucket_np(dist):
    dist = np.maximum(dist, 0)
    max_exact = N_BUCKETS // 2
    ratio = np.log(np.maximum(dist, max_exact).astype(np.float32) / max_exact) / math.log(MAX_DISTANCE / max_exact)
    large = np.minimum(max_exact + (ratio * (N_BUCKETS - max_exact)).astype(np.int32), N_BUCKETS - 1)
    return np.where(dist < max_exact, dist, large).astype(np.int32)


def _bias_table_kernel(rb_ref, bk_ref, o_ref):
    h = pl.program_id(0)
    for kind in range(3):
        bk = bk_ref[kind]
        acc = jnp.zeros(bk.shape, F32)
        for b in range(N_BUCKETS):
            acc = jnp.where(bk == b, rb_ref[b, h], acc)
        o_ref[kind] = acc


def bias_tables(rel_bias, tq):
    s = np.arange(tq)[:, None]
    t = np.arange(tq)[None, :]
    buckets = np.stack([_t5_bucket_np(t - s), _t5_bucket_np(tq + t - s), _t5_bucket_np(2 * tq + t - s)])
    assert tq >= MAX_DISTANCE and (buckets[2] == N_BUCKETS - 1).all()
    return pl.pallas_call(
        _bias_table_kernel,
        grid=(ATT_HEADS,),
        in_specs=[pl.BlockSpec(memory_space=pltpu.SMEM),
                  pl.BlockSpec(buckets.shape, lambda h: (0, 0, 0))],
        out_specs=pl.BlockSpec((None, 3, tq, tq), lambda h: (h, 0, 0, 0)),
        out_shape=jax.ShapeDtypeStruct((ATT_HEADS, 3, tq, tq), F32),
        compiler_params=_cparams("parallel"),
        name="t5_bias_tiles",
    )(rel_bias, jnp.asarray(buckets))


def _order_key(score):
    score = jnp.where(score == 0.0, 0.0, score)
    bits = pltpu.bitcast(score, I32)
    return bits ^ ((bits >> 31) & 0x7FFFFFFF)


def _dsa_prompt_kernel(q_ref, qi_ref, wit_ref, kib_ref, kg_ref, vt_ref, bias_ref, tril_ref, o_ref,
                       key_ref, selb_ref, qis_ref, qg_ref, sc0_ref, sc1_ref, sc2_ref, sc3_ref, ot_ref,
                       *, tq, dh, topk):
    i = pl.program_id(1)
    nk = i + 1
    g_heads = ATT_HEADS // ATT_KV_HEADS
    s_id = lax.broadcasted_iota(I32, (tq, tq), 0)
    t_id = lax.broadcasted_iota(I32, (tq, tq), 1)
    fold8 = lambda a, op: op(a.reshape(a.shape[0] // 8, 8, a.shape[1]), axis=0)

    for h in range(IDX_HEADS):
        qis_ref[h * tq:(h + 1) * tq, :] = qi_ref[:, h * IDX_DH:(h + 1) * IDX_DH]
    for hh in range(ATT_HEADS):
        g, a = divmod(hh, g_heads)
        qg_ref[g, a * tq:(a + 1) * tq, :] = q_ref[:, hh * dh:(hh + 1) * dh]
    w8 = wit_ref[...]

    n_pairs = (nk + 1) // 2

    def score_body(jj, carry):
        for j in (2 * jj, 2 * jj + 1):
            kij = kib_ref[pl.ds(pl.multiple_of(j * tq, tq), tq), :]
            lg = _bdot_nt(kij, qis_ref[...])
            acc = jnp.zeros((tq, tq), F32)
            for h in range(IDX_HEADS):
                acc = acc + jnp.maximum(lg[:, h * tq:(h + 1) * tq], 0.0) * w8[h:h + 1, :]
            valid = (s_id + j * tq) <= (t_id + i * tq)
            key_ref[j] = _order_key(jnp.where(valid, acc, NEG_INF))
        return carry

    lax.fori_loop(0, n_pairs, score_body, 0)

    @pl.when(nk % 2 == 1)
    def _():
        pad = jnp.minimum(nk, key_ref.shape[0] - 1)
        key_ref[pad] = jnp.full((tq, tq), INT_MIN, I32)
        selb_ref[pad] = jnp.full((tq, tq), NEG_INF, F32)

    def counts(preds):
        def body(jj, accs):
            ka, kb = key_ref[2 * jj], key_ref[2 * jj + 1]
            return tuple(acc8 + fold8(jnp.where(p(ka), 1.0, 0.0) + jnp.where(p(kb), 1.0, 0.0), jnp.sum)
                         for p, acc8 in zip(preds, accs))
        accs = lax.fori_loop(0, n_pairs, body, (jnp.zeros((8, tq), F32),) * len(preds))
        return [jnp.sum(a, axis=0, keepdims=True) for a in accs]

    def bits_body(it, ans):
        shift = 30 - 2 * it
        cands = [ans | (jnp.int32(v) << shift) for v in (3, 2, 1)]
        c3, c2, c1 = counts([lambda kj, c=c: kj >= (c ^ INT_MIN) for c in cands])
        return jnp.where(c3 >= topk, cands[0], jnp.where(c2 >= topk, cands[1], jnp.where(c1 >= topk, cands[2], ans)))

    thr = lax.fori_loop(0, 16, bits_body, jnp.zeros((1, tq), I32)) ^ INT_MIN
    n_gt, n_tie = counts([lambda kj: kj > thr, lambda kj: kj == thr])
    need = topk - n_gt

    def causal(j):
        return (s_id + j * tq) <= (t_id + i * tq)

    some_ties_dropped = jnp.max(n_tie - need) > 0.0

    @pl.when(jnp.logical_not(some_ties_dropped))
    def _():
        def sel_body(j, carry):
            selb_ref[j] = jnp.where(jnp.logical_and(key_ref[j] >= thr, causal(j)), 0.0, NEG_INF)
            return carry
        lax.fori_loop(0, nk, sel_body, 0)

    @pl.when(some_ties_dropped)
    def _():
        def sel_body(j, run):
            kj = key_ref[j]
            tie = kj == thr
            tie_f = jnp.where(tie, 1.0, 0.0)
            before = run + jnp.dot(tril_ref[...], tie_f.astype(BF16), preferred_element_type=F32)
            take = jnp.where(kj > thr, 0.0, jnp.where(tie, jnp.where(before < need, 0.0, NEG_INF), NEG_INF))
            selb_ref[j] = jnp.where(causal(j), take, NEG_INF)
            return run + jnp.sum(tie_f, axis=0, keepdims=True)
        lax.fori_loop(0, nk, sel_body, jnp.zeros((1, tq), F32))

    ones_rows = jnp.ones((16, 2 * tq), BF16)
    sc_refs = (sc0_ref, sc1_ref, sc2_ref, sc3_ref)
    for g0 in range(0, ATT_KV_HEADS, len(sc_refs)):
        groups = tuple((g0 + n, scr) for n, scr in enumerate(sc_refs))

        def logits_body(jj, mx):
            base = pl.multiple_of(jj * 2 * tq, 2 * tq)
            j0 = 2 * jj
            sels = [selb_ref[j0], selb_ref[j0 + 1]]
            kinds = [jnp.clip(i - j0 - c, 0, 2) for c in range(2)]
            out = []
            for (g, scr), mx8 in zip(groups, mx):
                s = _bdot_nt(kg_ref[g, pl.ds(base, 2 * tq), :], qg_ref[g])
                add = jnp.concatenate(
                    [jnp.concatenate([bias_ref[g * g_heads + a, kinds[c]] + sels[c] for a in range(g_heads)], axis=1)
                     for c in range(2)], axis=0)
                s = s + add
                scr[jj] = s
                out.append(jnp.maximum(mx8, fold8(s, jnp.max)))
            return tuple(out)

        neg = jnp.full((8, g_heads * tq), NEG_INF, F32)
        mx = lax.fori_loop(0, n_pairs, logits_body, (neg,) * len(groups))
        ms = []
        for mx8 in mx:
            m = jnp.max(mx8, axis=0, keepdims=True)
            ms.append(jnp.where(m == NEG_INF, 0.0, m))

        def pv_body(jj, accs):
            out = []
            for (g, scr), m, acc in zip(groups, ms, accs):
                p = jnp.exp(scr[jj] - m).astype(BF16)
                lhs = jnp.concatenate([vt_ref[jj, g * dh:(g + 1) * dh, :], ones_rows], axis=0)
                out.append(acc + jnp.dot(lhs, p, preferred_element_type=F32))
            return tuple(out)

        zero = jnp.zeros((dh + 16, g_heads * tq), F32)
        accs = lax.fori_loop(0, n_pairs, pv_body, (zero,) * len(groups))
        for (g, _), acc in zip(groups, accs):
            og = acc[:dh] / acc[dh:dh + 1]
            for a in range(g_heads):
                hh = g * g_heads + a
                ot_ref[hh * dh:(hh + 1) * dh, :] = og[:, a * tq:(a + 1) * tq]
    o_ref[...] = ot_ref[...].T.astype(o_ref.dtype)


def dsa_prompt(q, qi, wit, kib, kg, vt, bias_tab, batch, seq_len, tq=128):
    n, d = q.shape
    dh = d // ATT_HEADS
    nq = seq_len // tq
    g_heads = ATT_HEADS // ATT_KV_HEADS
    assert nq % 2 == 0 and vt.shape[2] == 2 * tq
    topk = min(TOPK_MAX, seq_len // 4)
    tril = np.tril(np.ones((tq, tq), np.float32), -1)
    blk_spec = lambda width: pl.BlockSpec((tq, width), lambda b, i: (b * nq + i, 0))
    return pl.pallas_call(
        functools.partial(_dsa_prompt_kernel, tq=tq, dh=dh, topk=float(topk)),
        grid=(batch, nq),
        in_specs=[blk_spec(d), blk_spec(qi.shape[1]),
                  pl.BlockSpec((IDX_HEADS, tq), lambda b, i: (0, b * nq + i)),
                  pl.BlockSpec((seq_len, IDX_DH), lambda b, i: (b, 0)),
                  pl.BlockSpec((ATT_KV_HEADS, seq_len, dh), lambda b, i: (0, b, 0)),
                  pl.BlockSpec((nq // 2,) + vt.shape[1:], lambda b, i: (b, 0, 0)),
                  pl.BlockSpec(bias_tab.shape, lambda b, i: (0, 0, 0, 0)),
                  pl.BlockSpec((tq, tq), lambda b, i: (0, 0))],
        out_specs=blk_spec(d),
        out_shape=jax.ShapeDtypeStruct((n, d), BF16),
        scratch_shapes=[pltpu.VMEM((nq, tq, tq), I32),
                        pltpu.VMEM((nq, tq, tq), F32),
                        pltpu.VMEM((IDX_HEADS * tq, IDX_DH), BF16),
                        pltpu.VMEM((ATT_KV_HEADS, g_heads * tq, dh), BF16),
                        *([pltpu.VMEM((nq // 2, 2 * tq, g_heads * tq), F32)] * ATT_KV_HEADS),
                        pltpu.VMEM((d, tq), F32)],
        compiler_params=_cparams("parallel", "arbitrary"),
        name="dsa_prompt",
    )(q, qi, wit, kib, kg, vt, bias_tab, jnp.asarray(tril, BF16))


def _mix_route_kernel(o_ref, wo_ref, x_ref, gt_ref, g_ref, sh_ref, sc_ref, wr_ref, rb_ref, tri_ref,
                      xo_ref, hp_ref, eidx_ref, wsel_ref, pos_ref, cnt_ref, carry_ref, *, n_exp):
    @pl.when(pl.program_id(0) == 0)
    def _():
        carry_ref[...] = jnp.zeros_like(carry_ref)

    x_new = x_ref[...] + gt_ref[...] * _bdot(o_ref[...], wo_ref[...])
    xo_ref[...] = x_new
    h = _rms(x_new, g_ref[...]) * (1.0 + sc_ref[...]) + sh_ref[...]
    _rows_store(hp_ref, _pack_rows(h))

    h_hi, h_mid, h_lo = _split3(h)
    w_hi, w_mid, w_lo = _split3(wr_ref[...])
    nt = lambda a, b: lax.dot_general(a, b, (((1,), (1,)), ((), ())), preferred_element_type=F32)
    logits = ((nt(w_lo, h_hi) + nt(w_hi, h_lo) + nt(w_mid, h_mid))
              + (nt(w_hi, h_mid) + nt(w_mid, h_hi))) + nt(w_hi, h_hi)
    s = _sigmoid(logits)
    sel = s + rb_ref[...]
    tm = sel.shape[1]
    gsz = n_exp // N_GROUPS
    io_g = lax.broadcasted_iota(I32, (gsz, tm), 0)

    gs = []
    for g in range(N_GROUPS):
        grp = sel[g * gsz:(g + 1) * gsz, :]
        m1 = jnp.max(grp, axis=0, keepdims=True)
        i1 = jnp.min(jnp.where(grp == m1, io_g, gsz), axis=0, keepdims=True)
        m2 = jnp.max(jnp.where(io_g == i1, NEG_INF, grp), axis=0, keepdims=True)
        gs.append(m1 + m2)
    masked = []
    for g in range(N_GROUPS):
        rank = jnp.zeros((1, tm), F32)
        for o in range(N_GROUPS):
            if o == g:
                continue
            ahead = (gs[o] >= gs[g]) if o < g else (gs[o] > gs[g])
            rank = rank + jnp.where(ahead, 1.0, 0.0)
        keep = jnp.where(rank < TOPK_GROUPS, 0.0, NEG_INF)
        masked.append(sel[g * gsz:(g + 1) * gsz, :] + keep)
    msel = jnp.concatenate(masked, axis=0)

    io_e = lax.broadcasted_iota(I32, (n_exp, tm), 0)
    chosen = jnp.zeros((n_exp, tm), F32)
    picks, weights = [], []
    for _ in range(TOP_K):
        m = jnp.max(msel, axis=0, keepdims=True)
        ei = jnp.min(jnp.where(msel == m, io_e, n_exp), axis=0, keepdims=True)
        pick = io_e == ei
        weights.append(jnp.sum(jnp.where(pick, s, 0.0), axis=0, keepdims=True))
        picks.append(ei)
        chosen = jnp.where(pick, 1.0, chosen)
        msel = jnp.where(pick, NEG_INF, msel)
    wsum = weights[0]
    for w in weights[1:]:
        wsum = wsum + w

    rank_in_expert = carry_ref[...] + jnp.dot(chosen.astype(BF16), tri_ref[...], preferred_element_type=F32)
    carry_new = carry_ref[...] + jnp.sum(chosen, axis=1, keepdims=True)
    carry_ref[...] = carry_new
    cnt_ref[...] = carry_new
    for kk in range(TOP_K):
        eidx_ref[kk:kk + 1, :] = picks[kk]
        wsel_ref[kk:kk + 1, :] = weights[kk] / wsum * ROUTE_SCALE
        pk = jnp.sum(jnp.where(io_e == picks[kk], rank_in_expert, 0.0), axis=0, keepdims=True)
        pos_ref[kk:kk + 1, :] = pk.astype(I32)


def mix_out_and_route(o, w_out, x, mod, norm_g, w_router, router_bias, tm, seq_len):
    n, d = x.shape
    kdim = o.shape[1]
    n_exp = w_router.shape[1]
    per = d // 2 // LANES
    tri = np.triu(np.ones((tm, tm), np.float32), 1)
    row8 = lambda dt: jax.ShapeDtypeStruct((TOP_K, n), dt)
    tok8 = pl.BlockSpec((TOP_K, tm), lambda i: (0, i))
    return pl.pallas_call(
        functools.partial(_mix_route_kernel, n_exp=n_exp),
        grid=(n // tm,),
        in_specs=[pl.BlockSpec((tm, kdim), lambda i: (i, 0)),
                  pl.BlockSpec((kdim, d), lambda i: (0, 0)),
                  pl.BlockSpec((tm, d), lambda i: (i, 0)),
                  _mod_spec(mod, 2, tm, seq_len, d),
                  pl.BlockSpec((1, d), lambda i: (0, 0)),
                  _mod_spec(mod, 3, tm, seq_len, d),
                  _mod_spec(mod, 4, tm, seq_len, d),
                  pl.BlockSpec((n_exp, d), lambda i: (0, 0)),
                  pl.BlockSpec((n_exp, 1), lambda i: (0, 0)),
                  pl.BlockSpec((tm, tm), lambda i: (0, 0))],
        out_specs=[pl.BlockSpec((tm, d), lambda i: (i, 0)),
                   pl.BlockSpec((tm * per, LANES), lambda i: (i, 0)),
                   tok8, tok8, tok8,
                   pl.BlockSpec((n_exp, 1), lambda i: (0, 0))],
        out_shape=[jax.ShapeDtypeStruct((n, d), F32), jax.ShapeDtypeStruct((n * per, LANES), I32),
                   row8(I32), row8(F32), row8(I32), jax.ShapeDtypeStruct((n_exp, 1), F32)],
        scratch_shapes=[pltpu.VMEM((n_exp, 1), F32)],
        compiler_params=_cparams("arbitrary"),
        name="mix_out_route",
    )(o, w_out.astype(BF16), x, mod, norm_g.reshape(1, d), mod, mod,
      w_router.T, router_bias.reshape(n_exp, 1), jnp.asarray(tri, BF16))


def _dispatch_kernel(dest_ref, h_ref, xs_ref, sem, *, per):
    tm = h_ref.shape[0] // per

    def row_copy(t, dst_row):
        return pltpu.make_async_copy(_row_tile(h_ref, t, per), _row_tile(xs_ref, dst_row, per), sem)

    def issue(t, c):
        for kk in range(TOP_K):
            row_copy(t, dest_ref[kk, t]).start()
        return c

    def drain(t, c):
        for kk in range(TOP_K):
            row_copy(t, dest_ref[kk, t]).wait()
        return c

    lax.fori_loop(0, tm, issue, 0)
    lax.fori_loop(0, tm, drain, 0)


def moe_dispatch(h, dest, tm):
    n = dest.shape[1]
    per = h.shape[0] // n
    return pl.pallas_call(
        functools.partial(_dispatch_kernel, per=per),
        grid=(n // tm,),
        in_specs=[pl.BlockSpec((TOP_K, tm), lambda i: (0, i), memory_space=pltpu.SMEM),
                  pl.BlockSpec((tm * per, LANES), lambda i: (i, 0))],
        out_specs=pl.BlockSpec(memory_space=pl.ANY),
        out_shape=jax.ShapeDtypeStruct((n * TOP_K * per, LANES), h.dtype),
        scratch_shapes=[pltpu.SemaphoreType.DMA(())],
        compiler_params=_cparams("arbitrary"),
        name="moe_dispatch",
    )(dest, h)


def _expert_kernel(blk_ref, exp_ref, lo_ref, hi_ref, x_ref, wg_ref, wu_ref, wd_ref, y_ref,
                   wgb_ref, wub_ref, wdb_ref):
    j = pl.program_id(0)
    per = wg_ref.shape[0] // 2 // LANES
    blk = x_ref.shape[0] // per
    lo = lo_ref[j] - blk_ref[j] * blk
    hi = hi_ref[j] - blk_ref[j] * blk

    @pl.when(jnp.logical_or(j == 0, exp_ref[j] != exp_ref[jnp.maximum(j - 1, 0)]))
    def _():
        wgb_ref[...] = wg_ref[...].astype(BF16)
        wub_ref[...] = wu_ref[...].astype(BF16)
        wdb_ref[...] = wd_ref[...].astype(BF16)

    @pl.when(hi > lo)
    def _():
        x = _unpack_rows(_rows_load(x_ref, per)).astype(BF16)
        g = jnp.dot(x, wgb_ref[...], preferred_element_type=F32)
        u = jnp.dot(x, wub_ref[...], preferred_element_type=F32)
        y = jnp.dot((_silu(g) * u).astype(BF16), wdb_ref[...], preferred_element_type=F32)

        @pl.when(jnp.logical_and(lo == 0, hi == blk))
        def _():
            _rows_store(y_ref, _pack_rows(y))

        @pl.when(jnp.logical_not(jnp.logical_and(lo == 0, hi == blk)))
        def _():
            row = lax.broadcasted_iota(I32, y.shape, 0)
            ym = jnp.where(jnp.logical_and(row >= lo, row < hi), y, 0.0)

            @pl.when(lo == 0)
            def _():
                _rows_store(y_ref, _pack_rows(ym))

            @pl.when(lo != 0)
            def _():
                _rows_store(y_ref, _pack_rows(_unpack_rows(_rows_load(y_ref, per)) + ym))


def moe_experts(xs, seg_blk, seg_exp, seg_lo, seg_hi, w_gate, w_up, w_down, layer, blk):
    d, ff = w_gate.shape[2:]
    wspec = lambda shape: pl.BlockSpec((None, None) + shape, lambda j, sb, se, lo, hi: (layer, se[j], 0, 0))
    rows = pl.BlockSpec((blk * (d // 2 // LANES), LANES), lambda j, sb, se, lo, hi: (sb[j], 0))
    return pl.pallas_call(
        _expert_kernel,
        grid_spec=pltpu.PrefetchScalarGridSpec(
            num_scalar_prefetch=4, grid=(seg_blk.shape[0],),
            in_specs=[rows, wspec((d, ff)), wspec((d, ff)), wspec((ff, d))],
            out_specs=rows,
            scratch_shapes=[pltpu.VMEM((d, ff), BF16), pltpu.VMEM((d, ff), BF16), pltpu.VMEM((ff, d), BF16)]),
        out_shape=jax.ShapeDtypeStruct(xs.shape, xs.dtype),
        compiler_params=_cparams("arbitrary"),
        name="moe_experts",
    )(seg_blk, seg_exp, seg_lo, seg_hi, xs, w_gate, w_up, w_down)


def _combine_kernel(dest_ref, dnext_ref, ys_ref, wsel_ref, h_ref, x_ref, gt_ref, sg_ref, su_ref, sd_ref, nf_ref,
                    o_ref, buf_ref, sems, *, final_norm):
    i = pl.program_id(0)
    tm, d = x_ref.shape
    per = d // 2 // LANES

    def row_copy(idx_ref, slot, t, kk):
        return pltpu.make_async_copy(_row_tile(ys_ref, idx_ref[kk, t], per),
                                     _row_tile(buf_ref.at[slot], kk * tm + t, per), sems.at[slot])

    def start_rows(idx_ref, slot, t0, t1):
        def issue(t, c):
            for kk in range(TOP_K):
                row_copy(idx_ref, slot, t, kk).start()
            return c
        lax.fori_loop(t0, t1, issue, 0)

    slot = i % 2
    has_next = i + 1 < pl.num_programs(0)
    quarter = tm // 4

    def start_next(part):
        @pl.when(has_next)
        def _():
            start_rows(dnext_ref, 1 - slot, part * quarter, tm if part == 3 else (part + 1) * quarter)

    @pl.when(i == 0)
    def _():
        start_rows(dest_ref, 0, 0, tm)

    start_next(0)
    h = _unpack_rows(_rows_load(h_ref, per)).astype(BF16)
    g = jnp.dot(h, sg_ref[...], preferred_element_type=F32)
    u = jnp.dot(h, su_ref[...], preferred_element_type=F32)
    start_next(1)
    y = _bdot(_silu(g) * u, sd_ref[...])

    def drain(t, c):
        for kk in range(TOP_K):
            row_copy(dest_ref, slot, t, kk).wait()
        return c

    lax.fori_loop(0, tm, drain, 0)
    w = wsel_ref[...]
    rows = buf_ref.at[slot]
    for kk in range(TOP_K):
        if kk in (0, TOP_K // 2):
            start_next(2 + kk // (TOP_K // 2))
        y = y + w[:, kk:kk + 1] * _unpack_rows(_rows_load(rows, per, first=kk * tm, rows=tm))
    x_new = x_ref[...] + gt_ref[...] * y
    o_ref[...] = _rms(x_new, nf_ref[...]) if final_norm else x_new


def moe_combine(ys, dest, wsel_t, h, x, mod, ws_gate, ws_up, ws_down, norm_final, final_norm, tm, seq_len):
    n, d = x.shape
    ff = ws_gate.shape[1]
    last = n // tm - 1
    return pl.pallas_call(
        functools.partial(_combine_kernel, final_norm=final_norm),
        grid=(n // tm,),
        in_specs=[pl.BlockSpec((TOP_K, tm), lambda i: (0, i), memory_space=pltpu.SMEM),
                  pl.BlockSpec((TOP_K, tm), lambda i: (0, jnp.minimum(i + 1, last)), memory_space=pltpu.SMEM),
                  pl.BlockSpec(memory_space=pl.ANY),
                  pl.BlockSpec((tm, TOP_K), lambda i: (i, 0)),
                  pl.BlockSpec((tm * (d // 2 // LANES), LANES), lambda i: (i, 0)),
                  pl.BlockSpec((tm, d), lambda i: (i, 0)),
                  _mod_spec(mod, 5, tm, seq_len, d),
                  pl.BlockSpec((d, ff), lambda i: (0, 0)),
                  pl.BlockSpec((d, ff), lambda i: (0, 0)),
                  pl.BlockSpec((ff, d), lambda i: (0, 0)),
                  pl.BlockSpec((1, d), lambda i: (0, 0))],
        out_specs=pl.BlockSpec((tm, d), lambda i: (i, 0)),
        out_shape=jax.ShapeDtypeStruct((n, d), F32),
        scratch_shapes=[pltpu.VMEM((2, TOP_K * tm * (d // 2 // LANES), LANES), I32),
                        pltpu.SemaphoreType.DMA((2,))],
        compiler_params=_cparams("arbitrary"),
        name="moe_combine",
    )(dest, dest, ys, wsel_t, h, x, mod, ws_gate.astype(BF16), ws_up.astype(BF16), ws_down.astype(BF16),
      norm_final.reshape(1, d))


def _slot_kernel(start_ref, eidx_ref, pos_ref, dest_ref):
    e = eidx_ref[...]
    base = jnp.zeros(e.shape, I32)
    for ex in range(start_ref.shape[0]):
        base = jnp.where(e == ex, start_ref[ex], base)
    dest_ref[...] = base + pos_ref[...]


def moe_slots(start, eidx, pos, tm):
    n = eidx.shape[1]
    spec = pl.BlockSpec((TOP_K, tm), lambda i: (0, i))
    return pl.pallas_call(
        _slot_kernel,
        grid=(n // tm,),
        in_specs=[pl.BlockSpec(memory_space=pltpu.SMEM), spec, spec],
        out_specs=spec,
        out_shape=jax.ShapeDtypeStruct(eidx.shape, I32),
        compiler_params=_cparams("parallel"),
        name="moe_slots",
    )(start, eidx, pos)


def mixer_out_and_moe(o, w_out, x, mod, norm_g, w_router, router_bias, w_gate, w_up, w_down, layer,
                      ws_gate, ws_up, ws_down, norm_final, final_norm, tm, seq_len, blk):
    n, d = x.shape
    n_exp = w_router.shape[1]
    x, h, eidx, wsel, pos, counts = mix_out_and_route(o, w_out, x, mod, norm_g, w_router, router_bias, tm, seq_len)
    n_rows = n * TOP_K
    n_blocks = n_rows // blk
    counts = counts.reshape(n_exp).astype(I32)
    end = jnp.cumsum(counts)
    start = end - counts
    dest = moe_slots(start, eidx, pos, min(tm, n))
    seg_lo = jnp.sort(jnp.concatenate([jnp.arange(n_blocks, dtype=I32) * blk, start[1:]]))
    seg_hi = jnp.concatenate([seg_lo[1:], jnp.full((1,), n_rows, I32)])
    seg_blk = jnp.minimum(seg_lo // blk, n_blocks - 1)
    seg_exp = jnp.minimum(jnp.sum((end[None, :] <= seg_lo[:, None]).astype(I32), axis=1), n_exp - 1)
    xs = moe_dispatch(h, dest, tm)
    ys = moe_experts(xs, seg_blk, seg_exp, seg_lo, seg_hi, w_gate, w_up, w_down, layer, blk)
    return moe_combine(ys, dest, wsel.T, h, x, mod, ws_gate, ws_up, ws_down, norm_final, final_norm,
                       min(tm, 128), seq_len)


def _sample_score_kernel(pt_ref, qi_ref, w_ref, *refs):
    page_refs, o_ref = refs[:-1], refs[-1]
    kp = jnp.concatenate([r[...] for r in page_refs], axis=1)
    lg = _bdot(qi_ref[...], kp)
    sc = jnp.sum(jnp.maximum(lg, 0.0) * w_ref[...], axis=0, keepdims=True)
    page = page_refs[0].shape[1]
    for p in range(len(page_refs)):
        o_ref[p:p + 1, :] = sc[:, p * page:(p + 1) * page]


def _page_specs(layer, pg, block):
    zeros = (0,) * (len(block) - 2)
    return [pl.BlockSpec(block, lambda b, g, pt, p=p: (layer, pt[b, g * pg + p]) + zeros) for p in range(pg)]


def dsa_sample_scores(qi3, wi3, cache_ki_t, layer, page_table, pg):
    b, n_pages = page_table.shape
    page = cache_ki_t.shape[3]
    return pl.pallas_call(
        _sample_score_kernel,
        grid_spec=pltpu.PrefetchScalarGridSpec(
            num_scalar_prefetch=1, grid=(b, n_pages // pg),
            in_specs=[pl.BlockSpec((None, IDX_HEADS, IDX_DH), lambda b, g, pt: (b, 0, 0)),
                      pl.BlockSpec((None, IDX_HEADS, 1), lambda b, g, pt: (b, 0, 0))]
                     + _page_specs(layer, pg, (None, None, IDX_DH, page)),
            out_specs=pl.BlockSpec((None, pg, page), lambda b, g, pt: (b, g, 0))),
        out_shape=jax.ShapeDtypeStruct((b, n_pages, page), F32),
        compiler_params=_cparams("parallel", "arbitrary"),
        name="dsa_sample_scores",
    )(page_table, qi3, wi3, *([cache_ki_t] * pg))


def _sample_select_kernel(sc_ref, qi_ref, w_ref, kin_ref, triu_ref, tril_ref, selb_ref, selbn_ref, *, topk):
    nb = sc_ref.shape[0]
    total = lambda a: jnp.sum(jnp.sum(a, axis=0, keepdims=True), axis=1, keepdims=True)
    keys, keys_new = [], []
    for b in range(nb):
        lg_new = jnp.sum(qi_ref[b].astype(F32) * kin_ref[b].astype(BF16).astype(F32),
                         axis=1, keepdims=True)
        s_new = jnp.sum(jnp.maximum(lg_new, 0.0) * w_ref[b], axis=0, keepdims=True)
        keys.append(_order_key(sc_ref[b]))
        keys_new.append(_order_key(s_new))

    def count(b, pred):
        return total(jnp.where(pred(keys[b]), 1.0, 0.0)) + jnp.where(pred(keys_new[b]), 1.0, 0.0)

    def bits_body(it, answers):
        shift = 30 - 2 * it
        out = []
        for b, ans in enumerate(answers):
            cands = [ans | (jnp.int32(v) << shift) for v in (3, 2, 1)]
            c3, c2, c1 = [count(b, lambda kj, c=c: kj >= (c ^ INT_MIN)) for c in cands]
            out.append(jnp.where(c3 >= topk, cands[0],
                                 jnp.where(c2 >= topk, cands[1], jnp.where(c1 >= topk, cands[2], ans))))
        return tuple(out)

    answers = lax.fori_loop(0, 16, bits_body, (jnp.zeros((1, 1), I32),) * nb)
    for b in range(nb):
        thr = answers[b] ^ INT_MIN
        need = topk - count(b, lambda kj: kj > thr)
        tie = keys[b] == thr
        tie_f = jnp.where(tie, 1.0, 0.0)
        tie_b = tie_f.astype(BF16)
        in_row = jnp.dot(tie_b, triu_ref[...], preferred_element_type=F32)
        rows_before = jnp.sum(jnp.dot(tril_ref[...], tie_b, preferred_element_type=F32), axis=1, keepdims=True)
        before = in_row + rows_before
        selb_ref[b] = jnp.where(keys[b] > thr, 0.0,
                                jnp.where(tie, jnp.where(before < need, 0.0, NEG_INF), NEG_INF))
        selbn_ref[b] = jnp.where(keys_new[b] > thr, 0.0,
                                 jnp.where(keys_new[b] == thr,
                                           jnp.where(total(tie_f) < need, 0.0, NEG_INF), NEG_INF))


def dsa_sample_select(scores, qi3, wi3, ki_new, topk):
    b, n_pages, page = scores.shape
    nb = 8 if b % 8 == 0 else 1
    triu = np.triu(np.ones((page, page), np.float32), 1)
    tril = np.tril(np.ones((n_pages, n_pages), np.float32), -1)
    per_b = lambda *tail: pl.BlockSpec((nb,) + tail, lambda i: (i,) + (0,) * len(tail))
    return pl.pallas_call(
        functools.partial(_sample_select_kernel, topk=float(topk)),
        grid=(b // nb,),
        in_specs=[per_b(n_pages, page), per_b(IDX_HEADS, IDX_DH), per_b(IDX_HEADS, 1), per_b(1, IDX_DH),
                  pl.BlockSpec((page, page), lambda i: (0, 0)),
                  pl.BlockSpec((n_pages, n_pages), lambda i: (0, 0))],
        out_specs=[per_b(n_pages, page), per_b(1, 1)],
        out_shape=[jax.ShapeDtypeStruct((b, n_pages, page), F32), jax.ShapeDtypeStruct((b, 1, 1), F32)],
        compiler_params=_cparams("parallel"),
        name="dsa_sample_select",
    )(scores, qi3, wi3, ki_new, jnp.asarray(triu, BF16), jnp.asarray(tril, BF16))


def _sample_attend_kernel(pt_ref, q_ref, selb_ref, selbn_ref, kn_ref, vn_ref, blast_ref, bfar_ref, bnew_ref,
                          *refs, pg, n_pages):
    k_refs, v_refs = refs[:pg], refs[pg:2 * pg]
    o_ref, m_ref, l_ref, acc_ref = refs[2 * pg:]
    step = pl.program_id(1)
    g_heads = ATT_HEADS // ATT_KV_HEADS
    page = k_refs[0].shape[2]
    dh = q_ref.shape[1]
    head_group = lax.broadcasted_iota(I32, (ATT_HEADS, 1), 0) // g_heads

    @pl.when(step == 0)
    def _():
        m_ref[...] = jnp.full(m_ref.shape, NEG_INF, F32)
        l_ref[...] = jnp.zeros(l_ref.shape, F32)
        acc_ref[...] = jnp.zeros(acc_ref.shape, F32)

    q = q_ref[...]
    s = jnp.zeros((ATT_HEADS, pg * page), F32)
    for g in range(ATT_KV_HEADS):
        kg = jnp.concatenate([r[g] for r in k_refs], axis=1)
        s = jnp.where(head_group == g, _bdot(q, kg), s)
    add = [jnp.where(step * pg + p == n_pages - 1, blast_ref[...], bfar_ref[...]) + selb_ref[p:p + 1, :]
           for p in range(pg)]
    s = s + jnp.concatenate(add, axis=1)
    m_old = m_ref[...]
    m_new = jnp.maximum(m_old, jnp.max(s, axis=1, keepdims=True))
    m_safe = jnp.where(m_new == NEG_INF, 0.0, m_new)
    alpha = jnp.exp(m_old - m_safe)
    p_ = jnp.exp(s - m_safe)
    pv = jnp.zeros((ATT_HEADS, dh), F32)
    for g in range(ATT_KV_HEADS):
        vg = jnp.concatenate([r[g] for r in v_refs], axis=1)
        pv = jnp.where(head_group == g, _bdot_nt(p_, vg), pv)
    l_new = alpha * l_ref[...] + jnp.sum(p_, axis=1, keepdims=True)
    acc_new = alpha * acc_ref[...] + pv
    m_ref[...] = m_new
    l_ref[...] = l_new
    acc_ref[...] = acc_new

    @pl.when(step == pl.num_programs(1) - 1)
    def _():
        kn = jnp.zeros((ATT_HEADS, dh), F32)
        vn = jnp.zeros((ATT_HEADS, dh), F32)
        for g in range(ATT_KV_HEADS):
            kn = jnp.where(head_group == g, kn_ref[g:g + 1, :], kn)
            vn = jnp.where(head_group == g, vn_ref[g:g + 1, :], vn)
        qk = q.astype(F32) * kn.astype(BF16).astype(F32)
        s_n = jnp.sum(qk, axis=1, keepdims=True) + bnew_ref[...] + selbn_ref[...]
        m_fin = jnp.maximum(m_new, s_n)
        m_fs = jnp.where(m_fin == NEG_INF, 0.0, m_fin)
        a2 = jnp.exp(m_new - m_fs)
        p_n = jnp.exp(s_n - m_fs)
        o_ref[...] = ((a2 * acc_new + p_n * vn) / (a2 * l_new + p_n)).astype(o_ref.dtype)


def dsa_sample_attend(q3, selb, selb_new, k_new, v_new, bias_tab, cache_k_t, cache_v_t, layer, page_table, pg):
    b, n_pages = page_table.shape
    kvh, dh, page = cache_k_t.shape[2:]
    assert page == bias_tab.shape[2] and page >= MAX_DISTANCE
    b_last = bias_tab[:, 1, :, 0]
    b_far = bias_tab[:, 2, 0, :1]
    b_new = bias_tab[:, 0, 0, :1]
    full = lambda shape: pl.BlockSpec(shape, lambda b, g, pt: (0,) * len(shape))
    per_b = lambda shape: pl.BlockSpec((None,) + shape, lambda b, g, pt: (b,) + (0,) * len(shape))
    return pl.pallas_call(
        functools.partial(_sample_attend_kernel, pg=pg, n_pages=n_pages),
        grid_spec=pltpu.PrefetchScalarGridSpec(
            num_scalar_prefetch=1, grid=(b, n_pages // pg),
            in_specs=[per_b((ATT_HEADS, dh)),
                      pl.BlockSpec((None, pg, page), lambda b, g, pt: (b, g, 0)),
                      per_b((1, 1)), per_b((kvh, dh)), per_b((kvh, dh)),
                      full((ATT_HEADS, page)), full((ATT_HEADS, 1)), full((ATT_HEADS, 1))]
                     + _page_specs(layer, pg, (None, None, kvh, dh, page))
                     + _page_specs(layer, pg, (None, None, kvh, dh, page)),
            out_specs=per_b((ATT_HEADS, dh)),
            scratch_shapes=[pltpu.VMEM((ATT_HEADS, 1), F32), pltpu.VMEM((ATT_HEADS, 1), F32),
                            pltpu.VMEM((ATT_HEADS, dh), F32)]),
        out_shape=jax.ShapeDtypeStruct((b, ATT_HEADS, dh), BF16),
        compiler_params=_cparams("parallel", "arbitrary"),
        name="dsa_sample_attend",
    )(page_table, q3, selb, selb_new, k_new, v_new, b_last, b_far, b_new,
      *([cache_k_t] * pg), *([cache_v_t] * pg))


def dsa_sample(q, k, v, qi, ki, wit, cache_k, cache_v, cache_ki, layer, page_table, bias_tab):
    b, d = q.shape
    dh = d // ATT_HEADS
    n_pages = page_table.shape[1]
    page = cache_k.shape[2]
    pg = 16 if n_pages % 16 == 0 else n_pages
    topk = min(TOPK_MAX, (n_pages * page + 1) // 4)
    qi3 = qi.reshape(b, IDX_HEADS, IDX_DH)
    wi3 = wit.T.reshape(b, IDX_HEADS, 1)
    cache_ki_t = jnp.transpose(cache_ki, (0, 1, 3, 2))
    cache_k_t = jnp.transpose(cache_k, (0, 1, 3, 4, 2))
    cache_v_t = jnp.transpose(cache_v, (0, 1, 3, 4, 2))
    scores = dsa_sample_scores(qi3, wi3, cache_ki_t, layer, page_table, 2 * pg if n_pages % (2 * pg) == 0 else pg)
    selb, selb_new = dsa_sample_select(scores, qi3, wi3, ki.reshape(b, 1, IDX_DH), topk)
    o = dsa_sample_attend(q.reshape(b, ATT_HEADS, dh), selb, selb_new, k.reshape(b, ATT_KV_HEADS, dh),
                          v.reshape(b, ATT_KV_HEADS, dh), bias_tab, cache_k_t, cache_v_t, layer, page_table, pg)
    return o.reshape(b, d)


def kernel(x_prompt, x_sample, c_prompt, c_sample, state_gla, cache_k, cache_v, cache_idx_k, page_table,
           rel_bias, w_ada, b_ada, norm_mix, norm_ffn, norm_final,
           gla_w_in, gla_w_g2, gla_b_g2, gla_gnorm, gla_w_out,
           dsa_w_in, dsa_idx_ln_g, dsa_idx_ln_b, dsa_w_out,
           moe_w_router, moe_router_bias, moe_w_gate, moe_w_up, moe_w_down,
           shared_w_gate, shared_w_up, shared_w_down):
    bp, t, d = x_prompt.shape
    bs = x_sample.shape[0]
    depth = w_ada.shape[0]
    dh = d // ATT_HEADS
    tm_p = min(256, t)
    tq = 128
    mods = ada_mod_all(jnp.concatenate([c_prompt, c_sample], axis=0), w_ada, b_ada)
    xp = x_prompt.reshape(bp * t, d)
    xs = x_sample.reshape(bs, d)
    bias_tab = bias_tables(rel_bias, tq)
    gla_p, gla_s, kp_l, vp_l, kip_l, ks_l, vs_l, kis_l = [], [], [], [], [], [], [], []
    for i in range(depth):
        mod_p = mods[i, :bp].reshape(bp, 1, 6 * d)
        mod_s = mods[i, bp:]
        j = i // 2
        if i % 2 == 0:
            qkvr_p, la_p = gla_project(xp, mod_p, norm_mix[i], gla_w_in[j], gla_w_g2[j], gla_b_g2[j], tm_p, t)
            o_p, s_p = gla_chunked(qkvr_p, la_p, gla_gnorm[j], bp, t)
            qkvr_s, la_s = gla_project(xs, mod_s, norm_mix[i], gla_w_in[j], gla_w_g2[j], gla_b_g2[j], bs, 1)
            o_s, s_s = gla_step(qkvr_s, la_s, state_gla[j], gla_gnorm[j])
            gla_p.append(s_p)
            gla_s.append(s_s)
            w_out = gla_w_out[j]
        else:
            dp = (dsa_w_in[j], dsa_idx_ln_g[j], dsa_idx_ln_b[j])
            q_p, qi_p, wit_p, kt_p, vt32_p, kit_p, kib_p, kg_p, vt_p = dsa_project(
                xp, mod_p, norm_mix[i], *dp, tm_p, t, True)
            o_p = dsa_prompt(q_p, qi_p, wit_p, kib_p, kg_p, vt_p, bias_tab, bp, t, tq)
            q_s, qi_s, wit_s, k_s, v_s, ki_s = dsa_project(xs, mod_s, norm_mix[i], *dp, bs, 1, False)
            o_s = dsa_sample(q_s, k_s, v_s, qi_s, ki_s, wit_s, cache_k, cache_v, cache_idx_k, j,
                             page_table, bias_tab)
            kp_l.append(kt_p.reshape(bp, ATT_KV_HEADS, dh, t).transpose(0, 3, 1, 2))
            vp_l.append(vt32_p.reshape(bp, ATT_KV_HEADS, dh, t).transpose(0, 3, 1, 2))
            kip_l.append(kit_p.transpose(0, 2, 1))
            ks_l.append(k_s.reshape(bs, 1, ATT_KV_HEADS, dh))
            vs_l.append(v_s.reshape(bs, 1, ATT_KV_HEADS, dh))
            kis_l.append(ki_s.reshape(bs, 1, IDX_DH))
            w_out = dsa_w_out[j]
        last = i == depth - 1
        mo = (norm_ffn[i], moe_w_router[i], moe_router_bias[i], moe_w_gate, moe_w_up, moe_w_down, i,
              shared_w_gate[i], shared_w_up[i], shared_w_down[i], norm_final, last)
        xp = mixer_out_and_moe(o_p, w_out, xp, mod_p, *mo, tm_p, t, 512)
        xs = mixer_out_and_moe(o_s, w_out, xs, mod_s, *mo, bs, 1, 32)
    return (xp.reshape(bp, t, d), xs.reshape(bs, 1, d), jnp.stack(gla_p), jnp.stack(gla_s),
            jnp.stack(kp_l), jnp.stack(vp_l), jnp.stack(kip_l),
            jnp.stack(ks_l), jnp.stack(vs_l), jnp.stack(kis_l))
```

```python
import functools
import math

import numpy as np
import jax
import jax.numpy as jnp
from jax import lax
from jax.experimental import pallas as pl
from jax.experimental.pallas import tpu as pltpu

F32 = jnp.float32
BF16 = jnp.bfloat16
I32 = jnp.int32

GLA_HEADS = 4
GLA_RANK = 16
GLA_NORMALIZER = 16.0
ATT_HEADS = 16
ATT_KV_HEADS = 4
IDX_HEADS = 8
IDX_DH = 64
TOPK_MAX = 256
N_BUCKETS = 32
MAX_DISTANCE = 128
N_GROUPS = 8
TOPK_GROUPS = 4
TOP_K = 8
ROUTE_SCALE = 2.5
EPS = 1e-6

LANES = 128
VMEM_LIMIT = 56 * 1024 * 1024
NEG_INF = float("-inf")
INT_MIN = -2 ** 31


def _cparams(*sem):
    return pltpu.CompilerParams(dimension_semantics=sem, vmem_limit_bytes=VMEM_LIMIT)


def _bdot(a, b):
    return jnp.dot(a.astype(BF16), b.astype(BF16), preferred_element_type=F32)


def _bdot_nt(a, b):
    return lax.dot_general(a.astype(BF16), b.astype(BF16), (((1,), (1,)), ((), ())),
                           preferred_element_type=F32)


def _bdot_tn(a, b):
    return lax.dot_general(a.astype(BF16), b.astype(BF16), (((0,), (0,)), ((), ())),
                           preferred_element_type=F32)


def _split3(a):
    hi = a.astype(BF16)
    r1 = a - hi.astype(F32)
    mid = r1.astype(BF16)
    lo = (r1 - mid.astype(F32)).astype(BF16)
    return hi, mid, lo


def _silu(x):
    return x * (1.0 / (1.0 + jnp.exp(-x)))


def _sigmoid(x):
    return 1.0 / (1.0 + jnp.exp(-x))


def _rms(x, g):
    return x * lax.rsqrt(jnp.mean(x * x, axis=-1, keepdims=True) + EPS) * g


def _ada_kernel(c_ref, w_ref, b_ref, o_ref):
    c_hi, c_mid, _ = _split3(_silu(c_ref[...]))
    w_hi, w_mid, _ = _split3(w_ref[...])
    dot = lambda a, b: jnp.dot(a, b, preferred_element_type=F32)
    o_ref[...] = (dot(c_hi, w_mid) + dot(c_mid, w_hi)) + dot(c_hi, w_hi) + b_ref[...]


def ada_mod_all(c, w_ada, b_ada, tn=512):
    depth, d, n6 = w_ada.shape
    rows = c.shape[0]
    return pl.pallas_call(
        _ada_kernel,
        grid=(depth, n6 // tn),
        in_specs=[pl.BlockSpec((rows, d), lambda l, j: (0, 0)),
                  pl.BlockSpec((None, d, tn), lambda l, j: (l, 0, j)),
                  pl.BlockSpec((None, 1, tn), lambda l, j: (l, 0, j))],
        out_specs=pl.BlockSpec((None, rows, tn), lambda l, j: (l, 0, j)),
        out_shape=jax.ShapeDtypeStruct((depth, rows, n6), F32),
        compiler_params=_cparams("parallel", "parallel"),
        name="ada_mod",
    )(c, w_ada, b_ada.reshape(depth, 1, n6))


def _mod_spec(mod, which, tm, seq_len, d):
    if mod.ndim == 3:
        per_seq = seq_len // tm
        return pl.BlockSpec((None, 1, d), lambda i: (i // per_seq, 0, which))
    return pl.BlockSpec((tm, d), lambda i: (i, which))


def _gla_proj_kernel(x_ref, g_ref, sh_ref, sc_ref, w_ref, wg2_ref, bg2_ref, qkvr_ref, la_ref, *, n_main):
    h = _rms(x_ref[...], g_ref[...]) * (1.0 + sc_ref[...]) + sh_ref[...]
    y = _bdot(h, w_ref[...])
    qkvr_ref[...] = y[:, :n_main]
    g1 = y[:, n_main:n_main + GLA_RANK]
    z = _bdot(g1, wg2_ref[...]) + bg2_ref[...]
    la_ref[...] = (jnp.minimum(z, 0.0) - jnp.log(1.0 + jnp.exp(-jnp.abs(z)))) * (1.0 / GLA_NORMALIZER)


def gla_project(x, mod, norm_g, w_in, w_g2, b_g2, tm, seq_len):
    n, d = x.shape
    hk = w_g2.shape[1]
    n_main = w_in.shape[1] - GLA_RANK
    n_pad = (-w_in.shape[1]) % LANES
    w = jnp.pad(w_in, ((0, 0), (0, n_pad))).astype(BF16)
    return pl.pallas_call(
        functools.partial(_gla_proj_kernel, n_main=n_main),
        grid=(n // tm,),
        in_specs=[pl.BlockSpec((tm, d), lambda i: (i, 0)),
                  pl.BlockSpec((1, d), lambda i: (0, 0)),
                  _mod_spec(mod, 0, tm, seq_len, d),
                  _mod_spec(mod, 1, tm, seq_len, d),
                  pl.BlockSpec(w.shape, lambda i: (0, 0)),
                  pl.BlockSpec(w_g2.shape, lambda i: (0, 0)),
                  pl.BlockSpec((1, hk), lambda i: (0, 0))],
        out_specs=[pl.BlockSpec((tm, n_main), lambda i: (i, 0)),
                   pl.BlockSpec((tm, hk), lambda i: (i, 0))],
        out_shape=[jax.ShapeDtypeStruct((n, n_main), F32),
                   jax.ShapeDtypeStruct((n, hk), F32)],
        compiler_params=_cparams("parallel"),
        name="gla_proj",
    )(x, norm_g.reshape(1, d), mod, mod, w, w_g2.astype(BF16), b_g2.reshape(1, hk))


def _gla_level_matrices(c):
    levels = int(math.log2(c))
    t = np.arange(c)[:, None]
    u = np.arange(c)[None, :]
    mats, masks = [], []
    for l in range(levels):
        m = 1 << l
        ref = (t // (2 * m)) * (2 * m) + m - 1
        right = (t % (2 * m)) >= m
        mat = np.where(right, (u > ref) & (u <= t), (u > t) & (u <= ref))
        mats.append(mat)
        masks.append((t // (2 * m) == u // (2 * m)) & right & ((u % (2 * m)) < m))
    mats.append(u <= t)
    masks.append(t == u)
    return (np.stack(mats).astype(np.float32).reshape(-1, c), np.stack(masks).astype(np.float32))


def _gla_chunk_kernel(q_ref, k_ref, v_ref, r_ref, la_ref, mat_ref, mask_ref, gn_ref,
                      o_ref, s_out_ref, st_ref, *, c, dk):
    ci = pl.program_id(1)
    nlev = mask_ref.shape[0] - 1
    hk = la_ref.shape[1]
    dv = v_ref.shape[1] // GLA_HEADS

    @pl.when(ci == 0)
    def _():
        st_ref[...] = jnp.zeros_like(st_ref)

    la = la_ref[...]
    hi = la.astype(BF16)
    lo = (la - hi.astype(F32)).astype(BF16)
    e_all = jnp.dot(mat_ref[...], jnp.concatenate([hi, lo], axis=1), preferred_element_type=F32)
    e_all = e_all[:, :hk] + e_all[:, hk:]

    for h in range(GLA_HEADS):
        ks, vs = slice(h * dk, (h + 1) * dk), slice(h * dv, (h + 1) * dv)
        q = q_ref[:, ks] * (dk ** -0.5)
        k = k_ref[:, ks]
        v = v_ref[:, vs]
        att = jnp.where(mask_ref[nlev] > 0.0, _bdot_nt(q, k), 0.0)
        for l in range(nlev):
            ex = jnp.exp(e_all[l * c:(l + 1) * c, ks])
            att = att + jnp.where(mask_ref[l] > 0.0, _bdot_nt(q * ex, k * ex), 0.0)

        b = e_all[nlev * c:(nlev + 1) * c, ks]
        b_end = b[c - 1:c, :]
        st = st_ref[h]
        o = _bdot(att, v) + _bdot_nt(q * jnp.exp(b), st)
        st_new = st * jnp.exp(b_end) + _bdot_tn(v, k * jnp.exp(b_end - b))
        st_ref[h] = st_new

        on = o * lax.rsqrt(jnp.mean(o * o, axis=-1, keepdims=True) + EPS) * gn_ref[...]
        o_ref[:, vs] = (on * _silu(r_ref[:, vs])).astype(o_ref.dtype)

        @pl.when(ci == pl.num_programs(1) - 1)
        def _():
            s_out_ref[h] = st_new.T


def gla_chunked(qkvr, log_a, gnorm, batch, seq_len, c=128):
    n = qkvr.shape[0]
    hk = log_a.shape[1]
    dk = hk // GLA_HEADS
    hv = (qkvr.shape[1] - 2 * hk) // 2
    dv = hv // GLA_HEADS
    nc = seq_len // c
    mats, masks = _gla_level_matrices(c)
    row = lambda b, ci: b * nc + ci
    return pl.pallas_call(
        functools.partial(_gla_chunk_kernel, c=c, dk=dk),
        grid=(batch, nc),
        in_specs=[pl.BlockSpec((c, hk), lambda b, ci: (row(b, ci), 0)),
                  pl.BlockSpec((c, hk), lambda b, ci: (row(b, ci), 1)),
                  pl.BlockSpec((c, hv), lambda b, ci: (row(b, ci), 2 * hk // hv)),
                  pl.BlockSpec((c, hv), lambda b, ci: (row(b, ci), 2 * hk // hv + 1)),
                  pl.BlockSpec((c, hk), lambda b, ci: (row(b, ci), 0)),
                  pl.BlockSpec(mats.shape, lambda b, ci: (0, 0)),
                  pl.BlockSpec(masks.shape, lambda b, ci: (0, 0, 0)),
                  pl.BlockSpec((1, dv), lambda b, ci: (0, 0))],
        out_specs=[pl.BlockSpec((c, hv), lambda b, ci: (row(b, ci), 0)),
                   pl.BlockSpec((None, GLA_HEADS, dk, dv), lambda b, ci: (b, 0, 0, 0))],
        out_shape=[jax.ShapeDtypeStruct((n, hv), BF16),
                   jax.ShapeDtypeStruct((batch, GLA_HEADS, dk, dv), F32)],
        scratch_shapes=[pltpu.VMEM((GLA_HEADS, dv, dk), F32)],
        compiler_params=_cparams("parallel", "arbitrary"),
        name="gla_chunked",
    )(qkvr, qkvr, qkvr, qkvr, log_a, jnp.asarray(mats, BF16), jnp.asarray(masks), gnorm.reshape(1, dv))


def _gla_step_kernel(q_ref, k_ref, g_ref, v_ref, r_ref, s_ref, gn_ref, o_ref, s_out_ref, *, dk):
    s_new = jnp.exp(g_ref[...]) * s_ref[...] + k_ref[...] * v_ref[...]
    s_out_ref[...] = s_new
    o = jnp.sum(q_ref[...] * (dk ** -0.5) * s_new, axis=0, keepdims=True)
    on = o * lax.rsqrt(jnp.mean(o * o, axis=-1, keepdims=True) + EPS) * gn_ref[...]
    o_ref[...] = (on * _silu(r_ref[...])).astype(o_ref.dtype)


def gla_step(qkvr, log_a, state, gnorm):
    batch = qkvr.shape[0]
    hk = log_a.shape[1]
    dk = hk // GLA_HEADS
    hv = (qkvr.shape[1] - 2 * hk) // 2
    dv = hv // GLA_HEADS
    col = lambda a: a.reshape(batch, GLA_HEADS, dk, 1)
    rowv = lambda a: a.reshape(batch, GLA_HEADS, 1, dv)
    q, k, v, r = (qkvr[:, :hk], qkvr[:, hk:2 * hk], qkvr[:, 2 * hk:2 * hk + hv], qkvr[:, 2 * hk + hv:])
    cspec = pl.BlockSpec((None, None, dk, 1), lambda b, h: (b, h, 0, 0))
    rspec = pl.BlockSpec((None, None, 1, dv), lambda b, h: (b, h, 0, 0))
    sspec = pl.BlockSpec((None, None, dk, dv), lambda b, h: (b, h, 0, 0))
    o, s_new = pl.pallas_call(
        functools.partial(_gla_step_kernel, dk=dk),
        grid=(batch, GLA_HEADS),
        in_specs=[cspec, cspec, cspec, rspec, rspec, sspec, pl.BlockSpec((1, dv), lambda b, h: (0, 0))],
        out_specs=[rspec, sspec],
        out_shape=[jax.ShapeDtypeStruct((batch, GLA_HEADS, 1, dv), BF16),
                   jax.ShapeDtypeStruct(state.shape, F32)],
        compiler_params=_cparams("parallel", "parallel"),
        name="gla_step",
    )(col(q), col(k), col(log_a), rowv(v), rowv(r), state, gnorm.reshape(1, dv))
    return o.reshape(batch, hv), s_new


HI_MASK = -65536


def _pack_rows(val):
    half = val.shape[1] // 2
    bits = lambda a: pltpu.bitcast(a.astype(BF16).astype(F32), I32)
    return lax.shift_right_logical(bits(val[:, :half]), 16) | (bits(val[:, half:]) & HI_MASK)


def _unpack_rows(words):
    return jnp.concatenate([pltpu.bitcast(words << 16, F32), pltpu.bitcast(words & HI_MASK, F32)], axis=1)


def _rows_load(ref, per, first=0, rows=None):
    rows = ref.shape[0] // per - first if rows is None else rows
    return jnp.concatenate([ref[pl.ds(first * per + j, rows, stride=per), :] for j in range(per)], axis=1)


def _rows_store(ref, words):
    rows = words.shape[0]
    per = words.shape[1] // LANES
    for j in range(per):
        ref[pl.ds(j, rows, stride=per), :] = words[:, j * LANES:(j + 1) * LANES]


def _row_tile(ref, r, per):
    return ref.at[pl.ds(pl.multiple_of(r * per, per), per), :]


def _dsa_proj_kernel(x_ref, g_ref, sh_ref, sc_ref, w_ref, wwi_ref, lg_ref, lb_ref,
                     q_ref, qi_ref, wit_ref, k_ref, v_ref, ki_ref, *seq_refs, cuts, dh):
    h = (_rms(x_ref[...], g_ref[...]) * (1.0 + sc_ref[...]) + sh_ref[...]).astype(BF16)
    y = jnp.dot(h, w_ref[...], preferred_element_type=F32)
    c0, c1, c2, c3, c4 = cuts
    q_ref[...] = (y[:, :c0] * (dh ** -0.5)).astype(BF16)
    qi_ref[...] = (y[:, c2:c3] * (IDX_DH ** -0.5)).astype(BF16)
    wit_ref[...] = _bdot_nt(wwi_ref[...], h) * (IDX_HEADS ** -0.5)
    k = y[:, c0:c1]
    v = y[:, c1:c2]
    ki = y[:, c3:c4]
    mu = jnp.mean(ki, axis=-1, keepdims=True)
    var = jnp.mean((ki - mu) * (ki - mu), axis=-1, keepdims=True)
    kin = (ki - mu) * lax.rsqrt(var + EPS) * lg_ref[...] + lb_ref[...]
    if not seq_refs:
        k_ref[...] = k
        v_ref[...] = v
        ki_ref[...] = kin
    else:
        kib_ref, kg_ref, vt_ref = seq_refs
        vt = v.T
        k_ref[...] = k.T
        v_ref[...] = vt
        ki_ref[...] = jnp.concatenate([kin, jnp.zeros_like(kin)], axis=1).T[:kin.shape[1]]
        kib_ref[...] = kin.astype(BF16)
        for g in range(ATT_KV_HEADS):
            kg_ref[g] = k[:, g * dh:(g + 1) * dh].astype(BF16)
        vt_ref[...] = vt.astype(BF16)


def dsa_project(x, mod, norm_g, w_in, ln_g, ln_b, tm, seq_len, seq_layouts):
    n, d = x.shape
    dh = d // ATT_HEADS
    nkv = ATT_KV_HEADS * dh
    nqi = IDX_HEADS * IDX_DH
    cuts = (d, d + nkv, d + 2 * nkv, d + 2 * nkv + nqi, d + 2 * nkv + nqi + IDX_DH)
    w = jnp.pad(w_in[:, :cuts[4]], ((0, 0), (0, (-cuts[4]) % LANES))).astype(BF16)
    w_wi_t = w_in[:, cuts[4]:cuts[4] + IDX_HEADS].T.astype(BF16)
    row = lambda wd: pl.BlockSpec((tm, wd), lambda i: (i, 0))
    out_specs = [row(d), row(nqi), pl.BlockSpec((IDX_HEADS, tm), lambda i: (0, i))]
    out_shape = [jax.ShapeDtypeStruct((n, d), BF16), jax.ShapeDtypeStruct((n, nqi), BF16),
                 jax.ShapeDtypeStruct((IDX_HEADS, n), F32)]
    if not seq_layouts:
        out_specs += [row(nkv), row(nkv), row(IDX_DH)]
        out_shape += [jax.ShapeDtypeStruct((n, wd), F32) for wd in (nkv, nkv, IDX_DH)]
    else:
        per_seq = seq_len // tm
        tmin = lambda wd: pl.BlockSpec((None, wd, tm), lambda i: (i // per_seq, 0, i % per_seq))
        out_specs += [tmin(nkv), tmin(nkv), tmin(IDX_DH),
                      row(IDX_DH),
                      pl.BlockSpec((ATT_KV_HEADS, tm, dh), lambda i: (0, i, 0)),
                      pl.BlockSpec((None, nkv, tm), lambda i: (i, 0, 0))]
        out_shape += [jax.ShapeDtypeStruct((n // seq_len, wd, seq_len), F32) for wd in (nkv, nkv, IDX_DH)]
        out_shape += [jax.ShapeDtypeStruct((n, IDX_DH), BF16),
                      jax.ShapeDtypeStruct((ATT_KV_HEADS, n, dh), BF16),
                      jax.ShapeDtypeStruct((n // tm, nkv, tm), BF16)]
    return pl.pallas_call(
        functools.partial(_dsa_proj_kernel, cuts=cuts, dh=dh),
        grid=(n // tm,),
        in_specs=[pl.BlockSpec((tm, d), lambda i: (i, 0)),
                  pl.BlockSpec((1, d), lambda i: (0, 0)),
                  _mod_spec(mod, 0, tm, seq_len, d),
                  _mod_spec(mod, 1, tm, seq_len, d),
                  pl.BlockSpec(w.shape, lambda i: (0, 0)),
                  pl.BlockSpec(w_wi_t.shape, lambda i: (0, 0)),
                  pl.BlockSpec((1, IDX_DH), lambda i: (0, 0)),
                  pl.BlockSpec((1, IDX_DH), lambda i: (0, 0))],
        out_specs=out_specs,
        out_shape=out_shape,
        compiler_params=_cparams("parallel"),
        name="dsa_proj",
    )(x, norm_g.reshape(1, d), mod, mod, w, w_wi_t, ln_g.reshape(1, IDX_DH), ln_b.reshape(1, IDX_DH))


def _t5_bucket_np(dist):
    dist = np.maximum(dist, 0)
    max_exact = N_BUCKETS // 2
    ratio = np.log(np.maximum(dist, max_exact).astype(np.float32) / max_exact) / math.log(MAX_DISTANCE / max_exact)
    large = np.minimum(max_exact + (ratio * (N_BUCKETS - max_exact)).astype(np.int32), N_BUCKETS - 1)
    return np.where(dist < max_exact, dist, large).astype(np.int32)


def _bias_table_kernel(rb_ref, bk_ref, o_ref):
    h = pl.program_id(0)
    for kind in range(3):
        bk = bk_ref[kind]
        acc = jnp.zeros(bk.shape, F32)
        for b in range(N_BUCKETS):
            acc = jnp.where(bk == b, rb_ref[b, h], acc)
        o_ref[kind] = acc


def bias_tables(rel_bias, tq):
    s = np.arange(tq)[:, None]
    t = np.arange(tq)[None, :]
    buckets = np.stack([_t5_bucket_np(t - s), _t5_bucket_np(tq + t - s), _t5_bucket_np(2 * tq + t - s)])
    assert tq >= MAX_DISTANCE and (buckets[2] == N_BUCKETS - 1).all()
    return pl.pallas_call(
        _bias_table_kernel,
        grid=(ATT_HEADS,),
        in_specs=[pl.BlockSpec(memory_space=pltpu.SMEM),
                  pl.BlockSpec(buckets.shape, lambda h: (0, 0, 0))],
        out_specs=pl.BlockSpec((None, 3, tq, tq), lambda h: (h, 0, 0, 0)),
        out_shape=jax.ShapeDtypeStruct((ATT_HEADS, 3, tq, tq), F32),
        compiler_params=_cparams("parallel"),
        name="t5_bias_tiles",
    )(rel_bias, jnp.asarray(buckets))


def _order_key(score):
    score = jnp.where(score == 0.0, 0.0, score)
    bits = pltpu.bitcast(score, I32)
    return bits ^ ((bits >> 31) & 0x7FFFFFFF)


def _dsa_prompt_kernel(q_ref, qi_ref, wit_ref, kib_ref, kg_ref, vt_ref, bias_ref, tril_ref, o_ref,
                       key_ref, selb_ref, qis_ref, qg_ref, sc0_ref, sc1_ref, sc2_ref, sc3_ref, ot_ref,
                       *, tq, dh, topk):
    i = pl.program_id(1)
    nk = i + 1
    g_heads = ATT_HEADS // ATT_KV_HEADS
    s_id = lax.broadcasted_iota(I32, (tq, tq), 0)
    t_id = lax.broadcasted_iota(I32, (tq, tq), 1)
    fold8 = lambda a, op: op(a.reshape(a.shape[0] // 8, 8, a.shape[1]), axis=0)

    for h in range(IDX_HEADS):
        qis_ref[h * tq:(h + 1) * tq, :] = qi_ref[:, h * IDX_DH:(h + 1) * IDX_DH]
    for hh in range(ATT_HEADS):
        g, a = divmod(hh, g_heads)
        qg_ref[g, a * tq:(a + 1) * tq, :] = q_ref[:, hh * dh:(hh + 1) * dh]
    w8 = wit_ref[...]

    n_pairs = (nk + 1) // 2

    def score_body(jj, carry):
        for j in (2 * jj, 2 * jj + 1):
            kij = kib_ref[pl.ds(pl.multiple_of(j * tq, tq), tq), :]
            lg = _bdot_nt(kij, qis_ref[...])
            acc = jnp.zeros((tq, tq), F32)
            for h in range(IDX_HEADS):
                acc = acc + jnp.maximum(lg[:, h * tq:(h + 1) * tq], 0.0) * w8[h:h + 1, :]
            valid = (s_id + j * tq) <= (t_id + i * tq)
            key_ref[j] = _order_key(jnp.where(valid, acc, NEG_INF))
        return carry

    lax.fori_loop(0, n_pairs, score_body, 0)

    @pl.when(nk % 2 == 1)
    def _():
        pad = jnp.minimum(nk, key_ref.shape[0] - 1)
        key_ref[pad] = jnp.full((tq, tq), INT_MIN, I32)
        selb_ref[pad] = jnp.full((tq, tq), NEG_INF, F32)

    def counts(preds):
        def body(jj, accs):
            ka, kb = key_ref[2 * jj], key_ref[2 * jj + 1]
            return tuple(acc8 + fold8(jnp.where(p(ka), 1.0, 0.0) + jnp.where(p(kb), 1.0, 0.0), jnp.sum)
                         for p, acc8 in zip(preds, accs))
        accs = lax.fori_loop(0, n_pairs, body, (jnp.zeros((8, tq), F32),) * len(preds))
        return [jnp.sum(a, axis=0, keepdims=True) for a in accs]

    def bits_body(it, ans):
        shift = 30 - 2 * it
        cands = [ans | (jnp.int32(v) << shift) for v in (3, 2, 1)]
        c3, c2, c1 = counts([lambda kj, c=c: kj >= (c ^ INT_MIN) for c in cands])
        return jnp.where(c3 >= topk, cands[0], jnp.where(c2 >= topk, cands[1], jnp.where(c1 >= topk, cands[2], ans)))

    thr = lax.fori_loop(0, 16, bits_body, jnp.zeros((1, tq), I32)) ^ INT_MIN
    n_gt, n_tie = counts([lambda kj: kj > thr, lambda kj: kj == thr])
    need = topk - n_gt

    def causal(j):
        return (s_id + j * tq) <= (t_id + i * tq)

    some_ties_dropped = jnp.max(n_tie - need) > 0.0

    @pl.when(jnp.logical_not(some_ties_dropped))
    def _():
        def sel_body(j, carry):
            selb_ref[j] = jnp.where(jnp.logical_and(key_ref[j] >= thr, causal(j)), 0.0, NEG_INF)
            return carry
        lax.fori_loop(0, nk, sel_body, 0)

    @pl.when(some_ties_dropped)
    def _():
        def sel_body(j, run):
            kj = key_ref[j]
            tie = kj == thr
            tie_f = jnp.where(tie, 1.0, 0.0)
            before = run + jnp.dot(tril_ref[...], tie_f.astype(BF16), preferred_element_type=F32)
            take = jnp.where(kj > thr, 0.0, jnp.where(tie, jnp.where(before < need, 0.0, NEG_INF), NEG_INF))
            selb_ref[j] = jnp.where(causal(j), take, NEG_INF)
            return run + jnp.sum(tie_f, axis=0, keepdims=True)
        lax.fori_loop(0, nk, sel_body, jnp.zeros((1, tq), F32))

    ones_rows = jnp.ones((16, 2 * tq), BF16)
    sc_refs = (sc0_ref, sc1_ref, sc2_ref, sc3_ref)
    for g0 in range(0, ATT_KV_HEADS, len(sc_refs)):
        groups = tuple((g0 + n, scr) for n, scr in enumerate(sc_refs))

        def logits_body(jj, mx):
            base = pl.multiple_of(jj * 2 * tq, 2 * tq)
            j0 = 2 * jj
            sels = [selb_ref[j0], selb_ref[j0 + 1]]
            kinds = [jnp.clip(i - j0 - c, 0, 2) for c in range(2)]
            out = []
            for (g, scr), mx8 in zip(groups, mx):
                s = _bdot_nt(kg_ref[g, pl.ds(base, 2 * tq), :], qg_ref[g])
                add = jnp.concatenate(
                    [jnp.concatenate([bias_ref[g * g_heads + a, kinds[c]] + sels[c] for a in range(g_heads)], axis=1)
                     for c in range(2)], axis=0)
                s = s + add
                scr[jj] = s
                out.append(jnp.maximum(mx8, fold8(s, jnp.max)))
            return tuple(out)

        neg = jnp.full((8, g_heads * tq), NEG_INF, F32)
        mx = lax.fori_loop(0, n_pairs, logits_body, (neg,) * len(groups))
        ms = []
        for mx8 in mx:
            m = jnp.max(mx8, axis=0, keepdims=True)
            ms.append(jnp.where(m == NEG_INF, 0.0, m))

        def pv_body(jj, accs):
            out = []
            for (g, scr), m, acc in zip(groups, ms, accs):
                p = jnp.exp(scr[jj] - m).astype(BF16)
                lhs = jnp.concatenate([vt_ref[jj, g * dh:(g + 1) * dh, :], ones_rows], axis=0)
                out.append(acc + jnp.dot(lhs, p, preferred_element_type=F32))
            return tuple(out)

        zero = jnp.zeros((dh + 16, g_heads * tq), F32)
        accs = lax.fori_loop(0, n_pairs, pv_body, (zero,) * len(groups))
        for (g, _), acc in zip(groups, accs):
            og = acc[:dh] / acc[dh:dh + 1]
            for a in range(g_heads):
                hh = g * g_heads + a
                ot_ref[hh * dh:(hh + 1) * dh, :] = og[:, a * tq:(a + 1) * tq]
    o_ref[...] = ot_ref[...].T.astype(o_ref.dtype)


def dsa_prompt(q, qi, wit, kib, kg, vt, bias_tab, batch, seq_len, tq=128):
    n, d = q.shape
    dh = d // ATT_HEADS
    nq = seq_len // tq
    g_heads = ATT_HEADS // ATT_KV_HEADS
    assert nq % 2 == 0 and vt.shape[2] == 2 * tq
    topk = min(TOPK_MAX, seq_len // 4)
    tril = np.tril(np.ones((tq, tq), np.float32), -1)
    blk_spec = lambda width: pl.BlockSpec((tq, width), lambda b, i: (b * nq + i, 0))
    return pl.pallas_call(
        functools.partial(_dsa_prompt_kernel, tq=tq, dh=dh, topk=float(topk)),
        grid=(batch, nq),
        in_specs=[blk_spec(d), blk_spec(qi.shape[1]),
                  pl.BlockSpec((IDX_HEADS, tq), lambda b, i: (0, b * nq + i)),
                  pl.BlockSpec((seq_len, IDX_DH), lambda b, i: (b, 0)),
                  pl.BlockSpec((ATT_KV_HEADS, seq_len, dh), lambda b, i: (0, b, 0)),
                  pl.BlockSpec((nq // 2,) + vt.shape[1:], lambda b, i: (b, 0, 0)),
                  pl.BlockSpec(bias_tab.shape, lambda b, i: (0, 0, 0, 0)),
                  pl.BlockSpec((tq, tq), lambda b, i: (0, 0))],
        out_specs=blk_spec(d),
        out_shape=jax.ShapeDtypeStruct((n, d), BF16),
        scratch_shapes=[pltpu.VMEM((nq, tq, tq), I32),
                        pltpu.VMEM((nq, tq, tq), F32),
                        pltpu.VMEM((IDX_HEADS * tq, IDX_DH), BF16),
                        pltpu.VMEM((ATT_KV_HEADS, g_heads * tq, dh), BF16),
                        *([pltpu.VMEM((nq // 2, 2 * tq, g_heads * tq), F32)] * ATT_KV_HEADS),
                        pltpu.VMEM((d, tq), F32)],
        compiler_params=_cparams("parallel", "arbitrary"),
        name="dsa_prompt",
    )(q, qi, wit, kib, kg, vt, bias_tab, jnp.asarray(tril, BF16))


def _mix_route_kernel(o_ref, wo_ref, x_ref, gt_ref, g_ref, sh_ref, sc_ref, wr_ref, rb_ref, tri_ref,
                      xo_ref, hp_ref, eidx_ref, wsel_ref, pos_ref, cnt_ref, carry_ref, *, n_exp):
    @pl.when(pl.program_id(0) == 0)
    def _():
        carry_ref[...] = jnp.zeros_like(carry_ref)

    x_new = x_ref[...] + gt_ref[...] * _bdot(o_ref[...], wo_ref[...])
    xo_ref[...] = x_new
    h = _rms(x_new, g_ref[...]) * (1.0 + sc_ref[...]) + sh_ref[...]
    _rows_store(hp_ref, _pack_rows(h))

    h_hi, h_mid, h_lo = _split3(h)
    w_hi, w_mid, w_lo = _split3(wr_ref[...])
    nt = lambda a, b: lax.dot_general(a, b, (((1,), (1,)), ((), ())), preferred_element_type=F32)
    logits = ((nt(w_lo, h_hi) + nt(w_hi, h_lo) + nt(w_mid, h_mid))
              + (nt(w_hi, h_mid) + nt(w_mid, h_hi))) + nt(w_hi, h_hi)
    s = _sigmoid(logits)
    sel = s + rb_ref[...]
    tm = sel.shape[1]
    gsz = n_exp // N_GROUPS
    io_g = lax.broadcasted_iota(I32, (gsz, tm), 0)

    gs = []
    for g in range(N_GROUPS):
        grp = sel[g * gsz:(g + 1) * gsz, :]
        m1 = jnp.max(grp, axis=0, keepdims=True)
        i1 = jnp.min(jnp.where(grp == m1, io_g, gsz), axis=0, keepdims=True)
        m2 = jnp.max(jnp.where(io_g == i1, NEG_INF, grp), axis=0, keepdims=True)
        gs.append(m1 + m2)
    masked = []
    for g in range(N_GROUPS):
        rank = jnp.zeros((1, tm), F32)
        for o in range(N_GROUPS):
            if o == g:
                continue
            ahead = (gs[o] >= gs[g]) if o < g else (gs[o] > gs[g])
            rank = rank + jnp.where(ahead, 1.0, 0.0)
        keep = jnp.where(rank < TOPK_GROUPS, 0.0, NEG_INF)
        masked.append(sel[g * gsz:(g + 1) * gsz, :] + keep)
    msel = jnp.concatenate(masked, axis=0)

    io_e = lax.broadcasted_iota(I32, (n_exp, tm), 0)
    chosen = jnp.zeros((n_exp, tm), F32)
    picks, weights = [], []
    for _ in range(TOP_K):
        m = jnp.max(msel, axis=0, keepdims=True)
        ei = jnp.min(jnp.where(msel == m, io_e, n_exp), axis=0, keepdims=True)
        pick = io_e == ei
        weights.append(jnp.sum(jnp.where(pick, s, 0.0), axis=0, keepdims=True))
        picks.append(ei)
        chosen = jnp.where(pick, 1.0, chosen)
        msel = jnp.where(pick, NEG_INF, msel)
    wsum = weights[0]
    for w in weights[1:]:
        wsum = wsum + w

    rank_in_expert = carry_ref[...] + jnp.dot(chosen.astype(BF16), tri_ref[...], preferred_element_type=F32)
    carry_new = carry_ref[...] + jnp.sum(chosen, axis=1, keepdims=True)
    carry_ref[...] = carry_new
    cnt_ref[...] = carry_new
    for kk in range(TOP_K):
        eidx_ref[kk:kk + 1, :] = picks[kk]
        wsel_ref[kk:kk + 1, :] = weights[kk] / wsum * ROUTE_SCALE
        pk = jnp.sum(jnp.where(io_e == picks[kk], rank_in_expert, 0.0), axis=0, keepdims=True)
        pos_ref[kk:kk + 1, :] = pk.astype(I32)


def mix_out_and_route(o, w_out, x, mod, norm_g, w_router, router_bias, tm, seq_len):
    n, d = x.shape
    kdim = o.shape[1]
    n_exp = w_router.shape[1]
    per = d // 2 // LANES
    tri = np.triu(np.ones((tm, tm), np.float32), 1)
    row8 = lambda dt: jax.ShapeDtypeStruct((TOP_K, n), dt)
    tok8 = pl.BlockSpec((TOP_K, tm), lambda i: (0, i))
    return pl.pallas_call(
        functools.partial(_mix_route_kernel, n_exp=n_exp),
        grid=(n // tm,),
        in_specs=[pl.BlockSpec((tm, kdim), lambda i: (i, 0)),
                  pl.BlockSpec((kdim, d), lambda i: (0, 0)),
                  pl.BlockSpec((tm, d), lambda i: (i, 0)),
                  _mod_spec(mod, 2, tm, seq_len, d),
                  pl.BlockSpec((1, d), lambda i: (0, 0)),
                  _mod_spec(mod, 3, tm, seq_len, d),
                  _mod_spec(mod, 4, tm, seq_len, d),
                  pl.BlockSpec((n_exp, d), lambda i: (0, 0)),
                  pl.BlockSpec((n_exp, 1), lambda i: (0, 0)),
                  pl.BlockSpec((tm, tm), lambda i: (0, 0))],
        out_specs=[pl.BlockSpec((tm, d), lambda i: (i, 0)),
                   pl.BlockSpec((tm * per, LANES), lambda i: (i, 0)),
                   tok8, tok8, tok8,
                   pl.BlockSpec((n_exp, 1), lambda i: (0, 0))],
        out_shape=[jax.ShapeDtypeStruct((n, d), F32), jax.ShapeDtypeStruct((n * per, LANES), I32),
                   row8(I32), row8(F32), row8(I32), jax.ShapeDtypeStruct((n_exp, 1), F32)],
        scratch_shapes=[pltpu.VMEM((n_exp, 1), F32)],
        compiler_params=_cparams("arbitrary"),
        name="mix_out_route",
    )(o, w_out.astype(BF16), x, mod, norm_g.reshape(1, d), mod, mod,
      w_router.T, router_bias.reshape(n_exp, 1), jnp.asarray(tri, BF16))


def _dispatch_kernel(dest_ref, h_ref, xs_ref, sem, *, per):
    tm = h_ref.shape[0] // per

    def row_copy(t, dst_row):
        return pltpu.make_async_copy(_row_tile(h_ref, t, per), _row_tile(xs_ref, dst_row, per), sem)

    def issue(t, c):
        for kk in range(TOP_K):
            row_copy(t, dest_ref[kk, t]).start()
        return c

    def drain(t, c):
        for kk in range(TOP_K):
            row_copy(t, dest_ref[kk, t]).wait()
        return c

    lax.fori_loop(0, tm, issue, 0)
    lax.fori_loop(0, tm, drain, 0)


def moe_dispatch(h, dest, tm):
    n = dest.shape[1]
    per = h.shape[0] // n
    return pl.pallas_call(
        functools.partial(_dispatch_kernel, per=per),
        grid=(n // tm,),
        in_specs=[pl.BlockSpec((TOP_K, tm), lambda i: (0, i), memory_space=pltpu.SMEM),
                  pl.BlockSpec((tm * per, LANES), lambda i: (i, 0))],
        out_specs=pl.BlockSpec(memory_space=pl.ANY),
        out_shape=jax.ShapeDtypeStruct((n * TOP_K * per, LANES), h.dtype),
        scratch_shapes=[pltpu.SemaphoreType.DMA(())],
        compiler_params=_cparams("arbitrary"),
        name="moe_dispatch",
    )(dest, h)


def _expert_kernel(blk_ref, exp_ref, lo_ref, hi_ref, x_ref, wg_ref, wu_ref, wd_ref, y_ref,
                   wgb_ref, wub_ref, wdb_ref):
    j = pl.program_id(0)
    per = wg_ref.shape[0] // 2 // LANES
    blk = x_ref.shape[0] // per
    lo = lo_ref[j] - blk_ref[j] * blk
    hi = hi_ref[j] - blk_ref[j] * blk

    @pl.when(jnp.logical_or(j == 0, exp_ref[j] != exp_ref[jnp.maximum(j - 1, 0)]))
    def _():
        wgb_ref[...] = wg_ref[...].astype(BF16)
        wub_ref[...] = wu_ref[...].astype(BF16)
        wdb_ref[...] = wd_ref[...].astype(BF16)

    @pl.when(hi > lo)
    def _():
        x = _unpack_rows(_rows_load(x_ref, per)).astype(BF16)
        g = jnp.dot(x, wgb_ref[...], preferred_element_type=F32)
        u = jnp.dot(x, wub_ref[...], preferred_element_type=F32)
        y = jnp.dot((_silu(g) * u).astype(BF16), wdb_ref[...], preferred_element_type=F32)

        @pl.when(jnp.logical_and(lo == 0, hi == blk))
        def _():
            _rows_store(y_ref, _pack_rows(y))

        @pl.when(jnp.logical_not(jnp.logical_and(lo == 0, hi == blk)))
        def _():
            row = lax.broadcasted_iota(I32, y.shape, 0)
            ym = jnp.where(jnp.logical_and(row >= lo, row < hi), y, 0.0)

            @pl.when(lo == 0)
            def _():
                _rows_store(y_ref, _pack_rows(ym))

            @pl.when(lo != 0)
            def _():
                _rows_store(y_ref, _pack_rows(_unpack_rows(_rows_load(y_ref, per)) + ym))


def moe_experts(xs, seg_blk, seg_exp, seg_lo, seg_hi, w_gate, w_up, w_down, layer, blk):
    d, ff = w_gate.shape[2:]
    wspec = lambda shape: pl.BlockSpec((None, None) + shape, lambda j, sb, se, lo, hi: (layer, se[j], 0, 0))
    rows = pl.BlockSpec((blk * (d // 2 // LANES), LANES), lambda j, sb, se, lo, hi: (sb[j], 0))
    return pl.pallas_call(
        _expert_kernel,
        grid_spec=pltpu.PrefetchScalarGridSpec(
            num_scalar_prefetch=4, grid=(seg_blk.shape[0],),
            in_specs=[rows, wspec((d, ff)), wspec((d, ff)), wspec((ff, d))],
            out_specs=rows,
            scratch_shapes=[pltpu.VMEM((d, ff), BF16), pltpu.VMEM((d, ff), BF16), pltpu.VMEM((ff, d), BF16)]),
        out_shape=jax.ShapeDtypeStruct(xs.shape, xs.dtype),
        compiler_params=_cparams("arbitrary"),
        name="moe_experts",
    )(seg_blk, seg_exp, seg_lo, seg_hi, xs, w_gate, w_up, w_down)


def _combine_kernel(dest_ref, dnext_ref, ys_ref, wsel_ref, h_ref, x_ref, gt_ref, sg_ref, su_ref, sd_ref, nf_ref,
                    o_ref, buf_ref, sems, *, final_norm):
    i = pl.program_id(0)
    tm, d = x_ref.shape
    per = d // 2 // LANES

    def row_copy(idx_ref, slot, t, kk):
        return pltpu.make_async_copy(_row_tile(ys_ref, idx_ref[kk, t], per),
                                     _row_tile(buf_ref.at[slot], kk * tm + t, per), sems.at[slot])

    def start_rows(idx_ref, slot, t0, t1):
        def issue(t, c):
            for kk in range(TOP_K):
                row_copy(idx_ref, slot, t, kk).start()
            return c
        lax.fori_loop(t0, t1, issue, 0)

    slot = i % 2
    has_next = i + 1 < pl.num_programs(0)
    quarter = tm // 4

    def start_next(part):
        @pl.when(has_next)
        def _():
            start_rows(dnext_ref, 1 - slot, part * quarter, tm if part == 3 else (part + 1) * quarter)

    @pl.when(i == 0)
    def _():
        start_rows(dest_ref, 0, 0, tm)

    start_next(0)
    h = _unpack_rows(_rows_load(h_ref, per)).astype(BF16)
    g = jnp.dot(h, sg_ref[...], preferred_element_type=F32)
    u = jnp.dot(h, su_ref[...], preferred_element_type=F32)
    start_next(1)
    y = _bdot(_silu(g) * u, sd_ref[...])

    def drain(t, c):
        for kk in range(TOP_K):
            row_copy(dest_ref, slot, t, kk).wait()
        return c

    lax.fori_loop(0, tm, drain, 0)
    w = wsel_ref[...]
    rows = buf_ref.at[slot]
    for kk in range(TOP_K):
        if kk in (0, TOP_K // 2):
            start_next(2 + kk // (TOP_K // 2))
        y = y + w[:, kk:kk + 1] * _unpack_rows(_rows_load(rows, per, first=kk * tm, rows=tm))
    x_new = x_ref[...] + gt_ref[...] * y
    o_ref[...] = _rms(x_new, nf_ref[...]) if final_norm else x_new


def moe_combine(ys, dest, wsel_t, h, x, mod, ws_gate, ws_up, ws_down, norm_final, final_norm, tm, seq_len):
    n, d = x.shape
    ff = ws_gate.shape[1]
    last = n // tm - 1
    return pl.pallas_call(
        functools.partial(_combine_kernel, final_norm=final_norm),
        grid=(n // tm,),
        in_specs=[pl.BlockSpec((TOP_K, tm), lambda i: (0, i), memory_space=pltpu.SMEM),
                  pl.BlockSpec((TOP_K, tm), lambda i: (0, jnp.minimum(i + 1, last)), memory_space=pltpu.SMEM),
                  pl.BlockSpec(memory_space=pl.ANY),
                  pl.BlockSpec((tm, TOP_K), lambda i: (i, 0)),
                  pl.BlockSpec((tm * (d // 2 // LANES), LANES), lambda i: (i, 0)),
                  pl.BlockSpec((tm, d), lambda i: (i, 0)),
                  _mod_spec(mod, 5, tm, seq_len, d),
                  pl.BlockSpec((d, ff), lambda i: (0, 0)),
                  pl.BlockSpec((d, ff), lambda i: (0, 0)),
                  pl.BlockSpec((ff, d), lambda i: (0, 0)),
                  pl.BlockSpec((1, d), lambda i: (0, 0))],
        out_specs=pl.BlockSpec((tm, d), lambda i: (i, 0)),
        out_shape=jax.ShapeDtypeStruct((n, d), F32),
        scratch_shapes=[pltpu.VMEM((2, TOP_K * tm * (d // 2 // LANES), LANES), I32),
                        pltpu.SemaphoreType.DMA((2,))],
        compiler_params=_cparams("arbitrary"),
        name="moe_combine",
    )(dest, dest, ys, wsel_t, h, x, mod, ws_gate.astype(BF16), ws_up.astype(BF16), ws_down.astype(BF16),
      norm_final.reshape(1, d))


def _slot_kernel(start_ref, eidx_ref, pos_ref, dest_ref):
    e = eidx_ref[...]
    base = jnp.zeros(e.shape, I32)
    for ex in range(start_ref.shape[0]):
        base = jnp.where(e == ex, start_ref[ex], base)
    dest_ref[...] = base + pos_ref[...]


def moe_slots(start, eidx, pos, tm):
    n = eidx.shape[1]
    spec = pl.BlockSpec((TOP_K, tm), lambda i: (0, i))
    return pl.pallas_call(
        _slot_kernel,
        grid=(n // tm,),
        in_specs=[pl.BlockSpec(memory_space=pltpu.SMEM), spec, spec],
        out_specs=spec,
        out_shape=jax.ShapeDtypeStruct(eidx.shape, I32),
        compiler_params=_cparams("parallel"),
        name="moe_slots",
    )(start, eidx, pos)


def mixer_out_and_moe(o, w_out, x, mod, norm_g, w_router, router_bias, w_gate, w_up, w_down, layer,
                      ws_gate, ws_up, ws_down, norm_final, final_norm, tm, seq_len, blk):
    n, d = x.shape
    n_exp = w_router.shape[1]
    x, h, eidx, wsel, pos, counts = mix_out_and_route(o, w_out, x, mod, norm_g, w_router, router_bias, tm, seq_len)
    n_rows = n * TOP_K
    n_blocks = n_rows // blk
    counts = counts.reshape(n_exp).astype(I32)
    end = jnp.cumsum(counts)
    start = end - counts
    dest = moe_slots(start, eidx, pos, min(tm, n))
    seg_lo = jnp.sort(jnp.concatenate([jnp.arange(n_blocks, dtype=I32) * blk, start[1:]]))
    seg_hi = jnp.concatenate([seg_lo[1:], jnp.full((1,), n_rows, I32)])
    seg_blk = jnp.minimum(seg_lo // blk, n_blocks - 1)
    seg_exp = jnp.minimum(jnp.sum((end[None, :] <= seg_lo[:, None]).astype(I32), axis=1), n_exp - 1)
    xs = moe_dispatch(h, dest, tm)
    ys = moe_experts(xs, seg_blk, seg_exp, seg_lo, seg_hi, w_gate, w_up, w_down, layer, blk)
    return moe_combine(ys, dest, wsel.T, h, x, mod, ws_gate, ws_up, ws_down, norm_final, final_norm,
                       tm, seq_len)


def _sample_score_kernel(pt_ref, qi_ref, w_ref, *refs):
    page_refs, o_ref = refs[:-1], refs[-1]
    kp = jnp.concatenate([r[...] for r in page_refs], axis=1)
    lg = _bdot(qi_ref[...], kp)
    sc = jnp.sum(jnp.maximum(lg, 0.0) * w_ref[...], axis=0, keepdims=True)
    page = page_refs[0].shape[1]
    for p in range(len(page_refs)):
        o_ref[p:p + 1, :] = sc[:, p * page:(p + 1) * page]


def _page_specs(layer, pg, block):
    zeros = (0,) * (len(block) - 2)
    return [pl.BlockSpec(block, lambda b, g, pt, p=p: (layer, pt[b, g * pg + p]) + zeros) for p in range(pg)]


def dsa_sample_scores(qi3, wi3, cache_ki_t, layer, page_table, pg):
    b, n_pages = page_table.shape
    page = cache_ki_t.shape[3]
    return pl.pallas_call(
        _sample_score_kernel,
        grid_spec=pltpu.PrefetchScalarGridSpec(
            num_scalar_prefetch=1, grid=(b, n_pages // pg),
            in_specs=[pl.BlockSpec((None, IDX_HEADS, IDX_DH), lambda b, g, pt: (b, 0, 0)),
                      pl.BlockSpec((None, IDX_HEADS, 1), lambda b, g, pt: (b, 0, 0))]
                     + _page_specs(layer, pg, (None, None, IDX_DH, page)),
            out_specs=pl.BlockSpec((None, pg, page), lambda b, g, pt: (b, g, 0))),
        out_shape=jax.ShapeDtypeStruct((b, n_pages, page), F32),
        compiler_params=_cparams("parallel", "arbitrary"),
        name="dsa_sample_scores",
    )(page_table, qi3, wi3, *([cache_ki_t] * pg))


def _sample_select_kernel(sc_ref, qi_ref, w_ref, kin_ref, triu_ref, tril_ref, selb_ref, selbn_ref, *, topk):
    nb = sc_ref.shape[0]
    total = lambda a: jnp.sum(jnp.sum(a, axis=0, keepdims=True), axis=1, keepdims=True)
    keys, keys_new = [], []
    for b in range(nb):
        lg_new = jnp.sum(qi_ref[b].astype(F32) * kin_ref[b].astype(BF16).astype(F32),
                         axis=1, keepdims=True)
        s_new = jnp.sum(jnp.maximum(lg_new, 0.0) * w_ref[b], axis=0, keepdims=True)
        keys.append(_order_key(sc_ref[b]))
        keys_new.append(_order_key(s_new))

    def count(b, pred):
        return total(jnp.where(pred(keys[b]), 1.0, 0.0)) + jnp.where(pred(keys_new[b]), 1.0, 0.0)

    def bits_body(it, answers):
        shift = 30 - 2 * it
        out = []
        for b, ans in enumerate(answers):
            cands = [ans | (jnp.int32(v) << shift) for v in (3, 2, 1)]
            c3, c2, c1 = [count(b, lambda kj, c=c: kj >= (c ^ INT_MIN)) for c in cands]
            out.append(jnp.where(c3 >= topk, cands[0],
                                 jnp.where(c2 >= topk, cands[1], jnp.where(c1 >= topk, cands[2], ans))))
        return tuple(out)

    answers = lax.fori_loop(0, 16, bits_body, (jnp.zeros((1, 1), I32),) * nb)
    for b in range(nb):
        thr = answers[b] ^ INT_MIN
        need = topk - count(b, lambda kj: kj > thr)
        tie = keys[b] == thr
        tie_f = jnp.where(tie, 1.0, 0.0)
        tie_b = tie_f.astype(BF16)
        in_row = jnp.dot(tie_b, triu_ref[...], preferred_element_type=F32)
        rows_before = jnp.sum(jnp.dot(tril_ref[...], tie_b, preferred_element_type=F32), axis=1, keepdims=True)
        before = in_row + rows_before
        selb_ref[b] = jnp.where(keys[b] > thr, 0.0,
                                jnp.where(tie, jnp.where(before < need, 0.0, NEG_INF), NEG_INF))
        selbn_ref[b] = jnp.where(keys_new[b] > thr, 0.0,
                                 jnp.where(keys_new[b] == thr,
                                           jnp.where(total(tie_f) < need, 0.0, NEG_INF), NEG_INF))


def dsa_sample_select(scores, qi3, wi3, ki_new, topk):
    b, n_pages, page = scores.shape
    nb = 8 if b % 8 == 0 else 1
    triu = np.triu(np.ones((page, page), np.float32), 1)
    tril = np.tril(np.ones((n_pages, n_pages), np.float32), -1)
    per_b = lambda *tail: pl.BlockSpec((nb,) + tail, lambda i: (i,) + (0,) * len(tail))
    return pl.pallas_call(
        functools.partial(_sample_select_kernel, topk=float(topk)),
        grid=(b // nb,),
        in_specs=[per_b(n_pages, page), per_b(IDX_HEADS, IDX_DH), per_b(IDX_HEADS, 1), per_b(1, IDX_DH),
                  pl.BlockSpec((page, page), lambda i: (0, 0)),
                  pl.BlockSpec((n_pages, n_pages), lambda i: (0, 0))],
        out_specs=[per_b(n_pages, page), per_b(1, 1)],
        out_shape=[jax.ShapeDtypeStruct((b, n_pages, page), F32), jax.ShapeDtypeStruct((b, 1, 1), F32)],
        compiler_params=_cparams("parallel"),
        name="dsa_sample_select",
    )(scores, qi3, wi3, ki_new, jnp.asarray(triu, BF16), jnp.asarray(tril, BF16))


def _sample_attend_kernel(pt_ref, q_ref, selb_ref, selbn_ref, kn_ref, vn_ref, blast_ref, bfar_ref, bnew_ref,
                          *refs, pg, n_pages):
    k_refs, v_refs = refs[:pg], refs[pg:2 * pg]
    o_ref, m_ref, l_ref, acc_ref = refs[2 * pg:]
    step = pl.program_id(1)
    g_heads = ATT_HEADS // ATT_KV_HEADS
    page = k_refs[0].shape[2]
    dh = q_ref.shape[1]
    head_group = lax.broadcasted_iota(I32, (ATT_HEADS, 1), 0) // g_heads

    @pl.when(step == 0)
    def _():
        m_ref[...] = jnp.full(m_ref.shape, NEG_INF, F32)
        l_ref[...] = jnp.zeros(l_ref.shape, F32)
        acc_ref[...] = jnp.zeros(acc_ref.shape, F32)

    q = q_ref[...]
    s = jnp.zeros((ATT_HEADS, pg * page), F32)
    for g in range(ATT_KV_HEADS):
        kg = jnp.concatenate([r[g] for r in k_refs], axis=1)
        s = jnp.where(head_group == g, _bdot(q, kg), s)
    add = [jnp.where(step * pg + p == n_pages - 1, blast_ref[...], bfar_ref[...]) + selb_ref[p:p + 1, :]
           for p in range(pg)]
    s = s + jnp.concatenate(add, axis=1)
    m_old = m_ref[...]
    m_new = jnp.maximum(m_old, jnp.max(s, axis=1, keepdims=True))
    m_safe = jnp.where(m_new == NEG_INF, 0.0, m_new)
    alpha = jnp.exp(m_old - m_safe)
    p_ = jnp.exp(s - m_safe)
    pv = jnp.zeros((ATT_HEADS, dh), F32)
    for g in range(ATT_KV_HEADS):
        vg = jnp.concatenate([r[g] for r in v_refs], axis=1)
        pv = jnp.where(head_group == g, _bdot_nt(p_, vg), pv)
    l_new = alpha * l_ref[...] + jnp.sum(p_, axis=1, keepdims=True)
    acc_new = alpha * acc_ref[...] + pv
    m_ref[...] = m_new
    l_ref[...] = l_new
    acc_ref[...] = acc_new

    @pl.when(step == pl.num_programs(1) - 1)
    def _():
        kn = jnp.zeros((ATT_HEADS, dh), F32)
        vn = jnp.zeros((ATT_HEADS, dh), F32)
        for g in range(ATT_KV_HEADS):
            kn = jnp.where(head_group == g, kn_ref[g:g + 1, :], kn)
            vn = jnp.where(head_group == g, vn_ref[g:g + 1, :], vn)
        qk = q.astype(F32) * kn.astype(BF16).astype(F32)
        s_n = jnp.sum(qk, axis=1, keepdims=True) + bnew_ref[...] + selbn_ref[...]
        m_fin = jnp.maximum(m_new, s_n)
        m_fs = jnp.where(m_fin == NEG_INF, 0.0, m_fin)
        a2 = jnp.exp(m_new - m_fs)
        p_n = jnp.exp(s_n - m_fs)
        o_ref[...] = ((a2 * acc_new + p_n * vn) / (a2 * l_new + p_n)).astype(o_ref.dtype)


def dsa_sample_attend(q3, selb, selb_new, k_new, v_new, bias_tab, cache_k_t, cache_v_t, layer, page_table, pg):
    b, n_pages = page_table.shape
    kvh, dh, page = cache_k_t.shape[2:]
    assert page == bias_tab.shape[2] and page >= MAX_DISTANCE
    b_last = bias_tab[:, 1, :, 0]
    b_far = bias_tab[:, 2, 0, :1]
    b_new = bias_tab[:, 0, 0, :1]
    full = lambda shape: pl.BlockSpec(shape, lambda b, g, pt: (0,) * len(shape))
    per_b = lambda shape: pl.BlockSpec((None,) + shape, lambda b, g, pt: (b,) + (0,) * len(shape))
    return pl.pallas_call(
        functools.partial(_sample_attend_kernel, pg=pg, n_pages=n_pages),
        grid_spec=pltpu.PrefetchScalarGridSpec(
            num_scalar_prefetch=1, grid=(b, n_pages // pg),
            in_specs=[per_b((ATT_HEADS, dh)),
                      pl.BlockSpec((None, pg, page), lambda b, g, pt: (b, g, 0)),
                      per_b((1, 1)), per_b((kvh, dh)), per_b((kvh, dh)),
                      full((ATT_HEADS, page)), full((ATT_HEADS, 1)), full((ATT_HEADS, 1))]
                     + _page_specs(layer, pg, (None, None, kvh, dh, page))
                     + _page_specs(layer, pg, (None, None, kvh, dh, page)),
            out_specs=per_b((ATT_HEADS, dh)),
            scratch_shapes=[pltpu.VMEM((ATT_HEADS, 1), F32), pltpu.VMEM((ATT_HEADS, 1), F32),
                            pltpu.VMEM((ATT_HEADS, dh), F32)]),
        out_shape=jax.ShapeDtypeStruct((b, ATT_HEADS, dh), BF16),
        compiler_params=_cparams("parallel", "arbitrary"),
        name="dsa_sample_attend",
    )(page_table, q3, selb, selb_new, k_new, v_new, b_last, b_far, b_new,
      *([cache_k_t] * pg), *([cache_v_t] * pg))


def dsa_sample(q, k, v, qi, ki, wit, cache_k, cache_v, cache_ki, layer, page_table, bias_tab):
    b, d = q.shape
    dh = d // ATT_HEADS
    n_pages = page_table.shape[1]
    page = cache_k.shape[2]
    pg = 16 if n_pages % 16 == 0 else n_pages
    topk = min(TOPK_MAX, (n_pages * page + 1) // 4)
    qi3 = qi.reshape(b, IDX_HEADS, IDX_DH)
    wi3 = wit.T.reshape(b, IDX_HEADS, 1)
    cache_ki_t = jnp.transpose(cache_ki, (0, 1, 3, 2))
    cache_k_t = jnp.transpose(cache_k, (0, 1, 3, 4, 2))
    cache_v_t = jnp.transpose(cache_v, (0, 1, 3, 4, 2))
    scores = dsa_sample_scores(qi3, wi3, cache_ki_t, layer, page_table, 2 * pg if n_pages % (2 * pg) == 0 else pg)
    selb, selb_new = dsa_sample_select(scores, qi3, wi3, ki.reshape(b, 1, IDX_DH), topk)
    o = dsa_sample_attend(q.reshape(b, ATT_HEADS, dh), selb, selb_new, k.reshape(b, ATT_KV_HEADS, dh),
                          v.reshape(b, ATT_KV_HEADS, dh), bias_tab, cache_k_t, cache_v_t, layer, page_table, pg)
    return o.reshape(b, d)


def kernel(x_prompt, x_sample, c_prompt, c_sample, state_gla, cache_k, cache_v, cache_idx_k, page_table,
           rel_bias, w_ada, b_ada, norm_mix, norm_ffn, norm_final,
           gla_w_in, gla_w_g2, gla_b_g2, gla_gnorm, gla_w_out,
           dsa_w_in, dsa_idx_ln_g, dsa_idx_ln_b, dsa_w_out,
           moe_w_router, moe_router_bias, moe_w_gate, moe_w_up, moe_w_down,
           shared_w_gate, shared_w_up, shared_w_down):
    bp, t, d = x_prompt.shape
    bs = x_sample.shape[0]
    depth = w_ada.shape[0]
    dh = d // ATT_HEADS
    tm_p = min(256, t)
    tq = 128
    mods = ada_mod_all(jnp.concatenate([c_prompt, c_sample], axis=0), w_ada, b_ada)
    xp = x_prompt.reshape(bp * t, d)
    xs = x_sample.reshape(bs, d)
    bias_tab = bias_tables(rel_bias, tq)
    gla_p, gla_s, kp_l, vp_l, kip_l, ks_l, vs_l, kis_l = [], [], [], [], [], [], [], []
    for i in range(depth):
        mod_p = mods[i, :bp].reshape(bp, 1, 6 * d)
        mod_s = mods[i, bp:]
        j = i // 2
        if i % 2 == 0:
            qkvr_p, la_p = gla_project(xp, mod_p, norm_mix[i], gla_w_in[j], gla_w_g2[j], gla_b_g2[j], tm_p, t)
            o_p, s_p = gla_chunked(qkvr_p, la_p, gla_gnorm[j], bp, t)
            qkvr_s, la_s = gla_project(xs, mod_s, norm_mix[i], gla_w_in[j], gla_w_g2[j], gla_b_g2[j], bs, 1)
            o_s, s_s = gla_step(qkvr_s, la_s, state_gla[j], gla_gnorm[j])
            gla_p.append(s_p)
            gla_s.append(s_s)
            w_out = gla_w_out[j]
        else:
            dp = (dsa_w_in[j], dsa_idx_ln_g[j], dsa_idx_ln_b[j])
            q_p, qi_p, wit_p, kt_p, vt32_p, kit_p, kib_p, kg_p, vt_p = dsa_project(
                xp, mod_p, norm_mix[i], *dp, tm_p, t, True)
            o_p = dsa_prompt(q_p, qi_p, wit_p, kib_p, kg_p, vt_p, bias_tab, bp, t, tq)
            q_s, qi_s, wit_s, k_s, v_s, ki_s = dsa_project(xs, mod_s, norm_mix[i], *dp, bs, 1, False)
            o_s = dsa_sample(q_s, k_s, v_s, qi_s, ki_s, wit_s, cache_k, cache_v, cache_idx_k, j,
                             page_table, bias_tab)
            kp_l.append(kt_p.reshape(bp, ATT_KV_HEADS, dh, t).transpose(0, 3, 1, 2))
            vp_l.append(vt32_p.reshape(bp, ATT_KV_HEADS, dh, t).transpose(0, 3, 1, 2))
            kip_l.append(kit_p.transpose(0, 2, 1))
            ks_l.append(k_s.reshape(bs, 1, ATT_KV_HEADS, dh))
            vs_l.append(v_s.reshape(bs, 1, ATT_KV_HEADS, dh))
            kis_l.append(ki_s.reshape(bs, 1, IDX_DH))
            w_out = dsa_w_out[j]
        last = i == depth - 1
        mo = (norm_ffn[i], moe_w_router[i], moe_router_bias[i], moe_w_gate, moe_w_up, moe_w_down, i,
              shared_w_gate[i], shared_w_up[i], shared_w_down[i], norm_final, last)
        xp = mixer_out_and_moe(o_p, w_out, xp, mod_p, *mo, tm_p, t, 512)
        xs = mixer_out_and_moe(o_s, w_out, xs, mod_s, *mo, bs, 1, 32)
    return (xp.reshape(bp, t, d), xs.reshape(bs, 1, d), jnp.stack(gla_p), jnp.stack(gla_s),
            jnp.stack(kp_l), jnp.stack(vp_l), jnp.stack(kip_l),
            jnp.stack(ks_l), jnp.stack(vs_l), jnp.stack(kis_l))
```

```python
import functools
import math

import numpy as np
import jax
import jax.numpy as jnp
from jax import lax
from jax.experimental import pallas as pl
from jax.experimental.pallas import tpu as pltpu

F32 = jnp.float32
BF16 = jnp.bfloat16
I32 = jnp.int32

GLA_HEADS = 4
GLA_RANK = 16
GLA_NORMALIZER = 16.0
ATT_HEADS = 16
ATT_KV_HEADS = 4
IDX_HEADS = 8
IDX_DH = 64
TOPK_MAX = 256
N_BUCKETS = 32
MAX_DISTANCE = 128
N_GROUPS = 8
TOPK_GROUPS = 4
TOP_K = 8
ROUTE_SCALE = 2.5
EPS = 1e-6

LANES = 128
VMEM_LIMIT = 56 * 1024 * 1024
NEG_INF = float("-inf")
INT_MIN = -2 ** 31


def _cparams(*sem):
    return pltpu.CompilerParams(dimension_semantics=sem, vmem_limit_bytes=VMEM_LIMIT)


def _bdot(a, b):
    return jnp.dot(a.astype(BF16), b.astype(BF16), preferred_element_type=F32)


def _bdot_nt(a, b):
    return lax.dot_general(a.astype(BF16), b.astype(BF16), (((1,), (1,)), ((), ())),
                           preferred_element_type=F32)


def _bdot_tn(a, b):
    return lax.dot_general(a.astype(BF16), b.astype(BF16), (((0,), (0,)), ((), ())),
                           preferred_element_type=F32)


def _split3(a):
    hi = a.astype(BF16)
    r1 = a - hi.astype(F32)
    mid = r1.astype(BF16)
    lo = (r1 - mid.astype(F32)).astype(BF16)
    return hi, mid, lo


def _silu(x):
    return x * (1.0 / (1.0 + jnp.exp(-x)))


def _sigmoid(x):
    return 1.0 / (1.0 + jnp.exp(-x))


def _rms(x, g):
    return x * lax.rsqrt(jnp.mean(x * x, axis=-1, keepdims=True) + EPS) * g


def _ada_kernel(c_ref, w_ref, b_ref, o_ref):
    c_hi, c_mid, _ = _split3(_silu(c_ref[...]))
    w_hi, w_mid, _ = _split3(w_ref[...])
    dot = lambda a, b: jnp.dot(a, b, preferred_element_type=F32)
    o_ref[...] = (dot(c_hi, w_mid) + dot(c_mid, w_hi)) + dot(c_hi, w_hi) + b_ref[...]


def ada_mod_all(c, w_ada, b_ada, tn=512):
    depth, d, n6 = w_ada.shape
    rows = c.shape[0]
    return pl.pallas_call(
        _ada_kernel,
        grid=(depth, n6 // tn),
        in_specs=[pl.BlockSpec((rows, d), lambda l, j: (0, 0)),
                  pl.BlockSpec((None, d, tn), lambda l, j: (l, 0, j)),
                  pl.BlockSpec((None, 1, tn), lambda l, j: (l, 0, j))],
        out_specs=pl.BlockSpec((None, rows, tn), lambda l, j: (l, 0, j)),
        out_shape=jax.ShapeDtypeStruct((depth, rows, n6), F32),
        compiler_params=_cparams("parallel", "parallel"),
        name="ada_mod",
    )(c, w_ada, b_ada.reshape(depth, 1, n6))


def _mod_spec(mod, which, tm, seq_len, d):
    if mod.ndim == 3:
        per_seq = seq_len // tm
        return pl.BlockSpec((None, 1, d), lambda i: (i // per_seq, 0, which))
    return pl.BlockSpec((tm, d), lambda i: (i, which))


def _gla_proj_kernel(x_ref, g_ref, sh_ref, sc_ref, w_ref, wg2_ref, bg2_ref, qkvr_ref, la_ref, *, n_main):
    h = _rms(x_ref[...], g_ref[...]) * (1.0 + sc_ref[...]) + sh_ref[...]
    y = _bdot(h, w_ref[...])
    qkvr_ref[...] = y[:, :n_main]
    g1 = y[:, n_main:n_main + GLA_RANK]
    z = _bdot(g1, wg2_ref[...]) + bg2_ref[...]
    la_ref[...] = (jnp.minimum(z, 0.0) - jnp.log(1.0 + jnp.exp(-jnp.abs(z)))) * (1.0 / GLA_NORMALIZER)


def gla_project(x, mod, norm_g, w_in, w_g2, b_g2, tm, seq_len):
    n, d = x.shape
    hk = w_g2.shape[1]
    n_main = w_in.shape[1] - GLA_RANK
    n_pad = (-w_in.shape[1]) % LANES
    w = jnp.pad(w_in, ((0, 0), (0, n_pad))).astype(BF16)
    return pl.pallas_call(
        functools.partial(_gla_proj_kernel, n_main=n_main),
        grid=(n // tm,),
        in_specs=[pl.BlockSpec((tm, d), lambda i: (i, 0)),
                  pl.BlockSpec((1, d), lambda i: (0, 0)),
                  _mod_spec(mod, 0, tm, seq_len, d),
                  _mod_spec(mod, 1, tm, seq_len, d),
                  pl.BlockSpec(w.shape, lambda i: (0, 0)),
                  pl.BlockSpec(w_g2.shape, lambda i: (0, 0)),
                  pl.BlockSpec((1, hk), lambda i: (0, 0))],
        out_specs=[pl.BlockSpec((tm, n_main), lambda i: (i, 0)),
                   pl.BlockSpec((tm, hk), lambda i: (i, 0))],
        out_shape=[jax.ShapeDtypeStruct((n, n_main), F32),
                   jax.ShapeDtypeStruct((n, hk), F32)],
        compiler_params=_cparams("parallel"),
        name="gla_proj",
    )(x, norm_g.reshape(1, d), mod, mod, w, w_g2.astype(BF16), b_g2.reshape(1, hk))


def _gla_level_matrices(c):
    levels = int(math.log2(c))
    t = np.arange(c)[:, None]
    u = np.arange(c)[None, :]
    mats, masks = [], []
    for l in range(levels):
        m = 1 << l
        ref = (t // (2 * m)) * (2 * m) + m - 1
        right = (t % (2 * m)) >= m
        mat = np.where(right, (u > ref) & (u <= t), (u > t) & (u <= ref))
        mats.append(mat)
        masks.append((t // (2 * m) == u // (2 * m)) & right & ((u % (2 * m)) < m))
    mats.append(u <= t)
    masks.append(t == u)
    return (np.stack(mats).astype(np.float32).reshape(-1, c), np.stack(masks).astype(np.float32))


def _gla_chunk_kernel(q_ref, k_ref, v_ref, r_ref, la_ref, mat_ref, mask_ref, gn_ref,
                      o_ref, s_out_ref, st_ref, *, c, dk):
    ci = pl.program_id(1)
    nlev = mask_ref.shape[0] - 1
    hk = la_ref.shape[1]
    dv = v_ref.shape[1] // GLA_HEADS

    @pl.when(ci == 0)
    def _():
        st_ref[...] = jnp.zeros_like(st_ref)

    la = la_ref[...]
    hi = la.astype(BF16)
    lo = (la - hi.astype(F32)).astype(BF16)
    e_all = jnp.dot(mat_ref[...], jnp.concatenate([hi, lo], axis=1), preferred_element_type=F32)
    e_all = e_all[:, :hk] + e_all[:, hk:]

    for h in range(GLA_HEADS):
        ks, vs = slice(h * dk, (h + 1) * dk), slice(h * dv, (h + 1) * dv)
        q = q_ref[:, ks] * (dk ** -0.5)
        k = k_ref[:, ks]
        v = v_ref[:, vs]
        att = jnp.where(mask_ref[nlev] > 0.0, _bdot_nt(q, k), 0.0)
        for l in range(nlev):
            ex = jnp.exp(e_all[l * c:(l + 1) * c, ks])
            att = att + jnp.where(mask_ref[l] > 0.0, _bdot_nt(q * ex, k * ex), 0.0)

        b = e_all[nlev * c:(nlev + 1) * c, ks]
        b_end = b[c - 1:c, :]
        st = st_ref[h]
        o = _bdot(att, v) + _bdot_nt(q * jnp.exp(b), st)
        st_new = st * jnp.exp(b_end) + _bdot_tn(v, k * jnp.exp(b_end - b))
        st_ref[h] = st_new

        on = o * lax.rsqrt(jnp.mean(o * o, axis=-1, keepdims=True) + EPS) * gn_ref[...]
        o_ref[:, vs] = (on * _silu(r_ref[:, vs])).astype(o_ref.dtype)

        @pl.when(ci == pl.num_programs(1) - 1)
        def _():
            s_out_ref[h] = st_new.T


def gla_chunked(qkvr, log_a, gnorm, batch, seq_len, c=128):
    n = qkvr.shape[0]
    hk = log_a.shape[1]
    dk = hk // GLA_HEADS
    hv = (qkvr.shape[1] - 2 * hk) // 2
    dv = hv // GLA_HEADS
    nc = seq_len // c
    mats, masks = _gla_level_matrices(c)
    row = lambda b, ci: b * nc + ci
    return pl.pallas_call(
        functools.partial(_gla_chunk_kernel, c=c, dk=dk),
        grid=(batch, nc),
        in_specs=[pl.BlockSpec((c, hk), lambda b, ci: (row(b, ci), 0)),
                  pl.BlockSpec((c, hk), lambda b, ci: (row(b, ci), 1)),
                  pl.BlockSpec((c, hv), lambda b, ci: (row(b, ci), 2 * hk // hv)),
                  pl.BlockSpec((c, hv), lambda b, ci: (row(b, ci), 2 * hk // hv + 1)),
                  pl.BlockSpec((c, hk), lambda b, ci: (row(b, ci), 0)),
                  pl.BlockSpec(mats.shape, lambda b, ci: (0, 0)),
                  pl.BlockSpec(masks.shape, lambda b, ci: (0, 0, 0)),
                  pl.BlockSpec((1, dv), lambda b, ci: (0, 0))],
        out_specs=[pl.BlockSpec((c, hv), lambda b, ci: (row(b, ci), 0)),
                   pl.BlockSpec((None, GLA_HEADS, dk, dv), lambda b, ci: (b, 0, 0, 0))],
        out_shape=[jax.ShapeDtypeStruct((n, hv), BF16),
                   jax.ShapeDtypeStruct((batch, GLA_HEADS, dk, dv), F32)],
        scratch_shapes=[pltpu.VMEM((GLA_HEADS, dv, dk), F32)],
        compiler_params=_cparams("parallel", "arbitrary"),
        name="gla_chunked",
    )(qkvr, qkvr, qkvr, qkvr, log_a, jnp.asarray(mats, BF16), jnp.asarray(masks), gnorm.reshape(1, dv))


def _gla_step_kernel(q_ref, k_ref, g_ref, v_ref, r_ref, s_ref, gn_ref, o_ref, s_out_ref, *, dk):
    s_new = jnp.exp(g_ref[...]) * s_ref[...] + k_ref[...] * v_ref[...]
    s_out_ref[...] = s_new
    o = jnp.sum(q_ref[...] * (dk ** -0.5) * s_new, axis=0, keepdims=True)
    on = o * lax.rsqrt(jnp.mean(o * o, axis=-1, keepdims=True) + EPS) * gn_ref[...]
    o_ref[...] = (on * _silu(r_ref[...])).astype(o_ref.dtype)


def gla_step(qkvr, log_a, state, gnorm):
    batch = qkvr.shape[0]
    hk = log_a.shape[1]
    dk = hk // GLA_HEADS
    hv = (qkvr.shape[1] - 2 * hk) // 2
    dv = hv // GLA_HEADS
    col = lambda a: a.reshape(batch, GLA_HEADS, dk, 1)
    rowv = lambda a: a.reshape(batch, GLA_HEADS, 1, dv)
    q, k, v, r = (qkvr[:, :hk], qkvr[:, hk:2 * hk], qkvr[:, 2 * hk:2 * hk + hv], qkvr[:, 2 * hk + hv:])
    cspec = pl.BlockSpec((None, None, dk, 1), lambda b, h: (b, h, 0, 0))
    rspec = pl.BlockSpec((None, None, 1, dv), lambda b, h: (b, h, 0, 0))
    sspec = pl.BlockSpec((None, None, dk, dv), lambda b, h: (b, h, 0, 0))
    o, s_new = pl.pallas_call(
        functools.partial(_gla_step_kernel, dk=dk),
        grid=(batch, GLA_HEADS),
        in_specs=[cspec, cspec, cspec, rspec, rspec, sspec, pl.BlockSpec((1, dv), lambda b, h: (0, 0))],
        out_specs=[rspec, sspec],
        out_shape=[jax.ShapeDtypeStruct((batch, GLA_HEADS, 1, dv), BF16),
                   jax.ShapeDtypeStruct(state.shape, F32)],
        compiler_params=_cparams("parallel", "parallel"),
        name="gla_step",
    )(col(q), col(k), col(log_a), rowv(v), rowv(r), state, gnorm.reshape(1, dv))
    return o.reshape(batch, hv), s_new


HI_MASK = -65536


def _pack_rows(val):
    half = val.shape[1] // 2
    bits = lambda a: pltpu.bitcast(a.astype(BF16).astype(F32), I32)
    return lax.shift_right_logical(bits(val[:, :half]), 16) | (bits(val[:, half:]) & HI_MASK)


def _unpack_rows(words):
    return jnp.concatenate([pltpu.bitcast(words << 16, F32), pltpu.bitcast(words & HI_MASK, F32)], axis=1)


def _rows_load(ref, per, first=0, rows=None):
    rows = ref.shape[0] // per - first if rows is None else rows
    return jnp.concatenate([ref[pl.ds(first * per + j, rows, stride=per), :] for j in range(per)], axis=1)


def _rows_store(ref, words):
    rows = words.shape[0]
    per = words.shape[1] // LANES
    for j in range(per):
        ref[pl.ds(j, rows, stride=per), :] = words[:, j * LANES:(j + 1) * LANES]


def _row_tile(ref, r, per):
    return ref.at[pl.ds(pl.multiple_of(r * per, per), per), :]


def _dsa_proj_kernel(x_ref, g_ref, sh_ref, sc_ref, w_ref, wwi_ref, lg_ref, lb_ref,
                     q_ref, qi_ref, wit_ref, k_ref, v_ref, ki_ref, *seq_refs, cuts, dh):
    h = (_rms(x_ref[...], g_ref[...]) * (1.0 + sc_ref[...]) + sh_ref[...]).astype(BF16)
    y = jnp.dot(h, w_ref[...], preferred_element_type=F32)
    c0, c1, c2, c3, c4 = cuts
    q_ref[...] = (y[:, :c0] * (dh ** -0.5)).astype(BF16)
    qi_ref[...] = (y[:, c2:c3] * (IDX_DH ** -0.5)).astype(BF16)
    wit_ref[...] = _bdot_nt(wwi_ref[...], h) * (IDX_HEADS ** -0.5)
    k = y[:, c0:c1]
    v = y[:, c1:c2]
    ki = y[:, c3:c4]
    mu = jnp.mean(ki, axis=-1, keepdims=True)
    var = jnp.mean((ki - mu) * (ki - mu), axis=-1, keepdims=True)
    kin = (ki - mu) * lax.rsqrt(var + EPS) * lg_ref[...] + lb_ref[...]
    if not seq_refs:
        k_ref[...] = k
        v_ref[...] = v
        ki_ref[...] = kin
    else:
        kib_ref, kg_ref, vt_ref = seq_refs
        vt = v.T
        k_ref[...] = k.T
        v_ref[...] = vt
        ki_ref[...] = jnp.concatenate([kin, jnp.zeros_like(kin)], axis=1).T[:kin.shape[1]]
        kib_ref[...] = kin.astype(BF16)
        for g in range(ATT_KV_HEADS):
            kg_ref[g] = k[:, g * dh:(g + 1) * dh].astype(BF16)
        vt_ref[...] = vt.astype(BF16)


def dsa_project(x, mod, norm_g, w_in, ln_g, ln_b, tm, seq_len, seq_layouts):
    n, d = x.shape
    dh = d // ATT_HEADS
    nkv = ATT_KV_HEADS * dh
    nqi = IDX_HEADS * IDX_DH
    cuts = (d, d + nkv, d + 2 * nkv, d + 2 * nkv + nqi, d + 2 * nkv + nqi + IDX_DH)
    w = jnp.pad(w_in[:, :cuts[4]], ((0, 0), (0, (-cuts[4]) % LANES))).astype(BF16)
    w_wi_t = w_in[:, cuts[4]:cuts[4] + IDX_HEADS].T.astype(BF16)
    row = lambda wd: pl.BlockSpec((tm, wd), lambda i: (i, 0))
    out_specs = [row(d), row(nqi), pl.BlockSpec((IDX_HEADS, tm), lambda i: (0, i))]
    out_shape = [jax.ShapeDtypeStruct((n, d), BF16), jax.ShapeDtypeStruct((n, nqi), BF16),
                 jax.ShapeDtypeStruct((IDX_HEADS, n), F32)]
    if not seq_layouts:
        out_specs += [row(nkv), row(nkv), row(IDX_DH)]
        out_shape += [jax.ShapeDtypeStruct((n, wd), F32) for wd in (nkv, nkv, IDX_DH)]
    else:
        per_seq = seq_len // tm
        tmin = lambda wd: pl.BlockSpec((None, wd, tm), lambda i: (i // per_seq, 0, i % per_seq))
        out_specs += [tmin(nkv), tmin(nkv), tmin(IDX_DH),
                      row(IDX_DH),
                      pl.BlockSpec((ATT_KV_HEADS, tm, dh), lambda i: (0, i, 0)),
                      pl.BlockSpec((None, nkv, tm), lambda i: (i, 0, 0))]
        out_shape += [jax.ShapeDtypeStruct((n // seq_len, wd, seq_len), F32) for wd in (nkv, nkv, IDX_DH)]
        out_shape += [jax.ShapeDtypeStruct((n, IDX_DH), BF16),
                      jax.ShapeDtypeStruct((ATT_KV_HEADS, n, dh), BF16),
                      jax.ShapeDtypeStruct((n // tm, nkv, tm), BF16)]
    return pl.pallas_call(
        functools.partial(_dsa_proj_kernel, cuts=cuts, dh=dh),
        grid=(n // tm,),
        in_specs=[pl.BlockSpec((tm, d), lambda i: (i, 0)),
                  pl.BlockSpec((1, d), lambda i: (0, 0)),
                  _mod_spec(mod, 0, tm, seq_len, d),
                  _mod_spec(mod, 1, tm, seq_len, d),
                  pl.BlockSpec(w.shape, lambda i: (0, 0)),
                  pl.BlockSpec(w_wi_t.shape, lambda i: (0, 0)),
                  pl.BlockSpec((1, IDX_DH), lambda i: (0, 0)),
                  pl.BlockSpec((1, IDX_DH), lambda i: (0, 0))],
        out_specs=out_specs,
        out_shape=out_shape,
        compiler_params=_cparams("parallel"),
        name="dsa_proj",
    )(x, norm_g.reshape(1, d), mod, mod, w, w_wi_t, ln_g.reshape(1, IDX_DH), ln_b.reshape(1, IDX_DH))


def _t5_bucket_np(dist):
    dist = np.maximum(dist, 0)
    max_exact = N_BUCKETS // 2
    ratio = np.log(np.maximum(dist, max_exact).astype(np.float32) / max_exact) / math.log(MAX_DISTANCE / max_exact)
    large = np.minimum(max_exact + (ratio * (N_BUCKETS - max_exact)).astype(np.int32), N_BUCKETS - 1)
    return np.where(dist < max_exact, dist, large).astype(np.int32)


def _bias_table_kernel(rb_ref, bk_ref, o_ref):
    h = pl.program_id(0)
    for kind in range(3):
        bk = bk_ref[kind]
        acc = jnp.zeros(bk.shape, F32)
        for b in range(N_BUCKETS):
            acc = jnp.where(bk == b, rb_ref[b, h], acc)
        o_ref[kind] = acc


def bias_tables(rel_bias, tq):
    s = np.arange(tq)[:, None]
    t = np.arange(tq)[None, :]
    buckets = np.stack([_t5_bucket_np(t - s), _t5_bucket_np(tq + t - s), _t5_bucket_np(2 * tq + t - s)])
    assert tq >= MAX_DISTANCE and (buckets[2] == N_BUCKETS - 1).all()
    return pl.pallas_call(
        _bias_table_kernel,
        grid=(ATT_HEADS,),
        in_specs=[pl.BlockSpec(memory_space=pltpu.SMEM),
                  pl.BlockSpec(buckets.shape, lambda h: (0, 0, 0))],
        out_specs=pl.BlockSpec((None, 3, tq, tq), lambda h: (h, 0, 0, 0)),
        out_shape=jax.ShapeDtypeStruct((ATT_HEADS, 3, tq, tq), F32),
        compiler_params=_cparams("parallel"),
        name="t5_bias_tiles",
    )(rel_bias, jnp.asarray(buckets))


def _order_key(score):
    score = jnp.where(score == 0.0, 0.0, score)
    bits = pltpu.bitcast(score, I32)
    return bits ^ ((bits >> 31) & 0x7FFFFFFF)


def _dsa_prompt_kernel(q_ref, qi_ref, wit_ref, kib_ref, kg_ref, vt_ref, bias_ref, tril_ref, o_ref,
                       key_ref, selb_ref, qis_ref, qg_ref, sc0_ref, sc1_ref, sc2_ref, sc3_ref, ot_ref,
                       *, tq, dh, topk):
    i = pl.program_id(1)
    nk = i + 1
    g_heads = ATT_HEADS // ATT_KV_HEADS
    s_id = lax.broadcasted_iota(I32, (tq, tq), 0)
    t_id = lax.broadcasted_iota(I32, (tq, tq), 1)
    fold8 = lambda a, op: op(a.reshape(a.shape[0] // 8, 8, a.shape[1]), axis=0)

    for h in range(IDX_HEADS):
        qis_ref[h * tq:(h + 1) * tq, :] = qi_ref[:, h * IDX_DH:(h + 1) * IDX_DH]
    for hh in range(ATT_HEADS):
        g, a = divmod(hh, g_heads)
        qg_ref[g, a * tq:(a + 1) * tq, :] = q_ref[:, hh * dh:(hh + 1) * dh]
    w8 = wit_ref[...]

    n_pairs = (nk + 1) // 2

    def score_body(jj, carry):
        for j in (2 * jj, 2 * jj + 1):
            kij = kib_ref[pl.ds(pl.multiple_of(j * tq, tq), tq), :]
            lg = _bdot_nt(kij, qis_ref[...])
            acc = jnp.zeros((tq, tq), F32)
            for h in range(IDX_HEADS):
                acc = acc + jnp.maximum(lg[:, h * tq:(h + 1) * tq], 0.0) * w8[h:h + 1, :]
            valid = (s_id + j * tq) <= (t_id + i * tq)
            key_ref[j] = _order_key(jnp.where(valid, acc, NEG_INF))
        return carry

    lax.fori_loop(0, n_pairs, score_body, 0)

    @pl.when(nk % 2 == 1)
    def _():
        pad = jnp.minimum(nk, key_ref.shape[0] - 1)
        key_ref[pad] = jnp.full((tq, tq), INT_MIN, I32)
        selb_ref[pad] = jnp.full((tq, tq), NEG_INF, F32)

    def counts(preds):
        def body(jj, accs):
            ka, kb = key_ref[2 * jj], key_ref[2 * jj + 1]
            return tuple(acc8 + fold8(jnp.where(p(ka), 1.0, 0.0) + jnp.where(p(kb), 1.0, 0.0), jnp.sum)
                         for p, acc8 in zip(preds, accs))
        accs = lax.fori_loop(0, n_pairs, body, (jnp.zeros((8, tq), F32),) * len(preds))
        return [jnp.sum(a, axis=0, keepdims=True) for a in accs]

    def bits_body(it, ans):
        shift = 30 - 2 * it
        cands = [ans | (jnp.int32(v) << shift) for v in (3, 2, 1)]
        c3, c2, c1 = counts([lambda kj, c=c: kj >= (c ^ INT_MIN) for c in cands])
        return jnp.where(c3 >= topk, cands[0], jnp.where(c2 >= topk, cands[1], jnp.where(c1 >= topk, cands[2], ans)))

    thr = lax.fori_loop(0, 16, bits_body, jnp.zeros((1, tq), I32)) ^ INT_MIN
    n_gt, n_tie = counts([lambda kj: kj > thr, lambda kj: kj == thr])
    need = topk - n_gt

    def causal(j):
        return (s_id + j * tq) <= (t_id + i * tq)

    some_ties_dropped = jnp.max(n_tie - need) > 0.0

    @pl.when(jnp.logical_not(some_ties_dropped))
    def _():
        def sel_body(j, carry):
            selb_ref[j] = jnp.where(jnp.logical_and(key_ref[j] >= thr, causal(j)), 0.0, NEG_INF)
            return carry
        lax.fori_loop(0, nk, sel_body, 0)

    @pl.when(some_ties_dropped)
    def _():
        def sel_body(j, run):
            kj = key_ref[j]
            tie = kj == thr
            tie_f = jnp.where(tie, 1.0, 0.0)
            before = run + jnp.dot(tril_ref[...], tie_f.astype(BF16), preferred_element_type=F32)
            take = jnp.where(kj > thr, 0.0, jnp.where(tie, jnp.where(before < need, 0.0, NEG_INF), NEG_INF))
            selb_ref[j] = jnp.where(causal(j), take, NEG_INF)
            return run + jnp.sum(tie_f, axis=0, keepdims=True)
        lax.fori_loop(0, nk, sel_body, jnp.zeros((1, tq), F32))

    ones_rows = jnp.ones((16, 2 * tq), BF16)
    sc_refs = (sc0_ref, sc1_ref, sc2_ref, sc3_ref)
    for g0 in range(0, ATT_KV_HEADS, len(sc_refs)):
        groups = tuple((g0 + n, scr) for n, scr in enumerate(sc_refs))

        def logits_body(jj, mx):
            base = pl.multiple_of(jj * 2 * tq, 2 * tq)
            j0 = 2 * jj
            sels = [selb_ref[j0], selb_ref[j0 + 1]]
            kinds = [jnp.clip(i - j0 - c, 0, 2) for c in range(2)]
            out = []
            for (g, scr), mx8 in zip(groups, mx):
                s = _bdot_nt(kg_ref[g, pl.ds(base, 2 * tq), :], qg_ref[g])
                add = jnp.concatenate(
                    [jnp.concatenate([bias_ref[g * g_heads + a, kinds[c]] + sels[c] for a in range(g_heads)], axis=1)
                     for c in range(2)], axis=0)
                s = s + add
                scr[jj] = s
                out.append(jnp.maximum(mx8, fold8(s, jnp.max)))
            return tuple(out)

        neg = jnp.full((8, g_heads * tq), NEG_INF, F32)
        mx = lax.fori_loop(0, n_pairs, logits_body, (neg,) * len(groups))
        ms = []
        for mx8 in mx:
            m = jnp.max(mx8, axis=0, keepdims=True)
            ms.append(jnp.where(m == NEG_INF, 0.0, m))

        def pv_body(jj, accs):
            out = []
            for (g, scr), m, acc in zip(groups, ms, accs):
                p = jnp.exp(scr[jj] - m).astype(BF16)
                lhs = jnp.concatenate([vt_ref[jj, g * dh:(g + 1) * dh, :], ones_rows], axis=0)
                out.append(acc + jnp.dot(lhs, p, preferred_element_type=F32))
            return tuple(out)

        zero = jnp.zeros((dh + 16, g_heads * tq), F32)
        accs = lax.fori_loop(0, n_pairs, pv_body, (zero,) * len(groups))
        for (g, _), acc in zip(groups, accs):
            og = acc[:dh] / acc[dh:dh + 1]
            for a in range(g_heads):
                hh = g * g_heads + a
                ot_ref[hh * dh:(hh + 1) * dh, :] = og[:, a * tq:(a + 1) * tq]
    o_ref[...] = ot_ref[...].T.astype(o_ref.dtype)


def dsa_prompt(q, qi, wit, kib, kg, vt, bias_tab, batch, seq_len, tq=128):
    n, d = q.shape
    dh = d // ATT_HEADS
    nq = seq_len // tq
    g_heads = ATT_HEADS // ATT_KV_HEADS
    assert nq % 2 == 0 and vt.shape[2] == 2 * tq
    topk = min(TOPK_MAX, seq_len // 4)
    tril = np.tril(np.ones((tq, tq), np.float32), -1)
    blk_spec = lambda width: pl.BlockSpec((tq, width), lambda b, i: (b * nq + i, 0))
    return pl.pallas_call(
        functools.partial(_dsa_prompt_kernel, tq=tq, dh=dh, topk=float(topk)),
        grid=(batch, nq),
        in_specs=[blk_spec(d), blk_spec(qi.shape[1]),
                  pl.BlockSpec((IDX_HEADS, tq), lambda b, i: (0, b * nq + i)),
                  pl.BlockSpec((seq_len, IDX_DH), lambda b, i: (b, 0)),
                  pl.BlockSpec((ATT_KV_HEADS, seq_len, dh), lambda b, i: (0, b, 0)),
                  pl.BlockSpec((nq // 2,) + vt.shape[1:], lambda b, i: (b, 0, 0)),
                  pl.BlockSpec(bias_tab.shape, lambda b, i: (0, 0, 0, 0)),
                  pl.BlockSpec((tq, tq), lambda b, i: (0, 0))],
        out_specs=blk_spec(d),
        out_shape=jax.ShapeDtypeStruct((n, d), BF16),
        scratch_shapes=[pltpu.VMEM((nq, tq, tq), I32),
                        pltpu.VMEM((nq, tq, tq), F32),
                        pltpu.VMEM((IDX_HEADS * tq, IDX_DH), BF16),
                        pltpu.VMEM((ATT_KV_HEADS, g_heads * tq, dh), BF16),
                        *([pltpu.VMEM((nq // 2, 2 * tq, g_heads * tq), F32)] * ATT_KV_HEADS),
                        pltpu.VMEM((d, tq), F32)],
        compiler_params=_cparams("parallel", "arbitrary"),
        name="dsa_prompt",
    )(q, qi, wit, kib, kg, vt, bias_tab, jnp.asarray(tril, BF16))


def _mix_route_kernel(o_ref, wo_ref, x_ref, gt_ref, g_ref, sh_ref, sc_ref, wr_ref, rb_ref, tri_ref,
                      xo_ref, hp_ref, eidx_ref, wsel_ref, pos_ref, cnt_ref, carry_ref, *, n_exp):
    @pl.when(pl.program_id(0) == 0)
    def _():
        carry_ref[...] = jnp.zeros_like(carry_ref)

    x_new = x_ref[...] + gt_ref[...] * _bdot(o_ref[...], wo_ref[...])
    xo_ref[...] = x_new
    h = _rms(x_new, g_ref[...]) * (1.0 + sc_ref[...]) + sh_ref[...]
    _rows_store(hp_ref, _pack_rows(h))

    h_hi, h_mid, h_lo = _split3(h)
    w_hi, w_mid, w_lo = _split3(wr_ref[...])
    nt = lambda a, b: lax.dot_general(a, b, (((1,), (1,)), ((), ())), preferred_element_type=F32)
    logits = ((nt(w_lo, h_hi) + nt(w_hi, h_lo) + nt(w_mid, h_mid))
              + (nt(w_hi, h_mid) + nt(w_mid, h_hi))) + nt(w_hi, h_hi)
    s = _sigmoid(logits)
    sel = s + rb_ref[...]
    tm = sel.shape[1]
    gsz = n_exp // N_GROUPS
    io_g = lax.broadcasted_iota(I32, (gsz, tm), 0)

    gs = []
    for g in range(N_GROUPS):
        grp = sel[g * gsz:(g + 1) * gsz, :]
        m1 = jnp.max(grp, axis=0, keepdims=True)
        i1 = jnp.min(jnp.where(grp == m1, io_g, gsz), axis=0, keepdims=True)
        m2 = jnp.max(jnp.where(io_g == i1, NEG_INF, grp), axis=0, keepdims=True)
        gs.append(m1 + m2)
    masked = []
    for g in range(N_GROUPS):
        rank = jnp.zeros((1, tm), F32)
        for o in range(N_GROUPS):
            if o == g:
                continue
            ahead = (gs[o] >= gs[g]) if o < g else (gs[o] > gs[g])
            rank = rank + jnp.where(ahead, 1.0, 0.0)
        keep = jnp.where(rank < TOPK_GROUPS, 0.0, NEG_INF)
        masked.append(sel[g * gsz:(g + 1) * gsz, :] + keep)
    msel = jnp.concatenate(masked, axis=0)

    io_e = lax.broadcasted_iota(I32, (n_exp, tm), 0)
    chosen = jnp.zeros((n_exp, tm), F32)
    picks, weights = [], []
    for _ in range(TOP_K):
        m = jnp.max(msel, axis=0, keepdims=True)
        ei = jnp.min(jnp.where(msel == m, io_e, n_exp), axis=0, keepdims=True)
        pick = io_e == ei
        weights.append(jnp.sum(jnp.where(pick, s, 0.0), axis=0, keepdims=True))
        picks.append(ei)
        chosen = jnp.where(pick, 1.0, chosen)
        msel = jnp.where(pick, NEG_INF, msel)
    wsum = weights[0]
    for w in weights[1:]:
        wsum = wsum + w

    rank_in_expert = carry_ref[...] + jnp.dot(chosen.astype(BF16), tri_ref[...], preferred_element_type=F32)
    carry_new = carry_ref[...] + jnp.sum(chosen, axis=1, keepdims=True)
    carry_ref[...] = carry_new
    cnt_ref[...] = carry_new
    for kk in range(TOP_K):
        eidx_ref[kk:kk + 1, :] = picks[kk]
        wsel_ref[kk:kk + 1, :] = weights[kk] / wsum * ROUTE_SCALE
        pk = jnp.sum(jnp.where(io_e == picks[kk], rank_in_expert, 0.0), axis=0, keepdims=True)
        pos_ref[kk:kk + 1, :] = pk.astype(I32)


def mix_out_and_route(o, w_out, x, mod, norm_g, w_router, router_bias, tm, seq_len):
    n, d = x.shape
    kdim = o.shape[1]
    n_exp = w_router.shape[1]
    per = d // 2 // LANES
    tri = np.triu(np.ones((tm, tm), np.float32), 1)
    row8 = lambda dt: jax.ShapeDtypeStruct((TOP_K, n), dt)
    tok8 = pl.BlockSpec((TOP_K, tm), lambda i: (0, i))
    return pl.pallas_call(
        functools.partial(_mix_route_kernel, n_exp=n_exp),
        grid=(n // tm,),
        in_specs=[pl.BlockSpec((tm, kdim), lambda i: (i, 0)),
                  pl.BlockSpec((kdim, d), lambda i: (0, 0)),
                  pl.BlockSpec((tm, d), lambda i: (i, 0)),
                  _mod_spec(mod, 2, tm, seq_len, d),
                  pl.BlockSpec((1, d), lambda i: (0, 0)),
                  _mod_spec(mod, 3, tm, seq_len, d),
                  _mod_spec(mod, 4, tm, seq_len, d),
                  pl.BlockSpec((n_exp, d), lambda i: (0, 0)),
                  pl.BlockSpec((n_exp, 1), lambda i: (0, 0)),
                  pl.BlockSpec((tm, tm), lambda i: (0, 0))],
        out_specs=[pl.BlockSpec((tm, d), lambda i: (i, 0)),
                   pl.BlockSpec((tm * per, LANES), lambda i: (i, 0)),
                   tok8, tok8, tok8,
                   pl.BlockSpec((n_exp, 1), lambda i: (0, 0))],
        out_shape=[jax.ShapeDtypeStruct((n, d), F32), jax.ShapeDtypeStruct((n * per, LANES), I32),
                   row8(I32), row8(F32), row8(I32), jax.ShapeDtypeStruct((n_exp, 1), F32)],
        scratch_shapes=[pltpu.VMEM((n_exp, 1), F32)],
        compiler_params=_cparams("arbitrary"),
        name="mix_out_route",
    )(o, w_out.astype(BF16), x, mod, norm_g.reshape(1, d), mod, mod,
      w_router.T, router_bias.reshape(n_exp, 1), jnp.asarray(tri, BF16))


def _dispatch_kernel(dest_ref, h_ref, xs_ref, sem, *, per):
    tm = h_ref.shape[0] // per

    def row_copy(t, dst_row):
        return pltpu.make_async_copy(_row_tile(h_ref, t, per), _row_tile(xs_ref, dst_row, per), sem)

    def issue(t, c):
        for kk in range(TOP_K):
            row_copy(t, dest_ref[kk, t]).start(priority=kk % 2)
        return c

    def drain(t, c):
        for kk in range(TOP_K):
            row_copy(t, dest_ref[kk, t]).wait()
        return c

    lax.fori_loop(0, tm, issue, 0)
    lax.fori_loop(0, tm, drain, 0)


def moe_dispatch(h, dest, tm):
    n = dest.shape[1]
    per = h.shape[0] // n
    return pl.pallas_call(
        functools.partial(_dispatch_kernel, per=per),
        grid=(n // tm,),
        in_specs=[pl.BlockSpec((TOP_K, tm), lambda i: (0, i), memory_space=pltpu.SMEM),
                  pl.BlockSpec((tm * per, LANES), lambda i: (i, 0))],
        out_specs=pl.BlockSpec(memory_space=pl.ANY),
        out_shape=jax.ShapeDtypeStruct((n * TOP_K * per, LANES), h.dtype),
        scratch_shapes=[pltpu.SemaphoreType.DMA(())],
        compiler_params=_cparams("arbitrary"),
        name="moe_dispatch",
    )(dest, h)


def _expert_kernel(blk_ref, exp_ref, lo_ref, hi_ref, x_ref, wg_ref, wu_ref, wd_ref, y_ref,
                   wgb_ref, wub_ref, wdb_ref):
    j = pl.program_id(0)
    per = wg_ref.shape[0] // 2 // LANES
    blk = x_ref.shape[0] // per
    lo = lo_ref[j] - blk_ref[j] * blk
    hi = hi_ref[j] - blk_ref[j] * blk

    @pl.when(jnp.logical_or(j == 0, exp_ref[j] != exp_ref[jnp.maximum(j - 1, 0)]))
    def _():
        wgb_ref[...] = wg_ref[...].astype(BF16)
        wub_ref[...] = wu_ref[...].astype(BF16)
        wdb_ref[...] = wd_ref[...].astype(BF16)

    @pl.when(hi > lo)
    def _():
        x = _unpack_rows(_rows_load(x_ref, per)).astype(BF16)
        g = jnp.dot(x, wgb_ref[...], preferred_element_type=F32)
        u = jnp.dot(x, wub_ref[...], preferred_element_type=F32)
        y = jnp.dot((_silu(g) * u).astype(BF16), wdb_ref[...], preferred_element_type=F32)

        @pl.when(jnp.logical_and(lo == 0, hi == blk))
        def _():
            _rows_store(y_ref, _pack_rows(y))

        @pl.when(jnp.logical_not(jnp.logical_and(lo == 0, hi == blk)))
        def _():
            row = lax.broadcasted_iota(I32, y.shape, 0)
            ym = jnp.where(jnp.logical_and(row >= lo, row < hi), y, 0.0)

            @pl.when(lo == 0)
            def _():
                _rows_store(y_ref, _pack_rows(ym))

            @pl.when(lo != 0)
            def _():
                _rows_store(y_ref, _pack_rows(_unpack_rows(_rows_load(y_ref, per)) + ym))


def moe_experts(xs, seg_blk, seg_exp, seg_lo, seg_hi, w_gate, w_up, w_down, layer, blk):
    d, ff = w_gate.shape[2:]
    wspec = lambda shape: pl.BlockSpec((None, None) + shape, lambda j, sb, se, lo, hi: (layer, se[j], 0, 0))
    rows = pl.BlockSpec((blk * (d // 2 // LANES), LANES), lambda j, sb, se, lo, hi: (sb[j], 0))
    return pl.pallas_call(
        _expert_kernel,
        grid_spec=pltpu.PrefetchScalarGridSpec(
            num_scalar_prefetch=4, grid=(seg_blk.shape[0],),
            in_specs=[rows, wspec((d, ff)), wspec((d, ff)), wspec((ff, d))],
            out_specs=rows,
            scratch_shapes=[pltpu.VMEM((d, ff), BF16), pltpu.VMEM((d, ff), BF16), pltpu.VMEM((ff, d), BF16)]),
        out_shape=jax.ShapeDtypeStruct(xs.shape, xs.dtype),
        compiler_params=_cparams("arbitrary"),
        name="moe_experts",
    )(seg_blk, seg_exp, seg_lo, seg_hi, xs, w_gate, w_up, w_down)


def _combine_kernel(dest_ref, dnext_ref, ys_ref, wsel_ref, h_ref, x_ref, gt_ref, sg_ref, su_ref, sd_ref, nf_ref,
                    o_ref, buf_ref, sems, *, final_norm):
    i = pl.program_id(0)
    tm, d = x_ref.shape
    per = d // 2 // LANES

    def row_copy(idx_ref, slot, t, kk):
        return pltpu.make_async_copy(_row_tile(ys_ref, idx_ref[kk, t], per),
                                     _row_tile(buf_ref.at[slot], kk * tm + t, per), sems.at[slot])

    def start_rows(idx_ref, slot, t0, t1):
        def issue(t, c):
            for kk in range(TOP_K):
                row_copy(idx_ref, slot, t, kk).start(priority=kk % 2)
            return c
        lax.fori_loop(t0, t1, issue, 0)

    slot = i % 2
    has_next = i + 1 < pl.num_programs(0)
    quarter = tm // 4

    def start_next(part):
        @pl.when(has_next)
        def _():
            start_rows(dnext_ref, 1 - slot, part * quarter, tm if part == 3 else (part + 1) * quarter)

    @pl.when(i == 0)
    def _():
        start_rows(dest_ref, 0, 0, tm)

    start_next(0)
    h = _unpack_rows(_rows_load(h_ref, per)).astype(BF16)
    g = jnp.dot(h, sg_ref[...], preferred_element_type=F32)
    u = jnp.dot(h, su_ref[...], preferred_element_type=F32)
    start_next(1)
    y = _bdot(_silu(g) * u, sd_ref[...])

    def drain(t, c):
        for kk in range(TOP_K):
            row_copy(dest_ref, slot, t, kk).wait()
        return c

    lax.fori_loop(0, tm, drain, 0)
    w = wsel_ref[...]
    rows = buf_ref.at[slot]
    for kk in range(TOP_K):
        if kk in (0, TOP_K // 2):
            start_next(2 + kk // (TOP_K // 2))
        y = y + w[:, kk:kk + 1] * _unpack_rows(_rows_load(rows, per, first=kk * tm, rows=tm))
    x_new = x_ref[...] + gt_ref[...] * y
    o_ref[...] = _rms(x_new, nf_ref[...]) if final_norm else x_new


def moe_combine(ys, dest, wsel_t, h, x, mod, ws_gate, ws_up, ws_down, norm_final, final_norm, tm, seq_len):
    n, d = x.shape
    ff = ws_gate.shape[1]
    last = n // tm - 1
    return pl.pallas_call(
        functools.partial(_combine_kernel, final_norm=final_norm),
        grid=(n // tm,),
        in_specs=[pl.BlockSpec((TOP_K, tm), lambda i: (0, i), memory_space=pltpu.SMEM),
                  pl.BlockSpec((TOP_K, tm), lambda i: (0, jnp.minimum(i + 1, last)), memory_space=pltpu.SMEM),
                  pl.BlockSpec(memory_space=pl.ANY),
                  pl.BlockSpec((tm, TOP_K), lambda i: (i, 0)),
                  pl.BlockSpec((tm * (d // 2 // LANES), LANES), lambda i: (i, 0)),
                  pl.BlockSpec((tm, d), lambda i: (i, 0)),
                  _mod_spec(mod, 5, tm, seq_len, d),
                  pl.BlockSpec((d, ff), lambda i: (0, 0)),
                  pl.BlockSpec((d, ff), lambda i: (0, 0)),
                  pl.BlockSpec((ff, d), lambda i: (0, 0)),
                  pl.BlockSpec((1, d), lambda i: (0, 0))],
        out_specs=pl.BlockSpec((tm, d), lambda i: (i, 0)),
        out_shape=jax.ShapeDtypeStruct((n, d), F32),
        scratch_shapes=[pltpu.VMEM((2, TOP_K * tm * (d // 2 // LANES), LANES), I32),
                        pltpu.SemaphoreType.DMA((2,))],
        compiler_params=_cparams("arbitrary"),
        name="moe_combine",
    )(dest, dest, ys, wsel_t, h, x, mod, ws_gate.astype(BF16), ws_up.astype(BF16), ws_down.astype(BF16),
      norm_final.reshape(1, d))


def _slot_kernel(start_ref, eidx_ref, pos_ref, dest_ref):
    e = eidx_ref[...]
    base = jnp.zeros(e.shape, I32)
    for ex in range(start_ref.shape[0]):
        base = jnp.where(e == ex, start_ref[ex], base)
    dest_ref[...] = base + pos_ref[...]


def moe_slots(start, eidx, pos, tm):
    n = eidx.shape[1]
    spec = pl.BlockSpec((TOP_K, tm), lambda i: (0, i))
    return pl.pallas_call(
        _slot_kernel,
        grid=(n // tm,),
        in_specs=[pl.BlockSpec(memory_space=pltpu.SMEM), spec, spec],
        out_specs=spec,
        out_shape=jax.ShapeDtypeStruct(eidx.shape, I32),
        compiler_params=_cparams("parallel"),
        name="moe_slots",
    )(start, eidx, pos)


def mixer_out_and_moe(o, w_out, x, mod, norm_g, w_router, router_bias, w_gate, w_up, w_down, layer,
                      ws_gate, ws_up, ws_down, norm_final, final_norm, tm, seq_len, blk):
    n, d = x.shape
    n_exp = w_router.shape[1]
    x, h, eidx, wsel, pos, counts = mix_out_and_route(o, w_out, x, mod, norm_g, w_router, router_bias, tm, seq_len)
    n_rows = n * TOP_K
    n_blocks = n_rows // blk
    counts = counts.reshape(n_exp).astype(I32)
    end = jnp.cumsum(counts)
    start = end - counts
    dest = moe_slots(start, eidx, pos, min(tm, n))
    seg_lo = jnp.sort(jnp.concatenate([jnp.arange(n_blocks, dtype=I32) * blk, start[1:]]))
    seg_hi = jnp.concatenate([seg_lo[1:], jnp.full((1,), n_rows, I32)])
    seg_blk = jnp.minimum(seg_lo // blk, n_blocks - 1)
    seg_exp = jnp.minimum(jnp.sum((end[None, :] <= seg_lo[:, None]).astype(I32), axis=1), n_exp - 1)
    xs = moe_dispatch(h, dest, tm)
    ys = moe_experts(xs, seg_blk, seg_exp, seg_lo, seg_hi, w_gate, w_up, w_down, layer, blk)
    return moe_combine(ys, dest, wsel.T, h, x, mod, ws_gate, ws_up, ws_down, norm_final, final_norm,
                       tm, seq_len)


def _sample_score_kernel(pt_ref, qi_ref, w_ref, *refs):
    page_refs, o_ref = refs[:-1], refs[-1]
    kp = jnp.concatenate([r[...] for r in page_refs], axis=1)
    lg = _bdot(qi_ref[...], kp)
    sc = jnp.sum(jnp.maximum(lg, 0.0) * w_ref[...], axis=0, keepdims=True)
    page = page_refs[0].shape[1]
    for p in range(len(page_refs)):
        o_ref[p:p + 1, :] = sc[:, p * page:(p + 1) * page]


def _page_specs(layer, pg, block):
    zeros = (0,) * (len(block) - 2)
    return [pl.BlockSpec(block, lambda b, g, pt, p=p: (layer, pt[b, g * pg + p]) + zeros) for p in range(pg)]


def dsa_sample_scores(qi3, wi3, cache_ki_t, layer, page_table, pg):
    b, n_pages = page_table.shape
    page = cache_ki_t.shape[3]
    return pl.pallas_call(
        _sample_score_kernel,
        grid_spec=pltpu.PrefetchScalarGridSpec(
            num_scalar_prefetch=1, grid=(b, n_pages // pg),
            in_specs=[pl.BlockSpec((None, IDX_HEADS, IDX_DH), lambda b, g, pt: (b, 0, 0)),
                      pl.BlockSpec((None, IDX_HEADS, 1), lambda b, g, pt: (b, 0, 0))]
                     + _page_specs(layer, pg, (None, None, IDX_DH, page)),
            out_specs=pl.BlockSpec((None, pg, page), lambda b, g, pt: (b, g, 0))),
        out_shape=jax.ShapeDtypeStruct((b, n_pages, page), F32),
        compiler_params=_cparams("parallel", "arbitrary"),
        name="dsa_sample_scores",
    )(page_table, qi3, wi3, *([cache_ki_t] * pg))


def _sample_select_kernel(sc_ref, qi_ref, w_ref, kin_ref, triu_ref, tril_ref, selb_ref, selbn_ref, *, topk):
    nb = sc_ref.shape[0]
    total = lambda a: jnp.sum(jnp.sum(a, axis=0, keepdims=True), axis=1, keepdims=True)
    keys, keys_new = [], []
    for b in range(nb):
        lg_new = jnp.sum(qi_ref[b].astype(F32) * kin_ref[b].astype(BF16).astype(F32),
                         axis=1, keepdims=True)
        s_new = jnp.sum(jnp.maximum(lg_new, 0.0) * w_ref[b], axis=0, keepdims=True)
        keys.append(_order_key(sc_ref[b]))
        keys_new.append(_order_key(s_new))

    def count(b, pred):
        return total(jnp.where(pred(keys[b]), 1.0, 0.0)) + jnp.where(pred(keys_new[b]), 1.0, 0.0)

    def bits_body(it, answers):
        shift = 30 - 2 * it
        out = []
        for b, ans in enumerate(answers):
            cands = [ans | (jnp.int32(v) << shift) for v in (3, 2, 1)]
            c3, c2, c1 = [count(b, lambda kj, c=c: kj >= (c ^ INT_MIN)) for c in cands]
            out.append(jnp.where(c3 >= topk, cands[0],
                                 jnp.where(c2 >= topk, cands[1], jnp.where(c1 >= topk, cands[2], ans))))
        return tuple(out)

    answers = lax.fori_loop(0, 16, bits_body, (jnp.zeros((1, 1), I32),) * nb)
    for b in range(nb):
        thr = answers[b] ^ INT_MIN
        need = topk - count(b, lambda kj: kj > thr)
        tie = keys[b] == thr
        tie_f = jnp.where(tie, 1.0, 0.0)
        tie_b = tie_f.astype(BF16)
        in_row = jnp.dot(tie_b, triu_ref[...], preferred_element_type=F32)
        rows_before = jnp.sum(jnp.dot(tril_ref[...], tie_b, preferred_element_type=F32), axis=1, keepdims=True)
        before = in_row + rows_before
        selb_ref[b] = jnp.where(keys[b] > thr, 0.0,
                                jnp.where(tie, jnp.where(before < need, 0.0, NEG_INF), NEG_INF))
        selbn_ref[b] = jnp.where(keys_new[b] > thr, 0.0,
                                 jnp.where(keys_new[b] == thr,
                                           jnp.where(total(tie_f) < need, 0.0, NEG_INF), NEG_INF))


def dsa_sample_select(scores, qi3, wi3, ki_new, topk):
    b, n_pages, page = scores.shape
    nb = 8 if b % 8 == 0 else 1
    triu = np.triu(np.ones((page, page), np.float32), 1)
    tril = np.tril(np.ones((n_pages, n_pages), np.float32), -1)
    per_b = lambda *tail: pl.BlockSpec((nb,) + tail, lambda i: (i,) + (0,) * len(tail))
    return pl.pallas_call(
        functools.partial(_sample_select_kernel, topk=float(topk)),
        grid=(b // nb,),
        in_specs=[per_b(n_pages, page), per_b(IDX_HEADS, IDX_DH), per_b(IDX_HEADS, 1), per_b(1, IDX_DH),
                  pl.BlockSpec((page, page), lambda i: (0, 0)),
                  pl.BlockSpec((n_pages, n_pages), lambda i: (0, 0))],
        out_specs=[per_b(n_pages, page), per_b(1, 1)],
        out_shape=[jax.ShapeDtypeStruct((b, n_pages, page), F32), jax.ShapeDtypeStruct((b, 1, 1), F32)],
        compiler_params=_cparams("parallel"),
        name="dsa_sample_select",
    )(scores, qi3, wi3, ki_new, jnp.asarray(triu, BF16), jnp.asarray(tril, BF16))


def _sample_attend_kernel(pt_ref, q_ref, selb_ref, selbn_ref, kn_ref, vn_ref, blast_ref, bfar_ref, bnew_ref,
                          *refs, pg, n_pages):
    k_refs, v_refs = refs[:pg], refs[pg:2 * pg]
    o_ref, m_ref, l_ref, acc_ref = refs[2 * pg:]
    step = pl.program_id(1)
    g_heads = ATT_HEADS // ATT_KV_HEADS
    page = k_refs[0].shape[2]
    dh = q_ref.shape[1]
    head_group = lax.broadcasted_iota(I32, (ATT_HEADS, 1), 0) // g_heads

    @pl.when(step == 0)
    def _():
        m_ref[...] = jnp.full(m_ref.shape, NEG_INF, F32)
        l_ref[...] = jnp.zeros(l_ref.shape, F32)
        acc_ref[...] = jnp.zeros(acc_ref.shape, F32)

    q = q_ref[...]
    s = jnp.zeros((ATT_HEADS, pg * page), F32)
    for g in range(ATT_KV_HEADS):
        kg = jnp.concatenate([r[g] for r in k_refs], axis=1)
        s = jnp.where(head_group == g, _bdot(q, kg), s)
    add = [jnp.where(step * pg + p == n_pages - 1, blast_ref[...], bfar_ref[...]) + selb_ref[p:p + 1, :]
           for p in range(pg)]
    s = s + jnp.concatenate(add, axis=1)
    m_old = m_ref[...]
    m_new = jnp.maximum(m_old, jnp.max(s, axis=1, keepdims=True))
    m_safe = jnp.where(m_new == NEG_INF, 0.0, m_new)
    alpha = jnp.exp(m_old - m_safe)
    p_ = jnp.exp(s - m_safe)
    pv = jnp.zeros((ATT_HEADS, dh), F32)
    for g in range(ATT_KV_HEADS):
        vg = jnp.concatenate([r[g] for r in v_refs], axis=1)
        pv = jnp.where(head_group == g, _bdot_nt(p_, vg), pv)
    l_new = alpha * l_ref[...] + jnp.sum(p_, axis=1, keepdims=True)
    acc_new = alpha * acc_ref[...] + pv
    m_ref[...] = m_new
    l_ref[...] = l_new
    acc_ref[...] = acc_new

    @pl.when(step == pl.num_programs(1) - 1)
    def _():
        kn = jnp.zeros((ATT_HEADS, dh), F32)
        vn = jnp.zeros((ATT_HEADS, dh), F32)
        for g in range(ATT_KV_HEADS):
            kn = jnp.where(head_group == g, kn_ref[g:g + 1, :], kn)
            vn = jnp.where(head_group == g, vn_ref[g:g + 1, :], vn)
        qk = q.astype(F32) * kn.astype(BF16).astype(F32)
        s_n = jnp.sum(qk, axis=1, keepdims=True) + bnew_ref[...] + selbn_ref[...]
        m_fin = jnp.maximum(m_new, s_n)
        m_fs = jnp.where(m_fin == NEG_INF, 0.0, m_fin)
        a2 = jnp.exp(m_new - m_fs)
        p_n = jnp.exp(s_n - m_fs)
        o_ref[...] = ((a2 * acc_new + p_n * vn) / (a2 * l_new + p_n)).astype(o_ref.dtype)


def dsa_sample_attend(q3, selb, selb_new, k_new, v_new, bias_tab, cache_k_t, cache_v_t, layer, page_table, pg):
    b, n_pages = page_table.shape
    kvh, dh, page = cache_k_t.shape[2:]
    assert page == bias_tab.shape[2] and page >= MAX_DISTANCE
    b_last = bias_tab[:, 1, :, 0]
    b_far = bias_tab[:, 2, 0, :1]
    b_new = bias_tab[:, 0, 0, :1]
    full = lambda shape: pl.BlockSpec(shape, lambda b, g, pt: (0,) * len(shape))
    per_b = lambda shape: pl.BlockSpec((None,) + shape, lambda b, g, pt: (b,) + (0,) * len(shape))
    return pl.pallas_call(
        functools.partial(_sample_attend_kernel, pg=pg, n_pages=n_pages),
        grid_spec=pltpu.PrefetchScalarGridSpec(
            num_scalar_prefetch=1, grid=(b, n_pages // pg),
            in_specs=[per_b((ATT_HEADS, dh)),
                      pl.BlockSpec((None, pg, page), lambda b, g, pt: (b, g, 0)),
                      per_b((1, 1)), per_b((kvh, dh)), per_b((kvh, dh)),
                      full((ATT_HEADS, page)), full((ATT_HEADS, 1)), full((ATT_HEADS, 1))]
                     + _page_specs(layer, pg, (None, None, kvh, dh, page))
                     + _page_specs(layer, pg, (None, None, kvh, dh, page)),
            out_specs=per_b((ATT_HEADS, dh)),
            scratch_shapes=[pltpu.VMEM((ATT_HEADS, 1), F32), pltpu.VMEM((ATT_HEADS, 1), F32),
                            pltpu.VMEM((ATT_HEADS, dh), F32)]),
        out_shape=jax.ShapeDtypeStruct((b, ATT_HEADS, dh), BF16),
        compiler_params=_cparams("parallel", "arbitrary"),
        name="dsa_sample_attend",
    )(page_table, q3, selb, selb_new, k_new, v_new, b_last, b_far, b_new,
      *([cache_k_t] * pg), *([cache_v_t] * pg))


def dsa_sample(q, k, v, qi, ki, wit, cache_k, cache_v, cache_ki, layer, page_table, bias_tab):
    b, d = q.shape
    dh = d // ATT_HEADS
    n_pages = page_table.shape[1]
    page = cache_k.shape[2]
    pg = 16 if n_pages % 16 == 0 else n_pages
    topk = min(TOPK_MAX, (n_pages * page + 1) // 4)
    qi3 = qi.reshape(b, IDX_HEADS, IDX_DH)
    wi3 = wit.T.reshape(b, IDX_HEADS, 1)
    cache_ki_t = jnp.transpose(cache_ki, (0, 1, 3, 2))
    cache_k_t = jnp.transpose(cache_k, (0, 1, 3, 4, 2))
    cache_v_t = jnp.transpose(cache_v, (0, 1, 3, 4, 2))
    scores = dsa_sample_scores(qi3, wi3, cache_ki_t, layer, page_table, 2 * pg if n_pages % (2 * pg) == 0 else pg)
    selb, selb_new = dsa_sample_select(scores, qi3, wi3, ki.reshape(b, 1, IDX_DH), topk)
    o = dsa_sample_attend(q.reshape(b, ATT_HEADS, dh), selb, selb_new, k.reshape(b, ATT_KV_HEADS, dh),
                          v.reshape(b, ATT_KV_HEADS, dh), bias_tab, cache_k_t, cache_v_t, layer, page_table, pg)
    return o.reshape(b, d)


def kernel(x_prompt, x_sample, c_prompt, c_sample, state_gla, cache_k, cache_v, cache_idx_k, page_table,
           rel_bias, w_ada, b_ada, norm_mix, norm_ffn, norm_final,
           gla_w_in, gla_w_g2, gla_b_g2, gla_gnorm, gla_w_out,
           dsa_w_in, dsa_idx_ln_g, dsa_idx_ln_b, dsa_w_out,
           moe_w_router, moe_router_bias, moe_w_gate, moe_w_up, moe_w_down,
           shared_w_gate, shared_w_up, shared_w_down):
    bp, t, d = x_prompt.shape
    bs = x_sample.shape[0]
    depth = w_ada.shape[0]
    dh = d // ATT_HEADS
    tm_p = min(256, t)
    tq = 128
    mods = ada_mod_all(jnp.concatenate([c_prompt, c_sample], axis=0), w_ada, b_ada)
    xp = x_prompt.reshape(bp * t, d)
    xs = x_sample.reshape(bs, d)
    bias_tab = bias_tables(rel_bias, tq)
    gla_p, gla_s, kp_l, vp_l, kip_l, ks_l, vs_l, kis_l = [], [], [], [], [], [], [], []
    for i in range(depth):
        mod_p = mods[i, :bp].reshape(bp, 1, 6 * d)
        mod_s = mods[i, bp:]
        j = i // 2
        if i % 2 == 0:
            qkvr_p, la_p = gla_project(xp, mod_p, norm_mix[i], gla_w_in[j], gla_w_g2[j], gla_b_g2[j], tm_p, t)
            o_p, s_p = gla_chunked(qkvr_p, la_p, gla_gnorm[j], bp, t)
            qkvr_s, la_s = gla_project(xs, mod_s, norm_mix[i], gla_w_in[j], gla_w_g2[j], gla_b_g2[j], bs, 1)
            o_s, s_s = gla_step(qkvr_s, la_s, state_gla[j], gla_gnorm[j])
            gla_p.append(s_p)
            gla_s.append(s_s)
            w_out = gla_w_out[j]
        else:
            dp = (dsa_w_in[j], dsa_idx_ln_g[j], dsa_idx_ln_b[j])
            q_p, qi_p, wit_p, kt_p, vt32_p, kit_p, kib_p, kg_p, vt_p = dsa_project(
                xp, mod_p, norm_mix[i], *dp, tm_p, t, True)
            o_p = dsa_prompt(q_p, qi_p, wit_p, kib_p, kg_p, vt_p, bias_tab, bp, t, tq)
            q_s, qi_s, wit_s, k_s, v_s, ki_s = dsa_project(xs, mod_s, norm_mix[i], *dp, bs, 1, False)
            o_s = dsa_sample(q_s, k_s, v_s, qi_s, ki_s, wit_s, cache_k, cache_v, cache_idx_k, j,
                             page_table, bias_tab)
            kp_l.append(kt_p.reshape(bp, ATT_KV_HEADS, dh, t).transpose(0, 3, 1, 2))
            vp_l.append(vt32_p.reshape(bp, ATT_KV_HEADS, dh, t).transpose(0, 3, 1, 2))
            kip_l.append(kit_p.transpose(0, 2, 1))
            ks_l.append(k_s.reshape(bs, 1, ATT_KV_HEADS, dh))
            vs_l.append(v_s.reshape(bs, 1, ATT_KV_HEADS, dh))
            kis_l.append(ki_s.reshape(bs, 1, IDX_DH))
            w_out = dsa_w_out[j]
        last = i == depth - 1
        mo = (norm_ffn[i], moe_w_router[i], moe_router_bias[i], moe_w_gate, moe_w_up, moe_w_down, i,
              shared_w_gate[i], shared_w_up[i], shared_w_down[i], norm_final, last)
        xp = mixer_out_and_moe(o_p, w_out, xp, mod_p, *mo, tm_p, t, 512)
        xs = mixer_out_and_moe(o_s, w_out, xs, mod_s, *mo, bs, 1, 32)
    return (xp.reshape(bp, t, d), xs.reshape(bs, 1, d), jnp.stack(gla_p), jnp.stack(gla_s),
            jnp.stack(kp_l), jnp.stack(vp_l), jnp.stack(kip_l),
            jnp.stack(ks_l), jnp.stack(vs_l), jnp.stack(kis_l))
```
